```python
import math
import jax, jax.numpy as jnp
from jax import lax
import numpy as np

D_MODEL = 2048
BATCH = 4
SEQ = 4096
DEPTH = 2

RET_HEADS = 4
RET_DK = 256
RET_DV = 256
ML_HEADS = 4
ML_DH = 256
RET_WIDTH = RET_HEADS * RET_DK
ML_WIDTH = ML_HEADS * ML_DH
CHUNK = 128
CONV_W = 4
RM_SPLITS = [RET_WIDTH * (j + 1) for j in range(4)] + [4 * RET_WIDTH + ML_WIDTH * (j + 1) for j in range(4)]
RM_COLS = 4 * RET_WIDTH + 4 * ML_WIDTH + 2 * ML_HEADS
ROPE_BASE = 10000.0
ATT_HEADS = 32
KV_HEADS = 4
HEAD_DIM = 64
WINDOW = 128
ATT_BLOCK = 128
SWA_COLS = (ATT_HEADS + 2 * KV_HEADS) * HEAD_DIM
N_BUCKETS = 32
MAX_DIST = 128
D_FF = -(-8 * D_MODEL // (3 * 256)) * 256
EPS = 1e-6

kernel_name = "hybrid_retention_mlstm_swa_sink_trunk"


def rms_norm(x, g):
    xf = x.astype(jnp.float32)
    y = xf * lax.rsqrt(jnp.mean(xf * xf, axis=-1, keepdims=True) + EPS)
    return (y * g.astype(jnp.float32)).astype(x.dtype)


def rotary(x, pos):
    d = x.shape[-1]
    inv = 1.0 / (ROPE_BASE ** jnp.linspace(0.0, 1.0, d // 2, dtype=jnp.float32))
    ang = pos.astype(jnp.float32)[:, None] * inv[None, :]
    cos = jnp.cos(ang)[None, :, None, :]
    sin = jnp.sin(ang)[None, :, None, :]
    xf = x.astype(jnp.float32)
    x1, x2 = xf[..., : d // 2], xf[..., d // 2:]
    return jnp.concatenate([x1 * cos - x2 * sin, x2 * cos + x1 * sin], axis=-1)


def causal_conv(u, w):
    K, C = w.shape
    return lax.conv_general_dilated(u, w[:, None, :].astype(u.dtype), window_strides=(1,),
                                    padding=[(K - 1, 0)],
                                    dimension_numbers=("NWC", "WIO", "NWC"),
                                    feature_group_count=C)


def retention(q, k, v):
    B, T, H, dk = q.shape
    dv = v.shape[-1]
    L = CHUNK
    NC = T // L
    log_g = jnp.log1p(-jnp.exp2(-5.0 - jnp.arange(H, dtype=jnp.float32)))
    q = q.reshape(B, NC, L, H, dk)
    k = k.reshape(B, NC, L, H, dk)
    v = v.reshape(B, NC, L, H, dv)
    idx = jnp.arange(L, dtype=jnp.float32)
    diff = idx[:, None] - idx[None, :]
    dmask = jnp.where(diff >= 0, jnp.exp(log_g[:, None, None] * jnp.maximum(diff, 0.0)), 0.0)
    s = jnp.einsum('bcihd,bcjhd->bchij', q, k) * dmask[None, None]
    intra = jnp.einsum('bchij,bcjhe->bcihe', s, v)
    w_k = jnp.exp(log_g[None, :] * (L - 1.0 - idx)[:, None])
    kv = jnp.einsum('bcjhd,jh,bcjhe->bchde', k, w_k, v)
    g_L = jnp.exp(log_g * L)

    def step(R, kv_c):
        return R * g_L[None, :, None, None] + kv_c, R

    _, R_prev = lax.scan(step, jnp.zeros((B, H, dk, dv), jnp.float32), jnp.moveaxis(kv, 1, 0))
    R_prev = jnp.moveaxis(R_prev, 0, 1)
    w_q = jnp.exp(log_g[None, :] * (idx + 1.0)[:, None])
    inter = jnp.einsum('bcihd,bchde->bcihe', q, R_prev) * w_q[None, None, :, :, None]
    return (intra + inter).reshape(B, T, H, dv)


def mlstm(q, k, v, i_pre, f_pre):
    B, T, H, d = q.shape
    L = CHUNK
    NC = T // L
    k = k * (d ** -0.5)
    lf = jax.nn.log_sigmoid(f_pre)
    to_c = lambda a: a.reshape(B, NC, L, H, d).transpose(1, 0, 3, 2, 4)
    to_g = lambda a: a.reshape(B, NC, L, H).transpose(1, 0, 3, 2)
    causal = jnp.tril(jnp.ones((L, L), dtype=bool))

    def step(carry, xs):
        C, n, m = carry
        qb, kb, vb, ib, fb = xs
        b = jnp.cumsum(fb, axis=-1)
        logD = jnp.where(causal, b[..., :, None] - b[..., None, :] + ib[..., None, :], -jnp.inf)
        inter_log = b + m[..., None]
        m_t = jnp.maximum(inter_log, jnp.max(logD, axis=-1))
        D = jnp.exp(logD - m_t[..., None])
        w_inter = jnp.exp(inter_log - m_t)
        s = jnp.einsum('bhid,bhjd->bhij', qb, kb) * D
        num = jnp.einsum('bhij,bhjd->bhid', s, vb) + w_inter[..., None] * jnp.einsum('bhid,bhde->bhie', qb, C)
        den = jnp.sum(s, axis=-1) + w_inter * jnp.einsum('bhid,bhd->bhi', qb, n)
        h = num / jnp.maximum(jnp.abs(den), jnp.exp(-m_t))[..., None]
        b_L = b[..., -1]
        log_w = b_L[..., None] - b + ib
        m_new = jnp.maximum(b_L + m, jnp.max(log_w, axis=-1))
        w = jnp.exp(log_w - m_new[..., None])
        decay = jnp.exp(b_L + m - m_new)
        C = decay[..., None, None] * C + jnp.einsum('bhj,bhjd,bhje->bhde', w, kb, vb)
        n = decay[..., None] * n + jnp.einsum('bhj,bhjd->bhd', w, kb)
        return (C, n, m_new), h

    init = (jnp.zeros((B, H, d, d), jnp.float32), jnp.zeros((B, H, d), jnp.float32),
            jnp.zeros((B, H), jnp.float32))
    _, hs = lax.scan(step, init, (to_c(q), to_c(k), to_c(v), to_g(i_pre), to_g(lf)))
    return hs.transpose(1, 0, 3, 2, 4).reshape(B, T, H, d)


def retention_mlstm_mixer(h, w_in, conv_w, gate_b, head_g, w_out):
    B, T, _ = h.shape
    proj = h @ w_in.astype(h.dtype)
    rq, rk, rv, rg, mq, mk, mv, mo, mif = jnp.split(proj, RM_SPLITS, axis=-1)
    pos = jnp.arange(T)
    rq = rotary(rq.reshape(B, T, RET_HEADS, RET_DK), pos)
    rk = rotary(rk.reshape(B, T, RET_HEADS, RET_DK), pos) * (RET_DK ** -0.5)
    ret = retention(rq, rk, rv.reshape(B, T, RET_HEADS, RET_DV).astype(jnp.float32))
    ret = rms_norm(ret, head_g[:RET_WIDTH].reshape(RET_HEADS, RET_DV))
    ret = ret * jax.nn.silu(rg.reshape(B, T, RET_HEADS, RET_DV).astype(jnp.float32))
    mqk = jax.nn.silu(causal_conv(jnp.concatenate([mq, mk], axis=-1), conv_w)).astype(jnp.float32)
    mq, mk = mqk[..., :ML_WIDTH], mqk[..., ML_WIDTH:]
    gates = mif.astype(jnp.float32) + gate_b.astype(jnp.float32)
    ml = mlstm(mq.reshape(B, T, ML_HEADS, ML_DH), mk.reshape(B, T, ML_HEADS, ML_DH),
               mv.reshape(B, T, ML_HEADS, ML_DH).astype(jnp.float32),
               gates[..., :ML_HEADS], gates[..., ML_HEADS:])
    ml = rms_norm(ml, head_g[RET_WIDTH:].reshape(ML_HEADS, ML_DH))
    ml = ml * jax.nn.sigmoid(mo.reshape(B, T, ML_HEADS, ML_DH).astype(jnp.float32))
    cat = jnp.concatenate([ret.reshape(B, T, RET_WIDTH), ml.reshape(B, T, ML_WIDTH)], axis=-1)
    return cat.astype(h.dtype) @ w_out.astype(h.dtype)


def t5_bucket(dist):
    n = jnp.maximum(dist, 0)
    max_exact = N_BUCKETS // 2
    nf = jnp.maximum(n, 1).astype(jnp.float32)
    large = max_exact + (jnp.log(nf / max_exact) / math.log(MAX_DIST / max_exact)
                         * (N_BUCKETS - max_exact)).astype(jnp.int32)
    large = jnp.minimum(large, N_BUCKETS - 1)
    return jnp.where(n < max_exact, n, large)


def swa_sink_mixer(h, w_in, sinks, rel_bias, w_out):
    B, T, _ = h.shape
    L = ATT_BLOCK
    NB = T // L
    G = ATT_HEADS // KV_HEADS
    proj = h @ w_in.astype(h.dtype)
    q, k, v = jnp.split(proj, [ATT_HEADS * HEAD_DIM, (ATT_HEADS + KV_HEADS) * HEAD_DIM], axis=-1)
    q = q.reshape(B, NB, L, KV_HEADS, G, HEAD_DIM)
    k = k.reshape(B, T, KV_HEADS, HEAD_DIM)
    v = v.reshape(B, T, KV_HEADS, HEAD_DIM)
    band = lambda a: jnp.concatenate(
        [jnp.pad(a, ((0, 0), (L, 0), (0, 0), (0, 0)))[:, :T].reshape(B, NB, L, KV_HEADS, HEAD_DIM),
         a.reshape(B, NB, L, KV_HEADS, HEAD_DIM)], axis=2)
    kb, vb = band(k), band(v)
    s = jnp.einsum('bnikgd,bnjkd->bnkgij', q, kb,
                   preferred_element_type=jnp.float32) * (HEAD_DIM ** -0.5)
    i = jnp.arange(L)
    j = jnp.arange(2 * L)
    dist = (L + i)[:, None] - j[None, :]
    bias = rel_bias.astype(jnp.float32)[t5_bucket(dist)]
    bias = bias.transpose(2, 0, 1).reshape(KV_HEADS, G, L, 2 * L)
    key_pos = jnp.arange(NB)[:, None] * L - L + j[None, :]
    valid = ((dist >= 0) & (dist < WINDOW))[None] & (key_pos >= 0)[:, None, :]
    s = jnp.where(valid[None, :, None, None], s + bias[None, None], -jnp.inf)
    sink = sinks.astype(jnp.float32).reshape(KV_HEADS, G)[None, None, :, :, None, None]
    mx = jnp.maximum(jnp.max(s, axis=-1, keepdims=True), sink)
    p = jnp.exp(s - mx)
    p = p / (jnp.sum(p, axis=-1, keepdims=True) + jnp.exp(sink - mx))
    o = jnp.einsum('bnkgij,bnjkd->bnikgd', p, vb.astype(jnp.float32))
    return o.reshape(B, T, ATT_HEADS * HEAD_DIM).astype(h.dtype) @ w_out.astype(h.dtype)


def swiglu(h, w_gu, w_down):
    gu = h @ w_gu.astype(h.dtype)
    g, u = gu[..., :D_FF], gu[..., D_FF:]
    return (jax.nn.silu(g) * u) @ w_down.astype(h.dtype)


def setup_inputs(seed: int = 0) -> dict:
    key = jax.random.key(seed)
    ks = jax.random.split(key, 16)
    n_even = (DEPTH + 1) // 2
    n_odd = DEPTH // 2
    f32 = jnp.float32
    nrm = lambda k, shape, scale: jax.random.normal(k, shape, f32) * scale
    mix_w = RET_WIDTH + ML_WIDTH
    f_bias = jnp.linspace(3.0, 6.0, ML_HEADS, dtype=f32)
    return {
        "x": nrm(ks[0], (BATCH, SEQ, D_MODEL), 1.0),
        "rel_bias": nrm(ks[1], (N_BUCKETS, ATT_HEADS), 0.5),
        "norm_g": 1.0 + nrm(ks[2], (DEPTH, 4, D_MODEL), 0.05),
        "ffn_w_gu": nrm(ks[3], (DEPTH, D_MODEL, 2 * D_FF), D_MODEL ** -0.5),
        "ffn_w_down": nrm(ks[4], (DEPTH, D_FF, D_MODEL), D_FF ** -0.5),
        "rm_w_in": nrm(ks[5], (n_even, D_MODEL, RM_COLS), D_MODEL ** -0.5),
        "ml_conv_w": nrm(ks[6], (n_even, CONV_W, 2 * ML_WIDTH), CONV_W ** -0.5),
        "ml_gate_b": jnp.concatenate([nrm(ks[7], (n_even, ML_HEADS), 0.1),
                                      f_bias + nrm(ks[8], (n_even, ML_HEADS), 0.1)], axis=-1),
        "rm_head_g": 1.0 + nrm(ks[9], (n_even, mix_w), 0.05),
        "rm_w_out": nrm(ks[10], (n_even, mix_w, D_MODEL), mix_w ** -0.5),
        "swa_w_in": nrm(ks[11], (n_odd, D_MODEL, SWA_COLS), D_MODEL ** -0.5),
        "swa_sinks": nrm(ks[12], (n_odd, ATT_HEADS), 0.5),
        "swa_w_out": nrm(ks[13], (n_odd, ATT_HEADS * HEAD_DIM, D_MODEL), (ATT_HEADS * HEAD_DIM) ** -0.5),
    }


def reference(x, rel_bias, norm_g, ffn_w_gu, ffn_w_down, rm_w_in, ml_conv_w, ml_gate_b,
              rm_head_g, rm_w_out, swa_w_in, swa_sinks, swa_w_out):
    for layer in range(DEPTH):
        g = norm_g[layer]
        h = rms_norm(x, g[0])
        if layer % 2 == 0:
            e = layer // 2
            h = retention_mlstm_mixer(h, rm_w_in[e], ml_conv_w[e], ml_gate_b[e], rm_head_g[e], rm_w_out[e])
        else:
            o = layer // 2
            h = swa_sink_mixer(h, swa_w_in[o], swa_sinks[o], rel_bias, swa_w_out[o])
        x = x + rms_norm(h, g[1])
        h = swiglu(rms_norm(x, g[2]), ffn_w_gu[layer], ffn_w_down[layer])
        x = x + rms_norm(h, g[3])
    return x
```

```python
import functools
import math

import jax
import jax.numpy as jnp
from jax import lax
from jax.experimental import pallas as pl
from jax.experimental.pallas import tpu as pltpu

F32 = jnp.float32
BF16 = jnp.bfloat16

EPS = 1e-6
CHUNK = 128
RET_HEADS = 4
ML_HEADS = 4
HEAD_W = 256
CONV_W = 4
ROPE_BASE = 10000.0
ATT_HEADS = 32
KV_HEADS = 4
HEAD_DIM = 64
WINDOW = 128
ATT_BLOCK = 128
N_BUCKETS = 32
MAX_DIST = 128
LANES = 128
SUBLANES = 8
VMEM_LIMIT = 56 * 1024 * 1024


def _params(vmem=VMEM_LIMIT):
    return pltpu.CompilerParams(vmem_limit_bytes=vmem)


def _rms_rows(x, g):
    ms = jnp.mean(x * x, axis=-1, keepdims=True)
    return x * lax.rsqrt(ms + EPS) * g


def _silu(x):
    return x * jax.nn.sigmoid(x)


def _norm_rows_into(x_ref, g_ref, h_ref, row_chunk):
    g = g_ref[...]

    def body(r, carry):
        rows = pl.ds(pl.multiple_of(r * row_chunk, row_chunk), row_chunk)
        h_ref[rows, :] = _rms_rows(x_ref[rows, :], g).astype(h_ref.dtype)
        return carry

    lax.fori_loop(0, x_ref.shape[0] // row_chunk, body, 0)


def _norm_matmul_body(x_ref, g_ref, w_ref, o_ref, h_ref):
    @pl.when(pl.program_id(1) == 0)
    def _():
        _norm_rows_into(x_ref, g_ref, h_ref, 64)

    o_ref[...] = jnp.dot(h_ref[...], w_ref[...], preferred_element_type=F32).astype(o_ref.dtype)


def _norm_matmul(x, g, w, out_dtype, tm, tn):
    n, d = x.shape
    cols = w.shape[1]
    return pl.pallas_call(
        _norm_matmul_body,
        grid=(n // tm, cols // tn),
        in_specs=[
            pl.BlockSpec((tm, d), lambda i, j: (i, 0)),
            pl.BlockSpec((1, d), lambda i, j: (0, 0)),
            pl.BlockSpec((d, tn), lambda i, j: (0, j)),
        ],
        out_specs=pl.BlockSpec((tm, tn), lambda i, j: (i, j)),
        out_shape=jax.ShapeDtypeStruct((n, cols), out_dtype),
        scratch_shapes=[pltpu.VMEM((tm, d), BF16)],
        compiler_params=_params(),
        name="norm_matmul",
    )(x, g.reshape(1, d), w)


def _proj_norm_res_body(*refs, n_in):
    a_refs = refs[:n_in]
    w_refs = refs[n_in:2 * n_in]
    x_ref, g_ref, o_ref = refs[2 * n_in:]
    y = jnp.dot(a_refs[0][...], w_refs[0][...], preferred_element_type=F32)
    for a_ref, w_ref in zip(a_refs[1:], w_refs[1:]):
        y = y + jnp.dot(a_ref[...], w_ref[...], preferred_element_type=F32)
    o_ref[...] = x_ref[...] + _rms_rows(y, g_ref[...])


def _proj_norm_res(a_list, w_list, x, g, tm):
    n, d = x.shape
    n_in = len(a_list)
    in_specs = [pl.BlockSpec((tm, a.shape[1]), lambda i: (i, 0)) for a in a_list]
    in_specs += [pl.BlockSpec(w.shape, lambda i: (0, 0)) for w in w_list]
    in_specs += [pl.BlockSpec((tm, d), lambda i: (i, 0)), pl.BlockSpec((1, d), lambda i: (0, 0))]
    return pl.pallas_call(
        functools.partial(_proj_norm_res_body, n_in=n_in),
        grid=(n // tm,),
        in_specs=in_specs,
        out_specs=pl.BlockSpec((tm, d), lambda i: (i, 0)),
        out_shape=jax.ShapeDtypeStruct((n, d), F32),
        compiler_params=_params(),
        name="proj_norm_res",
    )(*a_list, *w_list, x, g.reshape(1, d))


def _ffn_body(x_ref, g_in_ref, wg_ref, wu_ref, wd_ref, g_out_ref, o_ref, h_ref, acc_ref):
    f = pl.program_id(1)

    @pl.when(f == 0)
    def _():
        _norm_rows_into(x_ref, g_in_ref, h_ref, 64)

    h = h_ref[...]
    gate = jnp.dot(h, wg_ref[...], preferred_element_type=F32)
    up = jnp.dot(h, wu_ref[...], preferred_element_type=F32)
    act = (_silu(gate) * up).astype(BF16)
    part = jnp.dot(act, wd_ref[...], preferred_element_type=F32)

    @pl.when(f == 0)
    def _():
        acc_ref[...] = part

    @pl.when(f > 0)
    def _():
        acc_ref[...] += part

    @pl.when(f == pl.num_programs(1) - 1)
    def _():
        o_ref[...] = x_ref[...] + _rms_rows(acc_ref[...], g_out_ref[...])


def _ffn(x, g_in, w_gu, w_down, g_out, tm, tf):
    n, d = x.shape
    d_ff = w_down.shape[0]
    nf = d_ff // tf
    return pl.pallas_call(
        _ffn_body,
        grid=(n // tm, nf),
        in_specs=[
            pl.BlockSpec((tm, d), lambda i, f: (i, 0)),
            pl.BlockSpec((1, d), lambda i, f: (0, 0)),
            pl.BlockSpec((d, tf), lambda i, f: (0, f)),
            pl.BlockSpec((d, tf), lambda i, f: (0, f + nf)),
            pl.BlockSpec((tf, d), lambda i, f: (f, 0)),
            pl.BlockSpec((1, d), lambda i, f: (0, 0)),
        ],
        out_specs=pl.BlockSpec((tm, d), lambda i, f: (i, 0)),
        out_shape=jax.ShapeDtypeStruct((n, d), F32),
        scratch_shapes=[pltpu.VMEM((tm, d), BF16), pltpu.VMEM((tm, d), F32)],
        compiler_params=_params(),
        name="ffn",
    )(x, g_in.reshape(1, d), w_gu, w_gu, w_down, g_out.reshape(1, d))


def _retention_body(gl_ref, q_ref, k_ref, v_ref, g_ref, cos_ref, sin_ref, dm_ref, wq_ref, wk_ref,
                    hg_ref, o_ref, state_ref):
    h = pl.program_id(1)
    c = pl.program_id(2)

    @pl.when(c == 0)
    def _():
        state_ref[...] = jnp.zeros_like(state_ref)

    cos = cos_ref[...]
    sin = sin_ref[...]
    half = HEAD_W // 2

    def rot(x):
        x = x.astype(F32)
        x1, x2 = x[:, :half], x[:, half:]
        return jnp.concatenate([x1 * cos - x2 * sin, x2 * cos + x1 * sin], axis=-1)

    q = rot(q_ref[0])
    k = rot(k_ref[0]) * (HEAD_W ** -0.5)
    v = v_ref[0]
    qb = q.astype(BF16)
    s = lax.dot_general(qb, k.astype(BF16), (((1,), (1,)), ((), ())),
                        preferred_element_type=F32) * dm_ref[0]
    intra = jnp.dot(s.astype(BF16), v, preferred_element_type=F32)
    state = state_ref[...]
    inter = jnp.dot(qb, state.astype(BF16), preferred_element_type=F32) * wq_ref[0]
    out = intra + inter
    kw = (k * wk_ref[0]).astype(BF16)
    kv = lax.dot_general(kw, v, (((0,), (0,)), ((), ())), preferred_element_type=F32)
    state_ref[...] = state * gl_ref[h] + kv
    y = _rms_rows(out, hg_ref[0])
    o_ref[0] = (y * _silu(g_ref[0].astype(F32))).astype(o_ref.dtype)


def _retention(proj, head_g, cos, sin):
    b, t, _ = proj.shape
    nc = t // CHUNK
    hh = RET_HEADS
    log_g = jnp.log1p(-jnp.exp2(-5.0 - jnp.arange(hh, dtype=F32)))
    idx = jnp.arange(CHUNK, dtype=F32)
    diff = idx[:, None] - idx[None, :]
    dmask = jnp.where(diff >= 0, jnp.exp(log_g[:, None, None] * jnp.maximum(diff, 0.0)), 0.0)
    w_k = jnp.exp(log_g[:, None] * (CHUNK - 1.0 - idx)[None, :])[..., None]
    w_q = jnp.exp(log_g[:, None] * (idx + 1.0)[None, :])[..., None]
    g_l = jnp.exp(log_g * CHUNK)

    def col(group):
        return pl.BlockSpec((1, CHUNK, HEAD_W), lambda bi, h, c: (bi, c, group * hh + h))

    return pl.pallas_call(
        _retention_body,
        grid=(b, hh, nc),
        in_specs=[
            pl.BlockSpec(memory_space=pltpu.SMEM),
            col(0), col(1), col(2), col(3),
            pl.BlockSpec((CHUNK, HEAD_W // 2), lambda bi, h, c: (c, 0)),
            pl.BlockSpec((CHUNK, HEAD_W // 2), lambda bi, h, c: (c, 0)),
            pl.BlockSpec((1, CHUNK, CHUNK), lambda bi, h, c: (h, 0, 0)),
            pl.BlockSpec((1, CHUNK, 1), lambda bi, h, c: (h, 0, 0)),
            pl.BlockSpec((1, CHUNK, 1), lambda bi, h, c: (h, 0, 0)),
            pl.BlockSpec((1, 1, HEAD_W), lambda bi, h, c: (h, 0, 0)),
        ],
        out_specs=pl.BlockSpec((1, CHUNK, HEAD_W), lambda bi, h, c: (bi, c, h)),
        out_shape=jax.ShapeDtypeStruct((b, t, hh * HEAD_W), BF16),
        scratch_shapes=[pltpu.VMEM((HEAD_W, HEAD_W), F32)],
        compiler_params=_params(),
        name="retention",
    )(g_l, proj, proj, proj, proj, cos, sin, dmask, w_q, w_k, head_g.reshape(hh, 1, HEAD_W))


def _log_sigmoid(x):
    return jnp.minimum(x, 0.0) - jnp.log1p(jnp.exp(-jnp.abs(x)))


def _cumsum(x, axis):
    n = x.shape[axis]
    pos = lax.broadcasted_iota(jnp.int32, x.shape, axis)
    d = 1
    while d < n:
        x = x + jnp.where(pos >= d, pltpu.roll(x, d, axis=axis), 0.0)
        d *= 2
    return x


def _mlstm_body(q_ref, qt_ref, k_ref, kt_ref, v_ref, og_ref, gc_ref, gr_ref, gbc_ref, gbr_ref,
                cwq_ref, cwk_ref, hg_ref, o_ref, c_ref, n_ref, m_ref):
    h = pl.program_id(1)
    c = pl.program_id(2)
    first = c == 0

    @pl.when(first)
    def _():
        c_ref[...] = jnp.zeros_like(c_ref)
        n_ref[...] = jnp.zeros_like(n_ref)
        m_ref[...] = jnp.zeros_like(m_ref)

    row8 = lax.broadcasted_iota(jnp.int32, (SUBLANES, HEAD_W), 0)

    def conv_silu(u_ref, t_ref, w_ref):
        u = u_ref[0].astype(F32)
        tail = t_ref[0].astype(F32)[SUBLANES:]
        tail = jnp.where(first, 0.0, tail)
        w = w_ref[...]
        y = u * w[CONV_W - 1:CONV_W]
        for s in range(1, CONV_W):
            ru = pltpu.roll(u, s, axis=0)
            rt = pltpu.roll(tail, s, axis=0)
            top = jnp.where(row8 < s, rt, ru[:SUBLANES])
            shifted = jnp.concatenate([top, ru[SUBLANES:]], axis=0)
            y = y + shifted * w[CONV_W - 1 - s:CONV_W - s]
        return _silu(y)

    q = conv_silu(q_ref, qt_ref, cwq_ref)
    k = conv_silu(k_ref, kt_ref, cwk_ref) * (HEAD_W ** -0.5)
    v = v_ref[0]

    lane = lax.broadcasted_iota(jnp.int32, (CHUNK, LANES), 1)
    gc = gc_ref[0] + gbc_ref[...]
    i_col = jnp.sum(jnp.where(lane == h, gc, 0.0), axis=-1, keepdims=True)
    f_col = jnp.sum(jnp.where(lane == h + ML_HEADS, gc, 0.0), axis=-1, keepdims=True)
    sub = lax.broadcasted_iota(jnp.int32, (2 * ML_HEADS, CHUNK), 0)
    gr = gr_ref[0] + gbr_ref[...]
    i_row = jnp.sum(jnp.where(sub == h, gr, 0.0), axis=0, keepdims=True)
    f_row = jnp.sum(jnp.where(sub == h + ML_HEADS, gr, 0.0), axis=0, keepdims=True)

    b_cols = _cumsum(jnp.broadcast_to(_log_sigmoid(f_col), (CHUNK, CHUNK)), 0)
    b_rows = _cumsum(jnp.broadcast_to(_log_sigmoid(f_row), (SUBLANES, CHUNK)), 1)[:1]
    b_col = b_cols[:, :1]
    rowi = lax.broadcasted_iota(jnp.int32, (CHUNK, CHUNK), 0)
    coli = lax.broadcasted_iota(jnp.int32, (CHUNK, CHUNK), 1)
    log_d = jnp.where(rowi >= coli, b_cols - b_rows + i_row, -jnp.inf)
    m_prev = m_ref[:, :1]
    inter_log = b_col + m_prev
    m_t = jnp.maximum(inter_log, jnp.max(log_d, axis=-1, keepdims=True))
    d_mat = jnp.exp(log_d - m_t)
    w_inter = jnp.exp(inter_log - m_t)

    qb = q.astype(BF16)
    s = lax.dot_general(qb, k.astype(BF16), (((1,), (1,)), ((), ())),
                        preferred_element_type=F32) * d_mat
    c_state = c_ref[...]
    n_state = n_ref[...]
    num = (jnp.dot(s.astype(BF16), v, preferred_element_type=F32)
           + w_inter * jnp.dot(qb, c_state.astype(BF16), preferred_element_type=F32))
    den = (jnp.sum(s, axis=-1, keepdims=True)
           + w_inter * jnp.sum(q * n_state, axis=-1, keepdims=True))
    hid = num / jnp.maximum(jnp.abs(den), jnp.exp(-m_t))

    b_last = b_col[CHUNK - 1:CHUNK]
    log_w = b_last - b_col + i_col
    m_new = jnp.maximum(b_last + m_prev, jnp.max(log_w, axis=0, keepdims=True))
    w = jnp.exp(log_w - m_new)
    decay = jnp.exp(b_last + m_prev - m_new)
    kw = k * w
    c_ref[...] = decay * c_state + lax.dot_general(kw.astype(BF16), v, (((0,), (0,)), ((), ())),
                                                   preferred_element_type=F32)
    n_ref[...] = decay * n_state + jnp.sum(kw, axis=0, keepdims=True)
    m_ref[...] = jnp.broadcast_to(m_new, m_ref.shape)

    y = _rms_rows(hid, hg_ref[0])
    o_ref[0] = (y * jax.nn.sigmoid(og_ref[0].astype(F32))).astype(o_ref.dtype)


def _mlstm(proj, gates_col, gates_row, gate_b, conv_w, head_g):
    b, t, _ = proj.shape
    nc = t // CHUNK
    hh = ML_HEADS
    base = 4 * RET_HEADS
    tail_rows = 2 * SUBLANES
    per_chunk = CHUNK // tail_rows

    def col(group):
        return pl.BlockSpec((1, CHUNK, HEAD_W), lambda bi, h, c: (bi, c, base + group * hh + h))

    def tail(group):
        return pl.BlockSpec((1, tail_rows, HEAD_W),
                            lambda bi, h, c: (bi, jnp.maximum(c * per_chunk - 1, 0), base + group * hh + h))

    gb_col = jnp.pad(gate_b, (0, LANES - 2 * hh)).reshape(1, LANES)
    gb_row = gate_b.reshape(2 * hh, 1)
    return pl.pallas_call(
        _mlstm_body,
        grid=(b, hh, nc),
        in_specs=[
            col(0), tail(0), col(1), tail(1), col(2), col(3),
            pl.BlockSpec((1, CHUNK, LANES), lambda bi, h, c: (bi, c, 0)),
            pl.BlockSpec((1, 2 * hh, CHUNK), lambda bi, h, c: (bi, 0, c)),
            pl.BlockSpec((1, LANES), lambda bi, h, c: (0, 0)),
            pl.BlockSpec((2 * hh, 1), lambda bi, h, c: (0, 0)),
            pl.BlockSpec((CONV_W, HEAD_W), lambda bi, h, c: (0, h)),
            pl.BlockSpec((CONV_W, HEAD_W), lambda bi, h, c: (0, hh + h)),
            pl.BlockSpec((1, 1, HEAD_W), lambda bi, h, c: (h, 0, 0)),
        ],
        out_specs=pl.BlockSpec((1, CHUNK, HEAD_W), lambda bi, h, c: (bi, c, h)),
        out_shape=jax.ShapeDtypeStruct((b, t, hh * HEAD_W), BF16),
        scratch_shapes=[pltpu.VMEM((HEAD_W, HEAD_W), F32), pltpu.VMEM((1, HEAD_W), F32),
                        pltpu.VMEM((1, LANES), F32)],
        compiler_params=_params(),
        name="mlstm",
    )(proj, proj, proj, proj, proj, proj, gates_col, gates_row, gb_col, gb_row, conv_w, conv_w,
      head_g.reshape(hh, 1, HEAD_W))


def _swa_body(rb_ref, sink_ref, idx_ref, q_ref, kp_ref, kc_ref, vp_ref, vc_ref, o_ref, bias_ref):
    bi = pl.program_id(0)
    nb = pl.program_id(1)
    blk = ATT_BLOCK
    group = ATT_HEADS // KV_HEADS

    @pl.when((bi == 0) & (nb == 0))
    def _():
        idx = idx_ref[...]

        def head_body(hd, carry):
            def bucket_body(bk, acc):
                return jnp.where(idx == bk, rb_ref[bk * ATT_HEADS + hd], acc)

            bias_ref[hd] = lax.fori_loop(0, N_BUCKETS, bucket_body,
                                         jnp.full(idx.shape, -jnp.inf, F32))
            return carry

        lax.fori_loop(0, ATT_HEADS, head_body, 0)

    first_key = jnp.where(nb > 0, 0, blk)
    key_ok = lax.broadcasted_iota(jnp.int32, (blk, 2 * blk), 1) >= first_key
    low = lax.broadcasted_iota(jnp.int32, (2 * blk, LANES), 1) < HEAD_DIM
    kf = jnp.concatenate([kp_ref[0], kc_ref[0]], axis=0).astype(F32)
    vf = jnp.concatenate([vp_ref[0], vc_ref[0]], axis=0).astype(F32)
    scale = HEAD_DIM ** -0.5

    def halves(xf, kv):
        xc = xf[:, (kv // 2) * LANES:(kv // 2 + 1) * LANES]
        xr = pltpu.roll(xc, HEAD_DIM, axis=1)
        lo_src, hi_src = (xc, xr) if kv % 2 == 0 else (xr, xc)
        return (jnp.where(low, lo_src, 0.0).astype(BF16), jnp.where(low, 0.0, hi_src).astype(BF16))

    def probs(q2, k_half, hd):
        s = lax.dot_general(q2, k_half, (((1,), (1,)), ((), ())), preferred_element_type=F32)
        s = jnp.where(key_ok, s + bias_ref[hd], -jnp.inf)
        sink = sink_ref[hd]
        mx = jnp.maximum(jnp.max(s, axis=-1, keepdims=True), sink)
        p = jnp.exp(s - mx)
        den = jnp.sum(p, axis=-1, keepdims=True) + jnp.exp(sink - mx)
        return p.astype(BF16), den

    for kv in range(KV_HEADS):
        k_lo, k_hi = halves(kf, kv)
        v_lo, v_hi = halves(vf, kv)
        for pr in range(group // 2):
            hd = kv * group + 2 * pr
            cols = slice(hd * HEAD_DIM, (hd + 2) * HEAD_DIM)
            q2 = q_ref[0, :, cols] * scale
            p_a, den_a = probs(q2, k_lo, hd)
            p_b, den_b = probs(q2, k_hi, hd + 1)
            o2 = (jnp.dot(p_a, v_lo, preferred_element_type=F32) / den_a
                  + jnp.dot(p_b, v_hi, preferred_element_type=F32) / den_b)
            o_ref[0, :, cols] = o2.astype(o_ref.dtype)


def _t5_bucket(dist):
    n = jnp.maximum(dist, 0)
    max_exact = N_BUCKETS // 2
    nf = jnp.maximum(n, 1).astype(F32)
    large = max_exact + (jnp.log(nf / max_exact) / math.log(MAX_DIST / max_exact)
                         * (N_BUCKETS - max_exact)).astype(jnp.int32)
    large = jnp.minimum(large, N_BUCKETS - 1)
    return jnp.where(n < max_exact, n, large)


def _swa(proj, sinks, rel_bias):
    b, t, _ = proj.shape
    blk = ATT_BLOCK
    nblk = t // blk
    qw = ATT_HEADS * HEAD_DIM
    kvw = KV_HEADS * HEAD_DIM
    i = jnp.arange(blk)
    j = jnp.arange(2 * blk)
    dist = (blk + i)[:, None] - j[None, :]
    idx = jnp.where((dist >= 0) & (dist < WINDOW), _t5_bucket(dist), -1).astype(jnp.int32)
    k_blk = qw // kvw
    return pl.pallas_call(
        _swa_body,
        grid=(b, nblk),
        in_specs=[
            pl.BlockSpec(memory_space=pltpu.SMEM),
            pl.BlockSpec(memory_space=pltpu.SMEM),
            pl.BlockSpec((blk, 2 * blk), lambda bi, nb: (0, 0)),
            pl.BlockSpec((1, blk, qw), lambda bi, nb: (bi, nb, 0)),
            pl.BlockSpec((1, blk, kvw), lambda bi, nb: (bi, jnp.maximum(nb - 1, 0), k_blk)),
            pl.BlockSpec((1, blk, kvw), lambda bi, nb: (bi, nb, k_blk)),
            pl.BlockSpec((1, blk, kvw), lambda bi, nb: (bi, jnp.maximum(nb - 1, 0), k_blk + 1)),
            pl.BlockSpec((1, blk, kvw), lambda bi, nb: (bi, nb, k_blk + 1)),
        ],
        out_specs=pl.BlockSpec((1, blk, qw), lambda bi, nb: (bi, nb, 0)),
        out_shape=jax.ShapeDtypeStruct((b, t, qw), BF16),
        scratch_shapes=[pltpu.VMEM((ATT_HEADS, blk, 2 * blk), F32)],
        compiler_params=_params(),
        name="swa",
    )(rel_bias.reshape(-1), sinks, idx, proj, proj, proj, proj, proj)


def _rotary_tables(t):
    half = HEAD_W // 2
    inv = 1.0 / (ROPE_BASE ** jnp.linspace(0.0, 1.0, half, dtype=F32))
    ang = jnp.arange(t).astype(F32)[:, None] * inv[None, :]
    return jnp.cos(ang), jnp.sin(ang)


def kernel(x, rel_bias, norm_g, ffn_w_gu, ffn_w_down, rm_w_in, ml_conv_w, ml_gate_b, rm_head_g,
           rm_w_out, swa_w_in, swa_sinks, swa_w_out):
    b, t, d = x.shape
    n = b * t
    depth = norm_g.shape[0]
    ret_w = RET_HEADS * HEAD_W
    main_cols = 4 * ret_w + 4 * ML_HEADS * HEAD_W
    xs = x.reshape(n, d)
    for layer in range(depth):
        g = norm_g[layer]
        if layer % 2 == 0:
            e = layer // 2
            w_in = rm_w_in[e]
            gate_cols = w_in.shape[1] - main_cols
            w_gate = jnp.pad(w_in[:, main_cols:], ((0, 0), (0, LANES - gate_cols))).astype(BF16)
            proj = _norm_matmul(xs, g[0], w_in[:, :main_cols].astype(BF16), BF16, 1024, 512)
            gates = _norm_matmul(xs, g[0], w_gate, F32, 1024, LANES)
            proj = proj.reshape(b, t, main_cols)
            gates_col = gates.reshape(b, t, LANES)
            gates_row = gates_col[..., :gate_cols].transpose(0, 2, 1)
            cos, sin = _rotary_tables(t)
            ret = _retention(proj, rm_head_g[e][:ret_w], cos, sin)
            ml = _mlstm(proj, gates_col, gates_row, ml_gate_b[e], ml_conv_w[e], rm_head_g[e][ret_w:])
            w_out = rm_w_out[e].astype(BF16)
            xs = _proj_norm_res([ret.reshape(n, -1), ml.reshape(n, -1)],
                                [w_out[:ret_w], w_out[ret_w:]], xs, g[1], 512)
        else:
            o = layer // 2
            proj = _norm_matmul(xs, g[0], swa_w_in[o].astype(BF16), BF16, 1024, 512)
            att = _swa(proj.reshape(b, t, -1), swa_sinks[o], rel_bias)
            xs = _proj_norm_res([att.reshape(n, -1)], [swa_w_out[o].astype(BF16)], xs, g[1], 512)
        xs = _ffn(xs, g[2], ffn_w_gu[layer].astype(BF16), ffn_w_down[layer].astype(BF16), g[3], 512, 512)
    return xs.reshape(b, t, d)
```

```python
import functools
import math

import jax
import jax.numpy as jnp
from jax import lax
from jax.experimental import pallas as pl
from jax.experimental.pallas import tpu as pltpu

F32 = jnp.float32
BF16 = jnp.bfloat16

EPS = 1e-6
CHUNK = 128
RET_HEADS = 4
ML_HEADS = 4
HEAD_W = 256
CONV_W = 4
ROPE_BASE = 10000.0
ATT_HEADS = 32
KV_HEADS = 4
HEAD_DIM = 64
WINDOW = 128
ATT_BLOCK = 128
N_BUCKETS = 32
MAX_DIST = 128
LANES = 128
SUBLANES = 8
VMEM_LIMIT = 56 * 1024 * 1024


def _params(vmem=VMEM_LIMIT):
    return pltpu.CompilerParams(vmem_limit_bytes=vmem)


def _rms_rows(x, g):
    ms = jnp.mean(x * x, axis=-1, keepdims=True)
    return x * lax.rsqrt(ms + EPS) * g


def _silu(x):
    return x * jax.nn.sigmoid(x)


def _norm_rows_into(x_ref, g_ref, h_ref, row_chunk):
    g = g_ref[...]

    def body(r, carry):
        rows = pl.ds(pl.multiple_of(r * row_chunk, row_chunk), row_chunk)
        h_ref[rows, :] = _rms_rows(x_ref[rows, :], g).astype(h_ref.dtype)
        return carry

    lax.fori_loop(0, x_ref.shape[0] // row_chunk, body, 0)


def _norm_matmul_body(*refs, with_side):
    if with_side:
        x_ref, g_ref, w_ref, ws_ref, o_ref, os_ref, h_ref = refs
    else:
        x_ref, g_ref, w_ref, o_ref, h_ref = refs

    @pl.when(pl.program_id(1) == 0)
    def _():
        _norm_rows_into(x_ref, g_ref, h_ref, 64)
        if with_side:
            os_ref[...] = jnp.dot(h_ref[...], ws_ref[...], preferred_element_type=F32)

    o_ref[...] = jnp.dot(h_ref[...], w_ref[...], preferred_element_type=F32).astype(o_ref.dtype)


def _norm_matmul(x, g, w, cols, tm, tn, w_side=None):
    n, d = x.shape
    in_specs = [
        pl.BlockSpec((tm, d), lambda i, j: (i, 0)),
        pl.BlockSpec((1, d), lambda i, j: (0, 0)),
        pl.BlockSpec((d, tn), lambda i, j: (0, j)),
    ]
    out_specs = [pl.BlockSpec((tm, tn), lambda i, j: (i, j))]
    out_shape = [jax.ShapeDtypeStruct((n, cols), BF16)]
    args = [x, g.reshape(1, d), w]
    if w_side is not None:
        in_specs.append(pl.BlockSpec(w_side.shape, lambda i, j: (0, 0)))
        out_specs.append(pl.BlockSpec((tm, w_side.shape[1]), lambda i, j: (i, 0)))
        out_shape.append(jax.ShapeDtypeStruct((n, w_side.shape[1]), F32))
        args.append(w_side)
    return pl.pallas_call(
        functools.partial(_norm_matmul_body, with_side=w_side is not None),
        grid=(n // tm, cols // tn),
        in_specs=in_specs,
        out_specs=out_specs,
        out_shape=out_shape,
        scratch_shapes=[pltpu.VMEM((tm, d), BF16)],
        compiler_params=_params(),
        name="norm_matmul",
    )(*args)


def _proj_norm_res_body(a_ref, w_ref, x_ref, g_ref, o_ref):
    y = jnp.dot(a_ref[...], w_ref[...], preferred_element_type=F32)
    o_ref[...] = x_ref[...] + _rms_rows(y, g_ref[...])


def _proj_norm_res(a, w, x, g, tm):
    n, d = x.shape
    return pl.pallas_call(
        _proj_norm_res_body,
        grid=(n // tm,),
        in_specs=[
            pl.BlockSpec((tm, a.shape[1]), lambda i: (i, 0)),
            pl.BlockSpec(w.shape, lambda i: (0, 0)),
            pl.BlockSpec((tm, d), lambda i: (i, 0)),
            pl.BlockSpec((1, d), lambda i: (0, 0)),
        ],
        out_specs=pl.BlockSpec((tm, d), lambda i: (i, 0)),
        out_shape=jax.ShapeDtypeStruct((n, d), F32),
        compiler_params=_params(),
        name="proj_norm_res",
    )(a, w, x, g.reshape(1, d))


def _ffn_body(x_ref, g_in_ref, wg_ref, wu_ref, wd_ref, g_out_ref, o_ref, h_ref, acc_ref):
    f = pl.program_id(1)

    @pl.when(f == 0)
    def _():
        _norm_rows_into(x_ref, g_in_ref, h_ref, 64)
        acc_ref[...] = jnp.zeros_like(acc_ref)

    h = h_ref[...]
    gate = jnp.dot(h, wg_ref[...], preferred_element_type=F32)
    up = jnp.dot(h, wu_ref[...], preferred_element_type=F32)
    act = (_silu(gate) * up).astype(BF16)
    acc_ref[...] += jnp.dot(act, wd_ref[...], preferred_element_type=F32)

    @pl.when(f == pl.num_programs(1) - 1)
    def _():
        o_ref[...] = x_ref[...] + _rms_rows(acc_ref[...], g_out_ref[...])


def _ffn(x, g_in, w_gu, w_down, g_out, tm, tf):
    n, d = x.shape
    d_ff = w_down.shape[0]
    nf = d_ff // tf
    return pl.pallas_call(
        _ffn_body,
        grid=(n // tm, nf),
        in_specs=[
            pl.BlockSpec((tm, d), lambda i, f: (i, 0)),
            pl.BlockSpec((1, d), lambda i, f: (0, 0)),
            pl.BlockSpec((d, tf), lambda i, f: (0, f)),
            pl.BlockSpec((d, tf), lambda i, f: (0, f + nf)),
            pl.BlockSpec((tf, d), lambda i, f: (f, 0)),
            pl.BlockSpec((1, d), lambda i, f: (0, 0)),
        ],
        out_specs=pl.BlockSpec((tm, d), lambda i, f: (i, 0)),
        out_shape=jax.ShapeDtypeStruct((n, d), F32),
        scratch_shapes=[pltpu.VMEM((tm, d), BF16), pltpu.VMEM((tm, d), F32)],
        compiler_params=_params(),
        name="ffn",
    )(x, g_in.reshape(1, d), w_gu, w_gu, w_down, g_out.reshape(1, d))


def _log_sigmoid(x):
    return jnp.minimum(x, 0.0) - jnp.log1p(jnp.exp(-jnp.abs(x)))


def _cumsum_lanes(x):
    n = x.shape[1]
    pos = lax.broadcasted_iota(jnp.int32, x.shape, 1)
    d = 1
    while d < n:
        x = x + jnp.where(pos >= d, pltpu.roll(x, d, axis=1), 0.0)
        d *= 2
    return x


def _retention_head(h, q_ref, k_ref, v_ref, g_ref, cos, sin, dm_ref, wq_ref, wk_ref, gl_ref, hg_ref,
                    o_ref, state_ref):
    cols = slice(h * HEAD_W, (h + 1) * HEAD_W)
    half = HEAD_W // 2

    def rot(x):
        x = x.astype(F32)
        x1, x2 = x[:, :half], x[:, half:]
        return jnp.concatenate([x1 * cos - x2 * sin, x2 * cos + x1 * sin], axis=-1)

    q = rot(q_ref[0, :, cols])
    k = rot(k_ref[0, :, cols]) * (HEAD_W ** -0.5)
    v = v_ref[0, :, cols]
    qb = q.astype(BF16)
    s = lax.dot_general(qb, k.astype(BF16), (((1,), (1,)), ((), ())),
                        preferred_element_type=F32) * dm_ref[h]
    intra = jnp.dot(s.astype(BF16), v, preferred_element_type=F32)
    state = state_ref[h]
    inter = jnp.dot(qb, state.astype(BF16), preferred_element_type=F32) * wq_ref[h]
    out = intra + inter
    kw = (k * wk_ref[h]).astype(BF16)
    kv = lax.dot_general(kw, v, (((0,), (0,)), ((), ())), preferred_element_type=F32)
    state_ref[h] = state * gl_ref[h] + kv
    y = _rms_rows(out, hg_ref[:, cols])
    o_ref[0, :, cols] = (y * _silu(g_ref[0, :, cols].astype(F32))).astype(o_ref.dtype)


def _mlstm_head(h, first, q_ref, qt_ref, k_ref, kt_ref, v_ref, og_ref, gates, cw_ref, hg_ref, o_ref,
                c_ref, n_ref, m_ref):
    cols = slice(h * HEAD_W, (h + 1) * HEAD_W)
    ml_w = ML_HEADS * HEAD_W
    row8 = lax.broadcasted_iota(jnp.int32, (SUBLANES, HEAD_W), 0)

    def conv_silu(u_ref, t_ref, w):
        u = u_ref[0, :, cols].astype(F32)
        tail = t_ref[0, :, cols].astype(F32)[SUBLANES:]
        tail = jnp.where(first, 0.0, tail)
        y = u * w[CONV_W - 1:CONV_W]
        for s in range(1, CONV_W):
            ru = pltpu.roll(u, s, axis=0)
            rt = pltpu.roll(tail, s, axis=0)
            top = jnp.where(row8 < s, rt, ru[:SUBLANES])
            shifted = jnp.concatenate([top, ru[SUBLANES:]], axis=0)
            y = y + shifted * w[CONV_W - 1 - s:CONV_W - s]
        return _silu(y)

    q = conv_silu(q_ref, qt_ref, cw_ref[:, cols])
    k = conv_silu(k_ref, kt_ref, cw_ref[:, ml_w + h * HEAD_W:ml_w + (h + 1) * HEAD_W]) * (HEAD_W ** -0.5)
    v = v_ref[0, :, cols]

    i_row = gates[h:h + 1]
    f_row = gates[ML_HEADS + h:ML_HEADS + h + 1]
    b_row = _cumsum_lanes(jnp.broadcast_to(_log_sigmoid(f_row), (SUBLANES, CHUNK)))[:1]
    b_cols = jnp.broadcast_to(b_row, (CHUNK, CHUNK)).T
    i_col = jnp.broadcast_to(i_row, (CHUNK, CHUNK)).T[:, :1]
    b_col = b_cols[:, :1]
    rowi = lax.broadcasted_iota(jnp.int32, (CHUNK, CHUNK), 0)
    coli = lax.broadcasted_iota(jnp.int32, (CHUNK, CHUNK), 1)
    log_d = jnp.where(rowi >= coli, b_cols - b_row + i_row, -jnp.inf)
    m_prev = m_ref[h][:, :1]
    inter_log = b_col + m_prev
    m_t = jnp.maximum(inter_log, jnp.max(log_d, axis=-1, keepdims=True))
    d_mat = jnp.exp(log_d - m_t)
    w_inter = jnp.exp(inter_log - m_t)

    qb = q.astype(BF16)
    s = lax.dot_general(qb, k.astype(BF16), (((1,), (1,)), ((), ())),
                        preferred_element_type=F32) * d_mat
    c_state = c_ref[h]
    n_state = n_ref[h]
    num = (jnp.dot(s.astype(BF16), v, preferred_element_type=F32)
           + w_inter * jnp.dot(qb, c_state.astype(BF16), preferred_element_type=F32))
    den = (jnp.sum(s, axis=-1, keepdims=True)
           + w_inter * jnp.sum(q * n_state, axis=-1, keepdims=True))
    hid = num / jnp.maximum(jnp.abs(den), jnp.exp(-m_t))

    b_last = b_col[CHUNK - 1:CHUNK]
    log_w = b_last - b_col + i_col
    m_new = jnp.maximum(b_last + m_prev, jnp.max(log_w, axis=0, keepdims=True))
    w = jnp.exp(log_w - m_new)
    decay = jnp.exp(b_last + m_prev - m_new)
    kw = k * w
    c_ref[h] = decay * c_state + lax.dot_general(kw.astype(BF16), v, (((0,), (0,)), ((), ())),
                                                 preferred_element_type=F32)
    n_ref[h] = decay * n_state + jnp.sum(kw, axis=0, keepdims=True)
    m_ref[h] = jnp.broadcast_to(m_new, (1, LANES))

    y = _rms_rows(hid, hg_ref[:, RET_HEADS * HEAD_W + h * HEAD_W:RET_HEADS * HEAD_W + (h + 1) * HEAD_W])
    out_cols = slice(RET_HEADS * HEAD_W + h * HEAD_W, RET_HEADS * HEAD_W + (h + 1) * HEAD_W)
    o_ref[0, :, out_cols] = (y * jax.nn.sigmoid(og_ref[0, :, cols].astype(F32))).astype(o_ref.dtype)


def _mixer_body(gl_ref, rq_ref, rk_ref, rv_ref, rg_ref, mq_ref, mqt_ref, mk_ref, mkt_ref, mv_ref, mo_ref,
                cos_ref, sin_ref, dm_ref, wq_ref, wk_ref, gr_ref, gb_ref, cw_ref, hg_ref,
                o_ref, r_ref, c_ref, n_ref, m_ref):
    first = pl.program_id(1) == 0

    @pl.when(first)
    def _():
        r_ref[...] = jnp.zeros_like(r_ref)
        c_ref[...] = jnp.zeros_like(c_ref)
        n_ref[...] = jnp.zeros_like(n_ref)
        m_ref[...] = jnp.zeros_like(m_ref)

    cos = cos_ref[...]
    sin = sin_ref[...]
    gates = gr_ref[0] + gb_ref[...]
    for h in range(RET_HEADS):
        _retention_head(h, rq_ref, rk_ref, rv_ref, rg_ref, cos, sin, dm_ref, wq_ref, wk_ref, gl_ref,
                        hg_ref, o_ref, r_ref)
    for h in range(ML_HEADS):
        _mlstm_head(h, first, mq_ref, mqt_ref, mk_ref, mkt_ref, mv_ref, mo_ref, gates, cw_ref, hg_ref,
                    o_ref, c_ref, n_ref, m_ref)


def _mixer(proj, gates_row, gate_b, conv_w, head_g):
    b, t, _ = proj.shape
    nc = t // CHUNK
    ret_w = RET_HEADS * HEAD_W
    ml_w = ML_HEADS * HEAD_W
    assert ret_w == ml_w
    log_g = jnp.log1p(-jnp.exp2(-5.0 - jnp.arange(RET_HEADS, dtype=F32)))
    idx = jnp.arange(CHUNK, dtype=F32)
    diff = idx[:, None] - idx[None, :]
    dmask = jnp.where(diff >= 0, jnp.exp(log_g[:, None, None] * jnp.maximum(diff, 0.0)), 0.0)
    w_k = jnp.exp(log_g[:, None] * (CHUNK - 1.0 - idx)[None, :])[..., None]
    w_q = jnp.exp(log_g[:, None] * (idx + 1.0)[None, :])[..., None]
    g_l = jnp.exp(log_g * CHUNK)
    half = HEAD_W // 2
    inv = 1.0 / (ROPE_BASE ** jnp.linspace(0.0, 1.0, half, dtype=F32))
    ang = jnp.arange(t).astype(F32)[:, None] * inv[None, :]
    cos, sin = jnp.cos(ang), jnp.sin(ang)
    tail_rows = 2 * SUBLANES
    per_chunk = CHUNK // tail_rows

    def group(gi):
        return pl.BlockSpec((1, CHUNK, ret_w), lambda bi, c: (bi, c, gi))

    def tail(gi):
        return pl.BlockSpec((1, tail_rows, ml_w), lambda bi, c: (bi, jnp.maximum(c * per_chunk - 1, 0), gi))

    def whole(a):
        return pl.BlockSpec(a.shape, lambda bi, c: (0,) * a.ndim)

    gb = gate_b.reshape(2 * ML_HEADS, 1)
    hg = head_g.reshape(1, ret_w + ml_w)
    return pl.pallas_call(
        _mixer_body,
        grid=(b, nc),
        in_specs=[
            pl.BlockSpec(memory_space=pltpu.SMEM),
            group(0), group(1), group(2), group(3),
            group(4), tail(4), group(5), tail(5), group(6), group(7),
            pl.BlockSpec((CHUNK, half), lambda bi, c: (c, 0)),
            pl.BlockSpec((CHUNK, half), lambda bi, c: (c, 0)),
            whole(dmask), whole(w_q), whole(w_k),
            pl.BlockSpec((1, 2 * ML_HEADS, CHUNK), lambda bi, c: (bi, 0, c)),
            whole(gb), whole(conv_w), whole(hg),
        ],
        out_specs=pl.BlockSpec((1, CHUNK, ret_w + ml_w), lambda bi, c: (bi, c, 0)),
        out_shape=jax.ShapeDtypeStruct((b, t, ret_w + ml_w), BF16),
        scratch_shapes=[pltpu.VMEM((RET_HEADS, HEAD_W, HEAD_W), F32),
                        pltpu.VMEM((ML_HEADS, HEAD_W, HEAD_W), F32),
                        pltpu.VMEM((ML_HEADS, 1, HEAD_W), F32),
                        pltpu.VMEM((ML_HEADS, 1, LANES), F32)],
        compiler_params=_params(),
        name="mixer",
    )(g_l, proj, proj, proj, proj, proj, proj, proj, proj, proj, proj, cos, sin, dmask, w_q, w_k,
      gates_row, gb, conv_w, hg)


def _swa_body(rb_ref, sink_ref, idx_ref, q_ref, kp_ref, kc_ref, vp_ref, vc_ref, o_ref, bias_ref):
    bi = pl.program_id(0)
    nb = pl.program_id(1)
    blk = ATT_BLOCK
    group = ATT_HEADS // KV_HEADS

    @pl.when((bi == 0) & (nb == 0))
    def _():
        idx = idx_ref[...]

        def head_body(hd, carry):
            def bucket_body(bk, acc):
                return jnp.where(idx == bk, rb_ref[bk * ATT_HEADS + hd], acc)

            bias_ref[hd] = lax.fori_loop(0, N_BUCKETS, bucket_body,
                                         jnp.full(idx.shape, -jnp.inf, F32))
            return carry

        lax.fori_loop(0, ATT_HEADS, head_body, 0)

    first_key = jnp.where(nb > 0, 0, blk)
    key_ok = lax.broadcasted_iota(jnp.int32, (blk, 2 * blk), 1) >= first_key
    low = lax.broadcasted_iota(jnp.int32, (2 * blk, LANES), 1) < HEAD_DIM
    kf = jnp.concatenate([kp_ref[0], kc_ref[0]], axis=0).astype(F32)
    vf = jnp.concatenate([vp_ref[0], vc_ref[0]], axis=0).astype(F32)
    scale = HEAD_DIM ** -0.5

    def halves(xf, kv):
        xc = xf[:, (kv // 2) * LANES:(kv // 2 + 1) * LANES]
        xr = pltpu.roll(xc, HEAD_DIM, axis=1)
        lo_src, hi_src = (xc, xr) if kv % 2 == 0 else (xr, xc)
        return (jnp.where(low, lo_src, 0.0).astype(BF16), jnp.where(low, 0.0, hi_src).astype(BF16))

    def probs(q2, k_half, hd):
        s = lax.dot_general(q2, k_half, (((1,), (1,)), ((), ())), preferred_element_type=F32)
        s = jnp.where(key_ok, s + bias_ref[hd], -jnp.inf)
        sink = sink_ref[hd]
        mx = jnp.maximum(jnp.max(s, axis=-1, keepdims=True), sink)
        p = jnp.exp(s - mx)
        den = jnp.sum(p, axis=-1, keepdims=True) + jnp.exp(sink - mx)
        return p.astype(BF16), den

    for kv in range(KV_HEADS):
        k_lo, k_hi = halves(kf, kv)
        v_lo, v_hi = halves(vf, kv)
        for pr in range(group // 2):
            hd = kv * group + 2 * pr
            cols = slice(hd * HEAD_DIM, (hd + 2) * HEAD_DIM)
            q2 = q_ref[0, :, cols] * scale
            p_a, den_a = probs(q2, k_lo, hd)
            p_b, den_b = probs(q2, k_hi, hd + 1)
            o2 = (jnp.dot(p_a, v_lo, preferred_element_type=F32) / den_a
                  + jnp.dot(p_b, v_hi, preferred_element_type=F32) / den_b)
            o_ref[0, :, cols] = o2.astype(o_ref.dtype)


def _t5_bucket(dist):
    n = jnp.maximum(dist, 0)
    max_exact = N_BUCKETS // 2
    nf = jnp.maximum(n, 1).astype(F32)
    large = max_exact + (jnp.log(nf / max_exact) / math.log(MAX_DIST / max_exact)
                         * (N_BUCKETS - max_exact)).astype(jnp.int32)
    large = jnp.minimum(large, N_BUCKETS - 1)
    return jnp.where(n < max_exact, n, large)


def _swa(proj, sinks, rel_bias):
    b, t, _ = proj.shape
    blk = ATT_BLOCK
    nblk = t // blk
    qw = ATT_HEADS * HEAD_DIM
    kvw = KV_HEADS * HEAD_DIM
    i = jnp.arange(blk)
    j = jnp.arange(2 * blk)
    dist = (blk + i)[:, None] - j[None, :]
    idx = jnp.where((dist >= 0) & (dist < WINDOW), _t5_bucket(dist), -1).astype(jnp.int32)
    k_blk = qw // kvw
    return pl.pallas_call(
        _swa_body,
        grid=(b, nblk),
        in_specs=[
            pl.BlockSpec(memory_space=pltpu.SMEM),
            pl.BlockSpec(memory_space=pltpu.SMEM),
            pl.BlockSpec((blk, 2 * blk), lambda bi, nb: (0, 0)),
            pl.BlockSpec((1, blk, qw), lambda bi, nb: (bi, nb, 0)),
            pl.BlockSpec((1, blk, kvw), lambda bi, nb: (bi, jnp.maximum(nb - 1, 0), k_blk)),
            pl.BlockSpec((1, blk, kvw), lambda bi, nb: (bi, nb, k_blk)),
            pl.BlockSpec((1, blk, kvw), lambda bi, nb: (bi, jnp.maximum(nb - 1, 0), k_blk + 1)),
            pl.BlockSpec((1, blk, kvw), lambda bi, nb: (bi, nb, k_blk + 1)),
        ],
        out_specs=pl.BlockSpec((1, blk, qw), lambda bi, nb: (bi, nb, 0)),
        out_shape=jax.ShapeDtypeStruct((b, t, qw), BF16),
        scratch_shapes=[pltpu.VMEM((ATT_HEADS, blk, 2 * blk), F32)],
        compiler_params=_params(),
        name="swa",
    )(rel_bias.reshape(-1), sinks, idx, proj, proj, proj, proj, proj)


def kernel(x, rel_bias, norm_g, ffn_w_gu, ffn_w_down, rm_w_in, ml_conv_w, ml_gate_b, rm_head_g,
           rm_w_out, swa_w_in, swa_sinks, swa_w_out):
    b, t, d = x.shape
    n = b * t
    depth = norm_g.shape[0]
    main_cols = 4 * RET_HEADS * HEAD_W + 4 * ML_HEADS * HEAD_W
    xs = x.reshape(n, d)
    for layer in range(depth):
        g = norm_g[layer]
        if layer % 2 == 0:
            e = layer // 2
            w_in = rm_w_in[e]
            gate_cols = w_in.shape[1] - main_cols
            w_gate = jnp.pad(w_in[:, main_cols:], ((0, 0), (0, LANES - gate_cols))).astype(BF16)
            proj, gates = _norm_matmul(xs, g[0], w_in.astype(BF16), main_cols, 1024, 512, w_side=w_gate)
            gates_row = gates[:, :gate_cols].reshape(b, t, gate_cols).transpose(0, 2, 1)
            mixed = _mixer(proj.reshape(b, t, main_cols), gates_row, ml_gate_b[e], ml_conv_w[e], rm_head_g[e])
            xs = _proj_norm_res(mixed.reshape(n, -1), rm_w_out[e].astype(BF16), xs, g[1], 512)
        else:
            o = layer // 2
            w_in = swa_w_in[o].astype(BF16)
            proj, = _norm_matmul(xs, g[0], w_in, w_in.shape[1], 1024, 512)
            att = _swa(proj.reshape(b, t, -1), swa_sinks[o], rel_bias)
            xs = _proj_norm_res(att.reshape(n, -1), swa_w_out[o].astype(BF16), xs, g[1], 512)
        xs = _ffn(xs, g[2], ffn_w_gu[layer].astype(BF16), ffn_w_down[layer].astype(BF16), g[3], 512, 512)
    return xs.reshape(b, t, d)
```

```python
import functools
import math

import jax
import jax.numpy as jnp
from jax import lax
from jax.experimental import pallas as pl
from jax.experimental.pallas import tpu as pltpu

F32 = jnp.float32
BF16 = jnp.bfloat16

EPS = 1e-6
CHUNK = 128
RET_HEADS = 4
ML_HEADS = 4
HEAD_W = 256
CONV_W = 4
ROPE_BASE = 10000.0
ATT_HEADS = 32
KV_HEADS = 4
HEAD_DIM = 64
WINDOW = 128
ATT_BLOCK = 128
N_BUCKETS = 32
MAX_DIST = 128
LANES = 128
SUBLANES = 8
VMEM_LIMIT = 56 * 1024 * 1024


def _params(vmem=VMEM_LIMIT):
    return pltpu.CompilerParams(vmem_limit_bytes=vmem)


def _rms_rows(x, g):
    ms = jnp.mean(x * x, axis=-1, keepdims=True)
    return x * lax.rsqrt(ms + EPS) * g


def _silu(x):
    return x * jax.nn.sigmoid(x)


def _norm_rows_into(x_ref, g_ref, h_ref, row_chunk):
    g = g_ref[...]

    def body(r, carry):
        rows = pl.ds(pl.multiple_of(r * row_chunk, row_chunk), row_chunk)
        h_ref[rows, :] = _rms_rows(x_ref[rows, :], g).astype(h_ref.dtype)
        return carry

    lax.fori_loop(0, x_ref.shape[0] // row_chunk, body, 0)


def _norm_matmul_body(*refs, with_side):
    if with_side:
        x_ref, g_ref, w_ref, ws_ref, o_ref, os_ref, h_ref = refs
    else:
        x_ref, g_ref, w_ref, o_ref, h_ref = refs

    @pl.when(pl.program_id(1) == 0)
    def _():
        _norm_rows_into(x_ref, g_ref, h_ref, 64)
        if with_side:
            os_ref[...] = jnp.dot(h_ref[...], ws_ref[...], preferred_element_type=F32)

    o_ref[...] = jnp.dot(h_ref[...], w_ref[...], preferred_element_type=F32).astype(o_ref.dtype)


def _norm_matmul(x, g, w, cols, tm, tn, w_side=None):
    n, d = x.shape
    in_specs = [
        pl.BlockSpec((tm, d), lambda i, j: (i, 0)),
        pl.BlockSpec((1, d), lambda i, j: (0, 0)),
        pl.BlockSpec((d, tn), lambda i, j: (0, j)),
    ]
    out_specs = [pl.BlockSpec((tm, tn), lambda i, j: (i, j))]
    out_shape = [jax.ShapeDtypeStruct((n, cols), BF16)]
    args = [x, g.reshape(1, d), w]
    if w_side is not None:
        in_specs.append(pl.BlockSpec(w_side.shape, lambda i, j: (0, 0)))
        out_specs.append(pl.BlockSpec((tm, w_side.shape[1]), lambda i, j: (i, 0)))
        out_shape.append(jax.ShapeDtypeStruct((n, w_side.shape[1]), F32))
        args.append(w_side)
    return pl.pallas_call(
        functools.partial(_norm_matmul_body, with_side=w_side is not None),
        grid=(n // tm, cols // tn),
        in_specs=in_specs,
        out_specs=out_specs,
        out_shape=out_shape,
        scratch_shapes=[pltpu.VMEM((tm, d), BF16)],
        compiler_params=_params(),
        name="norm_matmul",
    )(*args)


def _proj_norm_res_body(a_ref, w_ref, x_ref, g_ref, o_ref):
    y = jnp.dot(a_ref[...], w_ref[...], preferred_element_type=F32)
    o_ref[...] = x_ref[...] + _rms_rows(y, g_ref[...])


def _proj_norm_res(a, w, x, g, tm):
    n, d = x.shape
    return pl.pallas_call(
        _proj_norm_res_body,
        grid=(n // tm,),
        in_specs=[
            pl.BlockSpec((tm, a.shape[1]), lambda i: (i, 0)),
            pl.BlockSpec(w.shape, lambda i: (0, 0)),
            pl.BlockSpec((tm, d), lambda i: (i, 0)),
            pl.BlockSpec((1, d), lambda i: (0, 0)),
        ],
        out_specs=pl.BlockSpec((tm, d), lambda i: (i, 0)),
        out_shape=jax.ShapeDtypeStruct((n, d), F32),
        compiler_params=_params(),
        name="proj_norm_res",
    )(a, w, x, g.reshape(1, d))


def _ffn_body(x_ref, g_in_ref, wg_ref, wu_ref, wd_ref, g_out_ref, o_ref, h_ref, acc_ref):
    f = pl.program_id(1)

    @pl.when(f == 0)
    def _():
        _norm_rows_into(x_ref, g_in_ref, h_ref, 64)
        acc_ref[...] = jnp.zeros_like(acc_ref)

    h = h_ref[...]
    gate = jnp.dot(h, wg_ref[...], preferred_element_type=F32)
    up = jnp.dot(h, wu_ref[...], preferred_element_type=F32)
    act = (_silu(gate) * up).astype(BF16)
    acc_ref[...] += jnp.dot(act, wd_ref[...], preferred_element_type=F32)

    @pl.when(f == pl.num_programs(1) - 1)
    def _():
        g_out = g_out_ref[...]
        row_chunk = 64

        def body(r, carry):
            rows = pl.ds(pl.multiple_of(r * row_chunk, row_chunk), row_chunk)
            o_ref[rows, :] = x_ref[rows, :] + _rms_rows(acc_ref[rows, :], g_out)
            return carry

        lax.fori_loop(0, x_ref.shape[0] // row_chunk, body, 0)


def _ffn(x, g_in, w_gu, w_down, g_out, layer, tm, tf):
    n, d = x.shape
    d_ff = w_down.shape[1]
    nf = d_ff // tf
    return pl.pallas_call(
        _ffn_body,
        grid=(n // tm, nf),
        in_specs=[
            pl.BlockSpec((tm, d), lambda i, f: (i, 0)),
            pl.BlockSpec((1, d), lambda i, f: (0, 0)),
            pl.BlockSpec((None, d, tf), lambda i, f: (layer, 0, f)),
            pl.BlockSpec((None, d, tf), lambda i, f: (layer, 0, f + nf)),
            pl.BlockSpec((None, tf, d), lambda i, f: (layer, f, 0)),
            pl.BlockSpec((1, d), lambda i, f: (0, 0)),
        ],
        out_specs=pl.BlockSpec((tm, d), lambda i, f: (i, 0)),
        out_shape=jax.ShapeDtypeStruct((n, d), F32),
        scratch_shapes=[pltpu.VMEM((tm, d), BF16), pltpu.VMEM((tm, d), F32)],
        compiler_params=_params(),
        name="ffn",
    )(x, g_in.reshape(1, d), w_gu, w_gu, w_down, g_out.reshape(1, d))


def _log_sigmoid(x):
    return jnp.minimum(x, 0.0) - jnp.log1p(jnp.exp(-jnp.abs(x)))


def _cumsum_lanes(x):
    n = x.shape[1]
    pos = lax.broadcasted_iota(jnp.int32, x.shape, 1)
    d = 1
    while d < n:
        x = x + jnp.where(pos >= d, pltpu.roll(x, d, axis=1), 0.0)
        d *= 2
    return x


def _retention_head(h, q_ref, k_ref, v_ref, g_ref, cos, sin, dm_ref, wq_ref, wk_ref, gl_ref, hg_ref,
                    o_ref, state_ref):
    cols = slice(h * HEAD_W, (h + 1) * HEAD_W)
    half = HEAD_W // 2

    def rot(x):
        x = x.astype(F32)
        x1, x2 = x[:, :half], x[:, half:]
        return jnp.concatenate([x1 * cos - x2 * sin, x2 * cos + x1 * sin], axis=-1)

    q = rot(q_ref[0, :, cols])
    k = rot(k_ref[0, :, cols]) * (HEAD_W ** -0.5)
    v = v_ref[0, :, cols]
    qb = q.astype(BF16)
    s = lax.dot_general(qb, k.astype(BF16), (((1,), (1,)), ((), ())),
                        preferred_element_type=F32) * dm_ref[h]
    intra = jnp.dot(s.astype(BF16), v, preferred_element_type=F32)
    state = state_ref[h]
    inter = jnp.dot(qb, state.astype(BF16), preferred_element_type=F32) * wq_ref[h]
    out = intra + inter
    kw = (k * wk_ref[h]).astype(BF16)
    kv = lax.dot_general(kw, v, (((0,), (0,)), ((), ())), preferred_element_type=F32)
    state_ref[h] = state * gl_ref[h] + kv
    y = _rms_rows(out, hg_ref[:, cols])
    o_ref[0, :, cols] = (y * _silu(g_ref[0, :, cols].astype(F32))).astype(o_ref.dtype)


def _mlstm_head(h, first, q_ref, qt_ref, k_ref, kt_ref, v_ref, og_ref, gates, cw_ref, hg_ref, o_ref,
                c_ref, n_ref, m_ref):
    cols = slice(h * HEAD_W, (h + 1) * HEAD_W)
    ml_w = ML_HEADS * HEAD_W
    row8 = lax.broadcasted_iota(jnp.int32, (SUBLANES, HEAD_W), 0)

    def conv_silu(u_ref, t_ref, w):
        u = u_ref[0, :, cols].astype(F32)
        tail = t_ref[0, :, cols].astype(F32)[SUBLANES:]
        tail = jnp.where(first, 0.0, tail)
        y = u * w[CONV_W - 1:CONV_W]
        for s in range(1, CONV_W):
            ru = pltpu.roll(u, s, axis=0)
            rt = pltpu.roll(tail, s, axis=0)
            top = jnp.where(row8 < s, rt, ru[:SUBLANES])
            shifted = jnp.concatenate([top, ru[SUBLANES:]], axis=0)
            y = y + shifted * w[CONV_W - 1 - s:CONV_W - s]
        return _silu(y)

    q = conv_silu(q_ref, qt_ref, cw_ref[:, cols])
    k = conv_silu(k_ref, kt_ref, cw_ref[:, ml_w + h * HEAD_W:ml_w + (h + 1) * HEAD_W]) * (HEAD_W ** -0.5)
    v = v_ref[0, :, cols]

    i_row = gates[h:h + 1]
    f_row = gates[ML_HEADS + h:ML_HEADS + h + 1]
    b_row = _cumsum_lanes(jnp.broadcast_to(_log_sigmoid(f_row), (SUBLANES, CHUNK)))[:1]
    b_cols = jnp.broadcast_to(b_row, (CHUNK, CHUNK)).T
    i_col = jnp.broadcast_to(i_row, (CHUNK, CHUNK)).T[:, :1]
    b_col = b_cols[:, :1]
    rowi = lax.broadcasted_iota(jnp.int32, (CHUNK, CHUNK), 0)
    coli = lax.broadcasted_iota(jnp.int32, (CHUNK, CHUNK), 1)
    log_d = jnp.where(rowi >= coli, b_cols - b_row + i_row, -jnp.inf)
    m_prev = m_ref[h][:, :1]
    inter_log = b_col + m_prev
    m_t = jnp.maximum(inter_log, jnp.max(log_d, axis=-1, keepdims=True))
    d_mat = jnp.exp(log_d - m_t)
    w_inter = jnp.exp(inter_log - m_t)

    qb = q.astype(BF16)
    s = lax.dot_general(qb, k.astype(BF16), (((1,), (1,)), ((), ())),
                        preferred_element_type=F32) * d_mat
    c_state = c_ref[h]
    n_state = n_ref[h]
    num = (jnp.dot(s.astype(BF16), v, preferred_element_type=F32)
           + w_inter * jnp.dot(qb, c_state.astype(BF16), preferred_element_type=F32))
    den = (jnp.sum(s, axis=-1, keepdims=True)
           + w_inter * jnp.sum(q * n_state, axis=-1, keepdims=True))
    hid = num / jnp.maximum(jnp.abs(den), jnp.exp(-m_t))

    b_last = b_col[CHUNK - 1:CHUNK]
    log_w = b_last - b_col + i_col
    m_new = jnp.maximum(b_last + m_prev, jnp.max(log_w, axis=0, keepdims=True))
    w = jnp.exp(log_w - m_new)
    decay = jnp.exp(b_last + m_prev - m_new)
    kw = k * w
    c_ref[h] = decay * c_state + lax.dot_general(kw.astype(BF16), v, (((0,), (0,)), ((), ())),
                                                 preferred_element_type=F32)
    n_ref[h] = decay * n_state + jnp.sum(kw, axis=0, keepdims=True)
    m_ref[h] = jnp.broadcast_to(m_new, (1, LANES))

    y = _rms_rows(hid, hg_ref[:, RET_HEADS * HEAD_W + h * HEAD_W:RET_HEADS * HEAD_W + (h + 1) * HEAD_W])
    out_cols = slice(RET_HEADS * HEAD_W + h * HEAD_W, RET_HEADS * HEAD_W + (h + 1) * HEAD_W)
    o_ref[0, :, out_cols] = (y * jax.nn.sigmoid(og_ref[0, :, cols].astype(F32))).astype(o_ref.dtype)


def _mixer_body(gl_ref, rq_ref, rk_ref, rv_ref, rg_ref, mq_ref, mqt_ref, mk_ref, mkt_ref, mv_ref, mo_ref,
                cos_ref, sin_ref, dm_ref, wq_ref, wk_ref, gr_ref, gb_ref, cw_ref, hg_ref,
                o_ref, r_ref, c_ref, n_ref, m_ref):
    first = pl.program_id(1) == 0

    @pl.when(first)
    def _():
        r_ref[...] = jnp.zeros_like(r_ref)
        c_ref[...] = jnp.zeros_like(c_ref)
        n_ref[...] = jnp.zeros_like(n_ref)
        m_ref[...] = jnp.zeros_like(m_ref)

    cos = cos_ref[...]
    sin = sin_ref[...]
    gates = gr_ref[0] + gb_ref[...]
    for h in range(RET_HEADS):
        _retention_head(h, rq_ref, rk_ref, rv_ref, rg_ref, cos, sin, dm_ref, wq_ref, wk_ref, gl_ref,
                        hg_ref, o_ref, r_ref)
    for h in range(ML_HEADS):
        _mlstm_head(h, first, mq_ref, mqt_ref, mk_ref, mkt_ref, mv_ref, mo_ref, gates, cw_ref, hg_ref,
                    o_ref, c_ref, n_ref, m_ref)


def _mixer(proj, gates_row, gate_b, conv_w, head_g):
    b, t, _ = proj.shape
    nc = t // CHUNK
    ret_w = RET_HEADS * HEAD_W
    ml_w = ML_HEADS * HEAD_W
    assert ret_w == ml_w
    log_g = jnp.log1p(-jnp.exp2(-5.0 - jnp.arange(RET_HEADS, dtype=F32)))
    idx = jnp.arange(CHUNK, dtype=F32)
    diff = idx[:, None] - idx[None, :]
    dmask = jnp.where(diff >= 0, jnp.exp(log_g[:, None, None] * jnp.maximum(diff, 0.0)), 0.0)
    w_k = jnp.exp(log_g[:, None] * (CHUNK - 1.0 - idx)[None, :])[..., None]
    w_q = jnp.exp(log_g[:, None] * (idx + 1.0)[None, :])[..., None]
    g_l = jnp.exp(log_g * CHUNK)
    half = HEAD_W // 2
    inv = 1.0 / (ROPE_BASE ** jnp.linspace(0.0, 1.0, half, dtype=F32))
    ang = jnp.arange(t).astype(F32)[:, None] * inv[None, :]
    cos, sin = jnp.cos(ang), jnp.sin(ang)
    tail_rows = 2 * SUBLANES
    per_chunk = CHUNK // tail_rows

    def group(gi):
        return pl.BlockSpec((1, CHUNK, ret_w), lambda bi, c: (bi, c, gi))

    def tail(gi):
        return pl.BlockSpec((1, tail_rows, ml_w), lambda bi, c: (bi, jnp.maximum(c * per_chunk - 1, 0), gi))

    def whole(a):
        return pl.BlockSpec(a.shape, lambda bi, c: (0,) * a.ndim)

    gb = gate_b.reshape(2 * ML_HEADS, 1)
    hg = head_g.reshape(1, ret_w + ml_w)
    return pl.pallas_call(
        _mixer_body,
        grid=(b, nc),
        in_specs=[
            pl.BlockSpec(memory_space=pltpu.SMEM),
            group(0), group(1), group(2), group(3),
            group(4), tail(4), group(5), tail(5), group(6), group(7),
            pl.BlockSpec((CHUNK, half), lambda bi, c: (c, 0)),
            pl.BlockSpec((CHUNK, half), lambda bi, c: (c, 0)),
            whole(dmask), whole(w_q), whole(w_k),
            pl.BlockSpec((1, 2 * ML_HEADS, CHUNK), lambda bi, c: (bi, 0, c)),
            whole(gb), whole(conv_w), whole(hg),
        ],
        out_specs=pl.BlockSpec((1, CHUNK, ret_w + ml_w), lambda bi, c: (bi, c, 0)),
        out_shape=jax.ShapeDtypeStruct((b, t, ret_w + ml_w), BF16),
        scratch_shapes=[pltpu.VMEM((RET_HEADS, HEAD_W, HEAD_W), F32),
                        pltpu.VMEM((ML_HEADS, HEAD_W, HEAD_W), F32),
                        pltpu.VMEM((ML_HEADS, 1, HEAD_W), F32),
                        pltpu.VMEM((ML_HEADS, 1, LANES), F32)],
        compiler_params=_params(),
        name="mixer",
    )(g_l, proj, proj, proj, proj, proj, proj, proj, proj, proj, proj, cos, sin, dmask, w_q, w_k,
      gates_row, gb, conv_w, hg)


PAIR_W = 2 * HEAD_DIM
GROUP = ATT_HEADS // KV_HEADS
PAIRS = GROUP // 2


def _swa_in_proj_body(x_ref, g_ref, w_ref, q_ref, kv_ref, h_ref, *, q_tiles):
    j = pl.program_id(1)

    @pl.when(j == 0)
    def _():
        _norm_rows_into(x_ref, g_ref, h_ref, 64)

    res = jnp.dot(h_ref[...], w_ref[...], preferred_element_type=F32)

    @pl.when(j < q_tiles)
    def _():
        for p in range(q_ref.shape[1]):
            q_ref[0, p] = res[:, p * PAIR_W:(p + 1) * PAIR_W].astype(q_ref.dtype)

    @pl.when(j == q_tiles)
    def _():
        kv_ref[...] = res.astype(kv_ref.dtype)


def _swa_in_proj(x, g, w, b, t, tm):
    n, d = x.shape
    qw = ATT_HEADS * HEAD_DIM
    tn = 2 * KV_HEADS * HEAD_DIM
    q_tiles = qw // tn
    tiles_per_seq = t // tm
    pairs_per_tile = tn // PAIR_W
    return pl.pallas_call(
        functools.partial(_swa_in_proj_body, q_tiles=q_tiles),
        grid=(n // tm, q_tiles + 1),
        in_specs=[
            pl.BlockSpec((tm, d), lambda i, j: (i, 0)),
            pl.BlockSpec((1, d), lambda i, j: (0, 0)),
            pl.BlockSpec((d, tn), lambda i, j: (0, j)),
        ],
        out_specs=[
            pl.BlockSpec((1, pairs_per_tile, tm, PAIR_W),
                         lambda i, j: (i // tiles_per_seq, jnp.minimum(j, q_tiles - 1), i % tiles_per_seq, 0)),
            pl.BlockSpec((tm, tn), lambda i, j: (i, 0)),
        ],
        out_shape=[jax.ShapeDtypeStruct((b, ATT_HEADS // 2, t, PAIR_W), BF16),
                   jax.ShapeDtypeStruct((n, tn), BF16)],
        scratch_shapes=[pltpu.VMEM((tm, d), BF16)],
        compiler_params=_params(),
        name="swa_in_proj",
    )(x, g.reshape(1, d), w)


def _swa_body(rb_ref, sink_ref, idx_ref, q_ref, kp_ref, kc_ref, vp_ref, vc_ref, o_ref, bias_ref, sinkcol_ref):
    bi = pl.program_id(0)
    nb = pl.program_id(1)
    blk = ATT_BLOCK
    rows = PAIRS * blk

    @pl.when((bi == 0) & (nb == 0))
    def _():
        idx = idx_ref[...]
        col = lax.broadcasted_iota(jnp.int32, idx.shape, 1)

        def head_body(hd, carry):
            def bucket_body(bk, acc):
                return jnp.where(idx == bk, rb_ref[bk * ATT_HEADS + hd], acc)

            tbl = lax.fori_loop(0, N_BUCKETS, bucket_body, jnp.full(idx.shape, -jnp.inf, F32))
            kv = hd // GROUP
            half = hd % 2
            at = pl.ds(pl.multiple_of(((hd % GROUP) // 2) * blk, blk), blk)
            bias_ref[0, kv, half, at, :] = tbl
            bias_ref[1, kv, half, at, :] = jnp.where(col >= blk, tbl, -jnp.inf)
            sinkcol_ref[kv, half, at, :] = jnp.full((blk, LANES), sink_ref[hd], F32)
            return carry

        lax.fori_loop(0, ATT_HEADS, head_body, 0)

    first = (nb == 0).astype(jnp.int32)
    low = lax.broadcasted_iota(jnp.int32, (2 * blk, LANES), 1) < HEAD_DIM
    low_out = lax.broadcasted_iota(jnp.int32, (rows, LANES), 1) < HEAD_DIM
    kf = jnp.concatenate([kp_ref[0], kc_ref[0]], axis=0).astype(F32)
    vf = jnp.concatenate([vp_ref[0], vc_ref[0]], axis=0).astype(F32)
    scale = HEAD_DIM ** -0.5

    def halves(xf, kv):
        xc = xf[:, (kv // 2) * LANES:(kv // 2 + 1) * LANES]
        xr = pltpu.roll(xc, HEAD_DIM, axis=1)
        lo_src, hi_src = (xc, xr) if kv % 2 == 0 else (xr, xc)
        return (jnp.where(low, lo_src, 0.0).astype(BF16), jnp.where(low, 0.0, hi_src).astype(BF16))

    for kv in range(KV_HEADS):
        k_lo, k_hi = halves(kf, kv)
        v_lo, v_hi = halves(vf, kv)
        q4 = q_ref[0, kv * PAIRS:(kv + 1) * PAIRS].reshape(rows, PAIR_W) * scale

        def probs(k_half, half):
            s = lax.dot_general(q4, k_half, (((1,), (1,)), ((), ())), preferred_element_type=F32)
            s = s + bias_ref[first, kv, half]
            sink = sinkcol_ref[kv, half]
            mx = jnp.maximum(jnp.max(s, axis=-1, keepdims=True), sink)
            p = jnp.exp(s - jnp.concatenate([mx, mx], axis=1))
            den = jnp.sum(p, axis=-1, keepdims=True) + jnp.exp(sink - mx)
            return p.astype(BF16), den

        p_lo, den_lo = probs(k_lo, 0)
        p_hi, den_hi = probs(k_hi, 1)
        o4 = (jnp.dot(p_lo, v_lo, preferred_element_type=F32)
              + jnp.dot(p_hi, v_hi, preferred_element_type=F32)) / jnp.where(low_out, den_lo, den_hi)
        for pr in range(PAIRS):
            at = (kv * PAIRS + pr) * PAIR_W
            o_ref[0, :, at:at + PAIR_W] = o4[pr * blk:(pr + 1) * blk].astype(o_ref.dtype)


def _t5_bucket(dist):
    n = jnp.maximum(dist, 0)
    max_exact = N_BUCKETS // 2
    nf = jnp.maximum(n, 1).astype(F32)
    large = max_exact + (jnp.log(nf / max_exact) / math.log(MAX_DIST / max_exact)
                         * (N_BUCKETS - max_exact)).astype(jnp.int32)
    large = jnp.minimum(large, N_BUCKETS - 1)
    return jnp.where(n < max_exact, n, large)


def _swa(q, kv, sinks, rel_bias):
    b, n_pairs, t, _ = q.shape
    blk = ATT_BLOCK
    nblk = t // blk
    qw = ATT_HEADS * HEAD_DIM
    kvw = KV_HEADS * HEAD_DIM
    i = jnp.arange(blk)
    j = jnp.arange(2 * blk)
    dist = (blk + i)[:, None] - j[None, :]
    idx = jnp.where((dist >= 0) & (dist < WINDOW), _t5_bucket(dist), -1).astype(jnp.int32)
    return pl.pallas_call(
        _swa_body,
        grid=(b, nblk),
        in_specs=[
            pl.BlockSpec(memory_space=pltpu.SMEM),
            pl.BlockSpec(memory_space=pltpu.SMEM),
            pl.BlockSpec((blk, 2 * blk), lambda bi, nb: (0, 0)),
            pl.BlockSpec((1, n_pairs, blk, PAIR_W), lambda bi, nb: (bi, 0, nb, 0)),
            pl.BlockSpec((1, blk, kvw), lambda bi, nb: (bi, jnp.maximum(nb - 1, 0), 0)),
            pl.BlockSpec((1, blk, kvw), lambda bi, nb: (bi, nb, 0)),
            pl.BlockSpec((1, blk, kvw), lambda bi, nb: (bi, jnp.maximum(nb - 1, 0), 1)),
            pl.BlockSpec((1, blk, kvw), lambda bi, nb: (bi, nb, 1)),
        ],
        out_specs=pl.BlockSpec((1, blk, qw), lambda bi, nb: (bi, nb, 0)),
        out_shape=jax.ShapeDtypeStruct((b, t, qw), BF16),
        scratch_shapes=[pltpu.VMEM((2, KV_HEADS, 2, PAIRS * blk, 2 * blk), F32),
                        pltpu.VMEM((KV_HEADS, 2, PAIRS * blk, LANES), F32)],
        compiler_params=_params(),
        name="swa",
    )(rel_bias.reshape(-1), sinks, idx, q, kv, kv, kv, kv)


def kernel(x, rel_bias, norm_g, ffn_w_gu, ffn_w_down, rm_w_in, ml_conv_w, ml_gate_b, rm_head_g,
           rm_w_out, swa_w_in, swa_sinks, swa_w_out):
    b, t, d = x.shape
    n = b * t
    depth = norm_g.shape[0]
    main_cols = 4 * RET_HEADS * HEAD_W + 4 * ML_HEADS * HEAD_W
    xs = x.reshape(n, d)
    w_gu = ffn_w_gu.astype(BF16)
    w_down = ffn_w_down.astype(BF16)
    for layer in range(depth):
        g = norm_g[layer]
        if layer % 2 == 0:
            e = layer // 2
            w_in = rm_w_in[e]
            gate_cols = w_in.shape[1] - main_cols
            w_gate = jnp.pad(w_in[:, main_cols:], ((0, 0), (0, LANES - gate_cols))).astype(BF16)
            proj, gates = _norm_matmul(xs, g[0], w_in.astype(BF16), main_cols, 1024, 512, w_side=w_gate)
            gates_row = gates[:, :gate_cols].reshape(b, t, gate_cols).transpose(0, 2, 1)
            mixed = _mixer(proj.reshape(b, t, main_cols), gates_row, ml_gate_b[e], ml_conv_w[e], rm_head_g[e])
            xs = _proj_norm_res(mixed.reshape(n, -1), rm_w_out[e].astype(BF16), xs, g[1], 512)
        else:
            o = layer // 2
            q, kv = _swa_in_proj(xs, g[0], swa_w_in[o].astype(BF16), b, t, 1024)
            att = _swa(q, kv.reshape(b, t, -1), swa_sinks[o], rel_bias)
            xs = _proj_norm_res(att.reshape(n, -1), swa_w_out[o].astype(BF16), xs, g[1], 512)
        xs = _ffn(xs, g[2], w_gu, w_down, g[3], layer, 512, 512)
    return xs.reshape(b, t, d)
```

```python
import functools
import math

import jax
import jax.numpy as jnp
from jax import lax
from jax.experimental import pallas as pl
from jax.experimental.pallas import tpu as pltpu

F32 = jnp.float32
BF16 = jnp.bfloat16

EPS = 1e-6
CHUNK = 128
RET_HEADS = 4
ML_HEADS = 4
HEAD_W = 256
CONV_W = 4
ROPE_BASE = 10000.0
ATT_HEADS = 32
KV_HEADS = 4
HEAD_DIM = 64
WINDOW = 128
ATT_BLOCK = 128
N_BUCKETS = 32
MAX_DIST = 128
LANES = 128
SUBLANES = 8
VMEM_LIMIT = 56 * 1024 * 1024

ROW_TILE = 1024
OUT_ROW_TILE = 512
PROJ_TN = 512
FFN_TF = 256
FFN_ROW_BLOCK = 512


def _params(vmem=VMEM_LIMIT):
    return pltpu.CompilerParams(vmem_limit_bytes=vmem)


def _rms_rows(x, g):
    ms = jnp.mean(x * x, axis=-1, keepdims=True)
    return x * lax.rsqrt(ms + EPS) * g


def _silu(x):
    return x * jax.nn.sigmoid(x)


def _norm_rows_into(x_ref, g_ref, h_ref, row_chunk):
    g = g_ref[...]

    def body(r, carry):
        rows = pl.ds(pl.multiple_of(r * row_chunk, row_chunk), row_chunk)
        h_ref[rows, :] = _rms_rows(x_ref[rows, :], g).astype(h_ref.dtype)
        return carry

    lax.fori_loop(0, x_ref.shape[0] // row_chunk, body, 0)


def _col_tiles(w, tn):
    *lead, d, cols = w.shape
    nl = len(lead)
    perm = (*range(nl), nl + 1, nl, nl + 2)
    return w.astype(BF16).reshape(*lead, d, cols // tn, tn).transpose(perm)


def _norm_matmul_body(x_ref, g_ref, w_ref, ws_ref, o_ref, os_ref, h_ref):
    @pl.when(pl.program_id(1) == 0)
    def _():
        _norm_rows_into(x_ref, g_ref, h_ref, 64)
        os_ref[...] = jnp.dot(h_ref[...], ws_ref[...], preferred_element_type=F32)

    o_ref[...] = jnp.dot(h_ref[...], w_ref[...], preferred_element_type=F32).astype(o_ref.dtype)


def _norm_matmul(x, g, w_tiles, w_side, tm):
    n, d = x.shape
    tiles, _, tn = w_tiles.shape
    ns = w_side.shape[1]
    return pl.pallas_call(
        _norm_matmul_body,
        grid=(n // tm, tiles),
        in_specs=[
            pl.BlockSpec((tm, d), lambda i, j: (i, 0)),
            pl.BlockSpec((1, d), lambda i, j: (0, 0)),
            pl.BlockSpec((None, d, tn), lambda i, j: (j, 0, 0)),
            pl.BlockSpec((d, ns), lambda i, j: (0, 0)),
        ],
        out_specs=[pl.BlockSpec((tm, tn), lambda i, j: (i, j)),
                   pl.BlockSpec((tm, ns), lambda i, j: (i, 0))],
        out_shape=[jax.ShapeDtypeStruct((n, tiles * tn), BF16), jax.ShapeDtypeStruct((n, ns), F32)],
        scratch_shapes=[pltpu.VMEM((tm, d), BF16)],
        compiler_params=_params(),
        name="norm_matmul",
    )(x, g.reshape(1, d), w_tiles, w_side)


def _proj_norm_res_body(a_ref, w_ref, x_ref, g_ref, o_ref):
    y = jnp.dot(a_ref[...], w_ref[...], preferred_element_type=F32)
    o_ref[...] = x_ref[...] + _rms_rows(y, g_ref[...])


def _proj_norm_res(a, w, x, g, tm):
    n, d = x.shape
    return pl.pallas_call(
        _proj_norm_res_body,
        grid=(n // tm,),
        in_specs=[
            pl.BlockSpec((tm, a.shape[1]), lambda i: (i, 0)),
            pl.BlockSpec(w.shape, lambda i: (0, 0)),
            pl.BlockSpec((tm, d), lambda i: (i, 0)),
            pl.BlockSpec((1, d), lambda i: (0, 0)),
        ],
        out_specs=pl.BlockSpec((tm, d), lambda i: (i, 0)),
        out_shape=jax.ShapeDtypeStruct((n, d), F32),
        compiler_params=_params(),
        name="proj_norm_res",
    )(a, w, x, g.reshape(1, d))


def _ffn_body(x_ref, g_in_ref, wg_ref, wu_ref, wd_ref, g_out_ref, o_ref, h_ref, *, row_block):
    f = pl.program_id(1)

    @pl.when(f == 0)
    def _():
        _norm_rows_into(x_ref, g_in_ref, h_ref, 64)
        o_ref[...] = jnp.zeros_like(o_ref)

    for r in range(x_ref.shape[0] // row_block):
        rows = slice(r * row_block, (r + 1) * row_block)
        h = h_ref[rows, :]
        gate = jnp.dot(h, wg_ref[...], preferred_element_type=F32)
        up = jnp.dot(h, wu_ref[...], preferred_element_type=F32)
        act = (_silu(gate) * up).astype(BF16)
        o_ref[rows, :] += jnp.dot(act, wd_ref[...], preferred_element_type=F32)

    @pl.when(f == pl.num_programs(1) - 1)
    def _():
        g_out = g_out_ref[...]
        row_chunk = 64

        def body(r, carry):
            rows = pl.ds(pl.multiple_of(r * row_chunk, row_chunk), row_chunk)
            o_ref[rows, :] = x_ref[rows, :] + _rms_rows(o_ref[rows, :], g_out)
            return carry

        lax.fori_loop(0, x_ref.shape[0] // row_chunk, body, 0)


def _ffn(x, g_in, w_gu, w_down, g_out, layer, tm, row_block):
    n, d = x.shape
    nf = w_gu.shape[1] // 2
    tf = w_gu.shape[3]
    return pl.pallas_call(
        functools.partial(_ffn_body, row_block=row_block),
        grid=(n // tm, nf),
        in_specs=[
            pl.BlockSpec((tm, d), lambda i, f: (i, 0)),
            pl.BlockSpec((1, d), lambda i, f: (0, 0)),
            pl.BlockSpec((None, None, d, tf), lambda i, f: (layer, f, 0, 0)),
            pl.BlockSpec((None, None, d, tf), lambda i, f: (layer, f + nf, 0, 0)),
            pl.BlockSpec((None, tf, d), lambda i, f: (layer, f, 0)),
            pl.BlockSpec((1, d), lambda i, f: (0, 0)),
        ],
        out_specs=pl.BlockSpec((tm, d), lambda i, f: (i, 0)),
        out_shape=jax.ShapeDtypeStruct((n, d), F32),
        scratch_shapes=[pltpu.VMEM((tm, d), BF16)],
        compiler_params=_params(),
        name="ffn",
    )(x, g_in.reshape(1, d), w_gu, w_gu, w_down, g_out.reshape(1, d))


def _log_sigmoid(x):
    return jnp.minimum(x, 0.0) - jnp.log1p(jnp.exp(-jnp.abs(x)))


def _cumsum_lanes(x):
    n = x.shape[1]
    pos = lax.broadcasted_iota(jnp.int32, x.shape, 1)
    d = 1
    while d < n:
        x = x + jnp.where(pos >= d, pltpu.roll(x, d, axis=1), 0.0)
        d *= 2
    return x


def _retention_head(h, q_ref, k_ref, v_ref, g_ref, cos, sin, dm_ref, wq_ref, wk_ref, gl_ref, hg_ref,
                    o_ref, state_ref):
    cols = slice(h * HEAD_W, (h + 1) * HEAD_W)
    half = HEAD_W // 2

    def rot(x):
        x = x.astype(F32)
        x1, x2 = x[:, :half], x[:, half:]
        return jnp.concatenate([x1 * cos - x2 * sin, x2 * cos + x1 * sin], axis=-1)

    q = rot(q_ref[0, :, cols])
    k = rot(k_ref[0, :, cols]) * (HEAD_W ** -0.5)
    v = v_ref[0, :, cols]
    qb = q.astype(BF16)
    s = lax.dot_general(qb, k.astype(BF16), (((1,), (1,)), ((), ())),
                        preferred_element_type=F32) * dm_ref[h]
    intra = jnp.dot(s.astype(BF16), v, preferred_element_type=F32)
    state = state_ref[h]
    inter = jnp.dot(qb, state.astype(BF16), preferred_element_type=F32) * wq_ref[h]
    out = intra + inter
    kw = (k * wk_ref[h]).astype(BF16)
    kv = lax.dot_general(kw, v, (((0,), (0,)), ((), ())), preferred_element_type=F32)
    state_ref[h] = state * gl_ref[h] + kv
    y = _rms_rows(out, hg_ref[:, cols])
    o_ref[0, :, cols] = (y * _silu(g_ref[0, :, cols].astype(F32))).astype(o_ref.dtype)


def _mlstm_head(h, first, q_ref, qt_ref, k_ref, kt_ref, v_ref, og_ref, gates, cw_ref, hg_ref, o_ref,
                c_ref, n_ref, m_ref):
    cols = slice(h * HEAD_W, (h + 1) * HEAD_W)
    ml_w = ML_HEADS * HEAD_W
    row8 = lax.broadcasted_iota(jnp.int32, (SUBLANES, HEAD_W), 0)

    def conv_silu(u_ref, t_ref, w):
        u = u_ref[0, :, cols].astype(F32)
        tail = t_ref[0, :, cols].astype(F32)[SUBLANES:]
        tail = jnp.where(first, 0.0, tail)
        y = u * w[CONV_W - 1:CONV_W]
        for s in range(1, CONV_W):
            ru = pltpu.roll(u, s, axis=0)
            rt = pltpu.roll(tail, s, axis=0)
            top = jnp.where(row8 < s, rt, ru[:SUBLANES])
            shifted = jnp.concatenate([top, ru[SUBLANES:]], axis=0)
            y = y + shifted * w[CONV_W - 1 - s:CONV_W - s]
        return _silu(y)

    q = conv_silu(q_ref, qt_ref, cw_ref[:, cols])
    k = conv_silu(k_ref, kt_ref, cw_ref[:, ml_w + h * HEAD_W:ml_w + (h + 1) * HEAD_W]) * (HEAD_W ** -0.5)
    v = v_ref[0, :, cols]

    i_row = gates[h:h + 1]
    f_row = gates[ML_HEADS + h:ML_HEADS + h + 1]
    b_row = _cumsum_lanes(jnp.broadcast_to(_log_sigmoid(f_row), (SUBLANES, CHUNK)))[:1]
    b_cols = jnp.broadcast_to(b_row, (CHUNK, CHUNK)).T
    i_col = jnp.broadcast_to(i_row, (CHUNK, CHUNK)).T[:, :1]
    b_col = b_cols[:, :1]
    rowi = lax.broadcasted_iota(jnp.int32, (CHUNK, CHUNK), 0)
    coli = lax.broadcasted_iota(jnp.int32, (CHUNK, CHUNK), 1)
    log_d = jnp.where(rowi >= coli, b_cols - b_row + i_row, -jnp.inf)
    m_prev = m_ref[h][:, :1]
    inter_log = b_col + m_prev
    m_t = jnp.maximum(inter_log, jnp.max(log_d, axis=-1, keepdims=True))
    d_mat = jnp.exp(log_d - m_t)
    w_inter = jnp.exp(inter_log - m_t)

    qb = q.astype(BF16)
    s = lax.dot_general(qb, k.astype(BF16), (((1,), (1,)), ((), ())),
                        preferred_element_type=F32) * d_mat
    c_state = c_ref[h]
    n_state = n_ref[h]
    num = (jnp.dot(s.astype(BF16), v, preferred_element_type=F32)
           + w_inter * jnp.dot(qb, c_state.astype(BF16), preferred_element_type=F32))
    den = (jnp.sum(s, axis=-1, keepdims=True)
           + w_inter * jnp.sum(q * n_state, axis=-1, keepdims=True))
    hid = num / jnp.maximum(jnp.abs(den), jnp.exp(-m_t))

    b_last = b_col[CHUNK - 1:CHUNK]
    log_w = b_last - b_col + i_col
    m_new = jnp.maximum(b_last + m_prev, jnp.max(log_w, axis=0, keepdims=True))
    w = jnp.exp(log_w - m_new)
    decay = jnp.exp(b_last + m_prev - m_new)
    kw = k * w
    c_ref[h] = decay * c_state + lax.dot_general(kw.astype(BF16), v, (((0,), (0,)), ((), ())),
                                                 preferred_element_type=F32)
    n_ref[h] = decay * n_state + jnp.sum(kw, axis=0, keepdims=True)
    m_ref[h] = jnp.broadcast_to(m_new, (1, LANES))

    y = _rms_rows(hid, hg_ref[:, RET_HEADS * HEAD_W + h * HEAD_W:RET_HEADS * HEAD_W + (h + 1) * HEAD_W])
    out_cols = slice(RET_HEADS * HEAD_W + h * HEAD_W, RET_HEADS * HEAD_W + (h + 1) * HEAD_W)
    o_ref[0, :, out_cols] = (y * jax.nn.sigmoid(og_ref[0, :, cols].astype(F32))).astype(o_ref.dtype)


def _mixer_body(gl_ref, rq_ref, rk_ref, rv_ref, rg_ref, mq_ref, mqt_ref, mk_ref, mkt_ref, mv_ref, mo_ref,
                cos_ref, sin_ref, dm_ref, wq_ref, wk_ref, gr_ref, gb_ref, cw_ref, hg_ref,
                o_ref, r_ref, c_ref, n_ref, m_ref):
    first = pl.program_id(1) == 0

    @pl.when(first)
    def _():
        r_ref[...] = jnp.zeros_like(r_ref)
        c_ref[...] = jnp.zeros_like(c_ref)
        n_ref[...] = jnp.zeros_like(n_ref)
        m_ref[...] = jnp.zeros_like(m_ref)

    cos = cos_ref[...]
    sin = sin_ref[...]
    gates = gr_ref[0] + gb_ref[...]
    for h in range(RET_HEADS):
        _retention_head(h, rq_ref, rk_ref, rv_ref, rg_ref, cos, sin, dm_ref, wq_ref, wk_ref, gl_ref,
                        hg_ref, o_ref, r_ref)
    for h in range(ML_HEADS):
        _mlstm_head(h, first, mq_ref, mqt_ref, mk_ref, mkt_ref, mv_ref, mo_ref, gates, cw_ref, hg_ref,
                    o_ref, c_ref, n_ref, m_ref)


def _mixer(proj, gates_row, gate_b, conv_w, head_g):
    b, t, _ = proj.shape
    nc = t // CHUNK
    ret_w = RET_HEADS * HEAD_W
    ml_w = ML_HEADS * HEAD_W
    assert ret_w == ml_w
    log_g = jnp.log1p(-jnp.exp2(-5.0 - jnp.arange(RET_HEADS, dtype=F32)))
    idx = jnp.arange(CHUNK, dtype=F32)
    diff = idx[:, None] - idx[None, :]
    dmask = jnp.where(diff >= 0, jnp.exp(log_g[:, None, None] * jnp.maximum(diff, 0.0)), 0.0)
    w_k = jnp.exp(log_g[:, None] * (CHUNK - 1.0 - idx)[None, :])[..., None]
    w_q = jnp.exp(log_g[:, None] * (idx + 1.0)[None, :])[..., None]
    g_l = jnp.exp(log_g * CHUNK)
    half = HEAD_W // 2
    inv = 1.0 / (ROPE_BASE ** jnp.linspace(0.0, 1.0, half, dtype=F32))
    ang = jnp.arange(t).astype(F32)[:, None] * inv[None, :]
    cos, sin = jnp.cos(ang), jnp.sin(ang)
    tail_rows = 2 * SUBLANES
    per_chunk = CHUNK // tail_rows

    def group(gi):
        return pl.BlockSpec((1, CHUNK, ret_w), lambda bi, c: (bi, c, gi))

    def tail(gi):
        return pl.BlockSpec((1, tail_rows, ml_w), lambda bi, c: (bi, jnp.maximum(c * per_chunk - 1, 0), gi))

    def whole(a):
        return pl.BlockSpec(a.shape, lambda bi, c: (0,) * a.ndim)

    gb = gate_b.reshape(2 * ML_HEADS, 1)
    hg = head_g.reshape(1, ret_w + ml_w)
    return pl.pallas_call(
        _mixer_body,
        grid=(b, nc),
        in_specs=[
            pl.BlockSpec(memory_space=pltpu.SMEM),
            group(0), group(1), group(2), group(3),
            group(4), tail(4), group(5), tail(5), group(6), group(7),
            pl.BlockSpec((CHUNK, half), lambda bi, c: (c, 0)),
            pl.BlockSpec((CHUNK, half), lambda bi, c: (c, 0)),
            whole(dmask), whole(w_q), whole(w_k),
            pl.BlockSpec((1, 2 * ML_HEADS, CHUNK), lambda bi, c: (bi, 0, c)),
            whole(gb), whole(conv_w), whole(hg),
        ],
        out_specs=pl.BlockSpec((1, CHUNK, ret_w + ml_w), lambda bi, c: (bi, c, 0)),
        out_shape=jax.ShapeDtypeStruct((b, t, ret_w + ml_w), BF16),
        scratch_shapes=[pltpu.VMEM((RET_HEADS, HEAD_W, HEAD_W), F32),
                        pltpu.VMEM((ML_HEADS, HEAD_W, HEAD_W), F32),
                        pltpu.VMEM((ML_HEADS, 1, HEAD_W), F32),
                        pltpu.VMEM((ML_HEADS, 1, LANES), F32)],
        compiler_params=_params(),
        name="mixer",
    )(g_l, proj, proj, proj, proj, proj, proj, proj, proj, proj, proj, cos, sin, dmask, w_q, w_k,
      gates_row, gb, conv_w, hg)


PAIR_W = 2 * HEAD_DIM
GROUP = ATT_HEADS // KV_HEADS
PAIRS = GROUP // 2


def _swa_in_proj_body(x_ref, g_ref, w_ref, q_ref, kv_ref, h_ref, *, q_tiles):
    j = pl.program_id(1)

    @pl.when(j == 0)
    def _():
        _norm_rows_into(x_ref, g_ref, h_ref, 64)

    res = jnp.dot(h_ref[...], w_ref[...], preferred_element_type=F32)

    @pl.when(j < q_tiles)
    def _():
        for p in range(q_ref.shape[1]):
            q_ref[0, p] = res[:, p * PAIR_W:(p + 1) * PAIR_W].astype(q_ref.dtype)

    @pl.when(j == q_tiles)
    def _():
        kv_ref[...] = res.astype(kv_ref.dtype)


SWA_TN = 2 * KV_HEADS * HEAD_DIM


def _swa_in_proj(x, g, w_tiles, b, t, tm):
    n, d = x.shape
    tn = SWA_TN
    q_tiles = w_tiles.shape[0] - 1
    tiles_per_seq = t // tm
    pairs_per_tile = tn // PAIR_W
    return pl.pallas_call(
        functools.partial(_swa_in_proj_body, q_tiles=q_tiles),
        grid=(n // tm, q_tiles + 1),
        in_specs=[
            pl.BlockSpec((tm, d), lambda i, j: (i, 0)),
            pl.BlockSpec((1, d), lambda i, j: (0, 0)),
            pl.BlockSpec((None, d, tn), lambda i, j: (j, 0, 0)),
        ],
        out_specs=[
            pl.BlockSpec((1, pairs_per_tile, tm, PAIR_W),
                         lambda i, j: (i // tiles_per_seq, jnp.minimum(j, q_tiles - 1), i % tiles_per_seq, 0)),
            pl.BlockSpec((tm, tn), lambda i, j: (i, 0)),
        ],
        out_shape=[jax.ShapeDtypeStruct((b, ATT_HEADS // 2, t, PAIR_W), BF16),
                   jax.ShapeDtypeStruct((n, tn), BF16)],
        scratch_shapes=[pltpu.VMEM((tm, d), BF16)],
        compiler_params=_params(),
        name="swa_in_proj",
    )(x, g.reshape(1, d), w_tiles)


def _swa_body(rb_ref, sink_ref, idx_ref, q_ref, kp_ref, kc_ref, vp_ref, vc_ref, o_ref, bias_ref, sinkcol_ref):
    bi = pl.program_id(0)
    nb = pl.program_id(1)
    blk = ATT_BLOCK
    rows = PAIRS * blk

    @pl.when((bi == 0) & (nb == 0))
    def _():
        idx = idx_ref[...]
        col = lax.broadcasted_iota(jnp.int32, idx.shape, 1)

        def head_body(hd, carry):
            def bucket_body(bk, acc):
                return jnp.where(idx == bk, rb_ref[bk * ATT_HEADS + hd], acc)

            tbl = lax.fori_loop(0, N_BUCKETS, bucket_body, jnp.full(idx.shape, -jnp.inf, F32))
            kv = hd // GROUP
            half = hd % 2
            at = pl.ds(pl.multiple_of(((hd % GROUP) // 2) * blk, blk), blk)
            bias_ref[0, kv, half, at, :] = tbl
            bias_ref[1, kv, half, at, :] = jnp.where(col >= blk, tbl, -jnp.inf)
            sinkcol_ref[kv, half, at, :] = jnp.full((blk, LANES), sink_ref[hd], F32)
            return carry

        lax.fori_loop(0, ATT_HEADS, head_body, 0)

    first = (nb == 0).astype(jnp.int32)
    low = lax.broadcasted_iota(jnp.int32, (2 * blk, LANES), 1) < HEAD_DIM
    low_out = lax.broadcasted_iota(jnp.int32, (rows, LANES), 1) < HEAD_DIM
    kf = jnp.concatenate([kp_ref[0], kc_ref[0]], axis=0).astype(F32)
    vf = jnp.concatenate([vp_ref[0], vc_ref[0]], axis=0).astype(F32)
    scale = HEAD_DIM ** -0.5

    def halves(xf, kv):
        xc = xf[:, (kv // 2) * LANES:(kv // 2 + 1) * LANES]
        xr = pltpu.roll(xc, HEAD_DIM, axis=1)
        lo_src, hi_src = (xc, xr) if kv % 2 == 0 else (xr, xc)
        return (jnp.where(low, lo_src, 0.0).astype(BF16), jnp.where(low, 0.0, hi_src).astype(BF16))

    for kv in range(KV_HEADS):
        k_lo, k_hi = halves(kf, kv)
        v_lo, v_hi = halves(vf, kv)
        q4 = q_ref[0, kv * PAIRS:(kv + 1) * PAIRS].reshape(rows, PAIR_W) * scale

        def probs(k_half, half):
            s = lax.dot_general(q4, k_half, (((1,), (1,)), ((), ())), preferred_element_type=F32)
            s = s + bias_ref[first, kv, half]
            sink = sinkcol_ref[kv, half]
            mx = jnp.maximum(jnp.max(s, axis=-1, keepdims=True), sink)
            p = jnp.exp(s - jnp.concatenate([mx, mx], axis=1))
            den = jnp.sum(p, axis=-1, keepdims=True) + jnp.exp(sink - mx)
            return p.astype(BF16), den

        p_lo, den_lo = probs(k_lo, 0)
        p_hi, den_hi = probs(k_hi, 1)
        o4 = (jnp.dot(p_lo, v_lo, preferred_element_type=F32)
              + jnp.dot(p_hi, v_hi, preferred_element_type=F32)) / jnp.where(low_out, den_lo, den_hi)
        for pr in range(PAIRS):
            at = (kv * PAIRS + pr) * PAIR_W
            o_ref[0, :, at:at + PAIR_W] = o4[pr * blk:(pr + 1) * blk].astype(o_ref.dtype)


def _t5_bucket(dist):
    n = jnp.maximum(dist, 0)
    max_exact = N_BUCKETS // 2
    nf = jnp.maximum(n, 1).astype(F32)
    large = max_exact + (jnp.log(nf / max_exact) / math.log(MAX_DIST / max_exact)
                         * (N_BUCKETS - max_exact)).astype(jnp.int32)
    large = jnp.minimum(large, N_BUCKETS - 1)
    return jnp.where(n < max_exact, n, large)


def _swa(q, kv, sinks, rel_bias):
    b, n_pairs, t, _ = q.shape
    blk = ATT_BLOCK
    nblk = t // blk
    qw = ATT_HEADS * HEAD_DIM
    kvw = KV_HEADS * HEAD_DIM
    i = jnp.arange(blk)
    j = jnp.arange(2 * blk)
    dist = (blk + i)[:, None] - j[None, :]
    idx = jnp.where((dist >= 0) & (dist < WINDOW), _t5_bucket(dist), -1).astype(jnp.int32)
    return pl.pallas_call(
        _swa_body,
        grid=(b, nblk),
        in_specs=[
            pl.BlockSpec(memory_space=pltpu.SMEM),
            pl.BlockSpec(memory_space=pltpu.SMEM),
            pl.BlockSpec((blk, 2 * blk), lambda bi, nb: (0, 0)),
            pl.BlockSpec((1, n_pairs, blk, PAIR_W), lambda bi, nb: (bi, 0, nb, 0)),
            pl.BlockSpec((1, blk, kvw), lambda bi, nb: (bi, jnp.maximum(nb - 1, 0), 0)),
            pl.BlockSpec((1, blk, kvw), lambda bi, nb: (bi, nb, 0)),
            pl.BlockSpec((1, blk, kvw), lambda bi, nb: (bi, jnp.maximum(nb - 1, 0), 1)),
            pl.BlockSpec((1, blk, kvw), lambda bi, nb: (bi, nb, 1)),
        ],
        out_specs=pl.BlockSpec((1, blk, qw), lambda bi, nb: (bi, nb, 0)),
        out_shape=jax.ShapeDtypeStruct((b, t, qw), BF16),
        scratch_shapes=[pltpu.VMEM((2, KV_HEADS, 2, PAIRS * blk, 2 * blk), F32),
                        pltpu.VMEM((KV_HEADS, 2, PAIRS * blk, LANES), F32)],
        compiler_params=_params(),
        name="swa",
    )(rel_bias.reshape(-1), sinks, idx, q, kv, kv, kv, kv)


def kernel(x, rel_bias, norm_g, ffn_w_gu, ffn_w_down, rm_w_in, ml_conv_w, ml_gate_b, rm_head_g,
           rm_w_out, swa_w_in, swa_sinks, swa_w_out):
    b, t, d = x.shape
    n = b * t
    depth = norm_g.shape[0]
    main_cols = 4 * RET_HEADS * HEAD_W + 4 * ML_HEADS * HEAD_W
    xs = x.reshape(n, d)
    w_gu = _col_tiles(ffn_w_gu, FFN_TF)
    w_down = ffn_w_down.astype(BF16)
    for layer in range(depth):
        g = norm_g[layer]
        if layer % 2 == 0:
            e = layer // 2
            w_in = rm_w_in[e]
            gate_cols = w_in.shape[1] - main_cols
            w_gate = jnp.pad(w_in[:, main_cols:], ((0, 0), (0, LANES - gate_cols))).astype(BF16)
            proj, gates = _norm_matmul(xs, g[0], _col_tiles(w_in[:, :main_cols], PROJ_TN), w_gate, ROW_TILE)
            gates_row = gates[:, :gate_cols].reshape(b, t, gate_cols).transpose(0, 2, 1)
            mixed = _mixer(proj.reshape(b, t, main_cols), gates_row, ml_gate_b[e], ml_conv_w[e], rm_head_g[e])
            xs = _proj_norm_res(mixed.reshape(n, -1), rm_w_out[e].astype(BF16), xs, g[1], OUT_ROW_TILE)
        else:
            o = layer // 2
            q, kv = _swa_in_proj(xs, g[0], _col_tiles(swa_w_in[o], SWA_TN), b, t, ROW_TILE)
            att = _swa(q, kv.reshape(b, t, -1), swa_sinks[o], rel_bias)
            xs = _proj_norm_res(att.reshape(n, -1), swa_w_out[o].astype(BF16), xs, g[1], OUT_ROW_TILE)
        xs = _ffn(xs, g[2], w_gu, w_down, g[3], layer, ROW_TILE, FFN_ROW_BLOCK)
    return xs.reshape(b, t, d)
```

```python
import functools
import math

import jax
import jax.numpy as jnp
from jax import lax
from jax.experimental import pallas as pl
from jax.experimental.pallas import tpu as pltpu

F32 = jnp.float32
BF16 = jnp.bfloat16

EPS = 1e-6
CHUNK = 128
RET_HEADS = 4
ML_HEADS = 4
HEAD_W = 256
CONV_W = 4
ROPE_BASE = 10000.0
ATT_HEADS = 32
KV_HEADS = 4
HEAD_DIM = 64
WINDOW = 128
ATT_BLOCK = 128
N_BUCKETS = 32
MAX_DIST = 128
LANES = 128
SUBLANES = 8
VMEM_LIMIT = 56 * 1024 * 1024

ROW_TILE = 1024
OUT_ROW_TILE = 512
PROJ_TN = 512
FFN_TF = 256
FFN_ROW_BLOCK = 512


def _params(vmem=VMEM_LIMIT):
    return pltpu.CompilerParams(vmem_limit_bytes=vmem)


def _rms_rows(x, g):
    ms = jnp.mean(x * x, axis=-1, keepdims=True)
    return x * lax.rsqrt(ms + EPS) * g


def _silu(x):
    return x * jax.nn.sigmoid(x)


def _norm_rows_into(x_ref, g_ref, h_ref, row_chunk):
    g = g_ref[...]

    def body(r, carry):
        rows = pl.ds(pl.multiple_of(r * row_chunk, row_chunk), row_chunk)
        h_ref[rows, :] = _rms_rows(x_ref[rows, :], g).astype(h_ref.dtype)
        return carry

    lax.fori_loop(0, x_ref.shape[0] // row_chunk, body, 0)


def _norm_matmul_body(x_ref, g_ref, w_ref, ws_ref, o_ref, os_ref, h_ref):
    @pl.when(pl.program_id(1) == 0)
    def _():
        _norm_rows_into(x_ref, g_ref, h_ref, 64)
        os_ref[...] = jnp.dot(h_ref[...], ws_ref[...], preferred_element_type=F32)

    o_ref[...] = jnp.dot(h_ref[...], w_ref[...], preferred_element_type=F32).astype(o_ref.dtype)


def _norm_matmul(x, g, w, w_side, tiles, tm, tn):
    n, d = x.shape
    ns = w_side.shape[1]
    return pl.pallas_call(
        _norm_matmul_body,
        grid=(n // tm, tiles),
        in_specs=[
            pl.BlockSpec((tm, d), lambda i, j: (i, 0)),
            pl.BlockSpec((1, d), lambda i, j: (0, 0)),
            pl.BlockSpec((d, tn), lambda i, j: (0, j)),
            pl.BlockSpec((d, ns), lambda i, j: (0, 0)),
        ],
        out_specs=[pl.BlockSpec((tm, tn), lambda i, j: (i, j)),
                   pl.BlockSpec((tm, ns), lambda i, j: (i, 0))],
        out_shape=[jax.ShapeDtypeStruct((n, tiles * tn), BF16), jax.ShapeDtypeStruct((n, ns), F32)],
        scratch_shapes=[pltpu.VMEM((tm, d), BF16)],
        compiler_params=_params(),
        name="norm_matmul",
    )(x, g.reshape(1, d), w, w_side)


def _proj_norm_res_body(a_ref, w_ref, x_ref, g_ref, o_ref):
    y = jnp.dot(a_ref[...], w_ref[...], preferred_element_type=F32)
    o_ref[...] = x_ref[...] + _rms_rows(y, g_ref[...])


def _proj_norm_res(a, w, x, g, tm):
    n, d = x.shape
    return pl.pallas_call(
        _proj_norm_res_body,
        grid=(n // tm,),
        in_specs=[
            pl.BlockSpec((tm, a.shape[1]), lambda i: (i, 0)),
            pl.BlockSpec(w.shape, lambda i: (0, 0)),
            pl.BlockSpec((tm, d), lambda i: (i, 0)),
            pl.BlockSpec((1, d), lambda i: (0, 0)),
        ],
        out_specs=pl.BlockSpec((tm, d), lambda i: (i, 0)),
        out_shape=jax.ShapeDtypeStruct((n, d), F32),
        compiler_params=_params(),
        name="proj_norm_res",
    )(a, w, x, g.reshape(1, d))


def _ffn_body(x_ref, g_in_ref, wg_ref, wu_ref, wd_ref, g_out_ref, o_ref, h_ref, *, row_block):
    f = pl.program_id(1)

    @pl.when(f == 0)
    def _():
        _norm_rows_into(x_ref, g_in_ref, h_ref, 64)
        o_ref[...] = jnp.zeros_like(o_ref)

    for r in range(x_ref.shape[0] // row_block):
        rows = slice(r * row_block, (r + 1) * row_block)
        h = h_ref[rows, :]
        gate = jnp.dot(h, wg_ref[...], preferred_element_type=F32)
        up = jnp.dot(h, wu_ref[...], preferred_element_type=F32)
        act = (_silu(gate) * up).astype(BF16)
        o_ref[rows, :] += jnp.dot(act, wd_ref[...], preferred_element_type=F32)

    @pl.when(f == pl.num_programs(1) - 1)
    def _():
        g_out = g_out_ref[...]
        row_chunk = 64

        def body(r, carry):
            rows = pl.ds(pl.multiple_of(r * row_chunk, row_chunk), row_chunk)
            o_ref[rows, :] = x_ref[rows, :] + _rms_rows(o_ref[rows, :], g_out)
            return carry

        lax.fori_loop(0, x_ref.shape[0] // row_chunk, body, 0)


def _ffn(x, g_in, w_gu, w_down, g_out, layer, tm, tf, row_block):
    n, d = x.shape
    nf = w_down.shape[1] // tf
    return pl.pallas_call(
        functools.partial(_ffn_body, row_block=row_block),
        grid=(n // tm, nf),
        in_specs=[
            pl.BlockSpec((tm, d), lambda i, f: (i, 0)),
            pl.BlockSpec((1, d), lambda i, f: (0, 0)),
            pl.BlockSpec((None, d, tf), lambda i, f: (layer, 0, f)),
            pl.BlockSpec((None, d, tf), lambda i, f: (layer, 0, f + nf)),
            pl.BlockSpec((None, tf, d), lambda i, f: (layer, f, 0)),
            pl.BlockSpec((1, d), lambda i, f: (0, 0)),
        ],
        out_specs=pl.BlockSpec((tm, d), lambda i, f: (i, 0)),
        out_shape=jax.ShapeDtypeStruct((n, d), F32),
        scratch_shapes=[pltpu.VMEM((tm, d), BF16)],
        compiler_params=_params(),
        name="ffn",
    )(x, g_in.reshape(1, d), w_gu, w_gu, w_down, g_out.reshape(1, d))


def _log_sigmoid(x):
    return jnp.minimum(x, 0.0) - jnp.log1p(jnp.exp(-jnp.abs(x)))


_NT = (((1,), (1,)), ((), ()))
_TN = (((0,), (0,)), ((), ()))


def _retention_head(h, q_ref, k_ref, v_ref, g_ref, cos, sin, dm_ref, wq_ref, wk_ref, gl_ref, hg_ref,
                    o_ref, state_ref):
    cols = slice(h * HEAD_W, (h + 1) * HEAD_W)
    half = HEAD_W // 2

    def rot(x):
        x = x.astype(F32)
        x1, x2 = x[:, :half], x[:, half:]
        return jnp.concatenate([x1 * cos - x2 * sin, x2 * cos + x1 * sin], axis=-1)

    q = rot(q_ref[0, :, cols])
    k = rot(k_ref[0, :, cols]) * (HEAD_W ** -0.5)
    v = v_ref[0, :, cols]
    qb = q.astype(BF16)
    s = lax.dot_general(qb, k.astype(BF16), (((1,), (1,)), ((), ())),
                        preferred_element_type=F32) * dm_ref[h]
    intra = jnp.dot(s.astype(BF16), v, preferred_element_type=F32)
    state = state_ref[h]
    inter = jnp.dot(qb, state.astype(BF16), preferred_element_type=F32) * wq_ref[h]
    out = intra + inter
    kw = (k * wk_ref[h]).astype(BF16)
    kv = lax.dot_general(kw, v, (((0,), (0,)), ((), ())), preferred_element_type=F32)
    state_ref[h] = state * gl_ref[h] + kv
    y = _rms_rows(out, hg_ref[:, cols])
    o_ref[0, :, cols] = (y * _silu(g_ref[0, :, cols].astype(F32))).astype(o_ref.dtype)


def _causal_conv(cur_ref, prev_ref, first, shift_ref, w):
    cur = cur_ref[0]
    prev = jnp.where(first, jnp.zeros_like(cur), prev_ref[0])
    shifted = jnp.dot(shift_ref[...], jnp.concatenate([prev, cur], axis=0), preferred_element_type=F32)
    y = cur.astype(F32) * w[CONV_W - 1:CONV_W]
    for s in range(1, CONV_W):
        y = y + shifted[(s - 1) * CHUNK:s * CHUNK] * w[CONV_W - 1 - s:CONV_W - s]
    return y


def _mlstm_head(h, q_all, k_all, v_ref, og_ref, gx, b_rows, lower_tri, hg_ref, o_ref, c_ref, n_ref, m_ref):
    cols = slice(h * HEAD_W, (h + 1) * HEAD_W)
    q = q_all[:, cols]
    k = k_all[:, cols]
    v = v_ref[0, :, cols]

    i_row = gx[h:h + 1]
    lf_row = gx[ML_HEADS + h:ML_HEADS + h + 1]
    b_row = b_rows[ML_HEADS + h:ML_HEADS + h + 1]
    b_cols = lax.dot_general(lower_tri, jnp.broadcast_to(lf_row, (CHUNK, CHUNK)), _NT,
                             precision=lax.Precision.HIGHEST, preferred_element_type=F32)
    i_cols = jnp.broadcast_to(i_row, (CHUNK, CHUNK)).T
    rowi = lax.broadcasted_iota(jnp.int32, (CHUNK, CHUNK), 0)
    coli = lax.broadcasted_iota(jnp.int32, (CHUNK, CHUNK), 1)
    log_d = jnp.where(rowi >= coli, b_cols - b_row + i_row, -jnp.inf)
    m_prev = m_ref[h]
    inter_log = b_cols + m_prev
    m_t = jnp.maximum(inter_log, jnp.max(log_d, axis=-1, keepdims=True))
    d_mat = jnp.exp(log_d - m_t)
    w_inter = jnp.exp(inter_log - m_t)

    def wide(a):
        return jnp.concatenate([a] * (HEAD_W // LANES), axis=1)

    qb = q.astype(BF16)
    s = lax.dot_general(qb, k.astype(BF16), _NT, preferred_element_type=F32) * d_mat
    c_state = c_ref[h]
    n_state = n_ref[h]
    num = (jnp.dot(s.astype(BF16), v, preferred_element_type=F32)
           + wide(w_inter) * jnp.dot(qb, c_state.astype(BF16), preferred_element_type=F32))
    den = (jnp.sum(s, axis=-1, keepdims=True)
           + w_inter * jnp.sum(q * n_state, axis=-1, keepdims=True))
    hid = num / wide(jnp.maximum(jnp.abs(den), jnp.exp(-m_t)))

    b_last = b_cols[CHUNK - 1:CHUNK]
    log_w = b_last - b_cols + i_cols
    m_new = jnp.maximum(b_last + m_prev, jnp.max(log_w, axis=0, keepdims=True))
    w = jnp.exp(log_w - m_new)
    decay = wide(jnp.exp(b_last + m_prev - m_new))
    kw = k * wide(w)
    c_ref[h] = decay * c_state + lax.dot_general(kw.astype(BF16), v, _TN, preferred_element_type=F32)
    n_ref[h] = decay * n_state + jnp.sum(kw, axis=0, keepdims=True)
    m_ref[h] = m_new

    out_cols = slice(RET_HEADS * HEAD_W + h * HEAD_W, RET_HEADS * HEAD_W + (h + 1) * HEAD_W)
    y = _rms_rows(hid, hg_ref[:, out_cols])
    o_ref[0, :, out_cols] = (y * jax.nn.sigmoid(og_ref[0, :, cols].astype(F32))).astype(o_ref.dtype)


def _mixer_body(gl_ref, rq_ref, rk_ref, rv_ref, rg_ref, mq_ref, mqp_ref, mk_ref, mkp_ref, mv_ref, mo_ref,
                cos_ref, sin_ref, dm_ref, wq_ref, wk_ref, gr_ref, gb_ref, cw_ref, shift_ref, hg_ref,
                o_ref, r_ref, c_ref, n_ref, m_ref):
    first = pl.program_id(1) == 0

    @pl.when(first)
    def _():
        r_ref[...] = jnp.zeros_like(r_ref)
        c_ref[...] = jnp.zeros_like(c_ref)
        n_ref[...] = jnp.zeros_like(n_ref)
        m_ref[...] = jnp.zeros_like(m_ref)

    cos = cos_ref[...]
    sin = sin_ref[...]
    for h in range(RET_HEADS):
        _retention_head(h, rq_ref, rk_ref, rv_ref, rg_ref, cos, sin, dm_ref, wq_ref, wk_ref, gl_ref,
                        hg_ref, o_ref, r_ref)

    ml_w = ML_HEADS * HEAD_W
    q_all = _silu(_causal_conv(mq_ref, mqp_ref, first, shift_ref, cw_ref[:, :ml_w]))
    k_all = _silu(_causal_conv(mk_ref, mkp_ref, first, shift_ref, cw_ref[:, ml_w:])) * (HEAD_W ** -0.5)
    gates = gr_ref[0] + gb_ref[...]
    is_input = lax.broadcasted_iota(jnp.int32, gates.shape, 0) < ML_HEADS
    gx = jnp.where(is_input, gates, _log_sigmoid(gates))
    rowi = lax.broadcasted_iota(jnp.int32, (CHUNK, CHUNK), 0)
    coli = lax.broadcasted_iota(jnp.int32, (CHUNK, CHUNK), 1)
    upper_tri = jnp.where(rowi <= coli, 1.0, 0.0).astype(F32)
    lower_tri = jnp.where(rowi >= coli, 1.0, 0.0).astype(F32)
    b_rows = jnp.dot(gx, upper_tri, precision=lax.Precision.HIGHEST, preferred_element_type=F32)
    for h in range(ML_HEADS):
        _mlstm_head(h, q_all, k_all, mv_ref, mo_ref, gx, b_rows, lower_tri, hg_ref, o_ref, c_ref, n_ref, m_ref)


def _mixer(proj, gates_row, gate_b, conv_w, head_g):
    b, t, _ = proj.shape
    nc = t // CHUNK
    ret_w = RET_HEADS * HEAD_W
    ml_w = ML_HEADS * HEAD_W
    assert ret_w == ml_w
    log_g = jnp.log1p(-jnp.exp2(-5.0 - jnp.arange(RET_HEADS, dtype=F32)))
    idx = jnp.arange(CHUNK, dtype=F32)
    diff = idx[:, None] - idx[None, :]
    dmask = jnp.where(diff >= 0, jnp.exp(log_g[:, None, None] * jnp.maximum(diff, 0.0)), 0.0)
    w_k = jnp.exp(log_g[:, None] * (CHUNK - 1.0 - idx)[None, :])
    w_q = jnp.exp(log_g[:, None] * (idx + 1.0)[None, :])
    w_k = jnp.broadcast_to(w_k[..., None], (RET_HEADS, CHUNK, HEAD_W))
    w_q = jnp.broadcast_to(w_q[..., None], (RET_HEADS, CHUNK, HEAD_W))
    g_l = jnp.exp(log_g * CHUNK)
    half = HEAD_W // 2
    inv = 1.0 / (ROPE_BASE ** jnp.linspace(0.0, 1.0, half, dtype=F32))
    ang = jnp.arange(t).astype(F32)[:, None] * inv[None, :]
    cos, sin = jnp.cos(ang), jnp.sin(ang)
    r = jnp.arange((CONV_W - 1) * CHUNK)
    src = CHUNK + r % CHUNK - (r // CHUNK + 1)
    shift = (jnp.arange(2 * CHUNK)[None, :] == src[:, None]).astype(BF16)

    def group(gi):
        return pl.BlockSpec((1, CHUNK, ret_w), lambda bi, c: (bi, c, gi))

    def prev_group(gi):
        return pl.BlockSpec((1, CHUNK, ml_w), lambda bi, c: (bi, jnp.maximum(c - 1, 0), gi))

    def whole(a):
        return pl.BlockSpec(a.shape, lambda bi, c: (0,) * a.ndim)

    gb = gate_b.reshape(2 * ML_HEADS, 1)
    hg = head_g.reshape(1, ret_w + ml_w)
    return pl.pallas_call(
        _mixer_body,
        grid=(b, nc),
        in_specs=[
            pl.BlockSpec(memory_space=pltpu.SMEM),
            group(0), group(1), group(2), group(3),
            group(4), prev_group(4), group(5), prev_group(5), group(6), group(7),
            pl.BlockSpec((CHUNK, half), lambda bi, c: (c, 0)),
            pl.BlockSpec((CHUNK, half), lambda bi, c: (c, 0)),
            whole(dmask), whole(w_q), whole(w_k),
            pl.BlockSpec((1, 2 * ML_HEADS, CHUNK), lambda bi, c: (bi, 0, c)),
            whole(gb), whole(conv_w), whole(shift), whole(hg),
        ],
        out_specs=pl.BlockSpec((1, CHUNK, ret_w + ml_w), lambda bi, c: (bi, c, 0)),
        out_shape=jax.ShapeDtypeStruct((b, t, ret_w + ml_w), BF16),
        scratch_shapes=[pltpu.VMEM((RET_HEADS, HEAD_W, HEAD_W), F32),
                        pltpu.VMEM((ML_HEADS, HEAD_W, HEAD_W), F32),
                        pltpu.VMEM((ML_HEADS, 1, HEAD_W), F32),
                        pltpu.VMEM((ML_HEADS, 1, LANES), F32)],
        compiler_params=_params(),
        name="mixer",
    )(g_l, proj, proj, proj, proj, proj, proj, proj, proj, proj, proj, cos, sin, dmask, w_q, w_k,
      gates_row, gb, conv_w, shift, hg)


PAIR_W = 2 * HEAD_DIM
GROUP = ATT_HEADS // KV_HEADS
PAIRS = GROUP // 2


def _swa_in_proj_body(x_ref, g_ref, w_ref, q_ref, kv_ref, h_ref, *, q_tiles):
    j = pl.program_id(1)

    @pl.when(j == 0)
    def _():
        _norm_rows_into(x_ref, g_ref, h_ref, 64)

    res = jnp.dot(h_ref[...], w_ref[...], preferred_element_type=F32)

    @pl.when(j < q_tiles)
    def _():
        for p in range(q_ref.shape[1]):
            q_ref[0, p] = res[:, p * PAIR_W:(p + 1) * PAIR_W].astype(q_ref.dtype)

    @pl.when(j == q_tiles)
    def _():
        kv_ref[...] = res.astype(kv_ref.dtype)


SWA_TN = 2 * KV_HEADS * HEAD_DIM


def _swa_in_proj(x, g, w, b, t, tm):
    n, d = x.shape
    tn = SWA_TN
    q_tiles = w.shape[1] // tn - 1
    tiles_per_seq = t // tm
    pairs_per_tile = tn // PAIR_W
    return pl.pallas_call(
        functools.partial(_swa_in_proj_body, q_tiles=q_tiles),
        grid=(n // tm, q_tiles + 1),
        in_specs=[
            pl.BlockSpec((tm, d), lambda i, j: (i, 0)),
            pl.BlockSpec((1, d), lambda i, j: (0, 0)),
            pl.BlockSpec((d, tn), lambda i, j: (0, j)),
        ],
        out_specs=[
            pl.BlockSpec((1, pairs_per_tile, tm, PAIR_W),
                         lambda i, j: (i // tiles_per_seq, jnp.minimum(j, q_tiles - 1), i % tiles_per_seq, 0)),
            pl.BlockSpec((tm, tn), lambda i, j: (i, 0)),
        ],
        out_shape=[jax.ShapeDtypeStruct((b, ATT_HEADS // 2, t, PAIR_W), BF16),
                   jax.ShapeDtypeStruct((n, tn), BF16)],
        scratch_shapes=[pltpu.VMEM((tm, d), BF16)],
        compiler_params=_params(),
        name="swa_in_proj",
    )(x, g.reshape(1, d), w)


def _swa_body(rb_ref, sink_ref, idx_ref, q_ref, kp_ref, kc_ref, vp_ref, vc_ref, o_ref, bias_ref, sinkcol_ref):
    bi = pl.program_id(0)
    nb = pl.program_id(1)
    blk = ATT_BLOCK
    rows = PAIRS * blk

    @pl.when((bi == 0) & (nb == 0))
    def _():
        idx = idx_ref[...]
        col = lax.broadcasted_iota(jnp.int32, idx.shape, 1)

        def head_body(hd, carry):
            def bucket_body(bk, acc):
                return jnp.where(idx == bk, rb_ref[bk * ATT_HEADS + hd], acc)

            tbl = lax.fori_loop(0, N_BUCKETS, bucket_body, jnp.full(idx.shape, -jnp.inf, F32))
            kv = hd // GROUP
            half = hd % 2
            at = pl.ds(pl.multiple_of(((hd % GROUP) // 2) * blk, blk), blk)
            bias_ref[0, kv, half, at, :] = tbl
            bias_ref[1, kv, half, at, :] = jnp.where(col >= blk, tbl, -jnp.inf)
            sinkcol_ref[kv, half, at, :] = jnp.full((blk, LANES), sink_ref[hd], F32)
            return carry

        lax.fori_loop(0, ATT_HEADS, head_body, 0)

    first = (nb == 0).astype(jnp.int32)
    low = lax.broadcasted_iota(jnp.int32, (2 * blk, LANES), 1) < HEAD_DIM
    low_out = lax.broadcasted_iota(jnp.int32, (rows, LANES), 1) < HEAD_DIM
    kf = jnp.concatenate([kp_ref[0], kc_ref[0]], axis=0).astype(F32)
    vf = jnp.concatenate([vp_ref[0], vc_ref[0]], axis=0).astype(F32)
    scale = HEAD_DIM ** -0.5

    def halves(xf, kv):
        xc = xf[:, (kv // 2) * LANES:(kv // 2 + 1) * LANES]
        xr = pltpu.roll(xc, HEAD_DIM, axis=1)
        lo_src, hi_src = (xc, xr) if kv % 2 == 0 else (xr, xc)
        return (jnp.where(low, lo_src, 0.0).astype(BF16), jnp.where(low, 0.0, hi_src).astype(BF16))

    for kv in range(KV_HEADS):
        k_lo, k_hi = halves(kf, kv)
        v_lo, v_hi = halves(vf, kv)
        q4 = q_ref[0, kv * PAIRS:(kv + 1) * PAIRS].reshape(rows, PAIR_W) * scale

        def probs(k_half, half):
            s = lax.dot_general(q4, k_half, (((1,), (1,)), ((), ())), preferred_element_type=F32)
            s = s + bias_ref[first, kv, half]
            sink = sinkcol_ref[kv, half]
            mx = jnp.maximum(jnp.max(s, axis=-1, keepdims=True), sink)
            p = jnp.exp(s - jnp.concatenate([mx, mx], axis=1))
            den = jnp.sum(p, axis=-1, keepdims=True) + jnp.exp(sink - mx)
            return p.astype(BF16), den

        p_lo, den_lo = probs(k_lo, 0)
        p_hi, den_hi = probs(k_hi, 1)
        o4 = (jnp.dot(p_lo, v_lo, preferred_element_type=F32)
              + jnp.dot(p_hi, v_hi, preferred_element_type=F32)) / jnp.where(low_out, den_lo, den_hi)
        for pr in range(PAIRS):
            at = (kv * PAIRS + pr) * PAIR_W
            o_ref[0, :, at:at + PAIR_W] = o4[pr * blk:(pr + 1) * blk].astype(o_ref.dtype)


def _t5_bucket(dist):
    n = jnp.maximum(dist, 0)
    max_exact = N_BUCKETS // 2
    nf = jnp.maximum(n, 1).astype(F32)
    large = max_exact + (jnp.log(nf / max_exact) / math.log(MAX_DIST / max_exact)
                         * (N_BUCKETS - max_exact)).astype(jnp.int32)
    large = jnp.minimum(large, N_BUCKETS - 1)
    return jnp.where(n < max_exact, n, large)


def _swa(q, kv, sinks, rel_bias):
    b, n_pairs, t, _ = q.shape
    blk = ATT_BLOCK
    nblk = t // blk
    qw = ATT_HEADS * HEAD_DIM
    kvw = KV_HEADS * HEAD_DIM
    i = jnp.arange(blk)
    j = jnp.arange(2 * blk)
    dist = (blk + i)[:, None] - j[None, :]
    idx = jnp.where((dist >= 0) & (dist < WINDOW), _t5_bucket(dist), -1).astype(jnp.int32)
    return pl.pallas_call(
        _swa_body,
        grid=(b, nblk),
        in_specs=[
            pl.BlockSpec(memory_space=pltpu.SMEM),
            pl.BlockSpec(memory_space=pltpu.SMEM),
            pl.BlockSpec((blk, 2 * blk), lambda bi, nb: (0, 0)),
            pl.BlockSpec((1, n_pairs, blk, PAIR_W), lambda bi, nb: (bi, 0, nb, 0)),
            pl.BlockSpec((1, blk, kvw), lambda bi, nb: (bi, jnp.maximum(nb - 1, 0), 0)),
            pl.BlockSpec((1, blk, kvw), lambda bi, nb: (bi, nb, 0)),
            pl.BlockSpec((1, blk, kvw), lambda bi, nb: (bi, jnp.maximum(nb - 1, 0), 1)),
            pl.BlockSpec((1, blk, kvw), lambda bi, nb: (bi, nb, 1)),
        ],
        out_specs=pl.BlockSpec((1, blk, qw), lambda bi, nb: (bi, nb, 0)),
        out_shape=jax.ShapeDtypeStruct((b, t, qw), BF16),
        scratch_shapes=[pltpu.VMEM((2, KV_HEADS, 2, PAIRS * blk, 2 * blk), F32),
                        pltpu.VMEM((KV_HEADS, 2, PAIRS * blk, LANES), F32)],
        compiler_params=_params(),
        name="swa",
    )(rel_bias.reshape(-1), sinks, idx, q, kv, kv, kv, kv)


def kernel(x, rel_bias, norm_g, ffn_w_gu, ffn_w_down, rm_w_in, ml_conv_w, ml_gate_b, rm_head_g,
           rm_w_out, swa_w_in, swa_sinks, swa_w_out):
    b, t, d = x.shape
    n = b * t
    depth = norm_g.shape[0]
    main_cols = 4 * RET_HEADS * HEAD_W + 4 * ML_HEADS * HEAD_W
    xs = x.reshape(n, d)
    w_gu = ffn_w_gu.astype(BF16)
    w_down = ffn_w_down.astype(BF16)
    for layer in range(depth):
        g = norm_g[layer]
        if layer % 2 == 0:
            e = layer // 2
            w_in = rm_w_in[e]
            gate_cols = w_in.shape[1] - main_cols
            w_gate = jnp.pad(w_in[:, main_cols:], ((0, 0), (0, LANES - gate_cols))).astype(BF16)
            proj, gates = _norm_matmul(xs, g[0], w_in.astype(BF16), w_gate, main_cols // PROJ_TN, ROW_TILE, PROJ_TN)
            gates_row = gates[:, :gate_cols].reshape(b, t, gate_cols).transpose(0, 2, 1)
            mixed = _mixer(proj.reshape(b, t, main_cols), gates_row, ml_gate_b[e], ml_conv_w[e], rm_head_g[e])
            xs = _proj_norm_res(mixed.reshape(n, -1), rm_w_out[e].astype(BF16), xs, g[1], OUT_ROW_TILE)
        else:
            o = layer // 2
            q, kv = _swa_in_proj(xs, g[0], swa_w_in[o].astype(BF16), b, t, ROW_TILE)
            att = _swa(q, kv.reshape(b, t, -1), swa_sinks[o], rel_bias)
            xs = _proj_norm_res(att.reshape(n, -1), swa_w_out[o].astype(BF16), xs, g[1], OUT_ROW_TILE)
        xs = _ffn(xs, g[2], w_gu, w_down, g[3], layer, ROW_TILE, FFN_TF, FFN_ROW_BLOCK)
    return xs.reshape(b, t, d)
```

```python
import functools
import math

import jax
import jax.numpy as jnp
from jax import lax
from jax.experimental import pallas as pl
from jax.experimental.pallas import tpu as pltpu

F32 = jnp.float32
BF16 = jnp.bfloat16

EPS = 1e-6
CHUNK = 128
RET_HEADS = 4
ML_HEADS = 4
HEAD_W = 256
CONV_W = 4
ROPE_BASE = 10000.0
ATT_HEADS = 32
KV_HEADS = 4
HEAD_DIM = 64
WINDOW = 128
ATT_BLOCK = 128
N_BUCKETS = 32
MAX_DIST = 128
LANES = 128
SUBLANES = 8
VMEM_LIMIT = 56 * 1024 * 1024

ROW_TILE = 1024
OUT_ROW_TILE = 512
PROJ_TN = 1024
FFN_TF = 256
FFN_ROW_BLOCK = 512


def _params(vmem=VMEM_LIMIT):
    return pltpu.CompilerParams(vmem_limit_bytes=vmem)


def _rms_rows(x, g):
    ms = jnp.mean(x * x, axis=-1, keepdims=True)
    return x * lax.rsqrt(ms + EPS) * g


def _silu(x):
    return x * jax.nn.sigmoid(x)


def _norm_rows_into(x_ref, g_ref, h_ref, row_chunk):
    g = g_ref[...]

    def body(r, carry):
        rows = pl.ds(pl.multiple_of(r * row_chunk, row_chunk), row_chunk)
        h_ref[rows, :] = _rms_rows(x_ref[rows, :], g).astype(h_ref.dtype)
        return carry

    lax.fori_loop(0, x_ref.shape[0] // row_chunk, body, 0)


def _norm_matmul_body(x_ref, g_ref, w_ref, ws_ref, o_ref, os_ref, h_ref):
    @pl.when(pl.program_id(1) == 0)
    def _():
        _norm_rows_into(x_ref, g_ref, h_ref, 64)
        os_ref[...] = jnp.dot(h_ref[...], ws_ref[...], preferred_element_type=F32)

    o_ref[...] = jnp.dot(h_ref[...], w_ref[...], preferred_element_type=F32).astype(o_ref.dtype)


def _norm_matmul(x, g, w, w_side, tiles, tm, tn):
    n, d = x.shape
    ns = w_side.shape[1]
    return pl.pallas_call(
        _norm_matmul_body,
        grid=(n // tm, tiles),
        in_specs=[
            pl.BlockSpec((tm, d), lambda i, j: (i, 0)),
            pl.BlockSpec((1, d), lambda i, j: (0, 0)),
            pl.BlockSpec((d, tn), lambda i, j: (0, j)),
            pl.BlockSpec((d, ns), lambda i, j: (0, 0)),
        ],
        out_specs=[pl.BlockSpec((None, tm, tn), lambda i, j: (j, i, 0)),
                   pl.BlockSpec((tm, ns), lambda i, j: (i, 0))],
        out_shape=[jax.ShapeDtypeStruct((tiles, n, tn), BF16), jax.ShapeDtypeStruct((n, ns), F32)],
        scratch_shapes=[pltpu.VMEM((tm, d), BF16)],
        compiler_params=_params(),
        name="norm_matmul",
    )(x, g.reshape(1, d), w, w_side)


def _proj_norm_res_body(a_ref, w_ref, x_ref, g_ref, o_ref):
    y = jnp.dot(a_ref[...], w_ref[...], preferred_element_type=F32)
    o_ref[...] = x_ref[...] + _rms_rows(y, g_ref[...])


def _proj_norm_res(a, w, x, g, tm):
    n, d = x.shape
    return pl.pallas_call(
        _proj_norm_res_body,
        grid=(n // tm,),
        in_specs=[
            pl.BlockSpec((tm, a.shape[1]), lambda i: (i, 0)),
            pl.BlockSpec(w.shape, lambda i: (0, 0)),
            pl.BlockSpec((tm, d), lambda i: (i, 0)),
            pl.BlockSpec((1, d), lambda i: (0, 0)),
        ],
        out_specs=pl.BlockSpec((tm, d), lambda i: (i, 0)),
        out_shape=jax.ShapeDtypeStruct((n, d), F32),
        compiler_params=_params(),
        name="proj_norm_res",
    )(a, w, x, g.reshape(1, d))


def _ffn_body(x_ref, g_in_ref, wg_ref, wu_ref, wd_ref, g_out_ref, o_ref, h_ref, *, row_block):
    f = pl.program_id(1)

    @pl.when(f == 0)
    def _():
        _norm_rows_into(x_ref, g_in_ref, h_ref, 64)
        o_ref[...] = jnp.zeros_like(o_ref)

    for r in range(x_ref.shape[0] // row_block):
        rows = slice(r * row_block, (r + 1) * row_block)
        h = h_ref[rows, :]
        gate = jnp.dot(h, wg_ref[...], preferred_element_type=F32)
        up = jnp.dot(h, wu_ref[...], preferred_element_type=F32)
        act = (_silu(gate) * up).astype(BF16)
        o_ref[rows, :] += jnp.dot(act, wd_ref[...], preferred_element_type=F32)

    @pl.when(f == pl.num_programs(1) - 1)
    def _():
        g_out = g_out_ref[...]
        row_chunk = 64

        def body(r, carry):
            rows = pl.ds(pl.multiple_of(r * row_chunk, row_chunk), row_chunk)
            o_ref[rows, :] = x_ref[rows, :] + _rms_rows(o_ref[rows, :], g_out)
            return carry

        lax.fori_loop(0, x_ref.shape[0] // row_chunk, body, 0)


def _ffn(x, g_in, w_gu, w_down, g_out, layer, tm, tf, row_block):
    n, d = x.shape
    nf = w_down.shape[1] // tf
    return pl.pallas_call(
        functools.partial(_ffn_body, row_block=row_block),
        grid=(n // tm, nf),
        in_specs=[
            pl.BlockSpec((tm, d), lambda i, f: (i, 0)),
            pl.BlockSpec((1, d), lambda i, f: (0, 0)),
            pl.BlockSpec((None, d, tf), lambda i, f: (layer, 0, f)),
            pl.BlockSpec((None, d, tf), lambda i, f: (layer, 0, f + nf)),
            pl.BlockSpec((None, tf, d), lambda i, f: (layer, f, 0)),
            pl.BlockSpec((1, d), lambda i, f: (0, 0)),
        ],
        out_specs=pl.BlockSpec((tm, d), lambda i, f: (i, 0)),
        out_shape=jax.ShapeDtypeStruct((n, d), F32),
        scratch_shapes=[pltpu.VMEM((tm, d), BF16)],
        compiler_params=_params(),
        name="ffn",
    )(x, g_in.reshape(1, d), w_gu, w_gu, w_down, g_out.reshape(1, d))


def _log_sigmoid(x):
    return jnp.minimum(x, 0.0) - jnp.log1p(jnp.exp(-jnp.abs(x)))


_NT = (((1,), (1,)), ((), ()))
_TN = (((0,), (0,)), ((), ()))


def _retention_head(h, q_ref, k_ref, v_ref, g_ref, cos, sin, dm_ref, wq_ref, wk_ref, gl_ref, hg_ref,
                    o_ref, state_ref):
    cols = slice(h * HEAD_W, (h + 1) * HEAD_W)
    half = HEAD_W // 2

    def rot(x):
        x = x.astype(F32)
        x1, x2 = x[:, :half], x[:, half:]
        return jnp.concatenate([x1 * cos - x2 * sin, x2 * cos + x1 * sin], axis=-1)

    q = rot(q_ref[0, :, cols])
    k = rot(k_ref[0, :, cols]) * (HEAD_W ** -0.5)
    v = v_ref[0, :, cols]
    qb = q.astype(BF16)
    s = lax.dot_general(qb, k.astype(BF16), (((1,), (1,)), ((), ())),
                        preferred_element_type=F32) * dm_ref[h]
    intra = jnp.dot(s.astype(BF16), v, preferred_element_type=F32)
    state = state_ref[h]
    inter = jnp.dot(qb, state.astype(BF16), preferred_element_type=F32) * wq_ref[h]
    out = intra + inter
    kw = (k * wk_ref[h]).astype(BF16)
    kv = lax.dot_general(kw, v, (((0,), (0,)), ((), ())), preferred_element_type=F32)
    state_ref[h] = state * gl_ref[h] + kv
    y = _rms_rows(out, hg_ref[:, cols])
    o_ref[0, :, cols] = (y * _silu(g_ref[0, :, cols].astype(F32))).astype(o_ref.dtype)


def _causal_conv(cur_ref, prev_ref, first, shift_ref, w):
    cur = cur_ref[0]
    prev = jnp.where(first, jnp.zeros_like(cur), prev_ref[0])
    shifted = jnp.dot(shift_ref[...], jnp.concatenate([prev, cur], axis=0), preferred_element_type=F32)
    y = cur.astype(F32) * w[CONV_W - 1:CONV_W]
    for s in range(1, CONV_W):
        y = y + shifted[(s - 1) * CHUNK:s * CHUNK] * w[CONV_W - 1 - s:CONV_W - s]
    return y


def _mlstm_head(h, q_all, k_all, v_ref, og_ref, gx, b_rows, lower_tri, hg_ref, o_ref, c_ref, n_ref, m_ref):
    cols = slice(h * HEAD_W, (h + 1) * HEAD_W)
    q = q_all[:, cols]
    k = k_all[:, cols]
    v = v_ref[0, :, cols]

    i_row = gx[h:h + 1]
    lf_row = gx[ML_HEADS + h:ML_HEADS + h + 1]
    b_row = b_rows[ML_HEADS + h:ML_HEADS + h + 1]
    b_cols = lax.dot_general(lower_tri, jnp.broadcast_to(lf_row, (CHUNK, CHUNK)), _NT,
                             precision=lax.Precision.HIGHEST, preferred_element_type=F32)
    i_cols = jnp.broadcast_to(i_row, (CHUNK, CHUNK)).T
    rowi = lax.broadcasted_iota(jnp.int32, (CHUNK, CHUNK), 0)
    coli = lax.broadcasted_iota(jnp.int32, (CHUNK, CHUNK), 1)
    log_d = jnp.where(rowi >= coli, b_cols - b_row + i_row, -jnp.inf)
    m_prev = m_ref[h]
    inter_log = b_cols + m_prev
    m_t = jnp.maximum(inter_log, jnp.max(log_d, axis=-1, keepdims=True))
    d_mat = jnp.exp(log_d - m_t)
    w_inter = jnp.exp(inter_log - m_t)

    def wide(a):
        return jnp.concatenate([a] * (HEAD_W // LANES), axis=1)

    qb = q.astype(BF16)
    s = lax.dot_general(qb, k.astype(BF16), _NT, preferred_element_type=F32) * d_mat
    c_state = c_ref[h]
    n_state = n_ref[h]
    num = (jnp.dot(s.astype(BF16), v, preferred_element_type=F32)
           + wide(w_inter) * jnp.dot(qb, c_state.astype(BF16), preferred_element_type=F32))
    den = (jnp.sum(s, axis=-1, keepdims=True)
           + w_inter * jnp.sum(q * n_state, axis=-1, keepdims=True))
    hid = num / wide(jnp.maximum(jnp.abs(den), jnp.exp(-m_t)))

    b_last = b_cols[CHUNK - 1:CHUNK]
    log_w = b_last - b_cols + i_cols
    m_new = jnp.maximum(b_last + m_prev, jnp.max(log_w, axis=0, keepdims=True))
    w = jnp.exp(log_w - m_new)
    decay = wide(jnp.exp(b_last + m_prev - m_new))
    kw = k * wide(w)
    c_ref[h] = decay * c_state + lax.dot_general(kw.astype(BF16), v, _TN, preferred_element_type=F32)
    n_ref[h] = decay * n_state + jnp.sum(kw, axis=0, keepdims=True)
    m_ref[h] = m_new

    out_cols = slice(RET_HEADS * HEAD_W + h * HEAD_W, RET_HEADS * HEAD_W + (h + 1) * HEAD_W)
    y = _rms_rows(hid, hg_ref[:, out_cols])
    o_ref[0, :, out_cols] = (y * jax.nn.sigmoid(og_ref[0, :, cols].astype(F32))).astype(o_ref.dtype)


def _mixer_body(gl_ref, rq_ref, rk_ref, rv_ref, rg_ref, mq_ref, mqp_ref, mk_ref, mkp_ref, mv_ref, mo_ref,
                cos_ref, sin_ref, dm_ref, wq_ref, wk_ref, gr_ref, gb_ref, cw_ref, shift_ref, hg_ref,
                o_ref, r_ref, c_ref, n_ref, m_ref):
    first = pl.program_id(1) == 0

    @pl.when(first)
    def _():
        r_ref[...] = jnp.zeros_like(r_ref)
        c_ref[...] = jnp.zeros_like(c_ref)
        n_ref[...] = jnp.zeros_like(n_ref)
        m_ref[...] = jnp.zeros_like(m_ref)

    cos = cos_ref[...]
    sin = sin_ref[...]
    for h in range(RET_HEADS):
        _retention_head(h, rq_ref, rk_ref, rv_ref, rg_ref, cos, sin, dm_ref, wq_ref, wk_ref, gl_ref,
                        hg_ref, o_ref, r_ref)

    ml_w = ML_HEADS * HEAD_W
    q_all = _silu(_causal_conv(mq_ref, mqp_ref, first, shift_ref, cw_ref[:, :ml_w]))
    k_all = _silu(_causal_conv(mk_ref, mkp_ref, first, shift_ref, cw_ref[:, ml_w:])) * (HEAD_W ** -0.5)
    gates = gr_ref[0] + gb_ref[...]
    is_input = lax.broadcasted_iota(jnp.int32, gates.shape, 0) < ML_HEADS
    gx = jnp.where(is_input, gates, _log_sigmoid(gates))
    rowi = lax.broadcasted_iota(jnp.int32, (CHUNK, CHUNK), 0)
    coli = lax.broadcasted_iota(jnp.int32, (CHUNK, CHUNK), 1)
    upper_tri = jnp.where(rowi <= coli, 1.0, 0.0).astype(F32)
    lower_tri = jnp.where(rowi >= coli, 1.0, 0.0).astype(F32)
    b_rows = jnp.dot(gx, upper_tri, precision=lax.Precision.HIGHEST, preferred_element_type=F32)
    for h in range(ML_HEADS):
        _mlstm_head(h, q_all, k_all, mv_ref, mo_ref, gx, b_rows, lower_tri, hg_ref, o_ref, c_ref, n_ref, m_ref)


def _mixer(proj, gates_row, gate_b, conv_w, head_g):
    _, b, t, _ = proj.shape
    nc = t // CHUNK
    ret_w = RET_HEADS * HEAD_W
    ml_w = ML_HEADS * HEAD_W
    assert ret_w == ml_w == proj.shape[3]
    log_g = jnp.log1p(-jnp.exp2(-5.0 - jnp.arange(RET_HEADS, dtype=F32)))
    idx = jnp.arange(CHUNK, dtype=F32)
    diff = idx[:, None] - idx[None, :]
    dmask = jnp.where(diff >= 0, jnp.exp(log_g[:, None, None] * jnp.maximum(diff, 0.0)), 0.0)
    w_k = jnp.exp(log_g[:, None] * (CHUNK - 1.0 - idx)[None, :])
    w_q = jnp.exp(log_g[:, None] * (idx + 1.0)[None, :])
    w_k = jnp.broadcast_to(w_k[..., None], (RET_HEADS, CHUNK, HEAD_W))
    w_q = jnp.broadcast_to(w_q[..., None], (RET_HEADS, CHUNK, HEAD_W))
    g_l = jnp.exp(log_g * CHUNK)
    half = HEAD_W // 2
    inv = 1.0 / (ROPE_BASE ** jnp.linspace(0.0, 1.0, half, dtype=F32))
    ang = jnp.arange(t).astype(F32)[:, None] * inv[None, :]
    cos, sin = jnp.cos(ang), jnp.sin(ang)
    r = jnp.arange((CONV_W - 1) * CHUNK)
    src = CHUNK + r % CHUNK - (r // CHUNK + 1)
    shift = (jnp.arange(2 * CHUNK)[None, :] == src[:, None]).astype(BF16)

    def group(gi):
        return pl.BlockSpec((None, 1, CHUNK, ret_w), lambda bi, c: (gi, bi, c, 0))

    def prev_group(gi):
        return pl.BlockSpec((None, 1, CHUNK, ml_w), lambda bi, c: (gi, bi, jnp.maximum(c - 1, 0), 0))

    def whole(a):
        return pl.BlockSpec(a.shape, lambda bi, c: (0,) * a.ndim)

    gb = gate_b.reshape(2 * ML_HEADS, 1)
    hg = head_g.reshape(1, ret_w + ml_w)
    return pl.pallas_call(
        _mixer_body,
        grid=(b, nc),
        in_specs=[
            pl.BlockSpec(memory_space=pltpu.SMEM),
            group(0), group(1), group(2), group(3),
            group(4), prev_group(4), group(5), prev_group(5), group(6), group(7),
            pl.BlockSpec((CHUNK, half), lambda bi, c: (c, 0)),
            pl.BlockSpec((CHUNK, half), lambda bi, c: (c, 0)),
            whole(dmask), whole(w_q), whole(w_k),
            pl.BlockSpec((1, 2 * ML_HEADS, CHUNK), lambda bi, c: (bi, 0, c)),
            whole(gb), whole(conv_w), whole(shift), whole(hg),
        ],
        out_specs=pl.BlockSpec((1, CHUNK, ret_w + ml_w), lambda bi, c: (bi, c, 0)),
        out_shape=jax.ShapeDtypeStruct((b, t, ret_w + ml_w), BF16),
        scratch_shapes=[pltpu.VMEM((RET_HEADS, HEAD_W, HEAD_W), F32),
                        pltpu.VMEM((ML_HEADS, HEAD_W, HEAD_W), F32),
                        pltpu.VMEM((ML_HEADS, 1, HEAD_W), F32),
                        pltpu.VMEM((ML_HEADS, 1, LANES), F32)],
        compiler_params=_params(),
        name="mixer",
    )(g_l, proj, proj, proj, proj, proj, proj, proj, proj, proj, proj, cos, sin, dmask, w_q, w_k,
      gates_row, gb, conv_w, shift, hg)


PAIR_W = 2 * HEAD_DIM
GROUP = ATT_HEADS // KV_HEADS
PAIRS = GROUP // 2


def _swa_in_proj_body(x_ref, g_ref, w_ref, q_ref, kv_ref, h_ref, *, q_tiles):
    j = pl.program_id(1)

    @pl.when(j == 0)
    def _():
        _norm_rows_into(x_ref, g_ref, h_ref, 64)

    res = jnp.dot(h_ref[...], w_ref[...], preferred_element_type=F32)

    @pl.when(j < q_tiles)
    def _():
        for nbl in range(q_ref.shape[1]):
            for p in range(q_ref.shape[2]):
                q_ref[0, nbl, p] = res[nbl * ATT_BLOCK:(nbl + 1) * ATT_BLOCK,
                                       p * PAIR_W:(p + 1) * PAIR_W].astype(q_ref.dtype)

    @pl.when(j == q_tiles)
    def _():
        kv_ref[...] = res.astype(kv_ref.dtype)


SWA_TN = 2 * KV_HEADS * HEAD_DIM


def _swa_in_proj(x, g, w, b, t, tm):
    n, d = x.shape
    tn = SWA_TN
    q_tiles = w.shape[1] // tn - 1
    tiles_per_seq = t // tm
    pairs_per_tile = tn // PAIR_W
    return pl.pallas_call(
        functools.partial(_swa_in_proj_body, q_tiles=q_tiles),
        grid=(n // tm, q_tiles + 1),
        in_specs=[
            pl.BlockSpec((tm, d), lambda i, j: (i, 0)),
            pl.BlockSpec((1, d), lambda i, j: (0, 0)),
            pl.BlockSpec((d, tn), lambda i, j: (0, j)),
        ],
        out_specs=[
            pl.BlockSpec((1, tm // ATT_BLOCK, pairs_per_tile, ATT_BLOCK, PAIR_W),
                         lambda i, j: (i // tiles_per_seq, i % tiles_per_seq, jnp.minimum(j, q_tiles - 1), 0, 0)),
            pl.BlockSpec((tm, tn), lambda i, j: (i, 0)),
        ],
        out_shape=[jax.ShapeDtypeStruct((b, t // ATT_BLOCK, ATT_HEADS // 2, ATT_BLOCK, PAIR_W), BF16),
                   jax.ShapeDtypeStruct((n, tn), BF16)],
        scratch_shapes=[pltpu.VMEM((tm, d), BF16)],
        compiler_params=_params(),
        name="swa_in_proj",
    )(x, g.reshape(1, d), w)


def _swa_body(rb_ref, sink_ref, idx_ref, q_ref, kp_ref, kc_ref, vp_ref, vc_ref, o_ref, bias_ref, sinkcol_ref):
    bi = pl.program_id(0)
    nb = pl.program_id(1)
    blk = ATT_BLOCK
    rows = PAIRS * blk

    @pl.when((bi == 0) & (nb == 0))
    def _():
        idx = idx_ref[...]
        col = lax.broadcasted_iota(jnp.int32, idx.shape, 1)

        def head_body(hd, carry):
            def bucket_body(bk, acc):
                return jnp.where(idx == bk, rb_ref[bk * ATT_HEADS + hd], acc)

            tbl = lax.fori_loop(0, N_BUCKETS, bucket_body, jnp.full(idx.shape, -jnp.inf, F32))
            kv = hd // GROUP
            half = hd % 2
            at = pl.ds(pl.multiple_of(((hd % GROUP) // 2) * blk, blk), blk)
            bias_ref[0, kv, half, at, :] = tbl
            bias_ref[1, kv, half, at, :] = jnp.where(col >= blk, tbl, -jnp.inf)
            sinkcol_ref[kv, half, at, :] = jnp.full((blk, LANES), sink_ref[hd], F32)
            return carry

        lax.fori_loop(0, ATT_HEADS, head_body, 0)

    first = (nb == 0).astype(jnp.int32)
    low = lax.broadcasted_iota(jnp.int32, (2 * blk, LANES), 1) < HEAD_DIM
    low_out = lax.broadcasted_iota(jnp.int32, (rows, LANES), 1) < HEAD_DIM
    kf = jnp.concatenate([kp_ref[0], kc_ref[0]], axis=0).astype(F32)
    vf = jnp.concatenate([vp_ref[0], vc_ref[0]], axis=0).astype(F32)
    scale = HEAD_DIM ** -0.5

    def halves(xf, kv):
        xc = xf[:, (kv // 2) * LANES:(kv // 2 + 1) * LANES]
        xr = pltpu.roll(xc, HEAD_DIM, axis=1)
        lo_src, hi_src = (xc, xr) if kv % 2 == 0 else (xr, xc)
        return (jnp.where(low, lo_src, 0.0).astype(BF16), jnp.where(low, 0.0, hi_src).astype(BF16))

    for kv in range(KV_HEADS):
        k_lo, k_hi = halves(kf, kv)
        v_lo, v_hi = halves(vf, kv)
        q4 = q_ref[kv * PAIRS:(kv + 1) * PAIRS].reshape(rows, PAIR_W) * scale

        def probs(k_half, half):
            s = lax.dot_general(q4, k_half, (((1,), (1,)), ((), ())), preferred_element_type=F32)
            s = s + bias_ref[first, kv, half]
            sink = sinkcol_ref[kv, half]
            mx = jnp.maximum(jnp.max(s, axis=-1, keepdims=True), sink)
            p = jnp.exp(s - jnp.concatenate([mx, mx], axis=1))
            den = jnp.sum(p, axis=-1, keepdims=True) + jnp.exp(sink - mx)
            return p.astype(BF16), den

        p_lo, den_lo = probs(k_lo, 0)
        p_hi, den_hi = probs(k_hi, 1)
        o4 = (jnp.dot(p_lo, v_lo, preferred_element_type=F32)
              + jnp.dot(p_hi, v_hi, preferred_element_type=F32)) / jnp.where(low_out, den_lo, den_hi)
        for pr in range(PAIRS):
            at = (kv * PAIRS + pr) * PAIR_W
            o_ref[0, :, at:at + PAIR_W] = o4[pr * blk:(pr + 1) * blk].astype(o_ref.dtype)


def _t5_bucket(dist):
    n = jnp.maximum(dist, 0)
    max_exact = N_BUCKETS // 2
    nf = jnp.maximum(n, 1).astype(F32)
    large = max_exact + (jnp.log(nf / max_exact) / math.log(MAX_DIST / max_exact)
                         * (N_BUCKETS - max_exact)).astype(jnp.int32)
    large = jnp.minimum(large, N_BUCKETS - 1)
    return jnp.where(n < max_exact, n, large)


def _swa(q, kv, sinks, rel_bias):
    b, nblk, n_pairs, blk, _ = q.shape
    t = nblk * blk
    qw = ATT_HEADS * HEAD_DIM
    kvw = KV_HEADS * HEAD_DIM
    i = jnp.arange(blk)
    j = jnp.arange(2 * blk)
    dist = (blk + i)[:, None] - j[None, :]
    idx = jnp.where((dist >= 0) & (dist < WINDOW), _t5_bucket(dist), -1).astype(jnp.int32)
    return pl.pallas_call(
        _swa_body,
        grid=(b, nblk),
        in_specs=[
            pl.BlockSpec(memory_space=pltpu.SMEM),
            pl.BlockSpec(memory_space=pltpu.SMEM),
            pl.BlockSpec((blk, 2 * blk), lambda bi, nb: (0, 0)),
            pl.BlockSpec((None, None, n_pairs, blk, PAIR_W), lambda bi, nb: (bi, nb, 0, 0, 0)),
            pl.BlockSpec((1, blk, kvw), lambda bi, nb: (bi, jnp.maximum(nb - 1, 0), 0)),
            pl.BlockSpec((1, blk, kvw), lambda bi, nb: (bi, nb, 0)),
            pl.BlockSpec((1, blk, kvw), lambda bi, nb: (bi, jnp.maximum(nb - 1, 0), 1)),
            pl.BlockSpec((1, blk, kvw), lambda bi, nb: (bi, nb, 1)),
        ],
        out_specs=pl.BlockSpec((1, blk, qw), lambda bi, nb: (bi, nb, 0)),
        out_shape=jax.ShapeDtypeStruct((b, t, qw), BF16),
        scratch_shapes=[pltpu.VMEM((2, KV_HEADS, 2, PAIRS * blk, 2 * blk), F32),
                        pltpu.VMEM((KV_HEADS, 2, PAIRS * blk, LANES), F32)],
        compiler_params=_params(),
        name="swa",
    )(rel_bias.reshape(-1), sinks, idx, q, kv, kv, kv, kv)


def kernel(x, rel_bias, norm_g, ffn_w_gu, ffn_w_down, rm_w_in, ml_conv_w, ml_gate_b, rm_head_g,
           rm_w_out, swa_w_in, swa_sinks, swa_w_out):
    b, t, d = x.shape
    n = b * t
    depth = norm_g.shape[0]
    main_cols = 4 * RET_HEADS * HEAD_W + 4 * ML_HEADS * HEAD_W
    xs = x.reshape(n, d)
    w_gu = ffn_w_gu.astype(BF16)
    w_down = ffn_w_down.astype(BF16)
    for layer in range(depth):
        g = norm_g[layer]
        if layer % 2 == 0:
            e = layer // 2
            w_in = rm_w_in[e]
            gate_cols = w_in.shape[1] - main_cols
            w_gate = jnp.pad(w_in[:, main_cols:], ((0, 0), (0, LANES - gate_cols))).astype(BF16)
            proj, gates = _norm_matmul(xs, g[0], w_in.astype(BF16), w_gate, main_cols // PROJ_TN, ROW_TILE, PROJ_TN)
            gates_row = gates[:, :gate_cols].reshape(b, t, gate_cols).transpose(0, 2, 1)
            mixed = _mixer(proj.reshape(main_cols // PROJ_TN, b, t, PROJ_TN), gates_row, ml_gate_b[e], ml_conv_w[e], rm_head_g[e])
            xs = _proj_norm_res(mixed.reshape(n, -1), rm_w_out[e].astype(BF16), xs, g[1], OUT_ROW_TILE)
        else:
            o = layer // 2
            q, kv = _swa_in_proj(xs, g[0], swa_w_in[o].astype(BF16), b, t, ROW_TILE)
            att = _swa(q, kv.reshape(b, t, -1), swa_sinks[o], rel_bias)
            xs = _proj_norm_res(att.reshape(n, -1), swa_w_out[o].astype(BF16), xs, g[1], OUT_ROW_TILE)
        xs = _ffn(xs, g[2], w_gu, w_down, g[3], layer, ROW_TILE, FFN_TF, FFN_ROW_BLOCK)
    return xs.reshape(b, t, d)
```

```python
import functools
import math

import jax
import jax.numpy as jnp
from jax import lax
from jax.experimental import pallas as pl
from jax.experimental.pallas import tpu as pltpu

F32 = jnp.float32
BF16 = jnp.bfloat16

EPS = 1e-6
CHUNK = 128
RET_HEADS = 4
ML_HEADS = 4
HEAD_W = 256
CONV_W = 4
ROPE_BASE = 10000.0
ATT_HEADS = 32
KV_HEADS = 4
HEAD_DIM = 64
WINDOW = 128
ATT_BLOCK = 128
N_BUCKETS = 32
MAX_DIST = 128
LANES = 128
SUBLANES = 8
VMEM_LIMIT = 56 * 1024 * 1024

ROW_TILE = 1024
OUT_ROW_TILE = 512
PROJ_TN = 1024
FFN_TF = 512
FFN_ROW_BLOCK = 512


def _params(vmem=VMEM_LIMIT):
    return pltpu.CompilerParams(vmem_limit_bytes=vmem)


def _rms_rows(x, g):
    ms = jnp.mean(x * x, axis=-1, keepdims=True)
    return x * lax.rsqrt(ms + EPS) * g


def _silu(x):
    return x * jax.nn.sigmoid(x)


def _norm_rows_into(x_ref, g_ref, h_ref, row_chunk):
    g = g_ref[...]

    def body(r, carry):
        rows = pl.ds(pl.multiple_of(r * row_chunk, row_chunk), row_chunk)
        h_ref[rows, :] = _rms_rows(x_ref[rows, :], g).astype(h_ref.dtype)
        return carry

    lax.fori_loop(0, x_ref.shape[0] // row_chunk, body, 0)


def _norm_matmul_body(x_ref, g_ref, w_ref, ws_ref, o_ref, os_ref, h_ref):
    @pl.when(pl.program_id(1) == 0)
    def _():
        _norm_rows_into(x_ref, g_ref, h_ref, 64)
        os_ref[...] = jnp.dot(h_ref[...], ws_ref[...], preferred_element_type=F32)

    o_ref[...] = jnp.dot(h_ref[...], w_ref[...], preferred_element_type=F32).astype(o_ref.dtype)


def _norm_matmul(x, g, w, w_side, tiles, tm, tn):
    n, d = x.shape
    ns = w_side.shape[1]
    return pl.pallas_call(
        _norm_matmul_body,
        grid=(n // tm, tiles),
        in_specs=[
            pl.BlockSpec((tm, d), lambda i, j: (i, 0)),
            pl.BlockSpec((1, d), lambda i, j: (0, 0)),
            pl.BlockSpec((d, tn), lambda i, j: (0, j)),
            pl.BlockSpec((d, ns), lambda i, j: (0, 0)),
        ],
        out_specs=[pl.BlockSpec((None, tm, tn), lambda i, j: (j, i, 0)),
                   pl.BlockSpec((tm, ns), lambda i, j: (i, 0))],
        out_shape=[jax.ShapeDtypeStruct((tiles, n, tn), BF16), jax.ShapeDtypeStruct((n, ns), F32)],
        scratch_shapes=[pltpu.VMEM((tm, d), BF16)],
        compiler_params=_params(),
        name="norm_matmul",
    )(x, g.reshape(1, d), w, w_side)


def _proj_norm_res_body(a_ref, w_ref, x_ref, g_ref, o_ref):
    y = jnp.dot(a_ref[...], w_ref[...], preferred_element_type=F32)
    o_ref[...] = x_ref[...] + _rms_rows(y, g_ref[...])


def _proj_norm_res(a, w, x, g, tm):
    n, d = x.shape
    return pl.pallas_call(
        _proj_norm_res_body,
        grid=(n // tm,),
        in_specs=[
            pl.BlockSpec((tm, a.shape[1]), lambda i: (i, 0)),
            pl.BlockSpec(w.shape, lambda i: (0, 0)),
            pl.BlockSpec((tm, d), lambda i: (i, 0)),
            pl.BlockSpec((1, d), lambda i: (0, 0)),
        ],
        out_specs=pl.BlockSpec((tm, d), lambda i: (i, 0)),
        out_shape=jax.ShapeDtypeStruct((n, d), F32),
        compiler_params=_params(),
        name="proj_norm_res",
    )(a, w, x, g.reshape(1, d))


def _ffn_body(x_ref, g_in_ref, wg_ref, wu_ref, wd_ref, g_out_ref, o_ref, h_ref, *, row_block):
    f = pl.program_id(1)

    @pl.when(f == 0)
    def _():
        _norm_rows_into(x_ref, g_in_ref, h_ref, 64)
        o_ref[...] = jnp.zeros_like(o_ref)

    for r in range(x_ref.shape[0] // row_block):
        rows = slice(r * row_block, (r + 1) * row_block)
        h = h_ref[rows, :]
        gate = jnp.dot(h, wg_ref[...], preferred_element_type=F32)
        up = jnp.dot(h, wu_ref[...], preferred_element_type=F32)
        act = (_silu(gate) * up).astype(BF16)
        o_ref[rows, :] += jnp.dot(act, wd_ref[...], preferred_element_type=F32)

    @pl.when(f == pl.num_programs(1) - 1)
    def _():
        g_out = g_out_ref[...]
        row_chunk = 64

        def body(r, carry):
            rows = pl.ds(pl.multiple_of(r * row_chunk, row_chunk), row_chunk)
            o_ref[rows, :] = x_ref[rows, :] + _rms_rows(o_ref[rows, :], g_out)
            return carry

        lax.fori_loop(0, x_ref.shape[0] // row_chunk, body, 0)


def _ffn(x, g_in, w_gu, w_down, g_out, layer, tm, tf, row_block):
    n, d = x.shape
    nf = w_down.shape[1] // tf
    return pl.pallas_call(
        functools.partial(_ffn_body, row_block=row_block),
        grid=(n // tm, nf),
        in_specs=[
            pl.BlockSpec((tm, d), lambda i, f: (i, 0)),
            pl.BlockSpec((1, d), lambda i, f: (0, 0)),
            pl.BlockSpec((None, d, tf), lambda i, f: (layer, 0, f)),
            pl.BlockSpec((None, d, tf), lambda i, f: (layer, 0, f + nf)),
            pl.BlockSpec((None, tf, d), lambda i, f: (layer, f, 0)),
            pl.BlockSpec((1, d), lambda i, f: (0, 0)),
        ],
        out_specs=pl.BlockSpec((tm, d), lambda i, f: (i, 0)),
        out_shape=jax.ShapeDtypeStruct((n, d), F32),
        scratch_shapes=[pltpu.VMEM((tm, d), BF16)],
        compiler_params=_params(),
        name="ffn",
    )(x, g_in.reshape(1, d), w_gu, w_gu, w_down, g_out.reshape(1, d))


def _log_sigmoid(x):
    return jnp.minimum(x, 0.0) - jnp.log1p(jnp.exp(-jnp.abs(x)))


_NT = (((1,), (1,)), ((), ()))
_TN = (((0,), (0,)), ((), ()))


def _retention_head(h, q_ref, k_ref, v_ref, g_ref, cos, sin, dm_ref, wq_ref, wk_ref, gl_ref, hg_ref,
                    o_ref, state_ref):
    cols = slice(h * HEAD_W, (h + 1) * HEAD_W)
    half = HEAD_W // 2

    def rot(x):
        x = x.astype(F32)
        x1, x2 = x[:, :half], x[:, half:]
        return jnp.concatenate([x1 * cos - x2 * sin, x2 * cos + x1 * sin], axis=-1)

    q = rot(q_ref[0, :, cols])
    k = rot(k_ref[0, :, cols]) * (HEAD_W ** -0.5)
    v = v_ref[0, :, cols]
    qb = q.astype(BF16)
    s = lax.dot_general(qb, k.astype(BF16), (((1,), (1,)), ((), ())),
                        preferred_element_type=F32) * dm_ref[h]
    intra = jnp.dot(s.astype(BF16), v, preferred_element_type=F32)
    state = state_ref[h]
    inter = jnp.dot(qb, state.astype(BF16), preferred_element_type=F32) * wq_ref[h]
    out = intra + inter
    kw = (k * wk_ref[h]).astype(BF16)
    kv = lax.dot_general(kw, v, (((0,), (0,)), ((), ())), preferred_element_type=F32)
    state_ref[h] = state * gl_ref[h] + kv
    y = _rms_rows(out, hg_ref[:, cols])
    o_ref[0, :, cols] = (y * _silu(g_ref[0, :, cols].astype(F32))).astype(o_ref.dtype)


def _causal_conv(cur_ref, prev_ref, first, shift_ref, w):
    cur = cur_ref[0]
    prev = jnp.where(first, jnp.zeros_like(cur), prev_ref[0])
    shifted = jnp.dot(shift_ref[...], jnp.concatenate([prev, cur], axis=0), preferred_element_type=F32)
    y = cur.astype(F32) * w[CONV_W - 1:CONV_W]
    for s in range(1, CONV_W):
        y = y + shifted[(s - 1) * CHUNK:s * CHUNK] * w[CONV_W - 1 - s:CONV_W - s]
    return y


def _mlstm_head(h, q_all, k_all, v_ref, og_ref, gx, b_rows, lower_tri, hg_ref, o_ref, c_ref, n_ref, m_ref):
    cols = slice(h * HEAD_W, (h + 1) * HEAD_W)
    q = q_all[:, cols]
    k = k_all[:, cols]
    v = v_ref[0, :, cols]

    i_row = gx[h:h + 1]
    lf_row = gx[ML_HEADS + h:ML_HEADS + h + 1]
    b_row = b_rows[ML_HEADS + h:ML_HEADS + h + 1]
    b_cols = lax.dot_general(lower_tri, jnp.broadcast_to(lf_row, (CHUNK, CHUNK)), _NT,
                             precision=lax.Precision.HIGHEST, preferred_element_type=F32)
    i_cols = jnp.broadcast_to(i_row, (CHUNK, CHUNK)).T
    rowi = lax.broadcasted_iota(jnp.int32, (CHUNK, CHUNK), 0)
    coli = lax.broadcasted_iota(jnp.int32, (CHUNK, CHUNK), 1)
    log_d = jnp.where(rowi >= coli, b_cols - b_row + i_row, -jnp.inf)
    m_prev = m_ref[h]
    inter_log = b_cols + m_prev
    m_t = jnp.maximum(inter_log, jnp.max(log_d, axis=-1, keepdims=True))
    d_mat = jnp.exp(log_d - m_t)
    w_inter = jnp.exp(inter_log - m_t)

    def wide(a):
        return jnp.concatenate([a] * (HEAD_W // LANES), axis=1)

    qb = q.astype(BF16)
    s = lax.dot_general(qb, k.astype(BF16), _NT, preferred_element_type=F32) * d_mat
    c_state = c_ref[h]
    n_state = n_ref[h]
    num = (jnp.dot(s.astype(BF16), v, preferred_element_type=F32)
           + wide(w_inter) * jnp.dot(qb, c_state.astype(BF16), preferred_element_type=F32))
    den = (jnp.sum(s, axis=-1, keepdims=True)
           + w_inter * jnp.sum(q * n_state, axis=-1, keepdims=True))
    hid = num / wide(jnp.maximum(jnp.abs(den), jnp.exp(-m_t)))

    b_last = b_cols[CHUNK - 1:CHUNK]
    log_w = b_last - b_cols + i_cols
    m_new = jnp.maximum(b_last + m_prev, jnp.max(log_w, axis=0, keepdims=True))
    w = jnp.exp(log_w - m_new)
    decay = wide(jnp.exp(b_last + m_prev - m_new))
    kw = k * wide(w)
    c_ref[h] = decay * c_state + lax.dot_general(kw.astype(BF16), v, _TN, preferred_element_type=F32)
    n_ref[h] = decay * n_state + jnp.sum(kw, axis=0, keepdims=True)
    m_ref[h] = m_new

    out_cols = slice(RET_HEADS * HEAD_W + h * HEAD_W, RET_HEADS * HEAD_W + (h + 1) * HEAD_W)
    y = _rms_rows(hid, hg_ref[:, out_cols])
    o_ref[0, :, out_cols] = (y * jax.nn.sigmoid(og_ref[0, :, cols].astype(F32))).astype(o_ref.dtype)


def _mixer_body(gl_ref, rq_ref, rk_ref, rv_ref, rg_ref, mq_ref, mqp_ref, mk_ref, mkp_ref, mv_ref, mo_ref,
                cos_ref, sin_ref, dm_ref, wq_ref, wk_ref, gr_ref, gb_ref, cw_ref, shift_ref, hg_ref,
                o_ref, r_ref, c_ref, n_ref, m_ref):
    first = pl.program_id(1) == 0

    @pl.when(first)
    def _():
        r_ref[...] = jnp.zeros_like(r_ref)
        c_ref[...] = jnp.zeros_like(c_ref)
        n_ref[...] = jnp.zeros_like(n_ref)
        m_ref[...] = jnp.zeros_like(m_ref)

    cos = cos_ref[...]
    sin = sin_ref[...]
    for h in range(RET_HEADS):
        _retention_head(h, rq_ref, rk_ref, rv_ref, rg_ref, cos, sin, dm_ref, wq_ref, wk_ref, gl_ref,
                        hg_ref, o_ref, r_ref)

    ml_w = ML_HEADS * HEAD_W
    q_all = _silu(_causal_conv(mq_ref, mqp_ref, first, shift_ref, cw_ref[:, :ml_w]))
    k_all = _silu(_causal_conv(mk_ref, mkp_ref, first, shift_ref, cw_ref[:, ml_w:])) * (HEAD_W ** -0.5)
    gates = gr_ref[0] + gb_ref[...]
    is_input = lax.broadcasted_iota(jnp.int32, gates.shape, 0) < ML_HEADS
    gx = jnp.where(is_input, gates, _log_sigmoid(gates))
    rowi = lax.broadcasted_iota(jnp.int32, (CHUNK, CHUNK), 0)
    coli = lax.broadcasted_iota(jnp.int32, (CHUNK, CHUNK), 1)
    upper_tri = jnp.where(rowi <= coli, 1.0, 0.0).astype(F32)
    lower_tri = jnp.where(rowi >= coli, 1.0, 0.0).astype(F32)
    b_rows = jnp.dot(gx, upper_tri, precision=lax.Precision.HIGHEST, preferred_element_type=F32)
    for h in range(ML_HEADS):
        _mlstm_head(h, q_all, k_all, mv_ref, mo_ref, gx, b_rows, lower_tri, hg_ref, o_ref, c_ref, n_ref, m_ref)


def _mixer(proj, gates_row, gate_b, conv_w, head_g):
    _, b, t, _ = proj.shape
    nc = t // CHUNK
    ret_w = RET_HEADS * HEAD_W
    ml_w = ML_HEADS * HEAD_W
    assert ret_w == ml_w == proj.shape[3]
    log_g = jnp.log1p(-jnp.exp2(-5.0 - jnp.arange(RET_HEADS, dtype=F32)))
    idx = jnp.arange(CHUNK, dtype=F32)
    diff = idx[:, None] - idx[None, :]
    dmask = jnp.where(diff >= 0, jnp.exp(log_g[:, None, None] * jnp.maximum(diff, 0.0)), 0.0)
    w_k = jnp.exp(log_g[:, None] * (CHUNK - 1.0 - idx)[None, :])
    w_q = jnp.exp(log_g[:, None] * (idx + 1.0)[None, :])
    w_k = jnp.broadcast_to(w_k[..., None], (RET_HEADS, CHUNK, HEAD_W))
    w_q = jnp.broadcast_to(w_q[..., None], (RET_HEADS, CHUNK, HEAD_W))
    g_l = jnp.exp(log_g * CHUNK)
    half = HEAD_W // 2
    inv = 1.0 / (ROPE_BASE ** jnp.linspace(0.0, 1.0, half, dtype=F32))
    ang = jnp.arange(t).astype(F32)[:, None] * inv[None, :]
    cos, sin = jnp.cos(ang), jnp.sin(ang)
    r = jnp.arange((CONV_W - 1) * CHUNK)
    src = CHUNK + r % CHUNK - (r // CHUNK + 1)
    shift = (jnp.arange(2 * CHUNK)[None, :] == src[:, None]).astype(BF16)

    def group(gi):
        return pl.BlockSpec((None, 1, CHUNK, ret_w), lambda bi, c: (gi, bi, c, 0))

    def prev_group(gi):
        return pl.BlockSpec((None, 1, CHUNK, ml_w), lambda bi, c: (gi, bi, jnp.maximum(c - 1, 0), 0))

    def whole(a):
        return pl.BlockSpec(a.shape, lambda bi, c: (0,) * a.ndim)

    gb = gate_b.reshape(2 * ML_HEADS, 1)
    hg = head_g.reshape(1, ret_w + ml_w)
    return pl.pallas_call(
        _mixer_body,
        grid=(b, nc),
        in_specs=[
            pl.BlockSpec(memory_space=pltpu.SMEM),
            group(0), group(1), group(2), group(3),
            group(4), prev_group(4), group(5), prev_group(5), group(6), group(7),
            pl.BlockSpec((CHUNK, half), lambda bi, c: (c, 0)),
            pl.BlockSpec((CHUNK, half), lambda bi, c: (c, 0)),
            whole(dmask), whole(w_q), whole(w_k),
            pl.BlockSpec((1, 2 * ML_HEADS, CHUNK), lambda bi, c: (bi, 0, c)),
            whole(gb), whole(conv_w), whole(shift), whole(hg),
        ],
        out_specs=pl.BlockSpec((1, CHUNK, ret_w + ml_w), lambda bi, c: (bi, c, 0)),
        out_shape=jax.ShapeDtypeStruct((b, t, ret_w + ml_w), BF16),
        scratch_shapes=[pltpu.VMEM((RET_HEADS, HEAD_W, HEAD_W), F32),
                        pltpu.VMEM((ML_HEADS, HEAD_W, HEAD_W), F32),
                        pltpu.VMEM((ML_HEADS, 1, HEAD_W), F32),
                        pltpu.VMEM((ML_HEADS, 1, LANES), F32)],
        compiler_params=_params(),
        name="mixer",
    )(g_l, proj, proj, proj, proj, proj, proj, proj, proj, proj, proj, cos, sin, dmask, w_q, w_k,
      gates_row, gb, conv_w, shift, hg)


PAIR_W = 2 * HEAD_DIM
GROUP = ATT_HEADS // KV_HEADS
PAIRS = GROUP // 2


def _swa_in_proj_body(x_ref, g_ref, w_ref, q_ref, kv_ref, h_ref, *, q_tiles):
    j = pl.program_id(1)

    @pl.when(j == 0)
    def _():
        _norm_rows_into(x_ref, g_ref, h_ref, 64)

    res = jnp.dot(h_ref[...], w_ref[...], preferred_element_type=F32)

    @pl.when(j < q_tiles)
    def _():
        for nbl in range(q_ref.shape[1]):
            for p in range(q_ref.shape[2]):
                q_ref[0, nbl, p] = res[nbl * ATT_BLOCK:(nbl + 1) * ATT_BLOCK,
                                       p * PAIR_W:(p + 1) * PAIR_W].astype(q_ref.dtype)

    @pl.when(j == q_tiles)
    def _():
        kv_ref[...] = res.astype(kv_ref.dtype)


SWA_TN = 2 * KV_HEADS * HEAD_DIM


def _swa_in_proj(x, g, w, b, t, tm):
    n, d = x.shape
    tn = SWA_TN
    q_tiles = w.shape[1] // tn - 1
    tiles_per_seq = t // tm
    pairs_per_tile = tn // PAIR_W
    return pl.pallas_call(
        functools.partial(_swa_in_proj_body, q_tiles=q_tiles),
        grid=(n // tm, q_tiles + 1),
        in_specs=[
            pl.BlockSpec((tm, d), lambda i, j: (i, 0)),
            pl.BlockSpec((1, d), lambda i, j: (0, 0)),
            pl.BlockSpec((d, tn), lambda i, j: (0, j)),
        ],
        out_specs=[
            pl.BlockSpec((1, tm // ATT_BLOCK, pairs_per_tile, ATT_BLOCK, PAIR_W),
                         lambda i, j: (i // tiles_per_seq, i % tiles_per_seq, jnp.minimum(j, q_tiles - 1), 0, 0)),
            pl.BlockSpec((tm, tn), lambda i, j: (i, 0)),
        ],
        out_shape=[jax.ShapeDtypeStruct((b, t // ATT_BLOCK, ATT_HEADS // 2, ATT_BLOCK, PAIR_W), BF16),
                   jax.ShapeDtypeStruct((n, tn), BF16)],
        scratch_shapes=[pltpu.VMEM((tm, d), BF16)],
        compiler_params=_params(),
        name="swa_in_proj",
    )(x, g.reshape(1, d), w)


def _swa_body(rb_ref, sink_ref, idx_ref, q_ref, kp_ref, kc_ref, vp_ref, vc_ref, o_ref, bias_ref, sinkcol_ref):
    bi = pl.program_id(0)
    nb = pl.program_id(1)
    blk = ATT_BLOCK
    rows = PAIRS * blk

    @pl.when((bi == 0) & (nb == 0))
    def _():
        idx = idx_ref[...]
        col = lax.broadcasted_iota(jnp.int32, idx.shape, 1)

        def head_body(hd, carry):
            def bucket_body(bk, acc):
                return jnp.where(idx == bk, rb_ref[bk * ATT_HEADS + hd], acc)

            tbl = lax.fori_loop(0, N_BUCKETS, bucket_body, jnp.full(idx.shape, -jnp.inf, F32))
            kv = hd // GROUP
            half = hd % 2
            at = pl.ds(pl.multiple_of(((hd % GROUP) // 2) * blk, blk), blk)
            bias_ref[0, kv, half, at, :] = tbl
            bias_ref[1, kv, half, at, :] = jnp.where(col >= blk, tbl, -jnp.inf)
            sinkcol_ref[kv, half, at, :] = jnp.full((blk, LANES), sink_ref[hd], F32)
            return carry

        lax.fori_loop(0, ATT_HEADS, head_body, 0)

    first = (nb == 0).astype(jnp.int32)
    low = lax.broadcasted_iota(jnp.int32, (2 * blk, LANES), 1) < HEAD_DIM
    low_out = lax.broadcasted_iota(jnp.int32, (rows, LANES), 1) < HEAD_DIM
    kf = jnp.concatenate([kp_ref[0], kc_ref[0]], axis=0).astype(F32)
    vf = jnp.concatenate([vp_ref[0], vc_ref[0]], axis=0).astype(F32)
    scale = HEAD_DIM ** -0.5

    def halves(xf, kv):
        xc = xf[:, (kv // 2) * LANES:(kv // 2 + 1) * LANES]
        xr = pltpu.roll(xc, HEAD_DIM, axis=1)
        lo_src, hi_src = (xc, xr) if kv % 2 == 0 else (xr, xc)
        return (jnp.where(low, lo_src, 0.0).astype(BF16), jnp.where(low, 0.0, hi_src).astype(BF16))

    for kv in range(KV_HEADS):
        k_lo, k_hi = halves(kf, kv)
        v_lo, v_hi = halves(vf, kv)
        q4 = q_ref[kv * PAIRS:(kv + 1) * PAIRS].reshape(rows, PAIR_W) * scale

        def probs(k_half, half):
            s = lax.dot_general(q4, k_half, (((1,), (1,)), ((), ())), preferred_element_type=F32)
            s = s + bias_ref[first, kv, half]
            sink = sinkcol_ref[kv, half]
            mx = jnp.maximum(jnp.max(s, axis=-1, keepdims=True), sink)
            p = jnp.exp(s - jnp.concatenate([mx, mx], axis=1))
            den = jnp.sum(p, axis=-1, keepdims=True) + jnp.exp(sink - mx)
            return p.astype(BF16), den

        p_lo, den_lo = probs(k_lo, 0)
        p_hi, den_hi = probs(k_hi, 1)
        o4 = (jnp.dot(p_lo, v_lo, preferred_element_type=F32)
              + jnp.dot(p_hi, v_hi, preferred_element_type=F32)) / jnp.where(low_out, den_lo, den_hi)
        for pr in range(PAIRS):
            at = (kv * PAIRS + pr) * PAIR_W
            o_ref[0, :, at:at + PAIR_W] = o4[pr * blk:(pr + 1) * blk].astype(o_ref.dtype)


def _t5_bucket(dist):
    n = jnp.maximum(dist, 0)
    max_exact = N_BUCKETS // 2
    nf = jnp.maximum(n, 1).astype(F32)
    large = max_exact + (jnp.log(nf / max_exact) / math.log(MAX_DIST / max_exact)
                         * (N_BUCKETS - max_exact)).astype(jnp.int32)
    large = jnp.minimum(large, N_BUCKETS - 1)
    return jnp.where(n < max_exact, n, large)


def _swa(q, kv, sinks, rel_bias):
    b, nblk, n_pairs, blk, _ = q.shape
    t = nblk * blk
    qw = ATT_HEADS * HEAD_DIM
    kvw = KV_HEADS * HEAD_DIM
    i = jnp.arange(blk)
    j = jnp.arange(2 * blk)
    dist = (blk + i)[:, None] - j[None, :]
    idx = jnp.where((dist >= 0) & (dist < WINDOW), _t5_bucket(dist), -1).astype(jnp.int32)
    return pl.pallas_call(
        _swa_body,
        grid=(b, nblk),
        in_specs=[
            pl.BlockSpec(memory_space=pltpu.SMEM),
            pl.BlockSpec(memory_space=pltpu.SMEM),
            pl.BlockSpec((blk, 2 * blk), lambda bi, nb: (0, 0)),
            pl.BlockSpec((None, None, n_pairs, blk, PAIR_W), lambda bi, nb: (bi, nb, 0, 0, 0)),
            pl.BlockSpec((1, blk, kvw), lambda bi, nb: (bi, jnp.maximum(nb - 1, 0), 0)),
            pl.BlockSpec((1, blk, kvw), lambda bi, nb: (bi, nb, 0)),
            pl.BlockSpec((1, blk, kvw), lambda bi, nb: (bi, jnp.maximum(nb - 1, 0), 1)),
            pl.BlockSpec((1, blk, kvw), lambda bi, nb: (bi, nb, 1)),
        ],
        out_specs=pl.BlockSpec((1, blk, qw), lambda bi, nb: (bi, nb, 0)),
        out_shape=jax.ShapeDtypeStruct((b, t, qw), BF16),
        scratch_shapes=[pltpu.VMEM((2, KV_HEADS, 2, PAIRS * blk, 2 * blk), F32),
                        pltpu.VMEM((KV_HEADS, 2, PAIRS * blk, LANES), F32)],
        compiler_params=_params(),
        name="swa",
    )(rel_bias.reshape(-1), sinks, idx, q, kv, kv, kv, kv)


def kernel(x, rel_bias, norm_g, ffn_w_gu, ffn_w_down, rm_w_in, ml_conv_w, ml_gate_b, rm_head_g,
           rm_w_out, swa_w_in, swa_sinks, swa_w_out):
    b, t, d = x.shape
    n = b * t
    depth = norm_g.shape[0]
    main_cols = 4 * RET_HEADS * HEAD_W + 4 * ML_HEADS * HEAD_W
    xs = x.reshape(n, d)
    w_gu = ffn_w_gu.astype(BF16)
    w_down = ffn_w_down.astype(BF16)
    for layer in range(depth):
        g = norm_g[layer]
        if layer % 2 == 0:
            e = layer // 2
            w_in = rm_w_in[e]
            gate_cols = w_in.shape[1] - main_cols
            w_gate = jnp.pad(w_in[:, main_cols:], ((0, 0), (0, LANES - gate_cols))).astype(BF16)
            proj, gates = _norm_matmul(xs, g[0], w_in.astype(BF16), w_gate, main_cols // PROJ_TN, ROW_TILE, PROJ_TN)
            gates_row = gates[:, :gate_cols].reshape(b, t, gate_cols).transpose(0, 2, 1)
            mixed = _mixer(proj.reshape(main_cols // PROJ_TN, b, t, PROJ_TN), gates_row, ml_gate_b[e], ml_conv_w[e], rm_head_g[e])
            xs = _proj_norm_res(mixed.reshape(n, -1), rm_w_out[e].astype(BF16), xs, g[1], OUT_ROW_TILE)
        else:
            o = layer // 2
            q, kv = _swa_in_proj(xs, g[0], swa_w_in[o].astype(BF16), b, t, ROW_TILE)
            att = _swa(q, kv.reshape(b, t, -1), swa_sinks[o], rel_bias)
            xs = _proj_norm_res(att.reshape(n, -1), swa_w_out[o].astype(BF16), xs, g[1], OUT_ROW_TILE)
        xs = _ffn(xs, g[2], w_gu, w_down, g[3], layer, ROW_TILE, FFN_TF, FFN_ROW_BLOCK)
    return xs.reshape(b, t, d)
```

```python
import functools
import math

import jax
import jax.numpy as jnp
from jax import lax
from jax.experimental import pallas as pl
from jax.experimental.pallas import tpu as pltpu

F32 = jnp.float32
BF16 = jnp.bfloat16

EPS = 1e-6
CHUNK = 128
RET_HEADS = 4
ML_HEADS = 4
HEAD_W = 256
CONV_W = 4
ROPE_BASE = 10000.0
ATT_HEADS = 32
KV_HEADS = 4
HEAD_DIM = 64
WINDOW = 128
ATT_BLOCK = 128
N_BUCKETS = 32
MAX_DIST = 128
LANES = 128
SUBLANES = 8
VMEM_LIMIT = 56 * 1024 * 1024

ROW_TILE = 1024
OUT_ROW_TILE = 512
PROJ_TN = 1024
FFN_TF = 512
FFN_ROW_BLOCK = 512


def _params(vmem=VMEM_LIMIT):
    return pltpu.CompilerParams(vmem_limit_bytes=vmem)


def _rms_rows(x, g):
    ms = jnp.mean(x * x, axis=-1, keepdims=True)
    return x * lax.rsqrt(ms + EPS) * g


def _silu(x):
    return x * jax.nn.sigmoid(x)


def _norm_rows_into(x_ref, g_ref, h_ref, row_chunk):
    g = g_ref[...]

    def body(r, carry):
        rows = pl.ds(pl.multiple_of(r * row_chunk, row_chunk), row_chunk)
        h_ref[rows, :] = _rms_rows(x_ref[rows, :], g).astype(h_ref.dtype)
        return carry

    lax.fori_loop(0, x_ref.shape[0] // row_chunk, body, 0)


def _norm_matmul_body(x_ref, g_ref, w_ref, ws_ref, o_ref, os_ref, h_ref):
    @pl.when(pl.program_id(1) == 0)
    def _():
        _norm_rows_into(x_ref, g_ref, h_ref, 64)
        os_ref[...] = jnp.dot(h_ref[...], ws_ref[...], preferred_element_type=F32)

    o_ref[...] = jnp.dot(h_ref[...], w_ref[...], preferred_element_type=F32).astype(o_ref.dtype)


def _norm_matmul(x, g, w, w_side, tiles, tm, tn):
    n, d = x.shape
    ns = w_side.shape[1]
    return pl.pallas_call(
        _norm_matmul_body,
        grid=(n // tm, tiles),
        in_specs=[
            pl.BlockSpec((tm, d), lambda i, j: (i, 0)),
            pl.BlockSpec((1, d), lambda i, j: (0, 0)),
            pl.BlockSpec((d, tn), lambda i, j: (0, j)),
            pl.BlockSpec((d, ns), lambda i, j: (0, 0)),
        ],
        out_specs=[pl.BlockSpec((None, tm, tn), lambda i, j: (j, i, 0)),
                   pl.BlockSpec((tm, ns), lambda i, j: (i, 0))],
        out_shape=[jax.ShapeDtypeStruct((tiles, n, tn), BF16), jax.ShapeDtypeStruct((n, ns), F32)],
        scratch_shapes=[pltpu.VMEM((tm, d), BF16)],
        compiler_params=_params(),
        name="norm_matmul",
    )(x, g.reshape(1, d), w, w_side)


def _proj_norm_res_body(a_ref, w_ref, x_ref, g_ref, o_ref):
    y = jnp.dot(a_ref[...], w_ref[...], preferred_element_type=F32)
    o_ref[...] = x_ref[...] + _rms_rows(y, g_ref[...])


def _proj_norm_res(a, w, x, g, tm):
    n, d = x.shape
    return pl.pallas_call(
        _proj_norm_res_body,
        grid=(n // tm,),
        in_specs=[
            pl.BlockSpec((tm, a.shape[1]), lambda i: (i, 0)),
            pl.BlockSpec(w.shape, lambda i: (0, 0)),
            pl.BlockSpec((tm, d), lambda i: (i, 0)),
            pl.BlockSpec((1, d), lambda i: (0, 0)),
        ],
        out_specs=pl.BlockSpec((tm, d), lambda i: (i, 0)),
        out_shape=jax.ShapeDtypeStruct((n, d), F32),
        compiler_params=_params(),
        name="proj_norm_res",
    )(a, w, x, g.reshape(1, d))


def _ffn_body(x_ref, g_in_ref, wg_ref, wu_ref, wd_ref, g_out_ref, o_ref, h_ref, *, row_block):
    f = pl.program_id(1)
    last = pl.num_programs(1) - 1
    blocks = [slice(r * row_block, (r + 1) * row_block) for r in range(x_ref.shape[0] // row_block)]

    def partial_down(rows):
        h = h_ref[rows, :]
        gate = jnp.dot(h, wg_ref[...], preferred_element_type=F32)
        up = jnp.dot(h, wu_ref[...], preferred_element_type=F32)
        act = (_silu(gate) * up).astype(BF16)
        return jnp.dot(act, wd_ref[...], preferred_element_type=F32)

    @pl.when(f == 0)
    def _():
        for rows in blocks:
            h_ref[rows, :] = _rms_rows(x_ref[rows, :], g_in_ref[...]).astype(h_ref.dtype)
            o_ref[rows, :] = partial_down(rows)

    @pl.when((f > 0) & (f < last))
    def _():
        for rows in blocks:
            o_ref[rows, :] += partial_down(rows)

    @pl.when(f == last)
    def _():
        for rows in blocks:
            y = o_ref[rows, :] + partial_down(rows)
            o_ref[rows, :] = x_ref[rows, :] + _rms_rows(y, g_out_ref[...])


def _ffn(x, g_in, w_gu, w_down, g_out, layer, tm, tf, row_block):
    n, d = x.shape
    nf = w_down.shape[1] // tf
    return pl.pallas_call(
        functools.partial(_ffn_body, row_block=row_block),
        grid=(n // tm, nf),
        in_specs=[
            pl.BlockSpec((tm, d), lambda i, f: (i, 0)),
            pl.BlockSpec((1, d), lambda i, f: (0, 0)),
            pl.BlockSpec((None, d, tf), lambda i, f: (layer, 0, f)),
            pl.BlockSpec((None, d, tf), lambda i, f: (layer, 0, f + nf)),
            pl.BlockSpec((None, tf, d), lambda i, f: (layer, f, 0)),
            pl.BlockSpec((1, d), lambda i, f: (0, 0)),
        ],
        out_specs=pl.BlockSpec((tm, d), lambda i, f: (i, 0)),
        out_shape=jax.ShapeDtypeStruct((n, d), F32),
        scratch_shapes=[pltpu.VMEM((tm, d), BF16)],
        compiler_params=_params(),
        name="ffn",
    )(x, g_in.reshape(1, d), w_gu, w_gu, w_down, g_out.reshape(1, d))


def _log_sigmoid(x):
    return jnp.minimum(x, 0.0) - jnp.log1p(jnp.exp(-jnp.abs(x)))


_NT = (((1,), (1,)), ((), ()))
_TN = (((0,), (0,)), ((), ()))


def _retention_head(h, q_ref, k_ref, v_ref, g_ref, cos, sin, dm_ref, wq_ref, wk_ref, gl_ref, hg_ref,
                    o_ref, state_ref):
    cols = slice(h * HEAD_W, (h + 1) * HEAD_W)
    half = HEAD_W // 2

    def rot(x):
        x = x.astype(F32)
        x1, x2 = x[:, :half], x[:, half:]
        return jnp.concatenate([x1 * cos - x2 * sin, x2 * cos + x1 * sin], axis=-1)

    q = rot(q_ref[0, :, cols])
    k = rot(k_ref[0, :, cols]) * (HEAD_W ** -0.5)
    v = v_ref[0, :, cols]
    qb = q.astype(BF16)
    s = lax.dot_general(qb, k.astype(BF16), (((1,), (1,)), ((), ())),
                        preferred_element_type=F32) * dm_ref[h]
    intra = jnp.dot(s.astype(BF16), v, preferred_element_type=F32)
    state = state_ref[h]
    inter = jnp.dot(qb, state.astype(BF16), preferred_element_type=F32) * wq_ref[h]
    out = intra + inter
    kw = (k * wk_ref[h]).astype(BF16)
    kv = lax.dot_general(kw, v, (((0,), (0,)), ((), ())), preferred_element_type=F32)
    state_ref[h] = state * gl_ref[h] + kv
    y = _rms_rows(out, hg_ref[:, cols])
    o_ref[0, :, cols] = (y * _silu(g_ref[0, :, cols].astype(F32))).astype(o_ref.dtype)


def _causal_conv(cur_ref, prev_ref, first, shift_ref, w):
    cur = cur_ref[0]
    prev = jnp.where(first, jnp.zeros_like(cur), prev_ref[0])
    shifted = jnp.dot(shift_ref[...], jnp.concatenate([prev, cur], axis=0), preferred_element_type=F32)
    y = cur.astype(F32) * w[CONV_W - 1:CONV_W]
    for s in range(1, CONV_W):
        y = y + shifted[(s - 1) * CHUNK:s * CHUNK] * w[CONV_W - 1 - s:CONV_W - s]
    return y


def _mlstm_head(h, q_all, k_all, v_ref, og_ref, gx, b_rows, lower_tri, hg_ref, o_ref, c_ref, n_ref, m_ref):
    cols = slice(h * HEAD_W, (h + 1) * HEAD_W)
    q = q_all[:, cols]
    k = k_all[:, cols]
    v = v_ref[0, :, cols]

    i_row = gx[h:h + 1]
    lf_row = gx[ML_HEADS + h:ML_HEADS + h + 1]
    b_row = b_rows[ML_HEADS + h:ML_HEADS + h + 1]
    b_cols = lax.dot_general(lower_tri, jnp.broadcast_to(lf_row, (CHUNK, CHUNK)), _NT,
                             precision=lax.Precision.HIGHEST, preferred_element_type=F32)
    i_cols = jnp.broadcast_to(i_row, (CHUNK, CHUNK)).T
    rowi = lax.broadcasted_iota(jnp.int32, (CHUNK, CHUNK), 0)
    coli = lax.broadcasted_iota(jnp.int32, (CHUNK, CHUNK), 1)
    log_d = jnp.where(rowi >= coli, b_cols - b_row + i_row, -jnp.inf)
    m_prev = m_ref[h]
    inter_log = b_cols + m_prev
    m_t = jnp.maximum(inter_log, jnp.max(log_d, axis=-1, keepdims=True))
    d_mat = jnp.exp(log_d - m_t)
    w_inter = jnp.exp(inter_log - m_t)

    def wide(a):
        return jnp.concatenate([a] * (HEAD_W // LANES), axis=1)

    qb = q.astype(BF16)
    s = lax.dot_general(qb, k.astype(BF16), _NT, preferred_element_type=F32) * d_mat
    c_state = c_ref[h]
    n_state = n_ref[h]
    num = (jnp.dot(s.astype(BF16), v, preferred_element_type=F32)
           + wide(w_inter) * jnp.dot(qb, c_state.astype(BF16), preferred_element_type=F32))
    den = (jnp.sum(s, axis=-1, keepdims=True)
           + w_inter * jnp.sum(q * n_state, axis=-1, keepdims=True))
    hid = num / wide(jnp.maximum(jnp.abs(den), jnp.exp(-m_t)))

    b_last = b_cols[CHUNK - 1:CHUNK]
    log_w = b_last - b_cols + i_cols
    m_new = jnp.maximum(b_last + m_prev, jnp.max(log_w, axis=0, keepdims=True))
    w = jnp.exp(log_w - m_new)
    decay = wide(jnp.exp(b_last + m_prev - m_new))
    kw = k * wide(w)
    c_ref[h] = decay * c_state + lax.dot_general(kw.astype(BF16), v, _TN, preferred_element_type=F32)
    n_ref[h] = decay * n_state + jnp.sum(kw, axis=0, keepdims=True)
    m_ref[h] = m_new

    out_cols = slice(RET_HEADS * HEAD_W + h * HEAD_W, RET_HEADS * HEAD_W + (h + 1) * HEAD_W)
    y = _rms_rows(hid, hg_ref[:, out_cols])
    o_ref[0, :, out_cols] = (y * jax.nn.sigmoid(og_ref[0, :, cols].astype(F32))).astype(o_ref.dtype)


def _mixer_body(gl_ref, rq_ref, rk_ref, rv_ref, rg_ref, mq_ref, mqp_ref, mk_ref, mkp_ref, mv_ref, mo_ref,
                cos_ref, sin_ref, dm_ref, wq_ref, wk_ref, gr_ref, gb_ref, cw_ref, shift_ref, hg_ref,
                o_ref, r_ref, c_ref, n_ref, m_ref):
    first = pl.program_id(1) == 0

    @pl.when(first)
    def _():
        r_ref[...] = jnp.zeros_like(r_ref)
        c_ref[...] = jnp.zeros_like(c_ref)
        n_ref[...] = jnp.zeros_like(n_ref)
        m_ref[...] = jnp.zeros_like(m_ref)

    cos = cos_ref[...]
    sin = sin_ref[...]
    for h in range(RET_HEADS):
        _retention_head(h, rq_ref, rk_ref, rv_ref, rg_ref, cos, sin, dm_ref, wq_ref, wk_ref, gl_ref,
                        hg_ref, o_ref, r_ref)

    ml_w = ML_HEADS * HEAD_W
    q_all = _silu(_causal_conv(mq_ref, mqp_ref, first, shift_ref, cw_ref[:, :ml_w]))
    k_all = _silu(_causal_conv(mk_ref, mkp_ref, first, shift_ref, cw_ref[:, ml_w:])) * (HEAD_W ** -0.5)
    gates = gr_ref[0] + gb_ref[...]
    is_input = lax.broadcasted_iota(jnp.int32, gates.shape, 0) < ML_HEADS
    gx = jnp.where(is_input, gates, _log_sigmoid(gates))
    rowi = lax.broadcasted_iota(jnp.int32, (CHUNK, CHUNK), 0)
    coli = lax.broadcasted_iota(jnp.int32, (CHUNK, CHUNK), 1)
    upper_tri = jnp.where(rowi <= coli, 1.0, 0.0).astype(F32)
    lower_tri = jnp.where(rowi >= coli, 1.0, 0.0).astype(F32)
    b_rows = jnp.dot(gx, upper_tri, precision=lax.Precision.HIGHEST, preferred_element_type=F32)
    for h in range(ML_HEADS):
        _mlstm_head(h, q_all, k_all, mv_ref, mo_ref, gx, b_rows, lower_tri, hg_ref, o_ref, c_ref, n_ref, m_ref)


def _mixer(proj, gates_row, gate_b, conv_w, head_g):
    _, b, t, _ = proj.shape
    nc = t // CHUNK
    ret_w = RET_HEADS * HEAD_W
    ml_w = ML_HEADS * HEAD_W
    assert ret_w == ml_w == proj.shape[3]
    log_g = jnp.log1p(-jnp.exp2(-5.0 - jnp.arange(RET_HEADS, dtype=F32)))
    idx = jnp.arange(CHUNK, dtype=F32)
    diff = idx[:, None] - idx[None, :]
    dmask = jnp.where(diff >= 0, jnp.exp(log_g[:, None, None] * jnp.maximum(diff, 0.0)), 0.0)
    w_k = jnp.exp(log_g[:, None] * (CHUNK - 1.0 - idx)[None, :])
    w_q = jnp.exp(log_g[:, None] * (idx + 1.0)[None, :])
    w_k = jnp.broadcast_to(w_k[..., None], (RET_HEADS, CHUNK, HEAD_W))
    w_q = jnp.broadcast_to(w_q[..., None], (RET_HEADS, CHUNK, HEAD_W))
    g_l = jnp.exp(log_g * CHUNK)
    half = HEAD_W // 2
    inv = 1.0 / (ROPE_BASE ** jnp.linspace(0.0, 1.0, half, dtype=F32))
    ang = jnp.arange(t).astype(F32)[:, None] * inv[None, :]
    cos, sin = jnp.cos(ang), jnp.sin(ang)
    r = jnp.arange((CONV_W - 1) * CHUNK)
    src = CHUNK + r % CHUNK - (r // CHUNK + 1)
    shift = (jnp.arange(2 * CHUNK)[None, :] == src[:, None]).astype(BF16)

    def group(gi):
        return pl.BlockSpec((None, 1, CHUNK, ret_w), lambda bi, c: (gi, bi, c, 0))

    def prev_group(gi):
        return pl.BlockSpec((None, 1, CHUNK, ml_w), lambda bi, c: (gi, bi, jnp.maximum(c - 1, 0), 0))

    def whole(a):
        return pl.BlockSpec(a.shape, lambda bi, c: (0,) * a.ndim)

    gb = gate_b.reshape(2 * ML_HEADS, 1)
    hg = head_g.reshape(1, ret_w + ml_w)
    return pl.pallas_call(
        _mixer_body,
        grid=(b, nc),
        in_specs=[
            pl.BlockSpec(memory_space=pltpu.SMEM),
            group(0), group(1), group(2), group(3),
            group(4), prev_group(4), group(5), prev_group(5), group(6), group(7),
            pl.BlockSpec((CHUNK, half), lambda bi, c: (c, 0)),
            pl.BlockSpec((CHUNK, half), lambda bi, c: (c, 0)),
            whole(dmask), whole(w_q), whole(w_k),
            pl.BlockSpec((1, 2 * ML_HEADS, CHUNK), lambda bi, c: (bi, 0, c)),
            whole(gb), whole(conv_w), whole(shift), whole(hg),
        ],
        out_specs=pl.BlockSpec((1, CHUNK, ret_w + ml_w), lambda bi, c: (bi, c, 0)),
        out_shape=jax.ShapeDtypeStruct((b, t, ret_w + ml_w), BF16),
        scratch_shapes=[pltpu.VMEM((RET_HEADS, HEAD_W, HEAD_W), F32),
                        pltpu.VMEM((ML_HEADS, HEAD_W, HEAD_W), F32),
                        pltpu.VMEM((ML_HEADS, 1, HEAD_W), F32),
                        pltpu.VMEM((ML_HEADS, 1, LANES), F32)],
        compiler_params=_params(),
        name="mixer",
    )(g_l, proj, proj, proj, proj, proj, proj, proj, proj, proj, proj, cos, sin, dmask, w_q, w_k,
      gates_row, gb, conv_w, shift, hg)


PAIR_W = 2 * HEAD_DIM
GROUP = ATT_HEADS // KV_HEADS
PAIRS = GROUP // 2


def _swa_in_proj_body(x_ref, g_ref, w_ref, q_ref, kv_ref, h_ref, *, q_tiles):
    j = pl.program_id(1)

    @pl.when(j == 0)
    def _():
        _norm_rows_into(x_ref, g_ref, h_ref, 64)

    res = jnp.dot(h_ref[...], w_ref[...], preferred_element_type=F32)

    @pl.when(j < q_tiles)
    def _():
        for nbl in range(q_ref.shape[1]):
            for p in range(q_ref.shape[2]):
                q_ref[0, nbl, p] = res[nbl * ATT_BLOCK:(nbl + 1) * ATT_BLOCK,
                                       p * PAIR_W:(p + 1) * PAIR_W].astype(q_ref.dtype)

    @pl.when(j == q_tiles)
    def _():
        kv_ref[...] = res.astype(kv_ref.dtype)


SWA_TN = 2 * KV_HEADS * HEAD_DIM


def _swa_in_proj(x, g, w, b, t, tm):
    n, d = x.shape
    tn = SWA_TN
    q_tiles = w.shape[1] // tn - 1
    tiles_per_seq = t // tm
    pairs_per_tile = tn // PAIR_W
    return pl.pallas_call(
        functools.partial(_swa_in_proj_body, q_tiles=q_tiles),
        grid=(n // tm, q_tiles + 1),
        in_specs=[
            pl.BlockSpec((tm, d), lambda i, j: (i, 0)),
            pl.BlockSpec((1, d), lambda i, j: (0, 0)),
            pl.BlockSpec((d, tn), lambda i, j: (0, j)),
        ],
        out_specs=[
            pl.BlockSpec((1, tm // ATT_BLOCK, pairs_per_tile, ATT_BLOCK, PAIR_W),
                         lambda i, j: (i // tiles_per_seq, i % tiles_per_seq, jnp.minimum(j, q_tiles - 1), 0, 0)),
            pl.BlockSpec((tm, tn), lambda i, j: (i, 0)),
        ],
        out_shape=[jax.ShapeDtypeStruct((b, t // ATT_BLOCK, ATT_HEADS // 2, ATT_BLOCK, PAIR_W), BF16),
                   jax.ShapeDtypeStruct((n, tn), BF16)],
        scratch_shapes=[pltpu.VMEM((tm, d), BF16)],
        compiler_params=_params(),
        name="swa_in_proj",
    )(x, g.reshape(1, d), w)


def _swa_body(rb_ref, sink_ref, idx_ref, q_ref, kp_ref, kc_ref, vp_ref, vc_ref, o_ref, bias_ref, sinkcol_ref):
    bi = pl.program_id(0)
    nb = pl.program_id(1)
    blk = ATT_BLOCK
    rows = PAIRS * blk

    @pl.when((bi == 0) & (nb == 0))
    def _():
        idx = idx_ref[...]
        col = lax.broadcasted_iota(jnp.int32, idx.shape, 1)

        def head_body(hd, carry):
            def bucket_body(bk, acc):
                return jnp.where(idx == bk, rb_ref[bk * ATT_HEADS + hd], acc)

            tbl = lax.fori_loop(0, N_BUCKETS, bucket_body, jnp.full(idx.shape, -jnp.inf, F32))
            kv = hd // GROUP
            half = hd % 2
            at = pl.ds(pl.multiple_of(((hd % GROUP) // 2) * blk, blk), blk)
            bias_ref[0, kv, half, at, :] = tbl
            bias_ref[1, kv, half, at, :] = jnp.where(col >= blk, tbl, -jnp.inf)
            sinkcol_ref[kv, half, at, :] = jnp.full((blk, LANES), sink_ref[hd], F32)
            return carry

        lax.fori_loop(0, ATT_HEADS, head_body, 0)

    first = (nb == 0).astype(jnp.int32)
    low = lax.broadcasted_iota(jnp.int32, (2 * blk, LANES), 1) < HEAD_DIM
    low_out = lax.broadcasted_iota(jnp.int32, (rows, LANES), 1) < HEAD_DIM
    kf = jnp.concatenate([kp_ref[0], kc_ref[0]], axis=0).astype(F32)
    vf = jnp.concatenate([vp_ref[0], vc_ref[0]], axis=0).astype(F32)
    scale = HEAD_DIM ** -0.5

    def halves(xf, kv):
        xc = xf[:, (kv // 2) * LANES:(kv // 2 + 1) * LANES]
        xr = pltpu.roll(xc, HEAD_DIM, axis=1)
        lo_src, hi_src = (xc, xr) if kv % 2 == 0 else (xr, xc)
        return (jnp.where(low, lo_src, 0.0).astype(BF16), jnp.where(low, 0.0, hi_src).astype(BF16))

    for kv in range(KV_HEADS):
        k_lo, k_hi = halves(kf, kv)
        v_lo, v_hi = halves(vf, kv)
        q4 = q_ref[kv * PAIRS:(kv + 1) * PAIRS].reshape(rows, PAIR_W) * scale

        def probs(k_half, half):
            s = lax.dot_general(q4, k_half, (((1,), (1,)), ((), ())), preferred_element_type=F32)
            s = s + bias_ref[first, kv, half]
            sink = sinkcol_ref[kv, half]
            mx = jnp.maximum(jnp.max(s, axis=-1, keepdims=True), sink)
            p = jnp.exp(s - jnp.concatenate([mx, mx], axis=1))
            den = jnp.sum(p, axis=-1, keepdims=True) + jnp.exp(sink - mx)
            return p.astype(BF16), den

        p_lo, den_lo = probs(k_lo, 0)
        p_hi, den_hi = probs(k_hi, 1)
        o4 = (jnp.dot(p_lo, v_lo, preferred_element_type=F32)
              + jnp.dot(p_hi, v_hi, preferred_element_type=F32)) / jnp.where(low_out, den_lo, den_hi)
        for pr in range(PAIRS):
            at = (kv * PAIRS + pr) * PAIR_W
            o_ref[0, :, at:at + PAIR_W] = o4[pr * blk:(pr + 1) * blk].astype(o_ref.dtype)


def _t5_bucket(dist):
    n = jnp.maximum(dist, 0)
    max_exact = N_BUCKETS // 2
    nf = jnp.maximum(n, 1).astype(F32)
    large = max_exact + (jnp.log(nf / max_exact) / math.log(MAX_DIST / max_exact)
                         * (N_BUCKETS - max_exact)).astype(jnp.int32)
    large = jnp.minimum(large, N_BUCKETS - 1)
    return jnp.where(n < max_exact, n, large)


def _swa(q, kv, sinks, rel_bias):
    b, nblk, n_pairs, blk, _ = q.shape
    t = nblk * blk
    qw = ATT_HEADS * HEAD_DIM
    kvw = KV_HEADS * HEAD_DIM
    i = jnp.arange(blk)
    j = jnp.arange(2 * blk)
    dist = (blk + i)[:, None] - j[None, :]
    idx = jnp.where((dist >= 0) & (dist < WINDOW), _t5_bucket(dist), -1).astype(jnp.int32)
    return pl.pallas_call(
        _swa_body,
        grid=(b, nblk),
        in_specs=[
            pl.BlockSpec(memory_space=pltpu.SMEM),
            pl.BlockSpec(memory_space=pltpu.SMEM),
            pl.BlockSpec((blk, 2 * blk), lambda bi, nb: (0, 0)),
            pl.BlockSpec((None, None, n_pairs, blk, PAIR_W), lambda bi, nb: (bi, nb, 0, 0, 0)),
            pl.BlockSpec((1, blk, kvw), lambda bi, nb: (bi, jnp.maximum(nb - 1, 0), 0)),
            pl.BlockSpec((1, blk, kvw), lambda bi, nb: (bi, nb, 0)),
            pl.BlockSpec((1, blk, kvw), lambda bi, nb: (bi, jnp.maximum(nb - 1, 0), 1)),
            pl.BlockSpec((1, blk, kvw), lambda bi, nb: (bi, nb, 1)),
        ],
        out_specs=pl.BlockSpec((1, blk, qw), lambda bi, nb: (bi, nb, 0)),
        out_shape=jax.ShapeDtypeStruct((b, t, qw), BF16),
        scratch_shapes=[pltpu.VMEM((2, KV_HEADS, 2, PAIRS * blk, 2 * blk), F32),
                        pltpu.VMEM((KV_HEADS, 2, PAIRS * blk, LANES), F32)],
        compiler_params=_params(),
        name="swa",
    )(rel_bias.reshape(-1), sinks, idx, q, kv, kv, kv, kv)


def kernel(x, rel_bias, norm_g, ffn_w_gu, ffn_w_down, rm_w_in, ml_conv_w, ml_gate_b, rm_head_g,
           rm_w_out, swa_w_in, swa_sinks, swa_w_out):
    b, t, d = x.shape
    n = b * t
    depth = norm_g.shape[0]
    main_cols = 4 * RET_HEADS * HEAD_W + 4 * ML_HEADS * HEAD_W
    xs = x.reshape(n, d)
    w_gu = ffn_w_gu.astype(BF16)
    w_down = ffn_w_down.astype(BF16)
    for layer in range(depth):
        g = norm_g[layer]
        if layer % 2 == 0:
            e = layer // 2
            w_in = rm_w_in[e]
            gate_cols = w_in.shape[1] - main_cols
            w_gate = jnp.pad(w_in[:, main_cols:], ((0, 0), (0, LANES - gate_cols))).astype(BF16)
            proj, gates = _norm_matmul(xs, g[0], w_in.astype(BF16), w_gate, main_cols // PROJ_TN, ROW_TILE, PROJ_TN)
            gates_row = gates[:, :gate_cols].reshape(b, t, gate_cols).transpose(0, 2, 1)
            mixed = _mixer(proj.reshape(main_cols // PROJ_TN, b, t, PROJ_TN), gates_row, ml_gate_b[e], ml_conv_w[e], rm_head_g[e])
            xs = _proj_norm_res(mixed.reshape(n, -1), rm_w_out[e].astype(BF16), xs, g[1], OUT_ROW_TILE)
        else:
            o = layer // 2
            q, kv = _swa_in_proj(xs, g[0], swa_w_in[o].astype(BF16), b, t, ROW_TILE)
            att = _swa(q, kv.reshape(b, t, -1), swa_sinks[o], rel_bias)
            xs = _proj_norm_res(att.reshape(n, -1), swa_w_out[o].astype(BF16), xs, g[1], OUT_ROW_TILE)
        xs = _ffn(xs, g[2], w_gu, w_down, g[3], layer, ROW_TILE, FFN_TF, FFN_ROW_BLOCK)
    return xs.reshape(b, t, d)
```

```python
import functools
import math

import jax
import jax.numpy as jnp
from jax import lax
from jax.experimental import pallas as pl
from jax.experimental.pallas import tpu as pltpu

F32 = jnp.float32
BF16 = jnp.bfloat16

EPS = 1e-6
CHUNK = 128
RET_HEADS = 4
ML_HEADS = 4
HEAD_W = 256
CONV_W = 4
ROPE_BASE = 10000.0
ATT_HEADS = 32
KV_HEADS = 4
HEAD_DIM = 64
WINDOW = 128
ATT_BLOCK = 128
N_BUCKETS = 32
MAX_DIST = 128
LANES = 128
SUBLANES = 8
VMEM_LIMIT = 56 * 1024 * 1024

ROW_TILE = 1024
OUT_ROW_TILE = 512
PROJ_TN = 1024
PROJ_ROW_BLOCK = 256
FFN_TF = 512
FFN_ROW_BLOCK = 512


def _params(vmem=VMEM_LIMIT):
    return pltpu.CompilerParams(vmem_limit_bytes=vmem)


def _rms_rows(x, g):
    ms = jnp.mean(x * x, axis=-1, keepdims=True)
    return x * lax.rsqrt(ms + EPS) * g


def _silu(x):
    return x * jax.nn.sigmoid(x)


def _norm_matmul_body(x_ref, g_ref, w_ref, ws_ref, o_ref, os_ref, h_ref, *, row_block):
    j = pl.program_id(1)

    @pl.when(j == 0)
    def _():
        for r in range(x_ref.shape[0] // row_block):
            rows = slice(r * row_block, (r + 1) * row_block)
            h = _rms_rows(x_ref[rows, :], g_ref[...]).astype(h_ref.dtype)
            h_ref[rows, :] = h
            os_ref[rows, :] = jnp.dot(h, ws_ref[...], preferred_element_type=F32)
            o_ref[rows, :] = jnp.dot(h, w_ref[...], preferred_element_type=F32).astype(o_ref.dtype)

    @pl.when(j > 0)
    def _():
        o_ref[...] = jnp.dot(h_ref[...], w_ref[...], preferred_element_type=F32).astype(o_ref.dtype)


def _norm_matmul(x, g, w, w_side, tiles, tm, tn):
    n, d = x.shape
    ns = w_side.shape[1]
    return pl.pallas_call(
        functools.partial(_norm_matmul_body, row_block=PROJ_ROW_BLOCK),
        grid=(n // tm, tiles),
        in_specs=[
            pl.BlockSpec((tm, d), lambda i, j: (i, 0)),
            pl.BlockSpec((1, d), lambda i, j: (0, 0)),
            pl.BlockSpec((d, tn), lambda i, j: (0, j)),
            pl.BlockSpec((d, ns), lambda i, j: (0, 0)),
        ],
        out_specs=[pl.BlockSpec((None, tm, tn), lambda i, j: (j, i, 0)),
                   pl.BlockSpec((tm, ns), lambda i, j: (i, 0))],
        out_shape=[jax.ShapeDtypeStruct((tiles, n, tn), BF16), jax.ShapeDtypeStruct((n, ns), F32)],
        scratch_shapes=[pltpu.VMEM((tm, d), BF16)],
        compiler_params=_params(),
        name="norm_matmul",
    )(x, g.reshape(1, d), w, w_side)


def _proj_norm_res_body(a_ref, w_ref, x_ref, g_ref, o_ref):
    y = jnp.dot(a_ref[...], w_ref[...], preferred_element_type=F32)
    o_ref[...] = x_ref[...] + _rms_rows(y, g_ref[...])


def _proj_norm_res(a, w, x, g, tm):
    n, d = x.shape
    return pl.pallas_call(
        _proj_norm_res_body,
        grid=(n // tm,),
        in_specs=[
            pl.BlockSpec((tm, a.shape[1]), lambda i: (i, 0)),
            pl.BlockSpec(w.shape, lambda i: (0, 0)),
            pl.BlockSpec((tm, d), lambda i: (i, 0)),
            pl.BlockSpec((1, d), lambda i: (0, 0)),
        ],
        out_specs=pl.BlockSpec((tm, d), lambda i: (i, 0)),
        out_shape=jax.ShapeDtypeStruct((n, d), F32),
        compiler_params=_params(),
        name="proj_norm_res",
    )(a, w, x, g.reshape(1, d))


def _ffn_body(x_ref, g_in_ref, wg_ref, wu_ref, wd_ref, g_out_ref, o_ref, h_ref, *, row_block):
    f = pl.program_id(1)
    last = pl.num_programs(1) - 1
    blocks = [slice(r * row_block, (r + 1) * row_block) for r in range(x_ref.shape[0] // row_block)]

    def partial_down(rows):
        h = h_ref[rows, :]
        gate = jnp.dot(h, wg_ref[...], preferred_element_type=F32)
        up = jnp.dot(h, wu_ref[...], preferred_element_type=F32)
        act = (_silu(gate) * up).astype(BF16)
        return jnp.dot(act, wd_ref[...], preferred_element_type=F32)

    @pl.when(f == 0)
    def _():
        for rows in blocks:
            h_ref[rows, :] = _rms_rows(x_ref[rows, :], g_in_ref[...]).astype(h_ref.dtype)
            o_ref[rows, :] = partial_down(rows)

    @pl.when((f > 0) & (f < last))
    def _():
        for rows in blocks:
            o_ref[rows, :] += partial_down(rows)

    @pl.when(f == last)
    def _():
        for rows in blocks:
            y = o_ref[rows, :] + partial_down(rows)
            o_ref[rows, :] = x_ref[rows, :] + _rms_rows(y, g_out_ref[...])


def _ffn(x, g_in, w_gu, w_down, g_out, layer, tm, tf, row_block):
    n, d = x.shape
    nf = w_down.shape[1] // tf
    return pl.pallas_call(
        functools.partial(_ffn_body, row_block=row_block),
        grid=(n // tm, nf),
        in_specs=[
            pl.BlockSpec((tm, d), lambda i, f: (i, 0)),
            pl.BlockSpec((1, d), lambda i, f: (0, 0)),
            pl.BlockSpec((None, d, tf), lambda i, f: (layer, 0, f)),
            pl.BlockSpec((None, d, tf), lambda i, f: (layer, 0, f + nf)),
            pl.BlockSpec((None, tf, d), lambda i, f: (layer, f, 0)),
            pl.BlockSpec((1, d), lambda i, f: (0, 0)),
        ],
        out_specs=pl.BlockSpec((tm, d), lambda i, f: (i, 0)),
        out_shape=jax.ShapeDtypeStruct((n, d), F32),
        scratch_shapes=[pltpu.VMEM((tm, d), BF16)],
        compiler_params=_params(),
        name="ffn",
    )(x, g_in.reshape(1, d), w_gu, w_gu, w_down, g_out.reshape(1, d))


def _log_sigmoid(x):
    return jnp.minimum(x, 0.0) - jnp.log1p(jnp.exp(-jnp.abs(x)))


_NT = (((1,), (1,)), ((), ()))
_TN = (((0,), (0,)), ((), ()))


def _retention_head(h, q_ref, k_ref, v_ref, g_ref, cos, sin, dm_ref, wq_ref, wk_ref, gl_ref, hg_ref,
                    o_ref, state_ref):
    cols = slice(h * HEAD_W, (h + 1) * HEAD_W)
    half = HEAD_W // 2

    def rot(x):
        x = x.astype(F32)
        x1, x2 = x[:, :half], x[:, half:]
        return jnp.concatenate([x1 * cos - x2 * sin, x2 * cos + x1 * sin], axis=-1)

    q = rot(q_ref[0, :, cols])
    k = rot(k_ref[0, :, cols]) * (HEAD_W ** -0.5)
    v = v_ref[0, :, cols]
    qb = q.astype(BF16)
    s = lax.dot_general(qb, k.astype(BF16), (((1,), (1,)), ((), ())),
                        preferred_element_type=F32) * dm_ref[h]
    intra = jnp.dot(s.astype(BF16), v, preferred_element_type=F32)
    state = state_ref[h]
    inter = jnp.dot(qb, state.astype(BF16), preferred_element_type=F32) * wq_ref[h]
    out = intra + inter
    kw = (k * wk_ref[h]).astype(BF16)
    kv = lax.dot_general(kw, v, (((0,), (0,)), ((), ())), preferred_element_type=F32)
    state_ref[h] = state * gl_ref[h] + kv
    y = _rms_rows(out, hg_ref[:, cols])
    o_ref[0, :, cols] = (y * _silu(g_ref[0, :, cols].astype(F32))).astype(o_ref.dtype)


def _causal_conv(cur_ref, prev_ref, first, shift_ref, w):
    cur = cur_ref[0]
    prev = jnp.where(first, jnp.zeros_like(cur), prev_ref[0])
    shifted = jnp.dot(shift_ref[...], jnp.concatenate([prev, cur], axis=0), preferred_element_type=F32)
    y = cur.astype(F32) * w[CONV_W - 1:CONV_W]
    for s in range(1, CONV_W):
        y = y + shifted[(s - 1) * CHUNK:s * CHUNK] * w[CONV_W - 1 - s:CONV_W - s]
    return y


def _mlstm_head(h, q_all, k_all, v_ref, og_ref, gx, b_rows, lower_tri, hg_ref, o_ref, c_ref, n_ref, m_ref):
    cols = slice(h * HEAD_W, (h + 1) * HEAD_W)
    q = q_all[:, cols]
    k = k_all[:, cols]
    v = v_ref[0, :, cols]

    i_row = gx[h:h + 1]
    lf_row = gx[ML_HEADS + h:ML_HEADS + h + 1]
    b_row = b_rows[ML_HEADS + h:ML_HEADS + h + 1]
    b_cols = lax.dot_general(lower_tri, jnp.broadcast_to(lf_row, (CHUNK, CHUNK)), _NT,
                             precision=lax.Precision.HIGHEST, preferred_element_type=F32)
    i_cols = jnp.broadcast_to(i_row, (CHUNK, CHUNK)).T
    rowi = lax.broadcasted_iota(jnp.int32, (CHUNK, CHUNK), 0)
    coli = lax.broadcasted_iota(jnp.int32, (CHUNK, CHUNK), 1)
    log_d = jnp.where(rowi >= coli, b_cols - b_row + i_row, -jnp.inf)
    m_prev = m_ref[h]
    inter_log = b_cols + m_prev
    m_t = jnp.maximum(inter_log, jnp.max(log_d, axis=-1, keepdims=True))
    d_mat = jnp.exp(log_d - m_t)
    w_inter = jnp.exp(inter_log - m_t)

    def wide(a):
        return jnp.concatenate([a] * (HEAD_W // LANES), axis=1)

    qb = q.astype(BF16)
    s = lax.dot_general(qb, k.astype(BF16), _NT, preferred_element_type=F32) * d_mat
    c_state = c_ref[h]
    n_state = n_ref[h]
    num = (jnp.dot(s.astype(BF16), v, preferred_element_type=F32)
           + wide(w_inter) * jnp.dot(qb, c_state.astype(BF16), preferred_element_type=F32))
    den = (jnp.sum(s, axis=-1, keepdims=True)
           + w_inter * jnp.sum(q * n_state, axis=-1, keepdims=True))
    hid = num / wide(jnp.maximum(jnp.abs(den), jnp.exp(-m_t)))

    b_last = b_cols[CHUNK - 1:CHUNK]
    log_w = b_last - b_cols + i_cols
    m_new = jnp.maximum(b_last + m_prev, jnp.max(log_w, axis=0, keepdims=True))
    w = jnp.exp(log_w - m_new)
    decay = wide(jnp.exp(b_last + m_prev - m_new))
    kw = k * wide(w)
    c_ref[h] = decay * c_state + lax.dot_general(kw.astype(BF16), v, _TN, preferred_element_type=F32)
    n_ref[h] = decay * n_state + jnp.sum(kw, axis=0, keepdims=True)
    m_ref[h] = m_new

    out_cols = slice(RET_HEADS * HEAD_W + h * HEAD_W, RET_HEADS * HEAD_W + (h + 1) * HEAD_W)
    y = _rms_rows(hid, hg_ref[:, out_cols])
    o_ref[0, :, out_cols] = (y * jax.nn.sigmoid(og_ref[0, :, cols].astype(F32))).astype(o_ref.dtype)


def _mixer_body(gl_ref, rq_ref, rk_ref, rv_ref, rg_ref, mq_ref, mqp_ref, mk_ref, mkp_ref, mv_ref, mo_ref,
                cos_ref, sin_ref, dm_ref, wq_ref, wk_ref, gr_ref, gb_ref, cw_ref, shift_ref, hg_ref,
                o_ref, r_ref, c_ref, n_ref, m_ref):
    first = pl.program_id(1) == 0

    @pl.when(first)
    def _():
        r_ref[...] = jnp.zeros_like(r_ref)
        c_ref[...] = jnp.zeros_like(c_ref)
        n_ref[...] = jnp.zeros_like(n_ref)
        m_ref[...] = jnp.zeros_like(m_ref)

    cos = cos_ref[...]
    sin = sin_ref[...]
    for h in range(RET_HEADS):
        _retention_head(h, rq_ref, rk_ref, rv_ref, rg_ref, cos, sin, dm_ref, wq_ref, wk_ref, gl_ref,
                        hg_ref, o_ref, r_ref)

    ml_w = ML_HEADS * HEAD_W
    q_all = _silu(_causal_conv(mq_ref, mqp_ref, first, shift_ref, cw_ref[:, :ml_w]))
    k_all = _silu(_causal_conv(mk_ref, mkp_ref, first, shift_ref, cw_ref[:, ml_w:])) * (HEAD_W ** -0.5)
    gates = gr_ref[0] + gb_ref[...]
    is_input = lax.broadcasted_iota(jnp.int32, gates.shape, 0) < ML_HEADS
    gx = jnp.where(is_input, gates, _log_sigmoid(gates))
    rowi = lax.broadcasted_iota(jnp.int32, (CHUNK, CHUNK), 0)
    coli = lax.broadcasted_iota(jnp.int32, (CHUNK, CHUNK), 1)
    upper_tri = jnp.where(rowi <= coli, 1.0, 0.0).astype(F32)
    lower_tri = jnp.where(rowi >= coli, 1.0, 0.0).astype(F32)
    b_rows = jnp.dot(gx, upper_tri, precision=lax.Precision.HIGHEST, preferred_element_type=F32)
    for h in range(ML_HEADS):
        _mlstm_head(h, q_all, k_all, mv_ref, mo_ref, gx, b_rows, lower_tri, hg_ref, o_ref, c_ref, n_ref, m_ref)


def _mixer(proj, gates_row, gate_b, conv_w, head_g):
    _, b, t, _ = proj.shape
    nc = t // CHUNK
    ret_w = RET_HEADS * HEAD_W
    ml_w = ML_HEADS * HEAD_W
    assert ret_w == ml_w == proj.shape[3]
    log_g = jnp.log1p(-jnp.exp2(-5.0 - jnp.arange(RET_HEADS, dtype=F32)))
    idx = jnp.arange(CHUNK, dtype=F32)
    diff = idx[:, None] - idx[None, :]
    dmask = jnp.where(diff >= 0, jnp.exp(log_g[:, None, None] * jnp.maximum(diff, 0.0)), 0.0)
    w_k = jnp.exp(log_g[:, None] * (CHUNK - 1.0 - idx)[None, :])
    w_q = jnp.exp(log_g[:, None] * (idx + 1.0)[None, :])
    w_k = jnp.broadcast_to(w_k[..., None], (RET_HEADS, CHUNK, HEAD_W))
    w_q = jnp.broadcast_to(w_q[..., None], (RET_HEADS, CHUNK, HEAD_W))
    g_l = jnp.exp(log_g * CHUNK)
    half = HEAD_W // 2
    inv = 1.0 / (ROPE_BASE ** jnp.linspace(0.0, 1.0, half, dtype=F32))
    ang = jnp.arange(t).astype(F32)[:, None] * inv[None, :]
    cos, sin = jnp.cos(ang), jnp.sin(ang)
    r = jnp.arange((CONV_W - 1) * CHUNK)
    src = CHUNK + r % CHUNK - (r // CHUNK + 1)
    shift = (jnp.arange(2 * CHUNK)[None, :] == src[:, None]).astype(BF16)

    def group(gi):
        return pl.BlockSpec((None, 1, CHUNK, ret_w), lambda bi, c: (gi, bi, c, 0))

    def prev_group(gi):
        return pl.BlockSpec((None, 1, CHUNK, ml_w), lambda bi, c: (gi, bi, jnp.maximum(c - 1, 0), 0))

    def whole(a):
        return pl.BlockSpec(a.shape, lambda bi, c: (0,) * a.ndim)

    gb = gate_b.reshape(2 * ML_HEADS, 1)
    hg = head_g.reshape(1, ret_w + ml_w)
    return pl.pallas_call(
        _mixer_body,
        grid=(b, nc),
        in_specs=[
            pl.BlockSpec(memory_space=pltpu.SMEM),
            group(0), group(1), group(2), group(3),
            group(4), prev_group(4), group(5), prev_group(5), group(6), group(7),
            pl.BlockSpec((CHUNK, half), lambda bi, c: (c, 0)),
            pl.BlockSpec((CHUNK, half), lambda bi, c: (c, 0)),
            whole(dmask), whole(w_q), whole(w_k),
            pl.BlockSpec((1, 2 * ML_HEADS, CHUNK), lambda bi, c: (bi, 0, c)),
            whole(gb), whole(conv_w), whole(shift), whole(hg),
        ],
        out_specs=pl.BlockSpec((1, CHUNK, ret_w + ml_w), lambda bi, c: (bi, c, 0)),
        out_shape=jax.ShapeDtypeStruct((b, t, ret_w + ml_w), BF16),
        scratch_shapes=[pltpu.VMEM((RET_HEADS, HEAD_W, HEAD_W), F32),
                        pltpu.VMEM((ML_HEADS, HEAD_W, HEAD_W), F32),
                        pltpu.VMEM((ML_HEADS, 1, HEAD_W), F32),
                        pltpu.VMEM((ML_HEADS, 1, LANES), F32)],
        compiler_params=_params(),
        name="mixer",
    )(g_l, proj, proj, proj, proj, proj, proj, proj, proj, proj, proj, cos, sin, dmask, w_q, w_k,
      gates_row, gb, conv_w, shift, hg)


PAIR_W = 2 * HEAD_DIM
GROUP = ATT_HEADS // KV_HEADS
PAIRS = GROUP // 2


def _swa_in_proj_body(x_ref, g_ref, w_ref, q_ref, kv_ref, *, row_block):
    qw = ATT_HEADS * HEAD_DIM
    per_block = row_block // ATT_BLOCK
    for r in range(x_ref.shape[0] // row_block):
        rows = slice(r * row_block, (r + 1) * row_block)
        h = _rms_rows(x_ref[rows, :], g_ref[...]).astype(BF16)
        res = jnp.dot(h, w_ref[...], preferred_element_type=F32)
        for nbl in range(per_block):
            sub = slice(nbl * ATT_BLOCK, (nbl + 1) * ATT_BLOCK)
            for p in range(q_ref.shape[2]):
                q_ref[0, r * per_block + nbl, p] = res[sub, p * PAIR_W:(p + 1) * PAIR_W].astype(q_ref.dtype)
        kv_ref[rows, :] = res[:, qw:].astype(kv_ref.dtype)


def _swa_in_proj(x, g, w, b, t, tm, row_block):
    n, d = x.shape
    cols = w.shape[1]
    kvw = cols - ATT_HEADS * HEAD_DIM
    tiles_per_seq = t // tm
    return pl.pallas_call(
        functools.partial(_swa_in_proj_body, row_block=row_block),
        grid=(n // tm,),
        in_specs=[
            pl.BlockSpec((tm, d), lambda i: (i, 0)),
            pl.BlockSpec((1, d), lambda i: (0, 0)),
            pl.BlockSpec((d, cols), lambda i: (0, 0)),
        ],
        out_specs=[
            pl.BlockSpec((1, tm // ATT_BLOCK, ATT_HEADS // 2, ATT_BLOCK, PAIR_W),
                         lambda i: (i // tiles_per_seq, i % tiles_per_seq, 0, 0, 0)),
            pl.BlockSpec((tm, kvw), lambda i: (i, 0)),
        ],
        out_shape=[jax.ShapeDtypeStruct((b, t // ATT_BLOCK, ATT_HEADS // 2, ATT_BLOCK, PAIR_W), BF16),
                   jax.ShapeDtypeStruct((n, kvw), BF16)],
        compiler_params=_params(),
        name="swa_in_proj",
    )(x, g.reshape(1, d), w)


def _swa_body(rb_ref, sink_ref, idx_ref, q_ref, kp_ref, kc_ref, vp_ref, vc_ref, o_ref, bias_ref, sinkcol_ref):
    bi = pl.program_id(0)
    nb = pl.program_id(1)
    blk = ATT_BLOCK
    rows = PAIRS * blk

    @pl.when((bi == 0) & (nb == 0))
    def _():
        idx = idx_ref[...]
        col = lax.broadcasted_iota(jnp.int32, idx.shape, 1)

        def head_body(hd, carry):
            def bucket_body(bk, acc):
                return jnp.where(idx == bk, rb_ref[bk * ATT_HEADS + hd], acc)

            tbl = lax.fori_loop(0, N_BUCKETS, bucket_body, jnp.full(idx.shape, -jnp.inf, F32))
            kv = hd // GROUP
            half = hd % 2
            at = pl.ds(pl.multiple_of(((hd % GROUP) // 2) * blk, blk), blk)
            bias_ref[0, kv, half, at, :] = tbl
            bias_ref[1, kv, half, at, :] = jnp.where(col >= blk, tbl, -jnp.inf)
            sinkcol_ref[kv, half, at, :] = jnp.full((blk, LANES), sink_ref[hd], F32)
            return carry

        lax.fori_loop(0, ATT_HEADS, head_body, 0)

    first = (nb == 0).astype(jnp.int32)
    low = lax.broadcasted_iota(jnp.int32, (2 * blk, LANES), 1) < HEAD_DIM
    low_out = lax.broadcasted_iota(jnp.int32, (rows, LANES), 1) < HEAD_DIM
    kf = jnp.concatenate([kp_ref[0], kc_ref[0]], axis=0).astype(F32)
    vf = jnp.concatenate([vp_ref[0], vc_ref[0]], axis=0).astype(F32)
    scale = HEAD_DIM ** -0.5

    def halves(xf, kv):
        xc = xf[:, (kv // 2) * LANES:(kv // 2 + 1) * LANES]
        xr = pltpu.roll(xc, HEAD_DIM, axis=1)
        lo_src, hi_src = (xc, xr) if kv % 2 == 0 else (xr, xc)
        return (jnp.where(low, lo_src, 0.0).astype(BF16), jnp.where(low, 0.0, hi_src).astype(BF16))

    for kv in range(KV_HEADS):
        k_lo, k_hi = halves(kf, kv)
        v_lo, v_hi = halves(vf, kv)
        q4 = q_ref[kv * PAIRS:(kv + 1) * PAIRS].reshape(rows, PAIR_W) * scale

        def probs(k_half, half):
            s = lax.dot_general(q4, k_half, (((1,), (1,)), ((), ())), preferred_element_type=F32)
            s = s + bias_ref[first, kv, half]
            sink = sinkcol_ref[kv, half]
            mx = jnp.maximum(jnp.max(s, axis=-1, keepdims=True), sink)
            p = jnp.exp(s - jnp.concatenate([mx, mx], axis=1))
            den = jnp.sum(p, axis=-1, keepdims=True) + jnp.exp(sink - mx)
            return p.astype(BF16), den

        p_lo, den_lo = probs(k_lo, 0)
        p_hi, den_hi = probs(k_hi, 1)
        o4 = (jnp.dot(p_lo, v_lo, preferred_element_type=F32)
              + jnp.dot(p_hi, v_hi, preferred_element_type=F32)) / jnp.where(low_out, den_lo, den_hi)
        for pr in range(PAIRS):
            at = (kv * PAIRS + pr) * PAIR_W
            o_ref[0, :, at:at + PAIR_W] = o4[pr * blk:(pr + 1) * blk].astype(o_ref.dtype)


def _t5_bucket(dist):
    n = jnp.maximum(dist, 0)
    max_exact = N_BUCKETS // 2
    nf = jnp.maximum(n, 1).astype(F32)
    large = max_exact + (jnp.log(nf / max_exact) / math.log(MAX_DIST / max_exact)
                         * (N_BUCKETS - max_exact)).astype(jnp.int32)
    large = jnp.minimum(large, N_BUCKETS - 1)
    return jnp.where(n < max_exact, n, large)


def _swa(q, kv, sinks, rel_bias):
    b, nblk, n_pairs, blk, _ = q.shape
    t = nblk * blk
    qw = ATT_HEADS * HEAD_DIM
    kvw = KV_HEADS * HEAD_DIM
    i = jnp.arange(blk)
    j = jnp.arange(2 * blk)
    dist = (blk + i)[:, None] - j[None, :]
    idx = jnp.where((dist >= 0) & (dist < WINDOW), _t5_bucket(dist), -1).astype(jnp.int32)
    return pl.pallas_call(
        _swa_body,
        grid=(b, nblk),
        in_specs=[
            pl.BlockSpec(memory_space=pltpu.SMEM),
            pl.BlockSpec(memory_space=pltpu.SMEM),
            pl.BlockSpec((blk, 2 * blk), lambda bi, nb: (0, 0)),
            pl.BlockSpec((None, None, n_pairs, blk, PAIR_W), lambda bi, nb: (bi, nb, 0, 0, 0)),
            pl.BlockSpec((1, blk, kvw), lambda bi, nb: (bi, jnp.maximum(nb - 1, 0), 0)),
            pl.BlockSpec((1, blk, kvw), lambda bi, nb: (bi, nb, 0)),
            pl.BlockSpec((1, blk, kvw), lambda bi, nb: (bi, jnp.maximum(nb - 1, 0), 1)),
            pl.BlockSpec((1, blk, kvw), lambda bi, nb: (bi, nb, 1)),
        ],
        out_specs=pl.BlockSpec((1, blk, qw), lambda bi, nb: (bi, nb, 0)),
        out_shape=jax.ShapeDtypeStruct((b, t, qw), BF16),
        scratch_shapes=[pltpu.VMEM((2, KV_HEADS, 2, PAIRS * blk, 2 * blk), F32),
                        pltpu.VMEM((KV_HEADS, 2, PAIRS * blk, LANES), F32)],
        compiler_params=_params(),
        name="swa",
    )(rel_bias.reshape(-1), sinks, idx, q, kv, kv, kv, kv)


def kernel(x, rel_bias, norm_g, ffn_w_gu, ffn_w_down, rm_w_in, ml_conv_w, ml_gate_b, rm_head_g,
           rm_w_out, swa_w_in, swa_sinks, swa_w_out):
    b, t, d = x.shape
    n = b * t
    depth = norm_g.shape[0]
    main_cols = 4 * RET_HEADS * HEAD_W + 4 * ML_HEADS * HEAD_W
    xs = x.reshape(n, d)
    w_gu = ffn_w_gu.astype(BF16)
    w_down = ffn_w_down.astype(BF16)
    for layer in range(depth):
        g = norm_g[layer]
        if layer % 2 == 0:
            e = layer // 2
            w_in = rm_w_in[e]
            gate_cols = w_in.shape[1] - main_cols
            w_gate = jnp.pad(w_in[:, main_cols:], ((0, 0), (0, LANES - gate_cols))).astype(BF16)
            proj, gates = _norm_matmul(xs, g[0], w_in.astype(BF16), w_gate, main_cols // PROJ_TN, ROW_TILE, PROJ_TN)
            gates_row = gates[:, :gate_cols].reshape(b, t, gate_cols).transpose(0, 2, 1)
            mixed = _mixer(proj.reshape(main_cols // PROJ_TN, b, t, PROJ_TN), gates_row, ml_gate_b[e], ml_conv_w[e], rm_head_g[e])
            xs = _proj_norm_res(mixed.reshape(n, -1), rm_w_out[e].astype(BF16), xs, g[1], OUT_ROW_TILE)
        else:
            o = layer // 2
            q, kv = _swa_in_proj(xs, g[0], swa_w_in[o].astype(BF16), b, t, ROW_TILE, PROJ_ROW_BLOCK)
            att = _swa(q, kv.reshape(b, t, -1), swa_sinks[o], rel_bias)
            xs = _proj_norm_res(att.reshape(n, -1), swa_w_out[o].astype(BF16), xs, g[1], OUT_ROW_TILE)
        xs = _ffn(xs, g[2], w_gu, w_down, g[3], layer, ROW_TILE, FFN_TF, FFN_ROW_BLOCK)
    return xs.reshape(b, t, d)
```

```python
import functools
import math

import jax
import jax.numpy as jnp
from jax import lax
from jax.experimental import pallas as pl
from jax.experimental.pallas import tpu as pltpu

F32 = jnp.float32
BF16 = jnp.bfloat16

EPS = 1e-6
CHUNK = 128
RET_HEADS = 4
ML_HEADS = 4
HEAD_W = 256
CONV_W = 4
ROPE_BASE = 10000.0
ATT_HEADS = 32
KV_HEADS = 4
HEAD_DIM = 64
WINDOW = 128
ATT_BLOCK = 128
N_BUCKETS = 32
MAX_DIST = 128
LANES = 128
SUBLANES = 8
VMEM_LIMIT = 56 * 1024 * 1024

ROW_TILE = 1024
OUT_ROW_TILE = 512
PROJ_TN = 1024
PROJ_ROW_BLOCK = 256
MIXER_CHUNKS = 2
FFN_TF = 512
FFN_ROW_BLOCK = 512


def _params(vmem=VMEM_LIMIT):
    return pltpu.CompilerParams(vmem_limit_bytes=vmem)


def _rms_rows(x, g):
    ms = jnp.mean(x * x, axis=-1, keepdims=True)
    return x * lax.rsqrt(ms + EPS) * g


def _silu(x):
    return x * jax.nn.sigmoid(x)


def _norm_matmul_body(x_ref, g_ref, w_ref, ws_ref, o_ref, os_ref, h_ref, *, row_block):
    j = pl.program_id(1)

    @pl.when(j == 0)
    def _():
        for r in range(x_ref.shape[0] // row_block):
            rows = slice(r * row_block, (r + 1) * row_block)
            h = _rms_rows(x_ref[rows, :], g_ref[...]).astype(h_ref.dtype)
            h_ref[rows, :] = h
            os_ref[rows, :] = jnp.dot(h, ws_ref[...], preferred_element_type=F32)
            o_ref[rows, :] = jnp.dot(h, w_ref[...], preferred_element_type=F32).astype(o_ref.dtype)

    @pl.when(j > 0)
    def _():
        o_ref[...] = jnp.dot(h_ref[...], w_ref[...], preferred_element_type=F32).astype(o_ref.dtype)


def _norm_matmul(x, g, w, w_side, tiles, tm, tn):
    n, d = x.shape
    ns = w_side.shape[1]
    return pl.pallas_call(
        functools.partial(_norm_matmul_body, row_block=PROJ_ROW_BLOCK),
        grid=(n // tm, tiles),
        in_specs=[
            pl.BlockSpec((tm, d), lambda i, j: (i, 0)),
            pl.BlockSpec((1, d), lambda i, j: (0, 0)),
            pl.BlockSpec((d, tn), lambda i, j: (0, j)),
            pl.BlockSpec((d, ns), lambda i, j: (0, 0)),
        ],
        out_specs=[pl.BlockSpec((None, tm, tn), lambda i, j: (j, i, 0)),
                   pl.BlockSpec((tm, ns), lambda i, j: (i, 0))],
        out_shape=[jax.ShapeDtypeStruct((tiles, n, tn), BF16), jax.ShapeDtypeStruct((n, ns), F32)],
        scratch_shapes=[pltpu.VMEM((tm, d), BF16)],
        compiler_params=_params(),
        name="norm_matmul",
    )(x, g.reshape(1, d), w, w_side)


def _proj_norm_res_body(a_ref, w_ref, x_ref, g_ref, o_ref):
    y = jnp.dot(a_ref[...], w_ref[...], preferred_element_type=F32)
    o_ref[...] = x_ref[...] + _rms_rows(y, g_ref[...])


def _proj_norm_res(a, w, x, g, tm):
    n, d = x.shape
    return pl.pallas_call(
        _proj_norm_res_body,
        grid=(n // tm,),
        in_specs=[
            pl.BlockSpec((tm, a.shape[1]), lambda i: (i, 0)),
            pl.BlockSpec(w.shape, lambda i: (0, 0)),
            pl.BlockSpec((tm, d), lambda i: (i, 0)),
            pl.BlockSpec((1, d), lambda i: (0, 0)),
        ],
        out_specs=pl.BlockSpec((tm, d), lambda i: (i, 0)),
        out_shape=jax.ShapeDtypeStruct((n, d), F32),
        compiler_params=_params(),
        name="proj_norm_res",
    )(a, w, x, g.reshape(1, d))


def _ffn_body(x_ref, g_in_ref, wg_ref, wu_ref, wd_ref, g_out_ref, o_ref, h_ref, *, row_block):
    f = pl.program_id(1)
    last = pl.num_programs(1) - 1
    blocks = [slice(r * row_block, (r + 1) * row_block) for r in range(x_ref.shape[0] // row_block)]

    def partial_down(rows):
        h = h_ref[rows, :]
        gate = jnp.dot(h, wg_ref[...], preferred_element_type=F32)
        up = jnp.dot(h, wu_ref[...], preferred_element_type=F32)
        act = (_silu(gate) * up).astype(BF16)
        return jnp.dot(act, wd_ref[...], preferred_element_type=F32)

    @pl.when(f == 0)
    def _():
        for rows in blocks:
            h_ref[rows, :] = _rms_rows(x_ref[rows, :], g_in_ref[...]).astype(h_ref.dtype)
            o_ref[rows, :] = partial_down(rows)

    @pl.when((f > 0) & (f < last))
    def _():
        for rows in blocks:
            o_ref[rows, :] += partial_down(rows)

    @pl.when(f == last)
    def _():
        for rows in blocks:
            y = o_ref[rows, :] + partial_down(rows)
            o_ref[rows, :] = x_ref[rows, :] + _rms_rows(y, g_out_ref[...])


def _ffn(x, g_in, w_gu, w_down, g_out, layer, tm, tf, row_block):
    n, d = x.shape
    nf = w_down.shape[1] // tf
    return pl.pallas_call(
        functools.partial(_ffn_body, row_block=row_block),
        grid=(n // tm, nf),
        in_specs=[
            pl.BlockSpec((tm, d), lambda i, f: (i, 0)),
            pl.BlockSpec((1, d), lambda i, f: (0, 0)),
            pl.BlockSpec((None, d, tf), lambda i, f: (layer, 0, f)),
            pl.BlockSpec((None, d, tf), lambda i, f: (layer, 0, f + nf)),
            pl.BlockSpec((None, tf, d), lambda i, f: (layer, f, 0)),
            pl.BlockSpec((1, d), lambda i, f: (0, 0)),
        ],
        out_specs=pl.BlockSpec((tm, d), lambda i, f: (i, 0)),
        out_shape=jax.ShapeDtypeStruct((n, d), F32),
        scratch_shapes=[pltpu.VMEM((tm, d), BF16)],
        compiler_params=_params(),
        name="ffn",
    )(x, g_in.reshape(1, d), w_gu, w_gu, w_down, g_out.reshape(1, d))


def _log_sigmoid(x):
    return jnp.minimum(x, 0.0) - jnp.log1p(jnp.exp(-jnp.abs(x)))


_NT = (((1,), (1,)), ((), ()))
_TN = (((0,), (0,)), ((), ()))


def _retention_head(h, rows, q_ref, k_ref, v_ref, g_ref, cos, sin, dm_ref, wq_ref, wk_ref, gl_ref, hg_ref,
                    o_ref, state_ref):
    cols = slice(h * HEAD_W, (h + 1) * HEAD_W)
    half = HEAD_W // 2

    def rot(x):
        x = x.astype(F32)
        x1, x2 = x[:, :half], x[:, half:]
        return jnp.concatenate([x1 * cos - x2 * sin, x2 * cos + x1 * sin], axis=-1)

    q = rot(q_ref[0, rows, cols])
    k = rot(k_ref[0, rows, cols]) * (HEAD_W ** -0.5)
    v = v_ref[0, rows, cols]
    qb = q.astype(BF16)
    s = lax.dot_general(qb, k.astype(BF16), _NT, preferred_element_type=F32) * dm_ref[h]
    intra = jnp.dot(s.astype(BF16), v, preferred_element_type=F32)
    state = state_ref[h]
    inter = jnp.dot(qb, state.astype(BF16), preferred_element_type=F32) * wq_ref[h]
    out = intra + inter
    kw = (k * wk_ref[h]).astype(BF16)
    state_ref[h] = state * gl_ref[h] + lax.dot_general(kw, v, _TN, preferred_element_type=F32)
    y = _rms_rows(out, hg_ref[:, cols])
    o_ref[rows, cols] = (y * _silu(g_ref[0, rows, cols].astype(F32))).astype(o_ref.dtype)


def _causal_conv(cur, prev, shift_ref, w):
    shifted = jnp.dot(shift_ref[...], jnp.concatenate([prev, cur], axis=0), preferred_element_type=F32)
    y = cur.astype(F32) * w[CONV_W - 1:CONV_W]
    for s in range(1, CONV_W):
        y = y + shifted[(s - 1) * CHUNK:s * CHUNK] * w[CONV_W - 1 - s:CONV_W - s]
    return y


def _mlstm_head(h, rows, q_all, k_all, v_ref, og_ref, gx, b_rows, lower_tri, hg_ref, o_ref, c_ref, n_ref,
                m_ref):
    cols = slice(h * HEAD_W, (h + 1) * HEAD_W)
    q = q_all[:, cols]
    k = k_all[:, cols]
    v = v_ref[0, rows, cols]

    i_row = gx[h:h + 1]
    lf_row = gx[ML_HEADS + h:ML_HEADS + h + 1]
    b_row = b_rows[ML_HEADS + h:ML_HEADS + h + 1]
    b_cols = lax.dot_general(lower_tri, jnp.broadcast_to(lf_row, (CHUNK, CHUNK)), _NT,
                             precision=lax.Precision.HIGHEST, preferred_element_type=F32)
    i_cols = jnp.broadcast_to(i_row, (CHUNK, CHUNK)).T
    rowi = lax.broadcasted_iota(jnp.int32, (CHUNK, CHUNK), 0)
    coli = lax.broadcasted_iota(jnp.int32, (CHUNK, CHUNK), 1)
    log_d = jnp.where(rowi >= coli, b_cols - b_row + i_row, -jnp.inf)
    m_prev = m_ref[h]
    inter_log = b_cols + m_prev
    m_t = jnp.maximum(inter_log, jnp.max(log_d, axis=-1, keepdims=True))
    d_mat = jnp.exp(log_d - m_t)
    w_inter = jnp.exp(inter_log - m_t)

    def wide(a):
        return jnp.concatenate([a] * (HEAD_W // LANES), axis=1)

    qb = q.astype(BF16)
    s = lax.dot_general(qb, k.astype(BF16), _NT, preferred_element_type=F32) * d_mat
    c_state = c_ref[h]
    n_state = n_ref[h]
    num = (jnp.dot(s.astype(BF16), v, preferred_element_type=F32)
           + wide(w_inter) * jnp.dot(qb, c_state.astype(BF16), preferred_element_type=F32))
    den = (jnp.sum(s, axis=-1, keepdims=True)
           + w_inter * jnp.sum(q * n_state, axis=-1, keepdims=True))
    hid = num / wide(jnp.maximum(jnp.abs(den), jnp.exp(-m_t)))

    b_last = b_cols[CHUNK - 1:CHUNK]
    log_w = b_last - b_cols + i_cols
    m_new = jnp.maximum(b_last + m_prev, jnp.max(log_w, axis=0, keepdims=True))
    w = jnp.exp(log_w - m_new)
    decay = wide(jnp.exp(b_last + m_prev - m_new))
    kw = k * wide(w)
    c_ref[h] = decay * c_state + lax.dot_general(kw.astype(BF16), v, _TN, preferred_element_type=F32)
    n_ref[h] = decay * n_state + jnp.sum(kw, axis=0, keepdims=True)
    m_ref[h] = m_new

    out_cols = slice(RET_HEADS * HEAD_W + h * HEAD_W, RET_HEADS * HEAD_W + (h + 1) * HEAD_W)
    y = _rms_rows(hid, hg_ref[:, out_cols])
    o_ref[rows, out_cols] = (y * jax.nn.sigmoid(og_ref[0, rows, cols].astype(F32))).astype(o_ref.dtype)


def _mixer_body(gl_ref, rq_ref, rk_ref, rv_ref, rg_ref, mq_ref, mqp_ref, mk_ref, mkp_ref, mv_ref, mo_ref,
                cos_ref, sin_ref, dm_ref, wq_ref, wk_ref, gr_ref, gb_ref, cw_ref, shift_ref, hg_ref,
                wo_ref, x_ref, go_ref, o_ref, r_ref, c_ref, n_ref, m_ref, mixed_ref):
    first = pl.program_id(1) == 0

    @pl.when(first)
    def _():
        r_ref[...] = jnp.zeros_like(r_ref)
        c_ref[...] = jnp.zeros_like(c_ref)
        n_ref[...] = jnp.zeros_like(n_ref)
        m_ref[...] = jnp.zeros_like(m_ref)

    ml_w = ML_HEADS * HEAD_W
    rowi = lax.broadcasted_iota(jnp.int32, (CHUNK, CHUNK), 0)
    coli = lax.broadcasted_iota(jnp.int32, (CHUNK, CHUNK), 1)
    upper_tri = jnp.where(rowi <= coli, 1.0, 0.0).astype(F32)
    lower_tri = jnp.where(rowi >= coli, 1.0, 0.0).astype(F32)
    is_input = lax.broadcasted_iota(jnp.int32, (2 * ML_HEADS, CHUNK), 0) < ML_HEADS

    for ci in range(mq_ref.shape[1] // CHUNK):
        rows = slice(ci * CHUNK, (ci + 1) * CHUNK)
        cos = cos_ref[rows, :]
        sin = sin_ref[rows, :]
        for h in range(RET_HEADS):
            _retention_head(h, rows, rq_ref, rk_ref, rv_ref, rg_ref, cos, sin, dm_ref, wq_ref, wk_ref, gl_ref,
                            hg_ref, mixed_ref, r_ref)

        if ci == 0:
            q_prev = jnp.where(first, jnp.zeros_like(mqp_ref[0]), mqp_ref[0])
            k_prev = jnp.where(first, jnp.zeros_like(mkp_ref[0]), mkp_ref[0])
        else:
            before = slice((ci - 1) * CHUNK, ci * CHUNK)
            q_prev, k_prev = mq_ref[0, before, :], mk_ref[0, before, :]
        q_all = _silu(_causal_conv(mq_ref[0, rows, :], q_prev, shift_ref, cw_ref[:, :ml_w]))
        k_all = _silu(_causal_conv(mk_ref[0, rows, :], k_prev, shift_ref, cw_ref[:, ml_w:])) * (HEAD_W ** -0.5)
        gates = gr_ref[0, :, rows] + gb_ref[...]
        gx = jnp.where(is_input, gates, _log_sigmoid(gates))
        b_rows = jnp.dot(gx, upper_tri, precision=lax.Precision.HIGHEST, preferred_element_type=F32)
        for h in range(ML_HEADS):
            _mlstm_head(h, rows, q_all, k_all, mv_ref, mo_ref, gx, b_rows, lower_tri, hg_ref, mixed_ref,
                        c_ref, n_ref, m_ref)

        y = jnp.dot(mixed_ref[rows, :], wo_ref[...], preferred_element_type=F32)
        o_ref[0, rows, :] = x_ref[0, rows, :] + _rms_rows(y, go_ref[...])


def _mixer(proj, gates_row, gate_b, conv_w, head_g, w_out, x, g_out):
    _, b, t, _ = proj.shape
    d = x.shape[2]
    step = MIXER_CHUNKS * CHUNK
    nc = t // step
    ret_w = RET_HEADS * HEAD_W
    ml_w = ML_HEADS * HEAD_W
    assert ret_w == ml_w == proj.shape[3]
    log_g = jnp.log1p(-jnp.exp2(-5.0 - jnp.arange(RET_HEADS, dtype=F32)))
    idx = jnp.arange(CHUNK, dtype=F32)
    diff = idx[:, None] - idx[None, :]
    dmask = jnp.where(diff >= 0, jnp.exp(log_g[:, None, None] * jnp.maximum(diff, 0.0)), 0.0)
    w_k = jnp.exp(log_g[:, None] * (CHUNK - 1.0 - idx)[None, :])
    w_q = jnp.exp(log_g[:, None] * (idx + 1.0)[None, :])
    w_k = jnp.broadcast_to(w_k[..., None], (RET_HEADS, CHUNK, HEAD_W))
    w_q = jnp.broadcast_to(w_q[..., None], (RET_HEADS, CHUNK, HEAD_W))
    g_l = jnp.exp(log_g * CHUNK)
    half = HEAD_W // 2
    inv = 1.0 / (ROPE_BASE ** jnp.linspace(0.0, 1.0, half, dtype=F32))
    ang = jnp.arange(t).astype(F32)[:, None] * inv[None, :]
    cos, sin = jnp.cos(ang), jnp.sin(ang)
    r = jnp.arange((CONV_W - 1) * CHUNK)
    src = CHUNK + r % CHUNK - (r // CHUNK + 1)
    shift = (jnp.arange(2 * CHUNK)[None, :] == src[:, None]).astype(BF16)

    def group(gi):
        return pl.BlockSpec((None, 1, step, ret_w), lambda bi, c: (gi, bi, c, 0))

    def prev_group(gi):
        return pl.BlockSpec((None, 1, CHUNK, ml_w),
                            lambda bi, c: (gi, bi, jnp.maximum(c * MIXER_CHUNKS - 1, 0), 0))

    def whole(a):
        return pl.BlockSpec(a.shape, lambda bi, c: (0,) * a.ndim)

    gb = gate_b.reshape(2 * ML_HEADS, 1)
    hg = head_g.reshape(1, ret_w + ml_w)
    go = g_out.reshape(1, d)
    return pl.pallas_call(
        _mixer_body,
        grid=(b, nc),
        in_specs=[
            pl.BlockSpec(memory_space=pltpu.SMEM),
            group(0), group(1), group(2), group(3),
            group(4), prev_group(4), group(5), prev_group(5), group(6), group(7),
            pl.BlockSpec((step, half), lambda bi, c: (c, 0)),
            pl.BlockSpec((step, half), lambda bi, c: (c, 0)),
            whole(dmask), whole(w_q), whole(w_k),
            pl.BlockSpec((1, 2 * ML_HEADS, step), lambda bi, c: (bi, 0, c)),
            whole(gb), whole(conv_w), whole(shift), whole(hg),
            whole(w_out),
            pl.BlockSpec((1, step, d), lambda bi, c: (bi, c, 0)),
            whole(go),
        ],
        out_specs=pl.BlockSpec((1, step, d), lambda bi, c: (bi, c, 0)),
        out_shape=jax.ShapeDtypeStruct((b, t, d), F32),
        scratch_shapes=[pltpu.VMEM((RET_HEADS, HEAD_W, HEAD_W), F32),
                        pltpu.VMEM((ML_HEADS, HEAD_W, HEAD_W), F32),
                        pltpu.VMEM((ML_HEADS, 1, HEAD_W), F32),
                        pltpu.VMEM((ML_HEADS, 1, LANES), F32),
                        pltpu.VMEM((step, ret_w + ml_w), BF16)],
        compiler_params=_params(),
        name="mixer",
    )(g_l, proj, proj, proj, proj, proj, proj, proj, proj, proj, proj, cos, sin, dmask, w_q, w_k,
      gates_row, gb, conv_w, shift, hg, w_out, x, go)


PAIR_W = 2 * HEAD_DIM
GROUP = ATT_HEADS // KV_HEADS
PAIRS = GROUP // 2


def _swa_in_proj_body(x_ref, g_ref, w_ref, q_ref, kv_ref, *, row_block):
    qw = ATT_HEADS * HEAD_DIM
    per_block = row_block // ATT_BLOCK
    for r in range(x_ref.shape[0] // row_block):
        rows = slice(r * row_block, (r + 1) * row_block)
        h = _rms_rows(x_ref[rows, :], g_ref[...]).astype(BF16)
        res = jnp.dot(h, w_ref[...], preferred_element_type=F32)
        for nbl in range(per_block):
            sub = slice(nbl * ATT_BLOCK, (nbl + 1) * ATT_BLOCK)
            for p in range(q_ref.shape[2]):
                q_ref[0, r * per_block + nbl, p] = res[sub, p * PAIR_W:(p + 1) * PAIR_W].astype(q_ref.dtype)
        kv_ref[rows, :] = res[:, qw:].astype(kv_ref.dtype)


def _swa_in_proj(x, g, w, b, t, tm, row_block):
    n, d = x.shape
    cols = w.shape[1]
    kvw = cols - ATT_HEADS * HEAD_DIM
    tiles_per_seq = t // tm
    return pl.pallas_call(
        functools.partial(_swa_in_proj_body, row_block=row_block),
        grid=(n // tm,),
        in_specs=[
            pl.BlockSpec((tm, d), lambda i: (i, 0)),
            pl.BlockSpec((1, d), lambda i: (0, 0)),
            pl.BlockSpec((d, cols), lambda i: (0, 0)),
        ],
        out_specs=[
            pl.BlockSpec((1, tm // ATT_BLOCK, ATT_HEADS // 2, ATT_BLOCK, PAIR_W),
                         lambda i: (i // tiles_per_seq, i % tiles_per_seq, 0, 0, 0)),
            pl.BlockSpec((tm, kvw), lambda i: (i, 0)),
        ],
        out_shape=[jax.ShapeDtypeStruct((b, t // ATT_BLOCK, ATT_HEADS // 2, ATT_BLOCK, PAIR_W), BF16),
                   jax.ShapeDtypeStruct((n, kvw), BF16)],
        compiler_params=_params(),
        name="swa_in_proj",
    )(x, g.reshape(1, d), w)


def _swa_body(rb_ref, sink_ref, idx_ref, q_ref, kp_ref, kc_ref, vp_ref, vc_ref, o_ref, bias_ref, sinkcol_ref):
    bi = pl.program_id(0)
    nb = pl.program_id(1)
    blk = ATT_BLOCK
    rows = PAIRS * blk

    @pl.when((bi == 0) & (nb == 0))
    def _():
        idx = idx_ref[...]
        col = lax.broadcasted_iota(jnp.int32, idx.shape, 1)

        def head_body(hd, carry):
            def bucket_body(bk, acc):
                return jnp.where(idx == bk, rb_ref[bk * ATT_HEADS + hd], acc)

            tbl = lax.fori_loop(0, N_BUCKETS, bucket_body, jnp.full(idx.shape, -jnp.inf, F32))
            kv = hd // GROUP
            half = hd % 2
            at = pl.ds(pl.multiple_of(((hd % GROUP) // 2) * blk, blk), blk)
            bias_ref[0, kv, half, at, :] = tbl
            bias_ref[1, kv, half, at, :] = jnp.where(col >= blk, tbl, -jnp.inf)
            sinkcol_ref[kv, half, at, :] = jnp.full((blk, LANES), sink_ref[hd], F32)
            return carry

        lax.fori_loop(0, ATT_HEADS, head_body, 0)

    first = (nb == 0).astype(jnp.int32)
    low = lax.broadcasted_iota(jnp.int32, (2 * blk, LANES), 1) < HEAD_DIM
    low_out = lax.broadcasted_iota(jnp.int32, (rows, LANES), 1) < HEAD_DIM
    kf = jnp.concatenate([kp_ref[0], kc_ref[0]], axis=0).astype(F32)
    vf = jnp.concatenate([vp_ref[0], vc_ref[0]], axis=0).astype(F32)
    scale = HEAD_DIM ** -0.5

    def halves(xf, kv):
        xc = xf[:, (kv // 2) * LANES:(kv // 2 + 1) * LANES]
        xr = pltpu.roll(xc, HEAD_DIM, axis=1)
        lo_src, hi_src = (xc, xr) if kv % 2 == 0 else (xr, xc)
        return (jnp.where(low, lo_src, 0.0).astype(BF16), jnp.where(low, 0.0, hi_src).astype(BF16))

    for kv in range(KV_HEADS):
        k_lo, k_hi = halves(kf, kv)
        v_lo, v_hi = halves(vf, kv)
        q4 = q_ref[kv * PAIRS:(kv + 1) * PAIRS].reshape(rows, PAIR_W) * scale

        def probs(k_half, half):
            s = lax.dot_general(q4, k_half, (((1,), (1,)), ((), ())), preferred_element_type=F32)
            s = s + bias_ref[first, kv, half]
            sink = sinkcol_ref[kv, half]
            mx = jnp.maximum(jnp.max(s, axis=-1, keepdims=True), sink)
            p = jnp.exp(s - jnp.concatenate([mx, mx], axis=1))
            den = jnp.sum(p, axis=-1, keepdims=True) + jnp.exp(sink - mx)
            return p.astype(BF16), den

        p_lo, den_lo = probs(k_lo, 0)
        p_hi, den_hi = probs(k_hi, 1)
        o4 = (jnp.dot(p_lo, v_lo, preferred_element_type=F32)
              + jnp.dot(p_hi, v_hi, preferred_element_type=F32)) / jnp.where(low_out, den_lo, den_hi)
        for pr in range(PAIRS):
            at = (kv * PAIRS + pr) * PAIR_W
            o_ref[0, :, at:at + PAIR_W] = o4[pr * blk:(pr + 1) * blk].astype(o_ref.dtype)


def _t5_bucket(dist):
    n = jnp.maximum(dist, 0)
    max_exact = N_BUCKETS // 2
    nf = jnp.maximum(n, 1).astype(F32)
    large = max_exact + (jnp.log(nf / max_exact) / math.log(MAX_DIST / max_exact)
                         * (N_BUCKETS - max_exact)).astype(jnp.int32)
    large = jnp.minimum(large, N_BUCKETS - 1)
    return jnp.where(n < max_exact, n, large)


def _swa(q, kv, sinks, rel_bias):
    b, nblk, n_pairs, blk, _ = q.shape
    t = nblk * blk
    qw = ATT_HEADS * HEAD_DIM
    kvw = KV_HEADS * HEAD_DIM
    i = jnp.arange(blk)
    j = jnp.arange(2 * blk)
    dist = (blk + i)[:, None] - j[None, :]
    idx = jnp.where((dist >= 0) & (dist < WINDOW), _t5_bucket(dist), -1).astype(jnp.int32)
    return pl.pallas_call(
        _swa_body,
        grid=(b, nblk),
        in_specs=[
            pl.BlockSpec(memory_space=pltpu.SMEM),
            pl.BlockSpec(memory_space=pltpu.SMEM),
            pl.BlockSpec((blk, 2 * blk), lambda bi, nb: (0, 0)),
            pl.BlockSpec((None, None, n_pairs, blk, PAIR_W), lambda bi, nb: (bi, nb, 0, 0, 0)),
            pl.BlockSpec((1, blk, kvw), lambda bi, nb: (bi, jnp.maximum(nb - 1, 0), 0)),
            pl.BlockSpec((1, blk, kvw), lambda bi, nb: (bi, nb, 0)),
            pl.BlockSpec((1, blk, kvw), lambda bi, nb: (bi, jnp.maximum(nb - 1, 0), 1)),
            pl.BlockSpec((1, blk, kvw), lambda bi, nb: (bi, nb, 1)),
        ],
        out_specs=pl.BlockSpec((1, blk, qw), lambda bi, nb: (bi, nb, 0)),
        out_shape=jax.ShapeDtypeStruct((b, t, qw), BF16),
        scratch_shapes=[pltpu.VMEM((2, KV_HEADS, 2, PAIRS * blk, 2 * blk), F32),
                        pltpu.VMEM((KV_HEADS, 2, PAIRS * blk, LANES), F32)],
        compiler_params=_params(),
        name="swa",
    )(rel_bias.reshape(-1), sinks, idx, q, kv, kv, kv, kv)


def kernel(x, rel_bias, norm_g, ffn_w_gu, ffn_w_down, rm_w_in, ml_conv_w, ml_gate_b, rm_head_g,
           rm_w_out, swa_w_in, swa_sinks, swa_w_out):
    b, t, d = x.shape
    n = b * t
    depth = norm_g.shape[0]
    main_cols = 4 * RET_HEADS * HEAD_W + 4 * ML_HEADS * HEAD_W
    xs = x.reshape(n, d)
    w_gu = ffn_w_gu.astype(BF16)
    w_down = ffn_w_down.astype(BF16)
    for layer in range(depth):
        g = norm_g[layer]
        if layer % 2 == 0:
            e = layer // 2
            w_in = rm_w_in[e]
            gate_cols = w_in.shape[1] - main_cols
            w_gate = jnp.pad(w_in[:, main_cols:], ((0, 0), (0, LANES - gate_cols))).astype(BF16)
            proj, gates = _norm_matmul(xs, g[0], w_in.astype(BF16), w_gate, main_cols // PROJ_TN, ROW_TILE, PROJ_TN)
            gates_row = gates[:, :gate_cols].reshape(b, t, gate_cols).transpose(0, 2, 1)
            xs = _mixer(proj.reshape(main_cols // PROJ_TN, b, t, PROJ_TN), gates_row, ml_gate_b[e], ml_conv_w[e],
                        rm_head_g[e], rm_w_out[e].astype(BF16), xs.reshape(b, t, d), g[1]).reshape(n, d)
        else:
            o = layer // 2
            q, kv = _swa_in_proj(xs, g[0], swa_w_in[o].astype(BF16), b, t, ROW_TILE, PROJ_ROW_BLOCK)
            att = _swa(q, kv.reshape(b, t, -1), swa_sinks[o], rel_bias)
            xs = _proj_norm_res(att.reshape(n, -1), swa_w_out[o].astype(BF16), xs, g[1], OUT_ROW_TILE)
        xs = _ffn(xs, g[2], w_gu, w_down, g[3], layer, ROW_TILE, FFN_TF, FFN_ROW_BLOCK)
    return xs.reshape(b, t, d)
```

```python
import functools
import math

import jax
import jax.numpy as jnp
from jax import lax
from jax.experimental import pallas as pl
from jax.experimental.pallas import tpu as pltpu

F32 = jnp.float32
BF16 = jnp.bfloat16

EPS = 1e-6
CHUNK = 128
RET_HEADS = 4
ML_HEADS = 4
HEAD_W = 256
CONV_W = 4
ROPE_BASE = 10000.0
ATT_HEADS = 32
KV_HEADS = 4
HEAD_DIM = 64
WINDOW = 128
ATT_BLOCK = 128
N_BUCKETS = 32
MAX_DIST = 128
LANES = 128
SUBLANES = 8
VMEM_LIMIT = 56 * 1024 * 1024

ROW_TILE = 1024
OUT_ROW_TILE = 512
PROJ_TN = 1024
PROJ_ROW_BLOCK = 256
MIXER_CHUNKS = 1
MIXER_SEQS = 2
FFN_TF = 512
FFN_ROW_BLOCK = 512


def _params(vmem=VMEM_LIMIT):
    return pltpu.CompilerParams(vmem_limit_bytes=vmem)


def _rms_rows(x, g):
    ms = jnp.mean(x * x, axis=-1, keepdims=True)
    return x * lax.rsqrt(ms + EPS) * g


def _silu(x):
    return x * jax.nn.sigmoid(x)


def _norm_matmul_body(x_ref, g_ref, w_ref, ws_ref, o_ref, os_ref, h_ref, *, row_block):
    j = pl.program_id(1)

    @pl.when(j == 0)
    def _():
        for r in range(x_ref.shape[0] // row_block):
            rows = slice(r * row_block, (r + 1) * row_block)
            h = _rms_rows(x_ref[rows, :], g_ref[...]).astype(h_ref.dtype)
            h_ref[rows, :] = h
            os_ref[rows, :] = jnp.dot(h, ws_ref[...], preferred_element_type=F32)
            o_ref[rows, :] = jnp.dot(h, w_ref[...], preferred_element_type=F32).astype(o_ref.dtype)

    @pl.when(j > 0)
    def _():
        o_ref[...] = jnp.dot(h_ref[...], w_ref[...], preferred_element_type=F32).astype(o_ref.dtype)


def _norm_matmul(x, g, w, w_side, tiles, tm, tn):
    n, d = x.shape
    ns = w_side.shape[1]
    return pl.pallas_call(
        functools.partial(_norm_matmul_body, row_block=PROJ_ROW_BLOCK),
        grid=(n // tm, tiles),
        in_specs=[
            pl.BlockSpec((tm, d), lambda i, j: (i, 0)),
            pl.BlockSpec((1, d), lambda i, j: (0, 0)),
            pl.BlockSpec((d, tn), lambda i, j: (0, j)),
            pl.BlockSpec((d, ns), lambda i, j: (0, 0)),
        ],
        out_specs=[pl.BlockSpec((None, tm, tn), lambda i, j: (j, i, 0)),
                   pl.BlockSpec((tm, ns), lambda i, j: (i, 0))],
        out_shape=[jax.ShapeDtypeStruct((tiles, n, tn), BF16), jax.ShapeDtypeStruct((n, ns), F32)],
        scratch_shapes=[pltpu.VMEM((tm, d), BF16)],
        compiler_params=_params(),
        name="norm_matmul",
    )(x, g.reshape(1, d), w, w_side)


def _proj_norm_res_body(a_ref, w_ref, x_ref, g_ref, o_ref):
    y = jnp.dot(a_ref[...], w_ref[...], preferred_element_type=F32)
    o_ref[...] = x_ref[...] + _rms_rows(y, g_ref[...])


def _proj_norm_res(a, w, x, g, tm):
    n, d = x.shape
    return pl.pallas_call(
        _proj_norm_res_body,
        grid=(n // tm,),
        in_specs=[
            pl.BlockSpec((tm, a.shape[1]), lambda i: (i, 0)),
            pl.BlockSpec(w.shape, lambda i: (0, 0)),
            pl.BlockSpec((tm, d), lambda i: (i, 0)),
            pl.BlockSpec((1, d), lambda i: (0, 0)),
        ],
        out_specs=pl.BlockSpec((tm, d), lambda i: (i, 0)),
        out_shape=jax.ShapeDtypeStruct((n, d), F32),
        compiler_params=_params(),
        name="proj_norm_res",
    )(a, w, x, g.reshape(1, d))


def _ffn_body(x_ref, g_in_ref, wg_ref, wu_ref, wd_ref, g_out_ref, o_ref, h_ref, *, row_block):
    f = pl.program_id(1)
    last = pl.num_programs(1) - 1
    blocks = [slice(r * row_block, (r + 1) * row_block) for r in range(x_ref.shape[0] // row_block)]

    def partial_down(rows):
        h = h_ref[rows, :]
        gate = jnp.dot(h, wg_ref[...], preferred_element_type=F32)
        up = jnp.dot(h, wu_ref[...], preferred_element_type=F32)
        act = (_silu(gate) * up).astype(BF16)
        return jnp.dot(act, wd_ref[...], preferred_element_type=F32)

    @pl.when(f == 0)
    def _():
        for rows in blocks:
            h_ref[rows, :] = _rms_rows(x_ref[rows, :], g_in_ref[...]).astype(h_ref.dtype)
            o_ref[rows, :] = partial_down(rows)

    @pl.when((f > 0) & (f < last))
    def _():
        for rows in blocks:
            o_ref[rows, :] += partial_down(rows)

    @pl.when(f == last)
    def _():
        for rows in blocks:
            y = o_ref[rows, :] + partial_down(rows)
            o_ref[rows, :] = x_ref[rows, :] + _rms_rows(y, g_out_ref[...])


def _ffn(x, g_in, w_gu, w_down, g_out, layer, tm, tf, row_block):
    n, d = x.shape
    nf = w_down.shape[1] // tf
    return pl.pallas_call(
        functools.partial(_ffn_body, row_block=row_block),
        grid=(n // tm, nf),
        in_specs=[
            pl.BlockSpec((tm, d), lambda i, f: (i, 0)),
            pl.BlockSpec((1, d), lambda i, f: (0, 0)),
            pl.BlockSpec((None, d, tf), lambda i, f: (layer, 0, f)),
            pl.BlockSpec((None, d, tf), lambda i, f: (layer, 0, f + nf)),
            pl.BlockSpec((None, tf, d), lambda i, f: (layer, f, 0)),
            pl.BlockSpec((1, d), lambda i, f: (0, 0)),
        ],
        out_specs=pl.BlockSpec((tm, d), lambda i, f: (i, 0)),
        out_shape=jax.ShapeDtypeStruct((n, d), F32),
        scratch_shapes=[pltpu.VMEM((tm, d), BF16)],
        compiler_params=_params(),
        name="ffn",
    )(x, g_in.reshape(1, d), w_gu, w_gu, w_down, g_out.reshape(1, d))


def _log_sigmoid(x):
    return jnp.minimum(x, 0.0) - jnp.log1p(jnp.exp(-jnp.abs(x)))


_NT = (((1,), (1,)), ((), ()))
_TN = (((0,), (0,)), ((), ()))


def _retention_head(h, bb, rows, q_ref, k_ref, v_ref, g_ref, cos, sin, dm_ref, wq_ref, wk_ref, gl_ref, hg_ref,
                    o_ref, state_ref):
    cols = slice(h * HEAD_W, (h + 1) * HEAD_W)
    half = HEAD_W // 2

    def rot(x):
        x = x.astype(F32)
        x1, x2 = x[:, :half], x[:, half:]
        return jnp.concatenate([x1 * cos - x2 * sin, x2 * cos + x1 * sin], axis=-1)

    q = rot(q_ref[bb, rows, cols])
    k = rot(k_ref[bb, rows, cols]) * (HEAD_W ** -0.5)
    v = v_ref[bb, rows, cols]
    qb = q.astype(BF16)
    s = lax.dot_general(qb, k.astype(BF16), _NT, preferred_element_type=F32) * dm_ref[h]
    intra = jnp.dot(s.astype(BF16), v, preferred_element_type=F32)
    si = bb * RET_HEADS + h
    state = state_ref[si]
    inter = jnp.dot(qb, state.astype(BF16), preferred_element_type=F32) * wq_ref[h]
    out = intra + inter
    kw = (k * wk_ref[h]).astype(BF16)
    state_ref[si] = state * gl_ref[h] + lax.dot_general(kw, v, _TN, preferred_element_type=F32)
    y = _rms_rows(out, hg_ref[:, cols])
    o_ref[bb, rows, cols] = (y * _silu(g_ref[bb, rows, cols].astype(F32))).astype(o_ref.dtype)


def _causal_conv(cur, prev, shift_ref, w):
    shifted = jnp.dot(shift_ref[...], jnp.concatenate([prev, cur], axis=0), preferred_element_type=F32)
    y = cur.astype(F32) * w[CONV_W - 1:CONV_W]
    for s in range(1, CONV_W):
        y = y + shifted[(s - 1) * CHUNK:s * CHUNK] * w[CONV_W - 1 - s:CONV_W - s]
    return y


def _mlstm_head(h, bb, rows, q_all, k_all, v_ref, og_ref, gx, b_rows, lower_tri, hg_ref, o_ref, c_ref, n_ref,
                m_ref):
    cols = slice(h * HEAD_W, (h + 1) * HEAD_W)
    q = q_all[:, cols]
    k = k_all[:, cols]
    v = v_ref[bb, rows, cols]
    si = bb * ML_HEADS + h

    i_row = gx[h:h + 1]
    lf_row = gx[ML_HEADS + h:ML_HEADS + h + 1]
    b_row = b_rows[ML_HEADS + h:ML_HEADS + h + 1]
    b_cols = lax.dot_general(lower_tri, jnp.broadcast_to(lf_row, (CHUNK, CHUNK)), _NT,
                             precision=lax.Precision.HIGHEST, preferred_element_type=F32)
    i_cols = jnp.broadcast_to(i_row, (CHUNK, CHUNK)).T
    rowi = lax.broadcasted_iota(jnp.int32, (CHUNK, CHUNK), 0)
    coli = lax.broadcasted_iota(jnp.int32, (CHUNK, CHUNK), 1)
    log_d = jnp.where(rowi >= coli, b_cols - b_row + i_row, -jnp.inf)
    m_prev = m_ref[si]
    inter_log = b_cols + m_prev
    m_t = jnp.maximum(inter_log, jnp.max(log_d, axis=-1, keepdims=True))
    d_mat = jnp.exp(log_d - m_t)
    w_inter = jnp.exp(inter_log - m_t)

    def wide(a):
        return jnp.concatenate([a] * (HEAD_W // LANES), axis=1)

    qb = q.astype(BF16)
    s = lax.dot_general(qb, k.astype(BF16), _NT, preferred_element_type=F32) * d_mat
    c_state = c_ref[si]
    n_state = n_ref[si]
    num = (jnp.dot(s.astype(BF16), v, preferred_element_type=F32)
           + wide(w_inter) * jnp.dot(qb, c_state.astype(BF16), preferred_element_type=F32))
    den = (jnp.sum(s, axis=-1, keepdims=True)
           + w_inter * jnp.sum(q * n_state, axis=-1, keepdims=True))
    hid = num / wide(jnp.maximum(jnp.abs(den), jnp.exp(-m_t)))

    b_last = b_cols[CHUNK - 1:CHUNK]
    log_w = b_last - b_cols + i_cols
    m_new = jnp.maximum(b_last + m_prev, jnp.max(log_w, axis=0, keepdims=True))
    w = jnp.exp(log_w - m_new)
    decay = wide(jnp.exp(b_last + m_prev - m_new))
    kw = k * wide(w)
    c_ref[si] = decay * c_state + lax.dot_general(kw.astype(BF16), v, _TN, preferred_element_type=F32)
    n_ref[si] = decay * n_state + jnp.sum(kw, axis=0, keepdims=True)
    m_ref[si] = m_new

    out_cols = slice(RET_HEADS * HEAD_W + h * HEAD_W, RET_HEADS * HEAD_W + (h + 1) * HEAD_W)
    y = _rms_rows(hid, hg_ref[:, out_cols])
    o_ref[bb, rows, out_cols] = (y * jax.nn.sigmoid(og_ref[bb, rows, cols].astype(F32))).astype(o_ref.dtype)


def _mixer_body(gl_ref, rq_ref, rk_ref, rv_ref, rg_ref, mq_ref, mqp_ref, mk_ref, mkp_ref, mv_ref, mo_ref,
                cos_ref, sin_ref, dm_ref, wq_ref, wk_ref, gr_ref, gb_ref, cw_ref, shift_ref, hg_ref,
                o_ref, r_ref, c_ref, n_ref, m_ref):
    first = pl.program_id(1) == 0

    @pl.when(first)
    def _():
        r_ref[...] = jnp.zeros_like(r_ref)
        c_ref[...] = jnp.zeros_like(c_ref)
        n_ref[...] = jnp.zeros_like(n_ref)
        m_ref[...] = jnp.zeros_like(m_ref)

    ml_w = ML_HEADS * HEAD_W
    rowi = lax.broadcasted_iota(jnp.int32, (CHUNK, CHUNK), 0)
    coli = lax.broadcasted_iota(jnp.int32, (CHUNK, CHUNK), 1)
    upper_tri = jnp.where(rowi <= coli, 1.0, 0.0).astype(F32)
    lower_tri = jnp.where(rowi >= coli, 1.0, 0.0).astype(F32)
    is_input = lax.broadcasted_iota(jnp.int32, (2 * ML_HEADS, CHUNK), 0) < ML_HEADS

    for bb in range(mq_ref.shape[0]):
        for ci in range(mq_ref.shape[1] // CHUNK):
            rows = slice(ci * CHUNK, (ci + 1) * CHUNK)
            cos = cos_ref[rows, :]
            sin = sin_ref[rows, :]
            for h in range(RET_HEADS):
                _retention_head(h, bb, rows, rq_ref, rk_ref, rv_ref, rg_ref, cos, sin, dm_ref, wq_ref, wk_ref,
                                gl_ref, hg_ref, o_ref, r_ref)

            if ci == 0:
                q_prev = jnp.where(first, jnp.zeros_like(mqp_ref[bb]), mqp_ref[bb])
                k_prev = jnp.where(first, jnp.zeros_like(mkp_ref[bb]), mkp_ref[bb])
            else:
                before = slice((ci - 1) * CHUNK, ci * CHUNK)
                q_prev, k_prev = mq_ref[bb, before, :], mk_ref[bb, before, :]
            q_all = _silu(_causal_conv(mq_ref[bb, rows, :], q_prev, shift_ref, cw_ref[:, :ml_w]))
            k_all = _silu(_causal_conv(mk_ref[bb, rows, :], k_prev, shift_ref, cw_ref[:, ml_w:])) * (HEAD_W ** -0.5)
            gates = gr_ref[bb, :, rows] + gb_ref[...]
            gx = jnp.where(is_input, gates, _log_sigmoid(gates))
            b_rows = jnp.dot(gx, upper_tri, precision=lax.Precision.HIGHEST, preferred_element_type=F32)
            for h in range(ML_HEADS):
                _mlstm_head(h, bb, rows, q_all, k_all, mv_ref, mo_ref, gx, b_rows, lower_tri, hg_ref, o_ref,
                            c_ref, n_ref, m_ref)


def _mixer(proj, gates_row, gate_b, conv_w, head_g):
    _, b, t, _ = proj.shape
    seqs = MIXER_SEQS
    step = MIXER_CHUNKS * CHUNK
    nc = t // step
    ret_w = RET_HEADS * HEAD_W
    ml_w = ML_HEADS * HEAD_W
    assert ret_w == ml_w == proj.shape[3]
    log_g = jnp.log1p(-jnp.exp2(-5.0 - jnp.arange(RET_HEADS, dtype=F32)))
    idx = jnp.arange(CHUNK, dtype=F32)
    diff = idx[:, None] - idx[None, :]
    dmask = jnp.where(diff >= 0, jnp.exp(log_g[:, None, None] * jnp.maximum(diff, 0.0)), 0.0)
    w_k = jnp.exp(log_g[:, None] * (CHUNK - 1.0 - idx)[None, :])
    w_q = jnp.exp(log_g[:, None] * (idx + 1.0)[None, :])
    w_k = jnp.broadcast_to(w_k[..., None], (RET_HEADS, CHUNK, HEAD_W))
    w_q = jnp.broadcast_to(w_q[..., None], (RET_HEADS, CHUNK, HEAD_W))
    g_l = jnp.exp(log_g * CHUNK)
    half = HEAD_W // 2
    inv = 1.0 / (ROPE_BASE ** jnp.linspace(0.0, 1.0, half, dtype=F32))
    ang = jnp.arange(t).astype(F32)[:, None] * inv[None, :]
    cos, sin = jnp.cos(ang), jnp.sin(ang)
    r = jnp.arange((CONV_W - 1) * CHUNK)
    src = CHUNK + r % CHUNK - (r // CHUNK + 1)
    shift = (jnp.arange(2 * CHUNK)[None, :] == src[:, None]).astype(BF16)

    def group(gi):
        return pl.BlockSpec((None, seqs, step, ret_w), lambda bi, c: (gi, bi, c, 0))

    def prev_group(gi):
        return pl.BlockSpec((None, seqs, CHUNK, ml_w),
                            lambda bi, c: (gi, bi, jnp.maximum(c * MIXER_CHUNKS - 1, 0), 0))

    def whole(a):
        return pl.BlockSpec(a.shape, lambda bi, c: (0,) * a.ndim)

    gb = gate_b.reshape(2 * ML_HEADS, 1)
    hg = head_g.reshape(1, ret_w + ml_w)
    return pl.pallas_call(
        _mixer_body,
        grid=(b // seqs, nc),
        in_specs=[
            pl.BlockSpec(memory_space=pltpu.SMEM),
            group(0), group(1), group(2), group(3),
            group(4), prev_group(4), group(5), prev_group(5), group(6), group(7),
            pl.BlockSpec((step, half), lambda bi, c: (c, 0)),
            pl.BlockSpec((step, half), lambda bi, c: (c, 0)),
            whole(dmask), whole(w_q), whole(w_k),
            pl.BlockSpec((seqs, 2 * ML_HEADS, step), lambda bi, c: (bi, 0, c)),
            whole(gb), whole(conv_w), whole(shift), whole(hg),
        ],
        out_specs=pl.BlockSpec((seqs, step, ret_w + ml_w), lambda bi, c: (bi, c, 0)),
        out_shape=jax.ShapeDtypeStruct((b, t, ret_w + ml_w), BF16),
        scratch_shapes=[pltpu.VMEM((seqs * RET_HEADS, HEAD_W, HEAD_W), F32),
                        pltpu.VMEM((seqs * ML_HEADS, HEAD_W, HEAD_W), F32),
                        pltpu.VMEM((seqs * ML_HEADS, 1, HEAD_W), F32),
                        pltpu.VMEM((seqs * ML_HEADS, 1, LANES), F32)],
        compiler_params=_params(),
        name="mixer",
    )(g_l, proj, proj, proj, proj, proj, proj, proj, proj, proj, proj, cos, sin, dmask, w_q, w_k,
      gates_row, gb, conv_w, shift, hg)


PAIR_W = 2 * HEAD_DIM
GROUP = ATT_HEADS // KV_HEADS
PAIRS = GROUP // 2


def _swa_in_proj_body(x_ref, g_ref, w_ref, q_ref, kv_ref, *, row_block):
    qw = ATT_HEADS * HEAD_DIM
    per_block = row_block // ATT_BLOCK
    for r in range(x_ref.shape[0] // row_block):
        rows = slice(r * row_block, (r + 1) * row_block)
        h = _rms_rows(x_ref[rows, :], g_ref[...]).astype(BF16)
        res = jnp.dot(h, w_ref[...], preferred_element_type=F32)
        for nbl in range(per_block):
            sub = slice(nbl * ATT_BLOCK, (nbl + 1) * ATT_BLOCK)
            for p in range(q_ref.shape[2]):
                q_ref[0, r * per_block + nbl, p] = res[sub, p * PAIR_W:(p + 1) * PAIR_W].astype(q_ref.dtype)
        kv_ref[rows, :] = res[:, qw:].astype(kv_ref.dtype)


def _swa_in_proj(x, g, w, b, t, tm, row_block):
    n, d = x.shape
    cols = w.shape[1]
    kvw = cols - ATT_HEADS * HEAD_DIM
    tiles_per_seq = t // tm
    return pl.pallas_call(
        functools.partial(_swa_in_proj_body, row_block=row_block),
        grid=(n // tm,),
        in_specs=[
            pl.BlockSpec((tm, d), lambda i: (i, 0)),
            pl.BlockSpec((1, d), lambda i: (0, 0)),
            pl.BlockSpec((d, cols), lambda i: (0, 0)),
        ],
        out_specs=[
            pl.BlockSpec((1, tm // ATT_BLOCK, ATT_HEADS // 2, ATT_BLOCK, PAIR_W),
                         lambda i: (i // tiles_per_seq, i % tiles_per_seq, 0, 0, 0)),
            pl.BlockSpec((tm, kvw), lambda i: (i, 0)),
        ],
        out_shape=[jax.ShapeDtypeStruct((b, t // ATT_BLOCK, ATT_HEADS // 2, ATT_BLOCK, PAIR_W), BF16),
                   jax.ShapeDtypeStruct((n, kvw), BF16)],
        compiler_params=_params(),
        name="swa_in_proj",
    )(x, g.reshape(1, d), w)


def _swa_body(rb_ref, sink_ref, idx_ref, q_ref, kp_ref, kc_ref, vp_ref, vc_ref, o_ref, bias_ref, sinkcol_ref):
    bi = pl.program_id(0)
    nb = pl.program_id(1)
    blk = ATT_BLOCK
    rows = PAIRS * blk

    @pl.when((bi == 0) & (nb == 0))
    def _():
        idx = idx_ref[...]
        col = lax.broadcasted_iota(jnp.int32, idx.shape, 1)

        def head_body(hd, carry):
            def bucket_body(bk, acc):
                return jnp.where(idx == bk, rb_ref[bk * ATT_HEADS + hd], acc)

            tbl = lax.fori_loop(0, N_BUCKETS, bucket_body, jnp.full(idx.shape, -jnp.inf, F32))
            kv = hd // GROUP
            half = hd % 2
            at = pl.ds(pl.multiple_of(((hd % GROUP) // 2) * blk, blk), blk)
            bias_ref[0, kv, half, at, :] = tbl
            bias_ref[1, kv, half, at, :] = jnp.where(col >= blk, tbl, -jnp.inf)
            sinkcol_ref[kv, half, at, :] = jnp.full((blk, LANES), sink_ref[hd], F32)
            return carry

        lax.fori_loop(0, ATT_HEADS, head_body, 0)

    first = (nb == 0).astype(jnp.int32)
    low = lax.broadcasted_iota(jnp.int32, (2 * blk, LANES), 1) < HEAD_DIM
    low_out = lax.broadcasted_iota(jnp.int32, (rows, LANES), 1) < HEAD_DIM
    kf = jnp.concatenate([kp_ref[0], kc_ref[0]], axis=0).astype(F32)
    vf = jnp.concatenate([vp_ref[0], vc_ref[0]], axis=0).astype(F32)
    scale = HEAD_DIM ** -0.5

    def halves(xf, kv):
        xc = xf[:, (kv // 2) * LANES:(kv // 2 + 1) * LANES]
        xr = pltpu.roll(xc, HEAD_DIM, axis=1)
        lo_src, hi_src = (xc, xr) if kv % 2 == 0 else (xr, xc)
        return (jnp.where(low, lo_src, 0.0).astype(BF16), jnp.where(low, 0.0, hi_src).astype(BF16))

    for kv in range(KV_HEADS):
        k_lo, k_hi = halves(kf, kv)
        v_lo, v_hi = halves(vf, kv)
        q4 = q_ref[kv * PAIRS:(kv + 1) * PAIRS].reshape(rows, PAIR_W) * scale

        def probs(k_half, half):
            s = lax.dot_general(q4, k_half, (((1,), (1,)), ((), ())), preferred_element_type=F32)
            s = s + bias_ref[first, kv, half]
            sink = sinkcol_ref[kv, half]
            mx = jnp.maximum(jnp.max(s, axis=-1, keepdims=True), sink)
            p = jnp.exp(s - jnp.concatenate([mx, mx], axis=1))
            den = jnp.sum(p, axis=-1, keepdims=True) + jnp.exp(sink - mx)
            return p.astype(BF16), den

        p_lo, den_lo = probs(k_lo, 0)
        p_hi, den_hi = probs(k_hi, 1)
        o4 = (jnp.dot(p_lo, v_lo, preferred_element_type=F32)
              + jnp.dot(p_hi, v_hi, preferred_element_type=F32)) / jnp.where(low_out, den_lo, den_hi)
        for pr in range(PAIRS):
            at = (kv * PAIRS + pr) * PAIR_W
            o_ref[0, :, at:at + PAIR_W] = o4[pr * blk:(pr + 1) * blk].astype(o_ref.dtype)


def _t5_bucket(dist):
    n = jnp.maximum(dist, 0)
    max_exact = N_BUCKETS // 2
    nf = jnp.maximum(n, 1).astype(F32)
    large = max_exact + (jnp.log(nf / max_exact) / math.log(MAX_DIST / max_exact)
                         * (N_BUCKETS - max_exact)).astype(jnp.int32)
    large = jnp.minimum(large, N_BUCKETS - 1)
    return jnp.where(n < max_exact, n, large)


def _swa(q, kv, sinks, rel_bias):
    b, nblk, n_pairs, blk, _ = q.shape
    t = nblk * blk
    qw = ATT_HEADS * HEAD_DIM
    kvw = KV_HEADS * HEAD_DIM
    i = jnp.arange(blk)
    j = jnp.arange(2 * blk)
    dist = (blk + i)[:, None] - j[None, :]
    idx = jnp.where((dist >= 0) & (dist < WINDOW), _t5_bucket(dist), -1).astype(jnp.int32)
    return pl.pallas_call(
        _swa_body,
        grid=(b, nblk),
        in_specs=[
            pl.BlockSpec(memory_space=pltpu.SMEM),
            pl.BlockSpec(memory_space=pltpu.SMEM),
            pl.BlockSpec((blk, 2 * blk), lambda bi, nb: (0, 0)),
            pl.BlockSpec((None, None, n_pairs, blk, PAIR_W), lambda bi, nb: (bi, nb, 0, 0, 0)),
            pl.BlockSpec((1, blk, kvw), lambda bi, nb: (bi, jnp.maximum(nb - 1, 0), 0)),
            pl.BlockSpec((1, blk, kvw), lambda bi, nb: (bi, nb, 0)),
            pl.BlockSpec((1, blk, kvw), lambda bi, nb: (bi, jnp.maximum(nb - 1, 0), 1)),
            pl.BlockSpec((1, blk, kvw), lambda bi, nb: (bi, nb, 1)),
        ],
        out_specs=pl.BlockSpec((1, blk, qw), lambda bi, nb: (bi, nb, 0)),
        out_shape=jax.ShapeDtypeStruct((b, t, qw), BF16),
        scratch_shapes=[pltpu.VMEM((2, KV_HEADS, 2, PAIRS * blk, 2 * blk), F32),
                        pltpu.VMEM((KV_HEADS, 2, PAIRS * blk, LANES), F32)],
        compiler_params=_params(),
        name="swa",
    )(rel_bias.reshape(-1), sinks, idx, q, kv, kv, kv, kv)


def kernel(x, rel_bias, norm_g, ffn_w_gu, ffn_w_down, rm_w_in, ml_conv_w, ml_gate_b, rm_head_g,
           rm_w_out, swa_w_in, swa_sinks, swa_w_out):
    b, t, d = x.shape
    n = b * t
    depth = norm_g.shape[0]
    main_cols = 4 * RET_HEADS * HEAD_W + 4 * ML_HEADS * HEAD_W
    xs = x.reshape(n, d)
    w_gu = ffn_w_gu.astype(BF16)
    w_down = ffn_w_down.astype(BF16)
    for layer in range(depth):
        g = norm_g[layer]
        if layer % 2 == 0:
            e = layer // 2
            w_in = rm_w_in[e]
            gate_cols = w_in.shape[1] - main_cols
            w_gate = jnp.pad(w_in[:, main_cols:], ((0, 0), (0, LANES - gate_cols))).astype(BF16)
            proj, gates = _norm_matmul(xs, g[0], w_in.astype(BF16), w_gate, main_cols // PROJ_TN, ROW_TILE, PROJ_TN)
            gates_row = gates[:, :gate_cols].reshape(b, t, gate_cols).transpose(0, 2, 1)
            mixed = _mixer(proj.reshape(main_cols // PROJ_TN, b, t, PROJ_TN), gates_row, ml_gate_b[e], ml_conv_w[e],
                           rm_head_g[e])
            xs = _proj_norm_res(mixed.reshape(n, -1), rm_w_out[e].astype(BF16), xs, g[1], OUT_ROW_TILE)
        else:
            o = layer // 2
            q, kv = _swa_in_proj(xs, g[0], swa_w_in[o].astype(BF16), b, t, ROW_TILE, PROJ_ROW_BLOCK)
            att = _swa(q, kv.reshape(b, t, -1), swa_sinks[o], rel_bias)
            xs = _proj_norm_res(att.reshape(n, -1), swa_w_out[o].astype(BF16), xs, g[1], OUT_ROW_TILE)
        xs = _ffn(xs, g[2], w_gu, w_down, g[3], layer, ROW_TILE, FFN_TF, FFN_ROW_BLOCK)
    return xs.reshape(b, t, d)
```

```python
import functools
import math

import jax
import jax.numpy as jnp
from jax import lax
from jax.experimental import pallas as pl
from jax.experimental.pallas import tpu as pltpu

F32 = jnp.float32
BF16 = jnp.bfloat16

EPS = 1e-6
CHUNK = 128
RET_HEADS = 4
ML_HEADS = 4
HEAD_W = 256
CONV_W = 4
ROPE_BASE = 10000.0
ATT_HEADS = 32
KV_HEADS = 4
HEAD_DIM = 64
WINDOW = 128
ATT_BLOCK = 128
N_BUCKETS = 32
MAX_DIST = 128
LANES = 128
SUBLANES = 8
VMEM_LIMIT = 56 * 1024 * 1024

ROW_TILE = 1024
OUT_ROW_TILE = 512
PROJ_TN = 1024
PROJ_ROW_BLOCK = 256
MIXER_CHUNKS = 1
MIXER_SEQS = 2
FFN_TF = 512
FFN_ROW_BLOCK = 512


def _params(vmem=VMEM_LIMIT):
    return pltpu.CompilerParams(vmem_limit_bytes=vmem)


def _rms_rows(x, g):
    ms = jnp.mean(x * x, axis=-1, keepdims=True)
    return x * lax.rsqrt(ms + EPS) * g


def _silu(x):
    return x * jax.nn.sigmoid(x)


def _cast_specs(casts):
    in_specs = [pl.BlockSpec((None, *blk), (lambda *ids, layer=layer, idx=idx: (layer, *idx(*ids))))
                for _, layer, blk, idx in casts]
    out_specs = [pl.BlockSpec(blk, idx) for _, _, blk, idx in casts]
    out_shape = [jax.ShapeDtypeStruct(src.shape[1:], BF16) for src, _, _, _ in casts]
    return in_specs, out_specs, out_shape


def _cast_pieces(src_refs, dst_refs):
    for src_ref, dst_ref in zip(src_refs, dst_refs):
        dst_ref[...] = src_ref[...].astype(dst_ref.dtype)


def _norm_matmul_body(x_ref, g_ref, w_ref, ws_ref, *rest, row_block, n_cast):
    cast_src, (o_ref, os_ref), cast_dst, h_ref = rest[:n_cast], rest[n_cast:n_cast + 2], rest[n_cast + 2:-1], rest[-1]
    j = pl.program_id(1)

    @pl.when(j == 0)
    def _():
        _cast_pieces(cast_src, cast_dst)
        for r in range(x_ref.shape[0] // row_block):
            rows = slice(r * row_block, (r + 1) * row_block)
            h = _rms_rows(x_ref[rows, :], g_ref[...]).astype(h_ref.dtype)
            h_ref[rows, :] = h
            os_ref[rows, :] = jnp.dot(h, ws_ref[...], preferred_element_type=F32)
            o_ref[rows, :] = jnp.dot(h, w_ref[...], preferred_element_type=F32).astype(o_ref.dtype)

    @pl.when(j > 0)
    def _():
        _cast_pieces(cast_src, cast_dst)
        o_ref[...] = jnp.dot(h_ref[...], w_ref[...], preferred_element_type=F32).astype(o_ref.dtype)


def _norm_matmul(x, g, w, w_side, tiles, tm, tn, casts):
    n, d = x.shape
    ns = w_side.shape[1]
    cast_in, cast_out, cast_shape = _cast_specs(casts)
    return pl.pallas_call(
        functools.partial(_norm_matmul_body, row_block=PROJ_ROW_BLOCK, n_cast=len(casts)),
        grid=(n // tm, tiles),
        in_specs=[
            pl.BlockSpec((tm, d), lambda i, j: (i, 0)),
            pl.BlockSpec((1, d), lambda i, j: (0, 0)),
            pl.BlockSpec((d, tn), lambda i, j: (0, j)),
            pl.BlockSpec((d, ns), lambda i, j: (0, 0)),
            *cast_in,
        ],
        out_specs=[pl.BlockSpec((None, tm, tn), lambda i, j: (j, i, 0)),
                   pl.BlockSpec((tm, ns), lambda i, j: (i, 0)),
                   *cast_out],
        out_shape=[jax.ShapeDtypeStruct((tiles, n, tn), BF16), jax.ShapeDtypeStruct((n, ns), F32), *cast_shape],
        scratch_shapes=[pltpu.VMEM((tm, d), BF16)],
        compiler_params=_params(),
        name="norm_matmul",
    )(x, g.reshape(1, d), w, w_side, *[c[0] for c in casts])


def _proj_norm_res_body(a_ref, w_ref, x_ref, g_ref, o_ref):
    y = jnp.dot(a_ref[...], w_ref[...], preferred_element_type=F32)
    o_ref[...] = x_ref[...] + _rms_rows(y, g_ref[...])


def _proj_norm_res(a, w, x, g, tm):
    n, d = x.shape
    return pl.pallas_call(
        _proj_norm_res_body,
        grid=(n // tm,),
        in_specs=[
            pl.BlockSpec((tm, a.shape[1]), lambda i: (i, 0)),
            pl.BlockSpec(w.shape, lambda i: (0, 0)),
            pl.BlockSpec((tm, d), lambda i: (i, 0)),
            pl.BlockSpec((1, d), lambda i: (0, 0)),
        ],
        out_specs=pl.BlockSpec((tm, d), lambda i: (i, 0)),
        out_shape=jax.ShapeDtypeStruct((n, d), F32),
        compiler_params=_params(),
        name="proj_norm_res",
    )(a, w, x, g.reshape(1, d))


def _ffn_body(x_ref, g_in_ref, wg_ref, wu_ref, wd_ref, g_out_ref, *rest, row_block, n_cast):
    cast_src, o_ref, cast_dst, h_ref = rest[:n_cast], rest[n_cast], rest[n_cast + 1:-1], rest[-1]
    f = pl.program_id(1)
    last = pl.num_programs(1) - 1
    blocks = [slice(r * row_block, (r + 1) * row_block) for r in range(x_ref.shape[0] // row_block)]

    def partial_down(rows):
        h = h_ref[rows, :]
        gate = jnp.dot(h, wg_ref[...], preferred_element_type=F32)
        up = jnp.dot(h, wu_ref[...], preferred_element_type=F32)
        act = (_silu(gate) * up).astype(BF16)
        return jnp.dot(act, wd_ref[...], preferred_element_type=F32)

    @pl.when(f == 0)
    def _():
        _cast_pieces(cast_src, cast_dst)
        for rows in blocks:
            h_ref[rows, :] = _rms_rows(x_ref[rows, :], g_in_ref[...]).astype(h_ref.dtype)
            o_ref[rows, :] = partial_down(rows)

    @pl.when((f > 0) & (f < last))
    def _():
        _cast_pieces(cast_src, cast_dst)
        for rows in blocks:
            o_ref[rows, :] += partial_down(rows)

    @pl.when(f == last)
    def _():
        _cast_pieces(cast_src, cast_dst)
        for rows in blocks:
            y = o_ref[rows, :] + partial_down(rows)
            o_ref[rows, :] = x_ref[rows, :] + _rms_rows(y, g_out_ref[...])


def _ffn(x, g_in, w_gu, w_down, g_out, tm, tf, row_block, casts):
    n, d = x.shape
    nf = w_down.shape[0] // tf
    cast_in, cast_out, cast_shape = _cast_specs(casts)
    return pl.pallas_call(
        functools.partial(_ffn_body, row_block=row_block, n_cast=len(casts)),
        grid=(n // tm, nf),
        in_specs=[
            pl.BlockSpec((tm, d), lambda i, f: (i, 0)),
            pl.BlockSpec((1, d), lambda i, f: (0, 0)),
            pl.BlockSpec((d, tf), lambda i, f: (0, f)),
            pl.BlockSpec((d, tf), lambda i, f: (0, f + nf)),
            pl.BlockSpec((tf, d), lambda i, f: (f, 0)),
            pl.BlockSpec((1, d), lambda i, f: (0, 0)),
            *cast_in,
        ],
        out_specs=[pl.BlockSpec((tm, d), lambda i, f: (i, 0)), *cast_out],
        out_shape=[jax.ShapeDtypeStruct((n, d), F32), *cast_shape],
        scratch_shapes=[pltpu.VMEM((tm, d), BF16)],
        compiler_params=_params(),
        name="ffn",
    )(x, g_in.reshape(1, d), w_gu, w_gu, w_down, g_out.reshape(1, d), *[c[0] for c in casts])


def _log_sigmoid(x):
    return jnp.minimum(x, 0.0) - jnp.log1p(jnp.exp(-jnp.abs(x)))


_NT = (((1,), (1,)), ((), ()))
_TN = (((0,), (0,)), ((), ()))


def _retention_head(h, bb, rows, q_ref, k_ref, v_ref, g_ref, cos, sin, dm_ref, wq_ref, wk_ref, gl_ref, hg_ref,
                    o_ref, state_ref):
    cols = slice(h * HEAD_W, (h + 1) * HEAD_W)
    half = HEAD_W // 2

    def rot(x):
        x = x.astype(F32)
        x1, x2 = x[:, :half], x[:, half:]
        return jnp.concatenate([x1 * cos - x2 * sin, x2 * cos + x1 * sin], axis=-1)

    q = rot(q_ref[bb, rows, cols])
    k = rot(k_ref[bb, rows, cols]) * (HEAD_W ** -0.5)
    v = v_ref[bb, rows, cols]
    qb = q.astype(BF16)
    s = lax.dot_general(qb, k.astype(BF16), _NT, preferred_element_type=F32) * dm_ref[h]
    intra = jnp.dot(s.astype(BF16), v, preferred_element_type=F32)
    si = bb * RET_HEADS + h
    state = state_ref[si]
    inter = jnp.dot(qb, state.astype(BF16), preferred_element_type=F32) * wq_ref[h]
    out = intra + inter
    kw = (k * wk_ref[h]).astype(BF16)
    state_ref[si] = state * gl_ref[h] + lax.dot_general(kw, v, _TN, preferred_element_type=F32)
    y = _rms_rows(out, hg_ref[:, cols])
    o_ref[bb, rows, cols] = (y * _silu(g_ref[bb, rows, cols].astype(F32))).astype(o_ref.dtype)


def _causal_conv(cur, prev, shift_ref, w):
    shifted = jnp.dot(shift_ref[...], jnp.concatenate([prev, cur], axis=0), preferred_element_type=F32)
    y = cur.astype(F32) * w[CONV_W - 1:CONV_W]
    for s in range(1, CONV_W):
        y = y + shifted[(s - 1) * CHUNK:s * CHUNK] * w[CONV_W - 1 - s:CONV_W - s]
    return y


def _mlstm_head(h, bb, rows, q_all, k_all, v_ref, og_ref, gx, b_rows, lower_tri, hg_ref, o_ref, c_ref, n_ref,
                m_ref):
    cols = slice(h * HEAD_W, (h + 1) * HEAD_W)
    q = q_all[:, cols]
    k = k_all[:, cols]
    v = v_ref[bb, rows, cols]
    si = bb * ML_HEADS + h

    i_row = gx[h:h + 1]
    lf_row = gx[ML_HEADS + h:ML_HEADS + h + 1]
    b_row = b_rows[ML_HEADS + h:ML_HEADS + h + 1]
    b_cols = lax.dot_general(lower_tri, jnp.broadcast_to(lf_row, (CHUNK, CHUNK)), _NT,
                             precision=lax.Precision.HIGHEST, preferred_element_type=F32)
    i_cols = jnp.broadcast_to(i_row, (CHUNK, CHUNK)).T
    rowi = lax.broadcasted_iota(jnp.int32, (CHUNK, CHUNK), 0)
    coli = lax.broadcasted_iota(jnp.int32, (CHUNK, CHUNK), 1)
    log_d = jnp.where(rowi >= coli, b_cols - b_row + i_row, -jnp.inf)
    m_prev = m_ref[si]
    inter_log = b_cols + m_prev
    m_t = jnp.maximum(inter_log, jnp.max(log_d, axis=-1, keepdims=True))
    d_mat = jnp.exp(log_d - m_t)
    w_inter = jnp.exp(inter_log - m_t)

    def wide(a):
        return jnp.concatenate([a] * (HEAD_W // LANES), axis=1)

    qb = q.astype(BF16)
    s = lax.dot_general(qb, k.astype(BF16), _NT, preferred_element_type=F32) * d_mat
    c_state = c_ref[si]
    n_state = n_ref[si]
    num = (jnp.dot(s.astype(BF16), v, preferred_element_type=F32)
           + wide(w_inter) * jnp.dot(qb, c_state.astype(BF16), preferred_element_type=F32))
    den = (jnp.sum(s, axis=-1, keepdims=True)
           + w_inter * jnp.sum(q * n_state, axis=-1, keepdims=True))
    hid = num / wide(jnp.maximum(jnp.abs(den), jnp.exp(-m_t)))

    b_last = b_cols[CHUNK - 1:CHUNK]
    log_w = b_last - b_cols + i_cols
    m_new = jnp.maximum(b_last + m_prev, jnp.max(log_w, axis=0, keepdims=True))
    w = jnp.exp(log_w - m_new)
    decay = wide(jnp.exp(b_last + m_prev - m_new))
    kw = k * wide(w)
    c_ref[si] = decay * c_state + lax.dot_general(kw.astype(BF16), v, _TN, preferred_element_type=F32)
    n_ref[si] = decay * n_state + jnp.sum(kw, axis=0, keepdims=True)
    m_ref[si] = m_new

    out_cols = slice(RET_HEADS * HEAD_W + h * HEAD_W, RET_HEADS * HEAD_W + (h + 1) * HEAD_W)
    y = _rms_rows(hid, hg_ref[:, out_cols])
    o_ref[bb, rows, out_cols] = (y * jax.nn.sigmoid(og_ref[bb, rows, cols].astype(F32))).astype(o_ref.dtype)


def _mixer_body(gl_ref, rq_ref, rk_ref, rv_ref, rg_ref, mq_ref, mqp_ref, mk_ref, mkp_ref, mv_ref, mo_ref,
                cos_ref, sin_ref, dm_ref, wq_ref, wk_ref, gr_ref, gb_ref, cw_ref, shift_ref, hg_ref,
                o_ref, r_ref, c_ref, n_ref, m_ref):
    first = pl.program_id(1) == 0

    @pl.when(first)
    def _():
        r_ref[...] = jnp.zeros_like(r_ref)
        c_ref[...] = jnp.zeros_like(c_ref)
        n_ref[...] = jnp.zeros_like(n_ref)
        m_ref[...] = jnp.zeros_like(m_ref)

    ml_w = ML_HEADS * HEAD_W
    rowi = lax.broadcasted_iota(jnp.int32, (CHUNK, CHUNK), 0)
    coli = lax.broadcasted_iota(jnp.int32, (CHUNK, CHUNK), 1)
    upper_tri = jnp.where(rowi <= coli, 1.0, 0.0).astype(F32)
    lower_tri = jnp.where(rowi >= coli, 1.0, 0.0).astype(F32)
    is_input = lax.broadcasted_iota(jnp.int32, (2 * ML_HEADS, CHUNK), 0) < ML_HEADS

    for bb in range(mq_ref.shape[0]):
        for ci in range(mq_ref.shape[1] // CHUNK):
            rows = slice(ci * CHUNK, (ci + 1) * CHUNK)
            cos = cos_ref[rows, :]
            sin = sin_ref[rows, :]
            for h in range(RET_HEADS):
                _retention_head(h, bb, rows, rq_ref, rk_ref, rv_ref, rg_ref, cos, sin, dm_ref, wq_ref, wk_ref,
                                gl_ref, hg_ref, o_ref, r_ref)

            if ci == 0:
                q_prev = jnp.where(first, jnp.zeros_like(mqp_ref[bb]), mqp_ref[bb])
                k_prev = jnp.where(first, jnp.zeros_like(mkp_ref[bb]), mkp_ref[bb])
            else:
                before = slice((ci - 1) * CHUNK, ci * CHUNK)
                q_prev, k_prev = mq_ref[bb, before, :], mk_ref[bb, before, :]
            q_all = _silu(_causal_conv(mq_ref[bb, rows, :], q_prev, shift_ref, cw_ref[:, :ml_w]))
            k_all = _silu(_causal_conv(mk_ref[bb, rows, :], k_prev, shift_ref, cw_ref[:, ml_w:])) * (HEAD_W ** -0.5)
            gates = gr_ref[bb, :, rows] + gb_ref[...]
            gx = jnp.where(is_input, gates, _log_sigmoid(gates))
            b_rows = jnp.dot(gx, upper_tri, precision=lax.Precision.HIGHEST, preferred_element_type=F32)
            for h in range(ML_HEADS):
                _mlstm_head(h, bb, rows, q_all, k_all, mv_ref, mo_ref, gx, b_rows, lower_tri, hg_ref, o_ref,
                            c_ref, n_ref, m_ref)


def _mixer(proj, gates_row, gate_b, conv_w, head_g):
    _, b, t, _ = proj.shape
    seqs = MIXER_SEQS
    step = MIXER_CHUNKS * CHUNK
    nc = t // step
    ret_w = RET_HEADS * HEAD_W
    ml_w = ML_HEADS * HEAD_W
    assert ret_w == ml_w == proj.shape[3]
    log_g = jnp.log1p(-jnp.exp2(-5.0 - jnp.arange(RET_HEADS, dtype=F32)))
    idx = jnp.arange(CHUNK, dtype=F32)
    diff = idx[:, None] - idx[None, :]
    dmask = jnp.where(diff >= 0, jnp.exp(log_g[:, None, None] * jnp.maximum(diff, 0.0)), 0.0)
    w_k = jnp.exp(log_g[:, None] * (CHUNK - 1.0 - idx)[None, :])
    w_q = jnp.exp(log_g[:, None] * (idx + 1.0)[None, :])
    w_k = jnp.broadcast_to(w_k[..., None], (RET_HEADS, CHUNK, HEAD_W))
    w_q = jnp.broadcast_to(w_q[..., None], (RET_HEADS, CHUNK, HEAD_W))
    g_l = jnp.exp(log_g * CHUNK)
    half = HEAD_W // 2
    inv = 1.0 / (ROPE_BASE ** jnp.linspace(0.0, 1.0, half, dtype=F32))
    ang = jnp.arange(t).astype(F32)[:, None] * inv[None, :]
    cos, sin = jnp.cos(ang), jnp.sin(ang)
    r = jnp.arange((CONV_W - 1) * CHUNK)
    src = CHUNK + r % CHUNK - (r // CHUNK + 1)
    shift = (jnp.arange(2 * CHUNK)[None, :] == src[:, None]).astype(BF16)

    def group(gi):
        return pl.BlockSpec((None, seqs, step, ret_w), lambda bi, c: (gi, bi, c, 0))

    def prev_group(gi):
        return pl.BlockSpec((None, seqs, CHUNK, ml_w),
                            lambda bi, c: (gi, bi, jnp.maximum(c * MIXER_CHUNKS - 1, 0), 0))

    def whole(a):
        return pl.BlockSpec(a.shape, lambda bi, c: (0,) * a.ndim)

    gb = gate_b.reshape(2 * ML_HEADS, 1)
    hg = head_g.reshape(1, ret_w + ml_w)
    return pl.pallas_call(
        _mixer_body,
        grid=(b // seqs, nc),
        in_specs=[
            pl.BlockSpec(memory_space=pltpu.SMEM),
            group(0), group(1), group(2), group(3),
            group(4), prev_group(4), group(5), prev_group(5), group(6), group(7),
            pl.BlockSpec((step, half), lambda bi, c: (c, 0)),
            pl.BlockSpec((step, half), lambda bi, c: (c, 0)),
            whole(dmask), whole(w_q), whole(w_k),
            pl.BlockSpec((seqs, 2 * ML_HEADS, step), lambda bi, c: (bi, 0, c)),
            whole(gb), whole(conv_w), whole(shift), whole(hg),
        ],
        out_specs=pl.BlockSpec((seqs, step, ret_w + ml_w), lambda bi, c: (bi, c, 0)),
        out_shape=jax.ShapeDtypeStruct((b, t, ret_w + ml_w), BF16),
        scratch_shapes=[pltpu.VMEM((seqs * RET_HEADS, HEAD_W, HEAD_W), F32),
                        pltpu.VMEM((seqs * ML_HEADS, HEAD_W, HEAD_W), F32),
                        pltpu.VMEM((seqs * ML_HEADS, 1, HEAD_W), F32),
                        pltpu.VMEM((seqs * ML_HEADS, 1, LANES), F32)],
        compiler_params=_params(),
        name="mixer",
    )(g_l, proj, proj, proj, proj, proj, proj, proj, proj, proj, proj, cos, sin, dmask, w_q, w_k,
      gates_row, gb, conv_w, shift, hg)


PAIR_W = 2 * HEAD_DIM
GROUP = ATT_HEADS // KV_HEADS
PAIRS = GROUP // 2


def _swa_in_proj_body(x_ref, g_ref, w_ref, q_ref, kv_ref, *, row_block):
    qw = ATT_HEADS * HEAD_DIM
    per_block = row_block // ATT_BLOCK
    for r in range(x_ref.shape[0] // row_block):
        rows = slice(r * row_block, (r + 1) * row_block)
        h = _rms_rows(x_ref[rows, :], g_ref[...]).astype(BF16)
        res = jnp.dot(h, w_ref[...], preferred_element_type=F32)
        for nbl in range(per_block):
            sub = slice(nbl * ATT_BLOCK, (nbl + 1) * ATT_BLOCK)
            for p in range(q_ref.shape[2]):
                q_ref[0, r * per_block + nbl, p] = res[sub, p * PAIR_W:(p + 1) * PAIR_W].astype(q_ref.dtype)
        kv_ref[rows, :] = res[:, qw:].astype(kv_ref.dtype)


def _swa_in_proj(x, g, w, b, t, tm, row_block):
    n, d = x.shape
    cols = w.shape[1]
    kvw = cols - ATT_HEADS * HEAD_DIM
    tiles_per_seq = t // tm
    return pl.pallas_call(
        functools.partial(_swa_in_proj_body, row_block=row_block),
        grid=(n // tm,),
        in_specs=[
            pl.BlockSpec((tm, d), lambda i: (i, 0)),
            pl.BlockSpec((1, d), lambda i: (0, 0)),
            pl.BlockSpec((d, cols), lambda i: (0, 0)),
        ],
        out_specs=[
            pl.BlockSpec((1, tm // ATT_BLOCK, ATT_HEADS // 2, ATT_BLOCK, PAIR_W),
                         lambda i: (i // tiles_per_seq, i % tiles_per_seq, 0, 0, 0)),
            pl.BlockSpec((tm, kvw), lambda i: (i, 0)),
        ],
        out_shape=[jax.ShapeDtypeStruct((b, t // ATT_BLOCK, ATT_HEADS // 2, ATT_BLOCK, PAIR_W), BF16),
                   jax.ShapeDtypeStruct((n, kvw), BF16)],
        compiler_params=_params(),
        name="swa_in_proj",
    )(x, g.reshape(1, d), w)


def _swa_body(rb_ref, sink_ref, idx_ref, q_ref, kp_ref, kc_ref, vp_ref, vc_ref, o_ref, bias_ref, sinkcol_ref):
    bi = pl.program_id(0)
    nb = pl.program_id(1)
    blk = ATT_BLOCK
    rows = PAIRS * blk

    @pl.when((bi == 0) & (nb == 0))
    def _():
        idx = idx_ref[...]
        col = lax.broadcasted_iota(jnp.int32, idx.shape, 1)

        def head_body(hd, carry):
            def bucket_body(bk, acc):
                return jnp.where(idx == bk, rb_ref[bk * ATT_HEADS + hd], acc)

            tbl = lax.fori_loop(0, N_BUCKETS, bucket_body, jnp.full(idx.shape, -jnp.inf, F32))
            kv = hd // GROUP
            half = hd % 2
            at = pl.ds(pl.multiple_of(((hd % GROUP) // 2) * blk, blk), blk)
            bias_ref[0, kv, half, at, :] = tbl
            bias_ref[1, kv, half, at, :] = jnp.where(col >= blk, tbl, -jnp.inf)
            sinkcol_ref[kv, half, at, :] = jnp.full((blk, LANES), sink_ref[hd], F32)
            return carry

        lax.fori_loop(0, ATT_HEADS, head_body, 0)

    first = (nb == 0).astype(jnp.int32)
    low = lax.broadcasted_iota(jnp.int32, (2 * blk, LANES), 1) < HEAD_DIM
    low_out = lax.broadcasted_iota(jnp.int32, (rows, LANES), 1) < HEAD_DIM
    kf = jnp.concatenate([kp_ref[0], kc_ref[0]], axis=0).astype(F32)
    vf = jnp.concatenate([vp_ref[0], vc_ref[0]], axis=0).astype(F32)
    scale = HEAD_DIM ** -0.5

    def halves(xf, kv):
        xc = xf[:, (kv // 2) * LANES:(kv // 2 + 1) * LANES]
        xr = pltpu.roll(xc, HEAD_DIM, axis=1)
        lo_src, hi_src = (xc, xr) if kv % 2 == 0 else (xr, xc)
        return (jnp.where(low, lo_src, 0.0).astype(BF16), jnp.where(low, 0.0, hi_src).astype(BF16))

    for kv in range(KV_HEADS):
        k_lo, k_hi = halves(kf, kv)
        v_lo, v_hi = halves(vf, kv)
        q4 = q_ref[kv * PAIRS:(kv + 1) * PAIRS].reshape(rows, PAIR_W) * scale

        def probs(k_half, half):
            s = lax.dot_general(q4, k_half, (((1,), (1,)), ((), ())), preferred_element_type=F32)
            s = s + bias_ref[first, kv, half]
            sink = sinkcol_ref[kv, half]
            mx = jnp.maximum(jnp.max(s, axis=-1, keepdims=True), sink)
            p = jnp.exp(s - jnp.concatenate([mx, mx], axis=1))
            den = jnp.sum(p, axis=-1, keepdims=True) + jnp.exp(sink - mx)
            return p.astype(BF16), den

        p_lo, den_lo = probs(k_lo, 0)
        p_hi, den_hi = probs(k_hi, 1)
        o4 = (jnp.dot(p_lo, v_lo, preferred_element_type=F32)
              + jnp.dot(p_hi, v_hi, preferred_element_type=F32)) / jnp.where(low_out, den_lo, den_hi)
        for pr in range(PAIRS):
            at = (kv * PAIRS + pr) * PAIR_W
            o_ref[0, :, at:at + PAIR_W] = o4[pr * blk:(pr + 1) * blk].astype(o_ref.dtype)


def _t5_bucket(dist):
    n = jnp.maximum(dist, 0)
    max_exact = N_BUCKETS // 2
    nf = jnp.maximum(n, 1).astype(F32)
    large = max_exact + (jnp.log(nf / max_exact) / math.log(MAX_DIST / max_exact)
                         * (N_BUCKETS - max_exact)).astype(jnp.int32)
    large = jnp.minimum(large, N_BUCKETS - 1)
    return jnp.where(n < max_exact, n, large)


def _swa(q, kv, sinks, rel_bias):
    b, nblk, n_pairs, blk, _ = q.shape
    t = nblk * blk
    qw = ATT_HEADS * HEAD_DIM
    kvw = KV_HEADS * HEAD_DIM
    i = jnp.arange(blk)
    j = jnp.arange(2 * blk)
    dist = (blk + i)[:, None] - j[None, :]
    idx = jnp.where((dist >= 0) & (dist < WINDOW), _t5_bucket(dist), -1).astype(jnp.int32)
    return pl.pallas_call(
        _swa_body,
        grid=(b, nblk),
        in_specs=[
            pl.BlockSpec(memory_space=pltpu.SMEM),
            pl.BlockSpec(memory_space=pltpu.SMEM),
            pl.BlockSpec((blk, 2 * blk), lambda bi, nb: (0, 0)),
            pl.BlockSpec((None, None, n_pairs, blk, PAIR_W), lambda bi, nb: (bi, nb, 0, 0, 0)),
            pl.BlockSpec((1, blk, kvw), lambda bi, nb: (bi, jnp.maximum(nb - 1, 0), 0)),
            pl.BlockSpec((1, blk, kvw), lambda bi, nb: (bi, nb, 0)),
            pl.BlockSpec((1, blk, kvw), lambda bi, nb: (bi, jnp.maximum(nb - 1, 0), 1)),
            pl.BlockSpec((1, blk, kvw), lambda bi, nb: (bi, nb, 1)),
        ],
        out_specs=pl.BlockSpec((1, blk, qw), lambda bi, nb: (bi, nb, 0)),
        out_shape=jax.ShapeDtypeStruct((b, t, qw), BF16),
        scratch_shapes=[pltpu.VMEM((2, KV_HEADS, 2, PAIRS * blk, 2 * blk), F32),
                        pltpu.VMEM((KV_HEADS, 2, PAIRS * blk, LANES), F32)],
        compiler_params=_params(),
        name="swa",
    )(rel_bias.reshape(-1), sinks, idx, q, kv, kv, kv, kv)


def kernel(x, rel_bias, norm_g, ffn_w_gu, ffn_w_down, rm_w_in, ml_conv_w, ml_gate_b, rm_head_g,
           rm_w_out, swa_w_in, swa_sinks, swa_w_out):
    b, t, d = x.shape
    n = b * t
    depth = norm_g.shape[0]
    main_cols = 4 * RET_HEADS * HEAD_W + 4 * ML_HEADS * HEAD_W
    xs = x.reshape(n, d)
    row_tiles = n // ROW_TILE
    proj_tiles = main_cols // PROJ_TN
    d_ff = ffn_w_down.shape[1]
    nf = d_ff // FFN_TF

    def ffn_casts(layer, outer, inner, flat):
        gu_blk = (d // (outer * inner), 2 * d_ff) if flat else (d // outer, 2 * d_ff // inner)
        gu_idx = (lambda i, j: (i * inner + j, 0)) if flat else (lambda i, j: (i, j))
        down_blk = (d_ff // outer, d // inner) if flat else (d_ff // inner, d // outer)
        down_idx = (lambda i, j: (i, j)) if flat else (lambda i, j: (j, i))
        return [(ffn_w_gu, layer, gu_blk, gu_idx), (ffn_w_down, layer, down_blk, down_idx)]

    ffn_weights = None
    for layer in range(depth):
        g = norm_g[layer]
        if layer % 2 == 0:
            e = layer // 2
            w_in = rm_w_in[e]
            gate_cols = w_in.shape[1] - main_cols
            w_gate = jnp.pad(w_in[:, main_cols:], ((0, 0), (0, LANES - gate_cols))).astype(BF16)
            proj, gates, *ffn_weights = _norm_matmul(xs, g[0], w_in.astype(BF16), w_gate, proj_tiles, ROW_TILE, PROJ_TN,
                                                     ffn_casts(layer, row_tiles, proj_tiles, True))
            gates_row = gates[:, :gate_cols].reshape(b, t, gate_cols).transpose(0, 2, 1)
            mixed = _mixer(proj.reshape(main_cols // PROJ_TN, b, t, PROJ_TN), gates_row, ml_gate_b[e], ml_conv_w[e],
                           rm_head_g[e])
            xs = _proj_norm_res(mixed.reshape(n, -1), rm_w_out[e].astype(BF16), xs, g[1], OUT_ROW_TILE)
        else:
            o = layer // 2
            q, kv = _swa_in_proj(xs, g[0], swa_w_in[o].astype(BF16), b, t, ROW_TILE, PROJ_ROW_BLOCK)
            att = _swa(q, kv.reshape(b, t, -1), swa_sinks[o], rel_bias)
            xs = _proj_norm_res(att.reshape(n, -1), swa_w_out[o].astype(BF16), xs, g[1], OUT_ROW_TILE)
        if ffn_weights is None:
            ffn_weights = [ffn_w_gu[layer].astype(BF16), ffn_w_down[layer].astype(BF16)]
        next_casts = ffn_casts(layer + 1, row_tiles, nf, False) if (layer % 2 == 0 and layer + 1 < depth) else []
        xs, *next_weights = _ffn(xs, g[2], ffn_weights[0], ffn_weights[1], g[3], ROW_TILE, FFN_TF, FFN_ROW_BLOCK,
                                 next_casts)
        ffn_weights = next_weights or None
    return xs.reshape(b, t, d)
```

```python
import functools
import math

import jax
import jax.numpy as jnp
from jax import lax
from jax.experimental import pallas as pl
from jax.experimental.pallas import tpu as pltpu

F32 = jnp.float32
BF16 = jnp.bfloat16

EPS = 1e-6
CHUNK = 128
RET_HEADS = 4
ML_HEADS = 4
HEAD_W = 256
CONV_W = 4
ROPE_BASE = 10000.0
ATT_HEADS = 32
KV_HEADS = 4
HEAD_DIM = 64
WINDOW = 128
ATT_BLOCK = 128
N_BUCKETS = 32
MAX_DIST = 128
LANES = 128
SUBLANES = 8
VMEM_LIMIT = 56 * 1024 * 1024

ROW_TILE = 1024
OUT_ROW_TILE = 1024
OUT_ROW_BLOCK = 256
PROJ_TN = 1024
PROJ_ROW_BLOCK = 256
MIXER_CHUNKS = 1
MIXER_SEQS = 2
CAST_COLS = 256
FFN_TF = 512
FFN_ROW_BLOCK = 512


def _params(vmem=VMEM_LIMIT):
    return pltpu.CompilerParams(vmem_limit_bytes=vmem)


def _rms_rows(x, g):
    ms = jnp.mean(x * x, axis=-1, keepdims=True)
    return x * lax.rsqrt(ms + EPS) * g


def _silu(x):
    return x * jax.nn.sigmoid(x)


def _cast_specs(casts):
    in_specs = [pl.BlockSpec((None, *blk), (lambda *ids, layer=layer, idx=idx: (layer, *idx(*ids))))
                for _, layer, blk, idx in casts]
    out_specs = [pl.BlockSpec(blk, idx) for _, _, blk, idx in casts]
    out_shape = [jax.ShapeDtypeStruct(src.shape[1:], BF16) for src, _, _, _ in casts]
    return in_specs, out_specs, out_shape


def _cast_pieces(src_refs, dst_refs):
    for src_ref, dst_ref in zip(src_refs, dst_refs):
        dst_ref[...] = src_ref[...].astype(dst_ref.dtype)


def _norm_matmul_body(x_ref, g_ref, w_ref, ws_ref, *rest, row_block, n_cast):
    cast_src, (o_ref, os_ref), cast_dst, h_ref = rest[:n_cast], rest[n_cast:n_cast + 2], rest[n_cast + 2:-1], rest[-1]
    j = pl.program_id(1)

    @pl.when(j == 0)
    def _():
        _cast_pieces(cast_src, cast_dst)
        for r in range(x_ref.shape[0] // row_block):
            rows = slice(r * row_block, (r + 1) * row_block)
            h = _rms_rows(x_ref[rows, :], g_ref[...]).astype(h_ref.dtype)
            h_ref[rows, :] = h
            os_ref[rows, :] = jnp.dot(h, ws_ref[...], preferred_element_type=F32)
            o_ref[rows, :] = jnp.dot(h, w_ref[...], preferred_element_type=F32).astype(o_ref.dtype)

    @pl.when(j > 0)
    def _():
        _cast_pieces(cast_src, cast_dst)
        o_ref[...] = jnp.dot(h_ref[...], w_ref[...], preferred_element_type=F32).astype(o_ref.dtype)


def _norm_matmul(x, g, w, w_side, tiles, tm, tn, casts):
    n, d = x.shape
    ns = w_side.shape[1]
    cast_in, cast_out, cast_shape = _cast_specs(casts)
    return pl.pallas_call(
        functools.partial(_norm_matmul_body, row_block=PROJ_ROW_BLOCK, n_cast=len(casts)),
        grid=(n // tm, tiles),
        in_specs=[
            pl.BlockSpec((tm, d), lambda i, j: (i, 0)),
            pl.BlockSpec((1, d), lambda i, j: (0, 0)),
            pl.BlockSpec((d, tn), lambda i, j: (0, j)),
            pl.BlockSpec((d, ns), lambda i, j: (0, 0)),
            *cast_in,
        ],
        out_specs=[pl.BlockSpec((None, tm, tn), lambda i, j: (j, i, 0)),
                   pl.BlockSpec((tm, ns), lambda i, j: (i, 0)),
                   *cast_out],
        out_shape=[jax.ShapeDtypeStruct((tiles, n, tn), BF16), jax.ShapeDtypeStruct((n, ns), F32), *cast_shape],
        scratch_shapes=[pltpu.VMEM((tm, d), BF16)],
        compiler_params=_params(),
        name="norm_matmul",
    )(x, g.reshape(1, d), w, w_side, *[c[0] for c in casts])


def _proj_norm_res_body(a_ref, w_ref, x_ref, g_ref, o_ref, *, row_block):
    for r in range(x_ref.shape[0] // row_block):
        rows = slice(r * row_block, (r + 1) * row_block)
        y = jnp.dot(a_ref[rows, :], w_ref[...], preferred_element_type=F32)
        o_ref[rows, :] = x_ref[rows, :] + _rms_rows(y, g_ref[...])


def _proj_norm_res(a, w, x, g, tm, row_block):
    n, d = x.shape
    return pl.pallas_call(
        functools.partial(_proj_norm_res_body, row_block=row_block),
        grid=(n // tm,),
        in_specs=[
            pl.BlockSpec((tm, a.shape[1]), lambda i: (i, 0)),
            pl.BlockSpec(w.shape, lambda i: (0, 0), pipeline_mode=pl.Buffered(1)),
            pl.BlockSpec((tm, d), lambda i: (i, 0)),
            pl.BlockSpec((1, d), lambda i: (0, 0)),
        ],
        out_specs=pl.BlockSpec((tm, d), lambda i: (i, 0)),
        out_shape=jax.ShapeDtypeStruct((n, d), F32),
        compiler_params=_params(),
        name="proj_norm_res",
    )(a, w, x, g.reshape(1, d))


def _ffn_body(x_ref, g_in_ref, wg_ref, wu_ref, wd_ref, g_out_ref, *rest, row_block, n_cast):
    cast_src, o_ref, cast_dst, h_ref = rest[:n_cast], rest[n_cast], rest[n_cast + 1:-1], rest[-1]
    f = pl.program_id(1)
    last = pl.num_programs(1) - 1
    blocks = [slice(r * row_block, (r + 1) * row_block) for r in range(x_ref.shape[0] // row_block)]

    def partial_down(rows):
        h = h_ref[rows, :]
        gate = jnp.dot(h, wg_ref[...], preferred_element_type=F32)
        up = jnp.dot(h, wu_ref[...], preferred_element_type=F32)
        act = (_silu(gate) * up).astype(BF16)
        return jnp.dot(act, wd_ref[...], preferred_element_type=F32)

    @pl.when(f == 0)
    def _():
        _cast_pieces(cast_src, cast_dst)
        for rows in blocks:
            h_ref[rows, :] = _rms_rows(x_ref[rows, :], g_in_ref[...]).astype(h_ref.dtype)
            o_ref[rows, :] = partial_down(rows)

    @pl.when((f > 0) & (f < last))
    def _():
        _cast_pieces(cast_src, cast_dst)
        for rows in blocks:
            o_ref[rows, :] += partial_down(rows)

    @pl.when(f == last)
    def _():
        _cast_pieces(cast_src, cast_dst)
        for rows in blocks:
            y = o_ref[rows, :] + partial_down(rows)
            o_ref[rows, :] = x_ref[rows, :] + _rms_rows(y, g_out_ref[...])


def _ffn(x, g_in, w_gu, w_down, g_out, tm, tf, row_block, casts):
    n, d = x.shape
    nf = w_down.shape[0] // tf
    cast_in, cast_out, cast_shape = _cast_specs(casts)
    return pl.pallas_call(
        functools.partial(_ffn_body, row_block=row_block, n_cast=len(casts)),
        grid=(n // tm, nf),
        in_specs=[
            pl.BlockSpec((tm, d), lambda i, f: (i, 0)),
            pl.BlockSpec((1, d), lambda i, f: (0, 0)),
            pl.BlockSpec((d, tf), lambda i, f: (0, f)),
            pl.BlockSpec((d, tf), lambda i, f: (0, f + nf)),
            pl.BlockSpec((tf, d), lambda i, f: (f, 0)),
            pl.BlockSpec((1, d), lambda i, f: (0, 0)),
            *cast_in,
        ],
        out_specs=[pl.BlockSpec((tm, d), lambda i, f: (i, 0)), *cast_out],
        out_shape=[jax.ShapeDtypeStruct((n, d), F32), *cast_shape],
        scratch_shapes=[pltpu.VMEM((tm, d), BF16)],
        compiler_params=_params(),
        name="ffn",
    )(x, g_in.reshape(1, d), w_gu, w_gu, w_down, g_out.reshape(1, d), *[c[0] for c in casts])


def _log_sigmoid(x):
    return jnp.minimum(x, 0.0) - jnp.log1p(jnp.exp(-jnp.abs(x)))


_NT = (((1,), (1,)), ((), ()))
_TN = (((0,), (0,)), ((), ()))


def _retention_head(h, bb, rows, q_ref, k_ref, v_ref, g_ref, cos, sin, dm_ref, wq_ref, wk_ref, gl_ref, hg_ref,
                    o_ref, state_ref):
    cols = slice(h * HEAD_W, (h + 1) * HEAD_W)
    half = HEAD_W // 2

    def rot(x):
        x = x.astype(F32)
        x1, x2 = x[:, :half], x[:, half:]
        return jnp.concatenate([x1 * cos - x2 * sin, x2 * cos + x1 * sin], axis=-1)

    q = rot(q_ref[bb, rows, cols])
    k = rot(k_ref[bb, rows, cols]) * (HEAD_W ** -0.5)
    v = v_ref[bb, rows, cols]
    qb = q.astype(BF16)
    s = lax.dot_general(qb, k.astype(BF16), _NT, preferred_element_type=F32) * dm_ref[h]
    intra = jnp.dot(s.astype(BF16), v, preferred_element_type=F32)
    si = bb * RET_HEADS + h
    state = state_ref[si]
    inter = jnp.dot(qb, state.astype(BF16), preferred_element_type=F32) * wq_ref[h]
    out = intra + inter
    kw = (k * wk_ref[h]).astype(BF16)
    state_ref[si] = state * gl_ref[h] + lax.dot_general(kw, v, _TN, preferred_element_type=F32)
    y = _rms_rows(out, hg_ref[:, cols])
    o_ref[bb, rows, cols] = (y * _silu(g_ref[bb, rows, cols].astype(F32))).astype(o_ref.dtype)


def _causal_conv(cur, prev, shift_ref, w):
    shifted = jnp.dot(shift_ref[...], jnp.concatenate([prev, cur], axis=0), preferred_element_type=F32)
    y = cur.astype(F32) * w[CONV_W - 1:CONV_W]
    for s in range(1, CONV_W):
        y = y + shifted[(s - 1) * CHUNK:s * CHUNK] * w[CONV_W - 1 - s:CONV_W - s]
    return y


def _mlstm_head(h, bb, rows, q_all, k_all, v_ref, og_ref, gx, b_rows, lower_tri, hg_ref, o_ref, c_ref, n_ref,
                m_ref):
    cols = slice(h * HEAD_W, (h + 1) * HEAD_W)
    q = q_all[:, cols]
    k = k_all[:, cols]
    v = v_ref[bb, rows, cols]
    si = bb * ML_HEADS + h

    i_row = gx[h:h + 1]
    lf_row = gx[ML_HEADS + h:ML_HEADS + h + 1]
    b_row = b_rows[ML_HEADS + h:ML_HEADS + h + 1]
    b_cols = lax.dot_general(lower_tri, jnp.broadcast_to(lf_row, (CHUNK, CHUNK)), _NT,
                             precision=lax.Precision.HIGHEST, preferred_element_type=F32)
    i_cols = jnp.broadcast_to(i_row, (CHUNK, CHUNK)).T
    rowi = lax.broadcasted_iota(jnp.int32, (CHUNK, CHUNK), 0)
    coli = lax.broadcasted_iota(jnp.int32, (CHUNK, CHUNK), 1)
    log_d = jnp.where(rowi >= coli, b_cols - b_row + i_row, -jnp.inf)
    m_prev = m_ref[si]
    inter_log = b_cols + m_prev
    m_t = jnp.maximum(inter_log, jnp.max(log_d, axis=-1, keepdims=True))
    d_mat = jnp.exp(log_d - m_t)
    w_inter = jnp.exp(inter_log - m_t)

    def wide(a):
        return jnp.concatenate([a] * (HEAD_W // LANES), axis=1)

    qb = q.astype(BF16)
    s = lax.dot_general(qb, k.astype(BF16), _NT, preferred_element_type=F32) * d_mat
    c_state = c_ref[si]
    n_state = n_ref[si]
    num = (jnp.dot(s.astype(BF16), v, preferred_element_type=F32)
           + wide(w_inter) * jnp.dot(qb, c_state.astype(BF16), preferred_element_type=F32))
    den = (jnp.sum(s, axis=-1, keepdims=True)
           + w_inter * jnp.sum(q * n_state, axis=-1, keepdims=True))
    hid = num / wide(jnp.maximum(jnp.abs(den), jnp.exp(-m_t)))

    b_last = b_cols[CHUNK - 1:CHUNK]
    log_w = b_last - b_cols + i_cols
    m_new = jnp.maximum(b_last + m_prev, jnp.max(log_w, axis=0, keepdims=True))
    w = jnp.exp(log_w - m_new)
    decay = wide(jnp.exp(b_last + m_prev - m_new))
    kw = k * wide(w)
    c_ref[si] = decay * c_state + lax.dot_general(kw.astype(BF16), v, _TN, preferred_element_type=F32)
    n_ref[si] = decay * n_state + jnp.sum(kw, axis=0, keepdims=True)
    m_ref[si] = m_new

    out_cols = slice(RET_HEADS * HEAD_W + h * HEAD_W, RET_HEADS * HEAD_W + (h + 1) * HEAD_W)
    y = _rms_rows(hid, hg_ref[:, out_cols])
    o_ref[bb, rows, out_cols] = (y * jax.nn.sigmoid(og_ref[bb, rows, cols].astype(F32))).astype(o_ref.dtype)


def _mixer_body(gl_ref, rq_ref, rk_ref, rv_ref, rg_ref, mq_ref, mqp_ref, mk_ref, mkp_ref, mv_ref, mo_ref,
                cos_ref, sin_ref, dm_ref, wq_ref, wk_ref, gr_ref, gb_ref, cw_ref, shift_ref, hg_ref,
                o_ref, r_ref, c_ref, n_ref, m_ref):
    first = pl.program_id(1) == 0

    @pl.when(first)
    def _():
        r_ref[...] = jnp.zeros_like(r_ref)
        c_ref[...] = jnp.zeros_like(c_ref)
        n_ref[...] = jnp.zeros_like(n_ref)
        m_ref[...] = jnp.zeros_like(m_ref)

    ml_w = ML_HEADS * HEAD_W
    rowi = lax.broadcasted_iota(jnp.int32, (CHUNK, CHUNK), 0)
    coli = lax.broadcasted_iota(jnp.int32, (CHUNK, CHUNK), 1)
    upper_tri = jnp.where(rowi <= coli, 1.0, 0.0).astype(F32)
    lower_tri = jnp.where(rowi >= coli, 1.0, 0.0).astype(F32)
    is_input = lax.broadcasted_iota(jnp.int32, (2 * ML_HEADS, CHUNK), 0) < ML_HEADS

    for bb in range(mq_ref.shape[0]):
        for ci in range(mq_ref.shape[1] // CHUNK):
            rows = slice(ci * CHUNK, (ci + 1) * CHUNK)
            cos = cos_ref[rows, :]
            sin = sin_ref[rows, :]
            for h in range(RET_HEADS):
                _retention_head(h, bb, rows, rq_ref, rk_ref, rv_ref, rg_ref, cos, sin, dm_ref, wq_ref, wk_ref,
                                gl_ref, hg_ref, o_ref, r_ref)

            if ci == 0:
                q_prev = jnp.where(first, jnp.zeros_like(mqp_ref[bb]), mqp_ref[bb])
                k_prev = jnp.where(first, jnp.zeros_like(mkp_ref[bb]), mkp_ref[bb])
            else:
                before = slice((ci - 1) * CHUNK, ci * CHUNK)
                q_prev, k_prev = mq_ref[bb, before, :], mk_ref[bb, before, :]
            q_all = _silu(_causal_conv(mq_ref[bb, rows, :], q_prev, shift_ref, cw_ref[:, :ml_w]))
            k_all = _silu(_causal_conv(mk_ref[bb, rows, :], k_prev, shift_ref, cw_ref[:, ml_w:])) * (HEAD_W ** -0.5)
            gates = gr_ref[bb, :, rows] + gb_ref[...]
            gx = jnp.where(is_input, gates, _log_sigmoid(gates))
            b_rows = jnp.dot(gx, upper_tri, precision=lax.Precision.HIGHEST, preferred_element_type=F32)
            for h in range(ML_HEADS):
                _mlstm_head(h, bb, rows, q_all, k_all, mv_ref, mo_ref, gx, b_rows, lower_tri, hg_ref, o_ref,
                            c_ref, n_ref, m_ref)


def _mixer(proj, gates_row, gate_b, conv_w, head_g):
    _, b, t, _ = proj.shape
    seqs = MIXER_SEQS
    step = MIXER_CHUNKS * CHUNK
    nc = t // step
    ret_w = RET_HEADS * HEAD_W
    ml_w = ML_HEADS * HEAD_W
    assert ret_w == ml_w == proj.shape[3]
    log_g = jnp.log1p(-jnp.exp2(-5.0 - jnp.arange(RET_HEADS, dtype=F32)))
    idx = jnp.arange(CHUNK, dtype=F32)
    diff = idx[:, None] - idx[None, :]
    dmask = jnp.where(diff >= 0, jnp.exp(log_g[:, None, None] * jnp.maximum(diff, 0.0)), 0.0)
    w_k = jnp.exp(log_g[:, None] * (CHUNK - 1.0 - idx)[None, :])
    w_q = jnp.exp(log_g[:, None] * (idx + 1.0)[None, :])
    w_k = jnp.broadcast_to(w_k[..., None], (RET_HEADS, CHUNK, HEAD_W))
    w_q = jnp.broadcast_to(w_q[..., None], (RET_HEADS, CHUNK, HEAD_W))
    g_l = jnp.exp(log_g * CHUNK)
    half = HEAD_W // 2
    inv = 1.0 / (ROPE_BASE ** jnp.linspace(0.0, 1.0, half, dtype=F32))
    ang = jnp.arange(t).astype(F32)[:, None] * inv[None, :]
    cos, sin = jnp.cos(ang), jnp.sin(ang)
    r = jnp.arange((CONV_W - 1) * CHUNK)
    src = CHUNK + r % CHUNK - (r // CHUNK + 1)
    shift = (jnp.arange(2 * CHUNK)[None, :] == src[:, None]).astype(BF16)

    def group(gi):
        return pl.BlockSpec((None, seqs, step, ret_w), lambda bi, c: (gi, bi, c, 0))

    def prev_group(gi):
        return pl.BlockSpec((None, seqs, CHUNK, ml_w),
                            lambda bi, c: (gi, bi, jnp.maximum(c * MIXER_CHUNKS - 1, 0), 0))

    def whole(a):
        return pl.BlockSpec(a.shape, lambda bi, c: (0,) * a.ndim)

    gb = gate_b.reshape(2 * ML_HEADS, 1)
    hg = head_g.reshape(1, ret_w + ml_w)
    return pl.pallas_call(
        _mixer_body,
        grid=(b // seqs, nc),
        in_specs=[
            pl.BlockSpec(memory_space=pltpu.SMEM),
            group(0), group(1), group(2), group(3),
            group(4), prev_group(4), group(5), prev_group(5), group(6), group(7),
            pl.BlockSpec((step, half), lambda bi, c: (c, 0)),
            pl.BlockSpec((step, half), lambda bi, c: (c, 0)),
            whole(dmask), whole(w_q), whole(w_k),
            pl.BlockSpec((seqs, 2 * ML_HEADS, step), lambda bi, c: (bi, 0, c)),
            whole(gb), whole(conv_w), whole(shift), whole(hg),
        ],
        out_specs=pl.BlockSpec((seqs, step, ret_w + ml_w), lambda bi, c: (bi, c, 0)),
        out_shape=jax.ShapeDtypeStruct((b, t, ret_w + ml_w), BF16),
        scratch_shapes=[pltpu.VMEM((seqs * RET_HEADS, HEAD_W, HEAD_W), F32),
                        pltpu.VMEM((seqs * ML_HEADS, HEAD_W, HEAD_W), F32),
                        pltpu.VMEM((seqs * ML_HEADS, 1, HEAD_W), F32),
                        pltpu.VMEM((seqs * ML_HEADS, 1, LANES), F32)],
        compiler_params=_params(),
        name="mixer",
    )(g_l, proj, proj, proj, proj, proj, proj, proj, proj, proj, proj, cos, sin, dmask, w_q, w_k,
      gates_row, gb, conv_w, shift, hg)


PAIR_W = 2 * HEAD_DIM
GROUP = ATT_HEADS // KV_HEADS
PAIRS = GROUP // 2


def _swa_in_proj_body(x_ref, g_ref, w_ref, q_ref, kv_ref, *, row_block):
    qw = ATT_HEADS * HEAD_DIM
    per_block = row_block // ATT_BLOCK
    for r in range(x_ref.shape[0] // row_block):
        rows = slice(r * row_block, (r + 1) * row_block)
        h = _rms_rows(x_ref[rows, :], g_ref[...]).astype(BF16)
        res = jnp.dot(h, w_ref[...], preferred_element_type=F32)
        for nbl in range(per_block):
            sub = slice(nbl * ATT_BLOCK, (nbl + 1) * ATT_BLOCK)
            for p in range(q_ref.shape[2]):
                q_ref[0, r * per_block + nbl, p] = res[sub, p * PAIR_W:(p + 1) * PAIR_W].astype(q_ref.dtype)
        kv_ref[rows, :] = res[:, qw:].astype(kv_ref.dtype)


def _swa_in_proj(x, g, w, b, t, tm, row_block):
    n, d = x.shape
    cols = w.shape[1]
    kvw = cols - ATT_HEADS * HEAD_DIM
    tiles_per_seq = t // tm
    return pl.pallas_call(
        functools.partial(_swa_in_proj_body, row_block=row_block),
        grid=(n // tm,),
        in_specs=[
            pl.BlockSpec((tm, d), lambda i: (i, 0)),
            pl.BlockSpec((1, d), lambda i: (0, 0)),
            pl.BlockSpec((d, cols), lambda i: (0, 0)),
        ],
        out_specs=[
            pl.BlockSpec((1, tm // ATT_BLOCK, ATT_HEADS // 2, ATT_BLOCK, PAIR_W),
                         lambda i: (i // tiles_per_seq, i % tiles_per_seq, 0, 0, 0)),
            pl.BlockSpec((tm, kvw), lambda i: (i, 0)),
        ],
        out_shape=[jax.ShapeDtypeStruct((b, t // ATT_BLOCK, ATT_HEADS // 2, ATT_BLOCK, PAIR_W), BF16),
                   jax.ShapeDtypeStruct((n, kvw), BF16)],
        compiler_params=_params(),
        name="swa_in_proj",
    )(x, g.reshape(1, d), w)


def _swa_body(rb_ref, sink_ref, idx_ref, q_ref, kp_ref, kc_ref, vp_ref, vc_ref, o_ref, bias_ref, sinkcol_ref):
    bi = pl.program_id(0)
    nb = pl.program_id(1)
    blk = ATT_BLOCK
    rows = PAIRS * blk

    @pl.when((bi == 0) & (nb == 0))
    def _():
        idx = idx_ref[...]
        col = lax.broadcasted_iota(jnp.int32, idx.shape, 1)

        def head_body(hd, carry):
            def bucket_body(bk, acc):
                return jnp.where(idx == bk, rb_ref[bk * ATT_HEADS + hd], acc)

            tbl = lax.fori_loop(0, N_BUCKETS, bucket_body, jnp.full(idx.shape, -jnp.inf, F32))
            kv = hd // GROUP
            half = hd % 2
            at = pl.ds(pl.multiple_of(((hd % GROUP) // 2) * blk, blk), blk)
            bias_ref[0, kv, half, at, :] = tbl
            bias_ref[1, kv, half, at, :] = jnp.where(col >= blk, tbl, -jnp.inf)
            sinkcol_ref[kv, half, at, :] = jnp.full((blk, LANES), sink_ref[hd], F32)
            return carry

        lax.fori_loop(0, ATT_HEADS, head_body, 0)

    first = (nb == 0).astype(jnp.int32)
    low = lax.broadcasted_iota(jnp.int32, (2 * blk, LANES), 1) < HEAD_DIM
    low_out = lax.broadcasted_iota(jnp.int32, (rows, LANES), 1) < HEAD_DIM
    kf = jnp.concatenate([kp_ref[0], kc_ref[0]], axis=0).astype(F32)
    vf = jnp.concatenate([vp_ref[0], vc_ref[0]], axis=0).astype(F32)
    scale = HEAD_DIM ** -0.5

    def halves(xf, kv):
        xc = xf[:, (kv // 2) * LANES:(kv // 2 + 1) * LANES]
        xr = pltpu.roll(xc, HEAD_DIM, axis=1)
        lo_src, hi_src = (xc, xr) if kv % 2 == 0 else (xr, xc)
        return (jnp.where(low, lo_src, 0.0).astype(BF16), jnp.where(low, 0.0, hi_src).astype(BF16))

    for kv in range(KV_HEADS):
        k_lo, k_hi = halves(kf, kv)
        v_lo, v_hi = halves(vf, kv)
        q4 = q_ref[kv * PAIRS:(kv + 1) * PAIRS].reshape(rows, PAIR_W) * scale

        def probs(k_half, half):
            s = lax.dot_general(q4, k_half, (((1,), (1,)), ((), ())), preferred_element_type=F32)
            s = s + bias_ref[first, kv, half]
            sink = sinkcol_ref[kv, half]
            mx = jnp.maximum(jnp.max(s, axis=-1, keepdims=True), sink)
            p = jnp.exp(s - jnp.concatenate([mx, mx], axis=1))
            den = jnp.sum(p, axis=-1, keepdims=True) + jnp.exp(sink - mx)
            return p.astype(BF16), den

        p_lo, den_lo = probs(k_lo, 0)
        p_hi, den_hi = probs(k_hi, 1)
        o4 = (jnp.dot(p_lo, v_lo, preferred_element_type=F32)
              + jnp.dot(p_hi, v_hi, preferred_element_type=F32)) / jnp.where(low_out, den_lo, den_hi)
        for pr in range(PAIRS):
            at = (kv * PAIRS + pr) * PAIR_W
            o_ref[0, :, at:at + PAIR_W] = o4[pr * blk:(pr + 1) * blk].astype(o_ref.dtype)


def _t5_bucket(dist):
    n = jnp.maximum(dist, 0)
    max_exact = N_BUCKETS // 2
    nf = jnp.maximum(n, 1).astype(F32)
    large = max_exact + (jnp.log(nf / max_exact) / math.log(MAX_DIST / max_exact)
                         * (N_BUCKETS - max_exact)).astype(jnp.int32)
    large = jnp.minimum(large, N_BUCKETS - 1)
    return jnp.where(n < max_exact, n, large)


def _swa(q, kv, sinks, rel_bias):
    b, nblk, n_pairs, blk, _ = q.shape
    t = nblk * blk
    qw = ATT_HEADS * HEAD_DIM
    kvw = KV_HEADS * HEAD_DIM
    i = jnp.arange(blk)
    j = jnp.arange(2 * blk)
    dist = (blk + i)[:, None] - j[None, :]
    idx = jnp.where((dist >= 0) & (dist < WINDOW), _t5_bucket(dist), -1).astype(jnp.int32)
    return pl.pallas_call(
        _swa_body,
        grid=(b, nblk),
        in_specs=[
            pl.BlockSpec(memory_space=pltpu.SMEM),
            pl.BlockSpec(memory_space=pltpu.SMEM),
            pl.BlockSpec((blk, 2 * blk), lambda bi, nb: (0, 0)),
            pl.BlockSpec((None, None, n_pairs, blk, PAIR_W), lambda bi, nb: (bi, nb, 0, 0, 0)),
            pl.BlockSpec((1, blk, kvw), lambda bi, nb: (bi, jnp.maximum(nb - 1, 0), 0)),
            pl.BlockSpec((1, blk, kvw), lambda bi, nb: (bi, nb, 0)),
            pl.BlockSpec((1, blk, kvw), lambda bi, nb: (bi, jnp.maximum(nb - 1, 0), 1)),
            pl.BlockSpec((1, blk, kvw), lambda bi, nb: (bi, nb, 1)),
        ],
        out_specs=pl.BlockSpec((1, blk, qw), lambda bi, nb: (bi, nb, 0)),
        out_shape=jax.ShapeDtypeStruct((b, t, qw), BF16),
        scratch_shapes=[pltpu.VMEM((2, KV_HEADS, 2, PAIRS * blk, 2 * blk), F32),
                        pltpu.VMEM((KV_HEADS, 2, PAIRS * blk, LANES), F32)],
        compiler_params=_params(),
        name="swa",
    )(rel_bias.reshape(-1), sinks, idx, q, kv, kv, kv, kv)


def kernel(x, rel_bias, norm_g, ffn_w_gu, ffn_w_down, rm_w_in, ml_conv_w, ml_gate_b, rm_head_g,
           rm_w_out, swa_w_in, swa_sinks, swa_w_out):
    b, t, d = x.shape
    n = b * t
    depth = norm_g.shape[0]
    main_cols = 4 * RET_HEADS * HEAD_W + 4 * ML_HEADS * HEAD_W
    xs = x.reshape(n, d)
    row_tiles = n // ROW_TILE
    proj_tiles = main_cols // PROJ_TN
    nf = ffn_w_down.shape[1] // FFN_TF

    def row_bands(src, layer, outer, inner):
        rows = src.shape[1] // (outer * inner)
        return src, layer, (rows, src.shape[2]), lambda i, j: (i * inner + j, 0)

    def blocks(src, layer, outer, col_blocks, transposed=False):
        if transposed:
            blk = (src.shape[1] // col_blocks, src.shape[2] // outer)
            return src, layer, blk, lambda i, j: (jnp.minimum(j, col_blocks - 1), i)
        blk = (src.shape[1] // outer, src.shape[2] // col_blocks)
        return src, layer, blk, lambda i, j: (i, jnp.minimum(j, col_blocks - 1))

    ready = {}

    def bf16_weight(name, src, layer):
        if (name, layer) in ready:
            return ready.pop((name, layer))
        return src[layer].astype(BF16)

    for layer in range(depth):
        g = norm_g[layer]
        if layer % 2 == 0:
            e = layer // 2
            w_in = rm_w_in[e]
            gate_cols = w_in.shape[1] - main_cols
            w_gate = jnp.pad(w_in[:, main_cols:], ((0, 0), (0, LANES - gate_cols))).astype(BF16)
            casts = [row_bands(ffn_w_gu, layer, row_tiles, proj_tiles),
                     blocks(ffn_w_down, layer, row_tiles, proj_tiles),
                     row_bands(rm_w_out, e, row_tiles, proj_tiles)]
            proj, gates, *cast = _norm_matmul(xs, g[0], w_in.astype(BF16), w_gate, proj_tiles, ROW_TILE, PROJ_TN, casts)
            ready[("gu", layer)], ready[("down", layer)], ready[("rm_out", e)] = cast
            gates_row = gates[:, :gate_cols].reshape(b, t, gate_cols).transpose(0, 2, 1)
            mixed = _mixer(proj.reshape(main_cols // PROJ_TN, b, t, PROJ_TN), gates_row, ml_gate_b[e], ml_conv_w[e],
                           rm_head_g[e])
            xs = _proj_norm_res(mixed.reshape(n, -1), bf16_weight("rm_out", rm_w_out, e), xs, g[1],
                                OUT_ROW_TILE, OUT_ROW_BLOCK)
        else:
            o = layer // 2
            q, kv = _swa_in_proj(xs, g[0], bf16_weight("swa_in", swa_w_in, o), b, t, ROW_TILE, PROJ_ROW_BLOCK)
            att = _swa(q, kv.reshape(b, t, -1), swa_sinks[o], rel_bias)
            xs = _proj_norm_res(att.reshape(n, -1), bf16_weight("swa_out", swa_w_out, o), xs, g[1],
                                OUT_ROW_TILE, OUT_ROW_BLOCK)
        casts, names = [], []
        if layer % 2 == 0 and layer + 1 < depth:
            o = (layer + 1) // 2
            casts = [blocks(ffn_w_gu, layer + 1, row_tiles, nf),
                     blocks(ffn_w_down, layer + 1, row_tiles, nf, transposed=True),
                     blocks(swa_w_in, o, row_tiles, swa_w_in.shape[2] // CAST_COLS),
                     blocks(swa_w_out, o, row_tiles, swa_w_out.shape[2] // CAST_COLS)]
            names = [("gu", layer + 1), ("down", layer + 1), ("swa_in", o), ("swa_out", o)]
        xs, *cast = _ffn(xs, g[2], bf16_weight("gu", ffn_w_gu, layer), bf16_weight("down", ffn_w_down, layer), g[3],
                         ROW_TILE, FFN_TF, FFN_ROW_BLOCK, casts)
        ready.update(zip(names, cast))
    return xs.reshape(b, t, d)
```

```python
import functools
import math

import jax
import jax.numpy as jnp
from jax import lax
from jax.experimental import pallas as pl
from jax.experimental.pallas import tpu as pltpu

F32 = jnp.float32
BF16 = jnp.bfloat16

EPS = 1e-6
CHUNK = 128
RET_HEADS = 4
ML_HEADS = 4
HEAD_W = 256
CONV_W = 4
ROPE_BASE = 10000.0
ATT_HEADS = 32
KV_HEADS = 4
HEAD_DIM = 64
WINDOW = 128
ATT_BLOCK = 128
N_BUCKETS = 32
MAX_DIST = 128
LANES = 128
SUBLANES = 8
VMEM_LIMIT = 56 * 1024 * 1024

ROW_TILE = 1024
OUT_ROW_TILE = 512
PROJ_TN = 1024
PROJ_ROW_BLOCK = 256
MIXER_CHUNKS = 1
MIXER_SEQS = 2
FFN_TF = 512
FFN_ROW_BLOCK = 512


def _params(vmem=VMEM_LIMIT):
    return pltpu.CompilerParams(vmem_limit_bytes=vmem)


def _rms_rows(x, g):
    ms = jnp.mean(x * x, axis=-1, keepdims=True)
    return x * lax.rsqrt(ms + EPS) * g


def _silu(x):
    return x * jax.nn.sigmoid(x)


def _cast_specs(casts):
    in_specs = [pl.BlockSpec((None, *blk), (lambda *ids, layer=layer, idx=idx: (layer, *idx(*ids))))
                for _, layer, blk, idx in casts]
    out_specs = [pl.BlockSpec(blk, idx) for _, _, blk, idx in casts]
    out_shape = [jax.ShapeDtypeStruct(src.shape[1:], BF16) for src, _, _, _ in casts]
    return in_specs, out_specs, out_shape


def _cast_pieces(src_refs, dst_refs):
    for src_ref, dst_ref in zip(src_refs, dst_refs):
        dst_ref[...] = src_ref[...].astype(dst_ref.dtype)


def _norm_matmul_body(x_ref, g_ref, w_ref, ws_ref, *rest, row_block, n_cast):
    cast_src, (o_ref, os_ref), cast_dst, h_ref = rest[:n_cast], rest[n_cast:n_cast + 2], rest[n_cast + 2:-1], rest[-1]
    j = pl.program_id(1)

    @pl.when(j == 0)
    def _():
        _cast_pieces(cast_src, cast_dst)
        for r in range(x_ref.shape[0] // row_block):
            rows = slice(r * row_block, (r + 1) * row_block)
            h = _rms_rows(x_ref[rows, :], g_ref[...]).astype(h_ref.dtype)
            h_ref[rows, :] = h
            os_ref[rows, :] = jnp.dot(h, ws_ref[...], preferred_element_type=F32)
            o_ref[rows, :] = jnp.dot(h, w_ref[...], preferred_element_type=F32).astype(o_ref.dtype)

    @pl.when(j > 0)
    def _():
        _cast_pieces(cast_src, cast_dst)
        o_ref[...] = jnp.dot(h_ref[...], w_ref[...], preferred_element_type=F32).astype(o_ref.dtype)


def _norm_matmul(x, g, w, w_side, tiles, tm, tn, casts):
    n, d = x.shape
    ns = w_side.shape[1]
    cast_in, cast_out, cast_shape = _cast_specs(casts)
    return pl.pallas_call(
        functools.partial(_norm_matmul_body, row_block=PROJ_ROW_BLOCK, n_cast=len(casts)),
        grid=(n // tm, tiles),
        in_specs=[
            pl.BlockSpec((tm, d), lambda i, j: (i, 0)),
            pl.BlockSpec((1, d), lambda i, j: (0, 0)),
            pl.BlockSpec((d, tn), lambda i, j: (0, j)),
            pl.BlockSpec((d, ns), lambda i, j: (0, 0)),
            *cast_in,
        ],
        out_specs=[pl.BlockSpec((None, tm, tn), lambda i, j: (j, i, 0)),
                   pl.BlockSpec((tm, ns), lambda i, j: (i, 0)),
                   *cast_out],
        out_shape=[jax.ShapeDtypeStruct((tiles, n, tn), BF16), jax.ShapeDtypeStruct((n, ns), F32), *cast_shape],
        scratch_shapes=[pltpu.VMEM((tm, d), BF16)],
        compiler_params=_params(),
        name="norm_matmul",
    )(x, g.reshape(1, d), w, w_side, *[c[0] for c in casts])


def _proj_norm_res_body(a_ref, w_ref, x_ref, g_ref, o_ref):
    y = jnp.dot(a_ref[...], w_ref[...], preferred_element_type=F32)
    o_ref[...] = x_ref[...] + _rms_rows(y, g_ref[...])


def _proj_norm_res(a, w, x, g, tm):
    n, d = x.shape
    return pl.pallas_call(
        _proj_norm_res_body,
        grid=(n // tm,),
        in_specs=[
            pl.BlockSpec((tm, a.shape[1]), lambda i: (i, 0)),
            pl.BlockSpec(w.shape, lambda i: (0, 0)),
            pl.BlockSpec((tm, d), lambda i: (i, 0)),
            pl.BlockSpec((1, d), lambda i: (0, 0)),
        ],
        out_specs=pl.BlockSpec((tm, d), lambda i: (i, 0)),
        out_shape=jax.ShapeDtypeStruct((n, d), F32),
        compiler_params=_params(),
        name="proj_norm_res",
    )(a, w, x, g.reshape(1, d))


def _ffn_body(x_ref, g_in_ref, wg_ref, wu_ref, wd_ref, g_out_ref, o_ref, h_ref, *, row_block):
    f = pl.program_id(1)
    last = pl.num_programs(1) - 1
    blocks = [slice(r * row_block, (r + 1) * row_block) for r in range(x_ref.shape[0] // row_block)]

    def partial_down(rows):
        h = h_ref[rows, :]
        gate = jnp.dot(h, wg_ref[...], preferred_element_type=F32)
        up = jnp.dot(h, wu_ref[...], preferred_element_type=F32)
        act = (_silu(gate) * up).astype(BF16)
        return jnp.dot(act, wd_ref[...], preferred_element_type=F32)

    @pl.when(f == 0)
    def _():
        for rows in blocks:
            h_ref[rows, :] = _rms_rows(x_ref[rows, :], g_in_ref[...]).astype(h_ref.dtype)
            o_ref[rows, :] = partial_down(rows)

    @pl.when((f > 0) & (f < last))
    def _():
        for rows in blocks:
            o_ref[rows, :] += partial_down(rows)

    @pl.when(f == last)
    def _():
        for rows in blocks:
            y = o_ref[rows, :] + partial_down(rows)
            o_ref[rows, :] = x_ref[rows, :] + _rms_rows(y, g_out_ref[...])


def _ffn(x, g_in, w_gu, w_down, g_out, tm, tf, row_block):
    n, d = x.shape
    nf = w_down.shape[0] // tf
    return pl.pallas_call(
        functools.partial(_ffn_body, row_block=row_block),
        grid=(n // tm, nf),
        in_specs=[
            pl.BlockSpec((tm, d), lambda i, f: (i, 0)),
            pl.BlockSpec((1, d), lambda i, f: (0, 0)),
            pl.BlockSpec((d, tf), lambda i, f: (0, f)),
            pl.BlockSpec((d, tf), lambda i, f: (0, f + nf)),
            pl.BlockSpec((tf, d), lambda i, f: (f, 0)),
            pl.BlockSpec((1, d), lambda i, f: (0, 0)),
        ],
        out_specs=pl.BlockSpec((tm, d), lambda i, f: (i, 0)),
        out_shape=jax.ShapeDtypeStruct((n, d), F32),
        scratch_shapes=[pltpu.VMEM((tm, d), BF16)],
        compiler_params=_params(),
        name="ffn",
    )(x, g_in.reshape(1, d), w_gu, w_gu, w_down, g_out.reshape(1, d))


def _log_sigmoid(x):
    return jnp.minimum(x, 0.0) - jnp.log1p(jnp.exp(-jnp.abs(x)))


_NT = (((1,), (1,)), ((), ()))
_TN = (((0,), (0,)), ((), ()))


def _retention_head(h, bb, rows, q_ref, k_ref, v_ref, g_ref, cos, sin, dm_ref, wq_ref, wk_ref, gl_ref, hg_ref,
                    o_ref, state_ref):
    cols = slice(h * HEAD_W, (h + 1) * HEAD_W)
    half = HEAD_W // 2

    def rot(x):
        x = x.astype(F32)
        x1, x2 = x[:, :half], x[:, half:]
        return jnp.concatenate([x1 * cos - x2 * sin, x2 * cos + x1 * sin], axis=-1)

    q = rot(q_ref[bb, rows, cols])
    k = rot(k_ref[bb, rows, cols]) * (HEAD_W ** -0.5)
    v = v_ref[bb, rows, cols]
    qb = q.astype(BF16)
    s = lax.dot_general(qb, k.astype(BF16), _NT, preferred_element_type=F32) * dm_ref[h]
    intra = jnp.dot(s.astype(BF16), v, preferred_element_type=F32)
    si = bb * RET_HEADS + h
    state = state_ref[si]
    inter = jnp.dot(qb, state.astype(BF16), preferred_element_type=F32) * wq_ref[h]
    out = intra + inter
    kw = (k * wk_ref[h]).astype(BF16)
    state_ref[si] = state * gl_ref[h] + lax.dot_general(kw, v, _TN, preferred_element_type=F32)
    y = _rms_rows(out, hg_ref[:, cols])
    o_ref[bb, rows, cols] = (y * _silu(g_ref[bb, rows, cols].astype(F32))).astype(o_ref.dtype)


def _causal_conv(cur, prev, shift_ref, w):
    shifted = jnp.dot(shift_ref[...], jnp.concatenate([prev, cur], axis=0), preferred_element_type=F32)
    y = cur.astype(F32) * w[CONV_W - 1:CONV_W]
    for s in range(1, CONV_W):
        y = y + shifted[(s - 1) * CHUNK:s * CHUNK] * w[CONV_W - 1 - s:CONV_W - s]
    return y


def _mlstm_head(h, bb, rows, q_all, k_all, v_ref, og_ref, gx, b_rows, lower_tri, hg_ref, o_ref, c_ref, n_ref,
                m_ref):
    cols = slice(h * HEAD_W, (h + 1) * HEAD_W)
    q = q_all[:, cols]
    k = k_all[:, cols]
    v = v_ref[bb, rows, cols]
    si = bb * ML_HEADS + h

    i_row = gx[h:h + 1]
    lf_row = gx[ML_HEADS + h:ML_HEADS + h + 1]
    b_row = b_rows[ML_HEADS + h:ML_HEADS + h + 1]
    b_cols = lax.dot_general(lower_tri, jnp.broadcast_to(lf_row, (CHUNK, CHUNK)), _NT,
                             precision=lax.Precision.HIGHEST, preferred_element_type=F32)
    i_cols = jnp.broadcast_to(i_row, (CHUNK, CHUNK)).T
    rowi = lax.broadcasted_iota(jnp.int32, (CHUNK, CHUNK), 0)
    coli = lax.broadcasted_iota(jnp.int32, (CHUNK, CHUNK), 1)
    log_d = jnp.where(rowi >= coli, b_cols - b_row + i_row, -jnp.inf)
    m_prev = m_ref[si]
    inter_log = b_cols + m_prev
    m_t = jnp.maximum(inter_log, jnp.max(log_d, axis=-1, keepdims=True))
    d_mat = jnp.exp(log_d - m_t)
    w_inter = jnp.exp(inter_log - m_t)

    def wide(a):
        return jnp.concatenate([a] * (HEAD_W // LANES), axis=1)

    qb = q.astype(BF16)
    s = lax.dot_general(qb, k.astype(BF16), _NT, preferred_element_type=F32) * d_mat
    c_state = c_ref[si]
    n_state = n_ref[si]
    num = (jnp.dot(s.astype(BF16), v, preferred_element_type=F32)
           + wide(w_inter) * jnp.dot(qb, c_state.astype(BF16), preferred_element_type=F32))
    den = (jnp.sum(s, axis=-1, keepdims=True)
           + w_inter * jnp.sum(q * n_state, axis=-1, keepdims=True))
    hid = num / wide(jnp.maximum(jnp.abs(den), jnp.exp(-m_t)))

    b_last = b_cols[CHUNK - 1:CHUNK]
    log_w = b_last - b_cols + i_cols
    m_new = jnp.maximum(b_last + m_prev, jnp.max(log_w, axis=0, keepdims=True))
    w = jnp.exp(log_w - m_new)
    decay = wide(jnp.exp(b_last + m_prev - m_new))
    kw = k * wide(w)
    c_ref[si] = decay * c_state + lax.dot_general(kw.astype(BF16), v, _TN, preferred_element_type=F32)
    n_ref[si] = decay * n_state + jnp.sum(kw, axis=0, keepdims=True)
    m_ref[si] = m_new

    out_cols = slice(RET_HEADS * HEAD_W + h * HEAD_W, RET_HEADS * HEAD_W + (h + 1) * HEAD_W)
    y = _rms_rows(hid, hg_ref[:, out_cols])
    o_ref[bb, rows, out_cols] = (y * jax.nn.sigmoid(og_ref[bb, rows, cols].astype(F32))).astype(o_ref.dtype)


def _mixer_body(gl_ref, rq_ref, rk_ref, rv_ref, rg_ref, mq_ref, mqp_ref, mk_ref, mkp_ref, mv_ref, mo_ref,
                cos_ref, sin_ref, dm_ref, wq_ref, wk_ref, gr_ref, gb_ref, cw_ref, shift_ref, hg_ref,
                o_ref, r_ref, c_ref, n_ref, m_ref):
    first = pl.program_id(1) == 0

    @pl.when(first)
    def _():
        r_ref[...] = jnp.zeros_like(r_ref)
        c_ref[...] = jnp.zeros_like(c_ref)
        n_ref[...] = jnp.zeros_like(n_ref)
        m_ref[...] = jnp.zeros_like(m_ref)

    ml_w = ML_HEADS * HEAD_W
    rowi = lax.broadcasted_iota(jnp.int32, (CHUNK, CHUNK), 0)
    coli = lax.broadcasted_iota(jnp.int32, (CHUNK, CHUNK), 1)
    upper_tri = jnp.where(rowi <= coli, 1.0, 0.0).astype(F32)
    lower_tri = jnp.where(rowi >= coli, 1.0, 0.0).astype(F32)
    is_input = lax.broadcasted_iota(jnp.int32, (2 * ML_HEADS, CHUNK), 0) < ML_HEADS

    for bb in range(mq_ref.shape[0]):
        for ci in range(mq_ref.shape[1] // CHUNK):
            rows = slice(ci * CHUNK, (ci + 1) * CHUNK)
            cos = cos_ref[rows, :]
            sin = sin_ref[rows, :]
            for h in range(RET_HEADS):
                _retention_head(h, bb, rows, rq_ref, rk_ref, rv_ref, rg_ref, cos, sin, dm_ref, wq_ref, wk_ref,
                                gl_ref, hg_ref, o_ref, r_ref)

            if ci == 0:
                q_prev = jnp.where(first, jnp.zeros_like(mqp_ref[bb]), mqp_ref[bb])
                k_prev = jnp.where(first, jnp.zeros_like(mkp_ref[bb]), mkp_ref[bb])
            else:
                before = slice((ci - 1) * CHUNK, ci * CHUNK)
                q_prev, k_prev = mq_ref[bb, before, :], mk_ref[bb, before, :]
            q_all = _silu(_causal_conv(mq_ref[bb, rows, :], q_prev, shift_ref, cw_ref[:, :ml_w]))
            k_all = _silu(_causal_conv(mk_ref[bb, rows, :], k_prev, shift_ref, cw_ref[:, ml_w:])) * (HEAD_W ** -0.5)
            gates = gr_ref[bb, :, rows] + gb_ref[...]
            gx = jnp.where(is_input, gates, _log_sigmoid(gates))
            b_rows = jnp.dot(gx, upper_tri, precision=lax.Precision.HIGHEST, preferred_element_type=F32)
            for h in range(ML_HEADS):
                _mlstm_head(h, bb, rows, q_all, k_all, mv_ref, mo_ref, gx, b_rows, lower_tri, hg_ref, o_ref,
                            c_ref, n_ref, m_ref)


def _mixer(proj, gates_row, gate_b, conv_w, head_g):
    _, b, t, _ = proj.shape
    seqs = MIXER_SEQS
    step = MIXER_CHUNKS * CHUNK
    nc = t // step
    ret_w = RET_HEADS * HEAD_W
    ml_w = ML_HEADS * HEAD_W
    assert ret_w == ml_w == proj.shape[3]
    log_g = jnp.log1p(-jnp.exp2(-5.0 - jnp.arange(RET_HEADS, dtype=F32)))
    idx = jnp.arange(CHUNK, dtype=F32)
    diff = idx[:, None] - idx[None, :]
    dmask = jnp.where(diff >= 0, jnp.exp(log_g[:, None, None] * jnp.maximum(diff, 0.0)), 0.0)
    w_k = jnp.exp(log_g[:, None] * (CHUNK - 1.0 - idx)[None, :])
    w_q = jnp.exp(log_g[:, None] * (idx + 1.0)[None, :])
    w_k = jnp.broadcast_to(w_k[..., None], (RET_HEADS, CHUNK, HEAD_W))
    w_q = jnp.broadcast_to(w_q[..., None], (RET_HEADS, CHUNK, HEAD_W))
    g_l = jnp.exp(log_g * CHUNK)
    half = HEAD_W // 2
    inv = 1.0 / (ROPE_BASE ** jnp.linspace(0.0, 1.0, half, dtype=F32))
    ang = jnp.arange(t).astype(F32)[:, None] * inv[None, :]
    cos, sin = jnp.cos(ang), jnp.sin(ang)
    r = jnp.arange((CONV_W - 1) * CHUNK)
    src = CHUNK + r % CHUNK - (r // CHUNK + 1)
    shift = (jnp.arange(2 * CHUNK)[None, :] == src[:, None]).astype(BF16)

    def group(gi):
        return pl.BlockSpec((None, seqs, step, ret_w), lambda bi, c: (gi, bi, c, 0))

    def prev_group(gi):
        return pl.BlockSpec((None, seqs, CHUNK, ml_w),
                            lambda bi, c: (gi, bi, jnp.maximum(c * MIXER_CHUNKS - 1, 0), 0))

    def whole(a):
        return pl.BlockSpec(a.shape, lambda bi, c: (0,) * a.ndim)

    gb = gate_b.reshape(2 * ML_HEADS, 1)
    hg = head_g.reshape(1, ret_w + ml_w)
    return pl.pallas_call(
        _mixer_body,
        grid=(b // seqs, nc),
        in_specs=[
            pl.BlockSpec(memory_space=pltpu.SMEM),
            group(0), group(1), group(2), group(3),
            group(4), prev_group(4), group(5), prev_group(5), group(6), group(7),
            pl.BlockSpec((step, half), lambda bi, c: (c, 0)),
            pl.BlockSpec((step, half), lambda bi, c: (c, 0)),
            whole(dmask), whole(w_q), whole(w_k),
            pl.BlockSpec((seqs, 2 * ML_HEADS, step), lambda bi, c: (bi, 0, c)),
            whole(gb), whole(conv_w), whole(shift), whole(hg),
        ],
        out_specs=pl.BlockSpec((seqs, step, ret_w + ml_w), lambda bi, c: (bi, c, 0)),
        out_shape=jax.ShapeDtypeStruct((b, t, ret_w + ml_w), BF16),
        scratch_shapes=[pltpu.VMEM((seqs * RET_HEADS, HEAD_W, HEAD_W), F32),
                        pltpu.VMEM((seqs * ML_HEADS, HEAD_W, HEAD_W), F32),
                        pltpu.VMEM((seqs * ML_HEADS, 1, HEAD_W), F32),
                        pltpu.VMEM((seqs * ML_HEADS, 1, LANES), F32)],
        compiler_params=_params(),
        name="mixer",
    )(g_l, proj, proj, proj, proj, proj, proj, proj, proj, proj, proj, cos, sin, dmask, w_q, w_k,
      gates_row, gb, conv_w, shift, hg)


PAIR_W = 2 * HEAD_DIM
GROUP = ATT_HEADS // KV_HEADS
PAIRS = GROUP // 2


def _swa_in_proj_body(x_ref, g_ref, w_ref, q_ref, kv_ref, *, row_block):
    qw = ATT_HEADS * HEAD_DIM
    per_block = row_block // ATT_BLOCK
    for r in range(x_ref.shape[0] // row_block):
        rows = slice(r * row_block, (r + 1) * row_block)
        h = _rms_rows(x_ref[rows, :], g_ref[...]).astype(BF16)
        res = jnp.dot(h, w_ref[...], preferred_element_type=F32)
        for nbl in range(per_block):
            sub = slice(nbl * ATT_BLOCK, (nbl + 1) * ATT_BLOCK)
            for p in range(q_ref.shape[2]):
                q_ref[0, r * per_block + nbl, p] = res[sub, p * PAIR_W:(p + 1) * PAIR_W].astype(q_ref.dtype)
        kv_ref[rows, :] = res[:, qw:].astype(kv_ref.dtype)


def _swa_in_proj(x, g, w, b, t, tm, row_block):
    n, d = x.shape
    cols = w.shape[1]
    kvw = cols - ATT_HEADS * HEAD_DIM
    tiles_per_seq = t // tm
    return pl.pallas_call(
        functools.partial(_swa_in_proj_body, row_block=row_block),
        grid=(n // tm,),
        in_specs=[
            pl.BlockSpec((tm, d), lambda i: (i, 0)),
            pl.BlockSpec((1, d), lambda i: (0, 0)),
            pl.BlockSpec((d, cols), lambda i: (0, 0)),
        ],
        out_specs=[
            pl.BlockSpec((1, tm // ATT_BLOCK, ATT_HEADS // 2, ATT_BLOCK, PAIR_W),
                         lambda i: (i // tiles_per_seq, i % tiles_per_seq, 0, 0, 0)),
            pl.BlockSpec((tm, kvw), lambda i: (i, 0)),
        ],
        out_shape=[jax.ShapeDtypeStruct((b, t // ATT_BLOCK, ATT_HEADS // 2, ATT_BLOCK, PAIR_W), BF16),
                   jax.ShapeDtypeStruct((n, kvw), BF16)],
        compiler_params=_params(),
        name="swa_in_proj",
    )(x, g.reshape(1, d), w)


def _swa_body(rb_ref, sink_ref, idx_ref, q_ref, kp_ref, kc_ref, vp_ref, vc_ref, o_ref, bias_ref, sinkcol_ref):
    bi = pl.program_id(0)
    nb = pl.program_id(1)
    blk = ATT_BLOCK
    rows = PAIRS * blk

    @pl.when((bi == 0) & (nb == 0))
    def _():
        idx = idx_ref[...]
        col = lax.broadcasted_iota(jnp.int32, idx.shape, 1)

        def head_body(hd, carry):
            def bucket_body(bk, acc):
                return jnp.where(idx == bk, rb_ref[bk * ATT_HEADS + hd], acc)

            tbl = lax.fori_loop(0, N_BUCKETS, bucket_body, jnp.full(idx.shape, -jnp.inf, F32))
            kv = hd // GROUP
            half = hd % 2
            at = pl.ds(pl.multiple_of(((hd % GROUP) // 2) * blk, blk), blk)
            bias_ref[0, kv, half, at, :] = tbl
            bias_ref[1, kv, half, at, :] = jnp.where(col >= blk, tbl, -jnp.inf)
            sinkcol_ref[kv, half, at, :] = jnp.full((blk, LANES), sink_ref[hd], F32)
            return carry

        lax.fori_loop(0, ATT_HEADS, head_body, 0)

    first = (nb == 0).astype(jnp.int32)
    low = lax.broadcasted_iota(jnp.int32, (2 * blk, LANES), 1) < HEAD_DIM
    low_out = lax.broadcasted_iota(jnp.int32, (rows, LANES), 1) < HEAD_DIM
    kf = jnp.concatenate([kp_ref[0], kc_ref[0]], axis=0).astype(F32)
    vf = jnp.concatenate([vp_ref[0], vc_ref[0]], axis=0).astype(F32)
    scale = HEAD_DIM ** -0.5

    def halves(xf, kv):
        xc = xf[:, (kv // 2) * LANES:(kv // 2 + 1) * LANES]
        xr = pltpu.roll(xc, HEAD_DIM, axis=1)
        lo_src, hi_src = (xc, xr) if kv % 2 == 0 else (xr, xc)
        return (jnp.where(low, lo_src, 0.0).astype(BF16), jnp.where(low, 0.0, hi_src).astype(BF16))

    for kv in range(KV_HEADS):
        k_lo, k_hi = halves(kf, kv)
        v_lo, v_hi = halves(vf, kv)
        q4 = q_ref[kv * PAIRS:(kv + 1) * PAIRS].reshape(rows, PAIR_W) * scale

        def probs(k_half, half):
            s = lax.dot_general(q4, k_half, (((1,), (1,)), ((), ())), preferred_element_type=F32)
            s = s + bias_ref[first, kv, half]
            sink = sinkcol_ref[kv, half]
            mx = jnp.maximum(jnp.max(s, axis=-1, keepdims=True), sink)
            p = jnp.exp(s - jnp.concatenate([mx, mx], axis=1))
            den = jnp.sum(p, axis=-1, keepdims=True) + jnp.exp(sink - mx)
            return p.astype(BF16), den

        p_lo, den_lo = probs(k_lo, 0)
        p_hi, den_hi = probs(k_hi, 1)
        o4 = (jnp.dot(p_lo, v_lo, preferred_element_type=F32)
              + jnp.dot(p_hi, v_hi, preferred_element_type=F32)) / jnp.where(low_out, den_lo, den_hi)
        for pr in range(PAIRS):
            at = (kv * PAIRS + pr) * PAIR_W
            o_ref[0, :, at:at + PAIR_W] = o4[pr * blk:(pr + 1) * blk].astype(o_ref.dtype)


def _t5_bucket(dist):
    n = jnp.maximum(dist, 0)
    max_exact = N_BUCKETS // 2
    nf = jnp.maximum(n, 1).astype(F32)
    large = max_exact + (jnp.log(nf / max_exact) / math.log(MAX_DIST / max_exact)
                         * (N_BUCKETS - max_exact)).astype(jnp.int32)
    large = jnp.minimum(large, N_BUCKETS - 1)
    return jnp.where(n < max_exact, n, large)


def _swa(q, kv, sinks, rel_bias):
    b, nblk, n_pairs, blk, _ = q.shape
    t = nblk * blk
    qw = ATT_HEADS * HEAD_DIM
    kvw = KV_HEADS * HEAD_DIM
    i = jnp.arange(blk)
    j = jnp.arange(2 * blk)
    dist = (blk + i)[:, None] - j[None, :]
    idx = jnp.where((dist >= 0) & (dist < WINDOW), _t5_bucket(dist), -1).astype(jnp.int32)
    return pl.pallas_call(
        _swa_body,
        grid=(b, nblk),
        in_specs=[
            pl.BlockSpec(memory_space=pltpu.SMEM),
            pl.BlockSpec(memory_space=pltpu.SMEM),
            pl.BlockSpec((blk, 2 * blk), lambda bi, nb: (0, 0)),
            pl.BlockSpec((None, None, n_pairs, blk, PAIR_W), lambda bi, nb: (bi, nb, 0, 0, 0)),
            pl.BlockSpec((1, blk, kvw), lambda bi, nb: (bi, jnp.maximum(nb - 1, 0), 0)),
            pl.BlockSpec((1, blk, kvw), lambda bi, nb: (bi, nb, 0)),
            pl.BlockSpec((1, blk, kvw), lambda bi, nb: (bi, jnp.maximum(nb - 1, 0), 1)),
            pl.BlockSpec((1, blk, kvw), lambda bi, nb: (bi, nb, 1)),
        ],
        out_specs=pl.BlockSpec((1, blk, qw), lambda bi, nb: (bi, nb, 0)),
        out_shape=jax.ShapeDtypeStruct((b, t, qw), BF16),
        scratch_shapes=[pltpu.VMEM((2, KV_HEADS, 2, PAIRS * blk, 2 * blk), F32),
                        pltpu.VMEM((KV_HEADS, 2, PAIRS * blk, LANES), F32)],
        compiler_params=_params(),
        name="swa",
    )(rel_bias.reshape(-1), sinks, idx, q, kv, kv, kv, kv)


def kernel(x, rel_bias, norm_g, ffn_w_gu, ffn_w_down, rm_w_in, ml_conv_w, ml_gate_b, rm_head_g,
           rm_w_out, swa_w_in, swa_sinks, swa_w_out):
    b, t, d = x.shape
    n = b * t
    depth = norm_g.shape[0]
    main_cols = 4 * RET_HEADS * HEAD_W + 4 * ML_HEADS * HEAD_W
    xs = x.reshape(n, d)
    row_tiles = n // ROW_TILE
    proj_tiles = main_cols // PROJ_TN

    def row_bands(src, layer):
        rows = src.shape[1] // (row_tiles * proj_tiles)
        return src, layer, (rows, src.shape[2]), lambda i, j: (i * proj_tiles + j, 0)

    def blocks(src, layer):
        return src, layer, (src.shape[1] // row_tiles, src.shape[2] // proj_tiles), lambda i, j: (i, j)

    ready = {}

    def bf16_weight(name, src, layer):
        if (name, layer) in ready:
            return ready.pop((name, layer))
        return src[layer].astype(BF16)

    for layer in range(depth):
        g = norm_g[layer]
        if layer % 2 == 0:
            e = layer // 2
            w_in = rm_w_in[e]
            gate_cols = w_in.shape[1] - main_cols
            w_gate = jnp.pad(w_in[:, main_cols:], ((0, 0), (0, LANES - gate_cols))).astype(BF16)
            names = [("gu", layer), ("down", layer), ("rm_out", e)]
            casts = [row_bands(ffn_w_gu, layer), blocks(ffn_w_down, layer), row_bands(rm_w_out, e)]
            if layer + 1 < depth:
                o = (layer + 1) // 2
                names += [("gu", layer + 1), ("down", layer + 1), ("swa_in", o), ("swa_out", o)]
                casts += [row_bands(ffn_w_gu, layer + 1), blocks(ffn_w_down, layer + 1),
                          row_bands(swa_w_in, o), row_bands(swa_w_out, o)]
            proj, gates, *cast = _norm_matmul(xs, g[0], w_in.astype(BF16), w_gate, proj_tiles, ROW_TILE, PROJ_TN, casts)
            ready.update(zip(names, cast))
            gates_row = gates[:, :gate_cols].reshape(b, t, gate_cols).transpose(0, 2, 1)
            mixed = _mixer(proj.reshape(main_cols // PROJ_TN, b, t, PROJ_TN), gates_row, ml_gate_b[e], ml_conv_w[e],
                           rm_head_g[e])
            xs = _proj_norm_res(mixed.reshape(n, -1), bf16_weight("rm_out", rm_w_out, e), xs, g[1], OUT_ROW_TILE)
        else:
            o = layer // 2
            q, kv = _swa_in_proj(xs, g[0], bf16_weight("swa_in", swa_w_in, o), b, t, ROW_TILE, PROJ_ROW_BLOCK)
            att = _swa(q, kv.reshape(b, t, -1), swa_sinks[o], rel_bias)
            xs = _proj_norm_res(att.reshape(n, -1), bf16_weight("swa_out", swa_w_out, o), xs, g[1], OUT_ROW_TILE)
        xs = _ffn(xs, g[2], bf16_weight("gu", ffn_w_gu, layer), bf16_weight("down", ffn_w_down, layer), g[3],
                  ROW_TILE, FFN_TF, FFN_ROW_BLOCK)
    return xs.reshape(b, t, d)
```

```python
import functools
import math

import jax
import jax.numpy as jnp
from jax import lax
from jax.experimental import pallas as pl
from jax.experimental.pallas import tpu as pltpu

F32 = jnp.float32
BF16 = jnp.bfloat16

EPS = 1e-6
CHUNK = 128
RET_HEADS = 4
ML_HEADS = 4
HEAD_W = 256
CONV_W = 4
ROPE_BASE = 10000.0
ATT_HEADS = 32
KV_HEADS = 4
HEAD_DIM = 64
WINDOW = 128
ATT_BLOCK = 128
N_BUCKETS = 32
MAX_DIST = 128
LANES = 128
SUBLANES = 8
VMEM_LIMIT = 56 * 1024 * 1024

ROW_TILE = 1024
OUT_ROW_TILE = 512
PROJ_TN = 1024
PROJ_ROW_BLOCK = 256
MIXER_CHUNKS = 1
MIXER_SEQS = 2
SWA_BLOCKS = 4
FFN_TF = 512
FFN_ROW_BLOCK = 512


def _params(vmem=VMEM_LIMIT):
    return pltpu.CompilerParams(vmem_limit_bytes=vmem)


def _rms_rows(x, g):
    ms = jnp.mean(x * x, axis=-1, keepdims=True)
    return x * lax.rsqrt(ms + EPS) * g


def _silu(x):
    return x * jax.nn.sigmoid(x)


def _cast_specs(casts):
    in_specs = [pl.BlockSpec((None, *blk), (lambda *ids, layer=layer, idx=idx: (layer, *idx(*ids))))
                for _, layer, blk, idx in casts]
    out_specs = [pl.BlockSpec(blk, idx) for _, _, blk, idx in casts]
    out_shape = [jax.ShapeDtypeStruct(src.shape[1:], BF16) for src, _, _, _ in casts]
    return in_specs, out_specs, out_shape


def _cast_pieces(src_refs, dst_refs):
    for src_ref, dst_ref in zip(src_refs, dst_refs):
        dst_ref[...] = src_ref[...].astype(dst_ref.dtype)


def _norm_matmul_body(x_ref, g_ref, w_ref, ws_ref, *rest, row_block, n_cast):
    cast_src, (o_ref, os_ref), cast_dst, h_ref = rest[:n_cast], rest[n_cast:n_cast + 2], rest[n_cast + 2:-1], rest[-1]
    j = pl.program_id(1)

    @pl.when(j == 0)
    def _():
        _cast_pieces(cast_src, cast_dst)
        for r in range(x_ref.shape[0] // row_block):
            rows = slice(r * row_block, (r + 1) * row_block)
            h = _rms_rows(x_ref[rows, :], g_ref[...]).astype(h_ref.dtype)
            h_ref[rows, :] = h
            os_ref[rows, :] = jnp.dot(h, ws_ref[...], preferred_element_type=F32)
            o_ref[rows, :] = jnp.dot(h, w_ref[...], preferred_element_type=F32).astype(o_ref.dtype)

    @pl.when(j > 0)
    def _():
        _cast_pieces(cast_src, cast_dst)
        o_ref[...] = jnp.dot(h_ref[...], w_ref[...], preferred_element_type=F32).astype(o_ref.dtype)


def _norm_matmul(x, g, w, w_side, tiles, tm, tn, casts):
    n, d = x.shape
    ns = w_side.shape[1]
    cast_in, cast_out, cast_shape = _cast_specs(casts)
    return pl.pallas_call(
        functools.partial(_norm_matmul_body, row_block=PROJ_ROW_BLOCK, n_cast=len(casts)),
        grid=(n // tm, tiles),
        in_specs=[
            pl.BlockSpec((tm, d), lambda i, j: (i, 0)),
            pl.BlockSpec((1, d), lambda i, j: (0, 0)),
            pl.BlockSpec((d, tn), lambda i, j: (0, j)),
            pl.BlockSpec((d, ns), lambda i, j: (0, 0)),
            *cast_in,
        ],
        out_specs=[pl.BlockSpec((None, tm, tn), lambda i, j: (j, i, 0)),
                   pl.BlockSpec((tm, ns), lambda i, j: (i, 0)),
                   *cast_out],
        out_shape=[jax.ShapeDtypeStruct((tiles, n, tn), BF16), jax.ShapeDtypeStruct((n, ns), F32), *cast_shape],
        scratch_shapes=[pltpu.VMEM((tm, d), BF16)],
        compiler_params=_params(),
        name="norm_matmul",
    )(x, g.reshape(1, d), w, w_side, *[c[0] for c in casts])


def _proj_norm_res_body(a_ref, w_ref, x_ref, g_ref, o_ref):
    y = jnp.dot(a_ref[...], w_ref[...], preferred_element_type=F32)
    o_ref[...] = x_ref[...] + _rms_rows(y, g_ref[...])


def _proj_norm_res(a, w, x, g, tm):
    n, d = x.shape
    return pl.pallas_call(
        _proj_norm_res_body,
        grid=(n // tm,),
        in_specs=[
            pl.BlockSpec((tm, a.shape[1]), lambda i: (i, 0)),
            pl.BlockSpec(w.shape, lambda i: (0, 0)),
            pl.BlockSpec((tm, d), lambda i: (i, 0)),
            pl.BlockSpec((1, d), lambda i: (0, 0)),
        ],
        out_specs=pl.BlockSpec((tm, d), lambda i: (i, 0)),
        out_shape=jax.ShapeDtypeStruct((n, d), F32),
        compiler_params=_params(),
        name="proj_norm_res",
    )(a, w, x, g.reshape(1, d))


def _ffn_body(x_ref, g_in_ref, wg_ref, wu_ref, wd_ref, g_out_ref, o_ref, h_ref, *, row_block):
    f = pl.program_id(1)
    last = pl.num_programs(1) - 1
    blocks = [slice(r * row_block, (r + 1) * row_block) for r in range(x_ref.shape[0] // row_block)]

    def partial_down(rows):
        h = h_ref[rows, :]
        gate = jnp.dot(h, wg_ref[...], preferred_element_type=F32)
        up = jnp.dot(h, wu_ref[...], preferred_element_type=F32)
        act = (_silu(gate) * up).astype(BF16)
        return jnp.dot(act, wd_ref[...], preferred_element_type=F32)

    @pl.when(f == 0)
    def _():
        for rows in blocks:
            h_ref[rows, :] = _rms_rows(x_ref[rows, :], g_in_ref[...]).astype(h_ref.dtype)
            o_ref[rows, :] = partial_down(rows)

    @pl.when((f > 0) & (f < last))
    def _():
        for rows in blocks:
            o_ref[rows, :] += partial_down(rows)

    @pl.when(f == last)
    def _():
        for rows in blocks:
            y = o_ref[rows, :] + partial_down(rows)
            o_ref[rows, :] = x_ref[rows, :] + _rms_rows(y, g_out_ref[...])


def _ffn(x, g_in, w_gu, w_down, g_out, tm, tf, row_block):
    n, d = x.shape
    nf = w_down.shape[0] // tf
    return pl.pallas_call(
        functools.partial(_ffn_body, row_block=row_block),
        grid=(n // tm, nf),
        in_specs=[
            pl.BlockSpec((tm, d), lambda i, f: (i, 0)),
            pl.BlockSpec((1, d), lambda i, f: (0, 0)),
            pl.BlockSpec((d, tf), lambda i, f: (0, f)),
            pl.BlockSpec((d, tf), lambda i, f: (0, f + nf)),
            pl.BlockSpec((tf, d), lambda i, f: (f, 0)),
            pl.BlockSpec((1, d), lambda i, f: (0, 0)),
        ],
        out_specs=pl.BlockSpec((tm, d), lambda i, f: (i, 0)),
        out_shape=jax.ShapeDtypeStruct((n, d), F32),
        scratch_shapes=[pltpu.VMEM((tm, d), BF16)],
        compiler_params=_params(),
        name="ffn",
    )(x, g_in.reshape(1, d), w_gu, w_gu, w_down, g_out.reshape(1, d))


def _log_sigmoid(x):
    return jnp.minimum(x, 0.0) - jnp.log1p(jnp.exp(-jnp.abs(x)))


_NT = (((1,), (1,)), ((), ()))
_TN = (((0,), (0,)), ((), ()))


def _retention_head(h, bb, rows, q_ref, k_ref, v_ref, g_ref, cos, sin, dm_ref, wq_ref, wk_ref, gl_ref, hg_ref,
                    o_ref, state_ref):
    cols = slice(h * HEAD_W, (h + 1) * HEAD_W)
    half = HEAD_W // 2

    def rot(x):
        x = x.astype(F32)
        x1, x2 = x[:, :half], x[:, half:]
        return jnp.concatenate([x1 * cos - x2 * sin, x2 * cos + x1 * sin], axis=-1)

    q = rot(q_ref[bb, rows, cols])
    k = rot(k_ref[bb, rows, cols]) * (HEAD_W ** -0.5)
    v = v_ref[bb, rows, cols]
    qb = q.astype(BF16)
    s = lax.dot_general(qb, k.astype(BF16), _NT, preferred_element_type=F32) * dm_ref[h]
    intra = jnp.dot(s.astype(BF16), v, preferred_element_type=F32)
    si = bb * RET_HEADS + h
    state = state_ref[si]
    inter = jnp.dot(qb, state.astype(BF16), preferred_element_type=F32) * wq_ref[h]
    out = intra + inter
    kw = (k * wk_ref[h]).astype(BF16)
    state_ref[si] = state * gl_ref[h] + lax.dot_general(kw, v, _TN, preferred_element_type=F32)
    y = _rms_rows(out, hg_ref[:, cols])
    o_ref[bb, rows, cols] = (y * _silu(g_ref[bb, rows, cols].astype(F32))).astype(o_ref.dtype)


def _causal_conv(cur, prev, shift_ref, w):
    shifted = jnp.dot(shift_ref[...], jnp.concatenate([prev, cur], axis=0), preferred_element_type=F32)
    y = cur.astype(F32) * w[CONV_W - 1:CONV_W]
    for s in range(1, CONV_W):
        y = y + shifted[(s - 1) * CHUNK:s * CHUNK] * w[CONV_W - 1 - s:CONV_W - s]
    return y


def _mlstm_head(h, bb, rows, q_all, k_all, v_ref, og_ref, gx, b_rows, lower_tri, hg_ref, o_ref, c_ref, n_ref,
                m_ref):
    cols = slice(h * HEAD_W, (h + 1) * HEAD_W)
    q = q_all[:, cols]
    k = k_all[:, cols]
    v = v_ref[bb, rows, cols]
    si = bb * ML_HEADS + h

    i_row = gx[h:h + 1]
    lf_row = gx[ML_HEADS + h:ML_HEADS + h + 1]
    b_row = b_rows[ML_HEADS + h:ML_HEADS + h + 1]
    b_cols = lax.dot_general(lower_tri, jnp.broadcast_to(lf_row, (CHUNK, CHUNK)), _NT,
                             precision=lax.Precision.HIGHEST, preferred_element_type=F32)
    i_cols = jnp.broadcast_to(i_row, (CHUNK, CHUNK)).T
    rowi = lax.broadcasted_iota(jnp.int32, (CHUNK, CHUNK), 0)
    coli = lax.broadcasted_iota(jnp.int32, (CHUNK, CHUNK), 1)
    log_d = jnp.where(rowi >= coli, b_cols - b_row + i_row, -jnp.inf)
    m_prev = m_ref[si]
    inter_log = b_cols + m_prev
    m_t = jnp.maximum(inter_log, jnp.max(log_d, axis=-1, keepdims=True))
    d_mat = jnp.exp(log_d - m_t)
    w_inter = jnp.exp(inter_log - m_t)

    def wide(a):
        return jnp.concatenate([a] * (HEAD_W // LANES), axis=1)

    qb = q.astype(BF16)
    s = lax.dot_general(qb, k.astype(BF16), _NT, preferred_element_type=F32) * d_mat
    c_state = c_ref[si]
    n_state = n_ref[si]
    num = (jnp.dot(s.astype(BF16), v, preferred_element_type=F32)
           + wide(w_inter) * jnp.dot(qb, c_state.astype(BF16), preferred_element_type=F32))
    den = (jnp.sum(s, axis=-1, keepdims=True)
           + w_inter * jnp.sum(q * n_state, axis=-1, keepdims=True))
    hid = num / wide(jnp.maximum(jnp.abs(den), jnp.exp(-m_t)))

    b_last = b_cols[CHUNK - 1:CHUNK]
    log_w = b_last - b_cols + i_cols
    m_new = jnp.maximum(b_last + m_prev, jnp.max(log_w, axis=0, keepdims=True))
    w = jnp.exp(log_w - m_new)
    decay = wide(jnp.exp(b_last + m_prev - m_new))
    kw = k * wide(w)
    c_ref[si] = decay * c_state + lax.dot_general(kw.astype(BF16), v, _TN, preferred_element_type=F32)
    n_ref[si] = decay * n_state + jnp.sum(kw, axis=0, keepdims=True)
    m_ref[si] = m_new

    out_cols = slice(RET_HEADS * HEAD_W + h * HEAD_W, RET_HEADS * HEAD_W + (h + 1) * HEAD_W)
    y = _rms_rows(hid, hg_ref[:, out_cols])
    o_ref[bb, rows, out_cols] = (y * jax.nn.sigmoid(og_ref[bb, rows, cols].astype(F32))).astype(o_ref.dtype)


def _mixer_body(gl_ref, rq_ref, rk_ref, rv_ref, rg_ref, mq_ref, mqp_ref, mk_ref, mkp_ref, mv_ref, mo_ref,
                cos_ref, sin_ref, dm_ref, wq_ref, wk_ref, gr_ref, gb_ref, cw_ref, shift_ref, hg_ref,
                o_ref, r_ref, c_ref, n_ref, m_ref):
    first = pl.program_id(1) == 0

    @pl.when(first)
    def _():
        r_ref[...] = jnp.zeros_like(r_ref)
        c_ref[...] = jnp.zeros_like(c_ref)
        n_ref[...] = jnp.zeros_like(n_ref)
        m_ref[...] = jnp.zeros_like(m_ref)

    ml_w = ML_HEADS * HEAD_W
    rowi = lax.broadcasted_iota(jnp.int32, (CHUNK, CHUNK), 0)
    coli = lax.broadcasted_iota(jnp.int32, (CHUNK, CHUNK), 1)
    upper_tri = jnp.where(rowi <= coli, 1.0, 0.0).astype(F32)
    lower_tri = jnp.where(rowi >= coli, 1.0, 0.0).astype(F32)
    is_input = lax.broadcasted_iota(jnp.int32, (2 * ML_HEADS, CHUNK), 0) < ML_HEADS

    for bb in range(mq_ref.shape[0]):
        for ci in range(mq_ref.shape[1] // CHUNK):
            rows = slice(ci * CHUNK, (ci + 1) * CHUNK)
            cos = cos_ref[rows, :]
            sin = sin_ref[rows, :]
            for h in range(RET_HEADS):
                _retention_head(h, bb, rows, rq_ref, rk_ref, rv_ref, rg_ref, cos, sin, dm_ref, wq_ref, wk_ref,
                                gl_ref, hg_ref, o_ref, r_ref)

            if ci == 0:
                q_prev = jnp.where(first, jnp.zeros_like(mqp_ref[bb]), mqp_ref[bb])
                k_prev = jnp.where(first, jnp.zeros_like(mkp_ref[bb]), mkp_ref[bb])
            else:
                before = slice((ci - 1) * CHUNK, ci * CHUNK)
                q_prev, k_prev = mq_ref[bb, before, :], mk_ref[bb, before, :]
            q_all = _silu(_causal_conv(mq_ref[bb, rows, :], q_prev, shift_ref, cw_ref[:, :ml_w]))
            k_all = _silu(_causal_conv(mk_ref[bb, rows, :], k_prev, shift_ref, cw_ref[:, ml_w:])) * (HEAD_W ** -0.5)
            gates = gr_ref[bb, :, rows] + gb_ref[...]
            gx = jnp.where(is_input, gates, _log_sigmoid(gates))
            b_rows = jnp.dot(gx, upper_tri, precision=lax.Precision.HIGHEST, preferred_element_type=F32)
            for h in range(ML_HEADS):
                _mlstm_head(h, bb, rows, q_all, k_all, mv_ref, mo_ref, gx, b_rows, lower_tri, hg_ref, o_ref,
                            c_ref, n_ref, m_ref)


def _mixer(proj, gates_row, gate_b, conv_w, head_g):
    _, b, t, _ = proj.shape
    seqs = MIXER_SEQS
    step = MIXER_CHUNKS * CHUNK
    nc = t // step
    ret_w = RET_HEADS * HEAD_W
    ml_w = ML_HEADS * HEAD_W
    assert ret_w == ml_w == proj.shape[3]
    log_g = jnp.log1p(-jnp.exp2(-5.0 - jnp.arange(RET_HEADS, dtype=F32)))
    idx = jnp.arange(CHUNK, dtype=F32)
    diff = idx[:, None] - idx[None, :]
    dmask = jnp.where(diff >= 0, jnp.exp(log_g[:, None, None] * jnp.maximum(diff, 0.0)), 0.0)
    w_k = jnp.exp(log_g[:, None] * (CHUNK - 1.0 - idx)[None, :])
    w_q = jnp.exp(log_g[:, None] * (idx + 1.0)[None, :])
    w_k = jnp.broadcast_to(w_k[..., None], (RET_HEADS, CHUNK, HEAD_W))
    w_q = jnp.broadcast_to(w_q[..., None], (RET_HEADS, CHUNK, HEAD_W))
    g_l = jnp.exp(log_g * CHUNK)
    half = HEAD_W // 2
    inv = 1.0 / (ROPE_BASE ** jnp.linspace(0.0, 1.0, half, dtype=F32))
    ang = jnp.arange(t).astype(F32)[:, None] * inv[None, :]
    cos, sin = jnp.cos(ang), jnp.sin(ang)
    r = jnp.arange((CONV_W - 1) * CHUNK)
    src = CHUNK + r % CHUNK - (r // CHUNK + 1)
    shift = (jnp.arange(2 * CHUNK)[None, :] == src[:, None]).astype(BF16)

    def group(gi):
        return pl.BlockSpec((None, seqs, step, ret_w), lambda bi, c: (gi, bi, c, 0))

    def prev_group(gi):
        return pl.BlockSpec((None, seqs, CHUNK, ml_w),
                            lambda bi, c: (gi, bi, jnp.maximum(c * MIXER_CHUNKS - 1, 0), 0))

    def whole(a):
        return pl.BlockSpec(a.shape, lambda bi, c: (0,) * a.ndim)

    gb = gate_b.reshape(2 * ML_HEADS, 1)
    hg = head_g.reshape(1, ret_w + ml_w)
    return pl.pallas_call(
        _mixer_body,
        grid=(b // seqs, nc),
        in_specs=[
            pl.BlockSpec(memory_space=pltpu.SMEM),
            group(0), group(1), group(2), group(3),
            group(4), prev_group(4), group(5), prev_group(5), group(6), group(7),
            pl.BlockSpec((step, half), lambda bi, c: (c, 0)),
            pl.BlockSpec((step, half), lambda bi, c: (c, 0)),
            whole(dmask), whole(w_q), whole(w_k),
            pl.BlockSpec((seqs, 2 * ML_HEADS, step), lambda bi, c: (bi, 0, c)),
            whole(gb), whole(conv_w), whole(shift), whole(hg),
        ],
        out_specs=pl.BlockSpec((seqs, step, ret_w + ml_w), lambda bi, c: (bi, c, 0)),
        out_shape=jax.ShapeDtypeStruct((b, t, ret_w + ml_w), BF16),
        scratch_shapes=[pltpu.VMEM((seqs * RET_HEADS, HEAD_W, HEAD_W), F32),
                        pltpu.VMEM((seqs * ML_HEADS, HEAD_W, HEAD_W), F32),
                        pltpu.VMEM((seqs * ML_HEADS, 1, HEAD_W), F32),
                        pltpu.VMEM((seqs * ML_HEADS, 1, LANES), F32)],
        compiler_params=_params(),
        name="mixer",
    )(g_l, proj, proj, proj, proj, proj, proj, proj, proj, proj, proj, cos, sin, dmask, w_q, w_k,
      gates_row, gb, conv_w, shift, hg)


PAIR_W = 2 * HEAD_DIM
GROUP = ATT_HEADS // KV_HEADS
PAIRS = GROUP // 2


def _swa_in_proj_body(x_ref, g_ref, w_ref, q_ref, kv_ref, *, row_block):
    qw = ATT_HEADS * HEAD_DIM
    per_block = row_block // ATT_BLOCK
    for r in range(x_ref.shape[0] // row_block):
        rows = slice(r * row_block, (r + 1) * row_block)
        h = _rms_rows(x_ref[rows, :], g_ref[...]).astype(BF16)
        res = jnp.dot(h, w_ref[...], preferred_element_type=F32)
        for nbl in range(per_block):
            sub = slice(nbl * ATT_BLOCK, (nbl + 1) * ATT_BLOCK)
            for p in range(q_ref.shape[2]):
                q_ref[0, r * per_block + nbl, p] = res[sub, p * PAIR_W:(p + 1) * PAIR_W].astype(q_ref.dtype)
        kv_ref[rows, :] = res[:, qw:].astype(kv_ref.dtype)


def _swa_in_proj(x, g, w, b, t, tm, row_block):
    n, d = x.shape
    cols = w.shape[1]
    kvw = cols - ATT_HEADS * HEAD_DIM
    tiles_per_seq = t // tm
    return pl.pallas_call(
        functools.partial(_swa_in_proj_body, row_block=row_block),
        grid=(n // tm,),
        in_specs=[
            pl.BlockSpec((tm, d), lambda i: (i, 0)),
            pl.BlockSpec((1, d), lambda i: (0, 0)),
            pl.BlockSpec((d, cols), lambda i: (0, 0)),
        ],
        out_specs=[
            pl.BlockSpec((1, tm // ATT_BLOCK, ATT_HEADS // 2, ATT_BLOCK, PAIR_W),
                         lambda i: (i // tiles_per_seq, i % tiles_per_seq, 0, 0, 0)),
            pl.BlockSpec((tm, kvw), lambda i: (i, 0)),
        ],
        out_shape=[jax.ShapeDtypeStruct((b, t // ATT_BLOCK, ATT_HEADS // 2, ATT_BLOCK, PAIR_W), BF16),
                   jax.ShapeDtypeStruct((n, kvw), BF16)],
        compiler_params=_params(),
        name="swa_in_proj",
    )(x, g.reshape(1, d), w)


def _swa_body(rb_ref, sink_ref, idx_ref, q_ref, kp_ref, kc_ref, vp_ref, vc_ref, o_ref, bias_ref, sinkcol_ref):
    bi = pl.program_id(0)
    nb = pl.program_id(1)
    blk = ATT_BLOCK
    rows = PAIRS * blk

    @pl.when((bi == 0) & (nb == 0))
    def _():
        idx = idx_ref[...]
        col = lax.broadcasted_iota(jnp.int32, idx.shape, 1)

        def head_body(hd, carry):
            def bucket_body(bk, acc):
                return jnp.where(idx == bk, rb_ref[bk * ATT_HEADS + hd], acc)

            tbl = lax.fori_loop(0, N_BUCKETS, bucket_body, jnp.full(idx.shape, -jnp.inf, F32))
            kv = hd // GROUP
            half = hd % 2
            at = pl.ds(pl.multiple_of(((hd % GROUP) // 2) * blk, blk), blk)
            bias_ref[0, kv, half, at, :] = tbl
            bias_ref[1, kv, half, at, :] = jnp.where(col >= blk, tbl, -jnp.inf)
            sinkcol_ref[kv, half, at, :] = jnp.full((blk, LANES), sink_ref[hd], F32)
            return carry

        lax.fori_loop(0, ATT_HEADS, head_body, 0)

    low = lax.broadcasted_iota(jnp.int32, (2 * blk, LANES), 1) < HEAD_DIM
    low_out = lax.broadcasted_iota(jnp.int32, (rows, LANES), 1) < HEAD_DIM
    scale = HEAD_DIM ** -0.5

    def halves(xf, kv):
        xc = xf[:, (kv // 2) * LANES:(kv // 2 + 1) * LANES]
        xr = pltpu.roll(xc, HEAD_DIM, axis=1)
        lo_src, hi_src = (xc, xr) if kv % 2 == 0 else (xr, xc)
        return (jnp.where(low, lo_src, 0.0).astype(BF16), jnp.where(low, 0.0, hi_src).astype(BF16))

    for sb in range(q_ref.shape[0]):
        here = slice(sb * blk, (sb + 1) * blk)
        before = slice((sb - 1) * blk, sb * blk)
        k_prev, v_prev = (kp_ref[0], vp_ref[0]) if sb == 0 else (kc_ref[0, before, :], vc_ref[0, before, :])
        kf = jnp.concatenate([k_prev, kc_ref[0, here, :]], axis=0).astype(F32)
        vf = jnp.concatenate([v_prev, vc_ref[0, here, :]], axis=0).astype(F32)
        first = (nb == 0).astype(jnp.int32) if sb == 0 else 0
        for kv in range(KV_HEADS):
            k_lo, k_hi = halves(kf, kv)
            v_lo, v_hi = halves(vf, kv)
            q4 = q_ref[sb, kv * PAIRS:(kv + 1) * PAIRS].reshape(rows, PAIR_W) * scale

            def probs(k_half, half):
                s = lax.dot_general(q4, k_half, (((1,), (1,)), ((), ())), preferred_element_type=F32)
                s = s + bias_ref[first, kv, half]
                sink = sinkcol_ref[kv, half]
                mx = jnp.maximum(jnp.max(s, axis=-1, keepdims=True), sink)
                p = jnp.exp(s - jnp.concatenate([mx, mx], axis=1))
                den = jnp.sum(p, axis=-1, keepdims=True) + jnp.exp(sink - mx)
                return p.astype(BF16), den

            p_lo, den_lo = probs(k_lo, 0)
            p_hi, den_hi = probs(k_hi, 1)
            o4 = (jnp.dot(p_lo, v_lo, preferred_element_type=F32)
                  + jnp.dot(p_hi, v_hi, preferred_element_type=F32)) / jnp.where(low_out, den_lo, den_hi)
            for pr in range(PAIRS):
                at = (kv * PAIRS + pr) * PAIR_W
                o_ref[0, here, at:at + PAIR_W] = o4[pr * blk:(pr + 1) * blk].astype(o_ref.dtype)


def _t5_bucket(dist):
    n = jnp.maximum(dist, 0)
    max_exact = N_BUCKETS // 2
    nf = jnp.maximum(n, 1).astype(F32)
    large = max_exact + (jnp.log(nf / max_exact) / math.log(MAX_DIST / max_exact)
                         * (N_BUCKETS - max_exact)).astype(jnp.int32)
    large = jnp.minimum(large, N_BUCKETS - 1)
    return jnp.where(n < max_exact, n, large)


def _swa(q, kv, sinks, rel_bias):
    b, nblk, n_pairs, blk, _ = q.shape
    t = nblk * blk
    qw = ATT_HEADS * HEAD_DIM
    kvw = KV_HEADS * HEAD_DIM
    i = jnp.arange(blk)
    j = jnp.arange(2 * blk)
    dist = (blk + i)[:, None] - j[None, :]
    idx = jnp.where((dist >= 0) & (dist < WINDOW), _t5_bucket(dist), -1).astype(jnp.int32)
    sb = SWA_BLOCKS
    return pl.pallas_call(
        _swa_body,
        grid=(b, nblk // sb),
        in_specs=[
            pl.BlockSpec(memory_space=pltpu.SMEM),
            pl.BlockSpec(memory_space=pltpu.SMEM),
            pl.BlockSpec((blk, 2 * blk), lambda bi, nb: (0, 0)),
            pl.BlockSpec((None, sb, n_pairs, blk, PAIR_W), lambda bi, nb: (bi, nb, 0, 0, 0)),
            pl.BlockSpec((1, blk, kvw), lambda bi, nb: (bi, jnp.maximum(nb * sb - 1, 0), 0)),
            pl.BlockSpec((1, sb * blk, kvw), lambda bi, nb: (bi, nb, 0)),
            pl.BlockSpec((1, blk, kvw), lambda bi, nb: (bi, jnp.maximum(nb * sb - 1, 0), 1)),
            pl.BlockSpec((1, sb * blk, kvw), lambda bi, nb: (bi, nb, 1)),
        ],
        out_specs=pl.BlockSpec((1, sb * blk, qw), lambda bi, nb: (bi, nb, 0)),
        out_shape=jax.ShapeDtypeStruct((b, t, qw), BF16),
        scratch_shapes=[pltpu.VMEM((2, KV_HEADS, 2, PAIRS * blk, 2 * blk), F32),
                        pltpu.VMEM((KV_HEADS, 2, PAIRS * blk, LANES), F32)],
        compiler_params=_params(),
        name="swa",
    )(rel_bias.reshape(-1), sinks, idx, q, kv, kv, kv, kv)


def kernel(x, rel_bias, norm_g, ffn_w_gu, ffn_w_down, rm_w_in, ml_conv_w, ml_gate_b, rm_head_g,
           rm_w_out, swa_w_in, swa_sinks, swa_w_out):
    b, t, d = x.shape
    n = b * t
    depth = norm_g.shape[0]
    main_cols = 4 * RET_HEADS * HEAD_W + 4 * ML_HEADS * HEAD_W
    xs = x.reshape(n, d)
    row_tiles = n // ROW_TILE
    proj_tiles = main_cols // PROJ_TN

    def row_bands(src, layer):
        rows = src.shape[1] // (row_tiles * proj_tiles)
        return src, layer, (rows, src.shape[2]), lambda i, j: (i * proj_tiles + j, 0)

    def blocks(src, layer):
        return src, layer, (src.shape[1] // row_tiles, src.shape[2] // proj_tiles), lambda i, j: (i, j)

    ready = {}

    def bf16_weight(name, src, layer):
        if (name, layer) in ready:
            return ready.pop((name, layer))
        return src[layer].astype(BF16)

    for layer in range(depth):
        g = norm_g[layer]
        if layer % 2 == 0:
            e = layer // 2
            w_in = rm_w_in[e]
            gate_cols = w_in.shape[1] - main_cols
            w_gate = jnp.pad(w_in[:, main_cols:], ((0, 0), (0, LANES - gate_cols))).astype(BF16)
            names = [("gu", layer), ("down", layer), ("rm_out", e)]
            casts = [row_bands(ffn_w_gu, layer), blocks(ffn_w_down, layer), row_bands(rm_w_out, e)]
            if layer + 1 < depth:
                o = (layer + 1) // 2
                names += [("gu", layer + 1), ("down", layer + 1), ("swa_in", o), ("swa_out", o)]
                casts += [row_bands(ffn_w_gu, layer + 1), blocks(ffn_w_down, layer + 1),
                          row_bands(swa_w_in, o), row_bands(swa_w_out, o)]
            proj, gates, *cast = _norm_matmul(xs, g[0], w_in.astype(BF16), w_gate, proj_tiles, ROW_TILE, PROJ_TN, casts)
            ready.update(zip(names, cast))
            gates_row = gates[:, :gate_cols].reshape(b, t, gate_cols).transpose(0, 2, 1)
            mixed = _mixer(proj.reshape(main_cols // PROJ_TN, b, t, PROJ_TN), gates_row, ml_gate_b[e], ml_conv_w[e],
                           rm_head_g[e])
            xs = _proj_norm_res(mixed.reshape(n, -1), bf16_weight("rm_out", rm_w_out, e), xs, g[1], OUT_ROW_TILE)
        else:
            o = layer // 2
            q, kv = _swa_in_proj(xs, g[0], bf16_weight("swa_in", swa_w_in, o), b, t, ROW_TILE, PROJ_ROW_BLOCK)
            att = _swa(q, kv.reshape(b, t, -1), swa_sinks[o], rel_bias)
            xs = _proj_norm_res(att.reshape(n, -1), bf16_weight("swa_out", swa_w_out, o), xs, g[1], OUT_ROW_TILE)
        xs = _ffn(xs, g[2], bf16_weight("gu", ffn_w_gu, layer), bf16_weight("down", ffn_w_down, layer), g[3],
                  ROW_TILE, FFN_TF, FFN_ROW_BLOCK)
    return xs.reshape(b, t, d)
```

```python
import functools
import math

import jax
import jax.numpy as jnp
from jax import lax
from jax.experimental import pallas as pl
from jax.experimental.pallas import tpu as pltpu

F32 = jnp.float32
BF16 = jnp.bfloat16

EPS = 1e-6
CHUNK = 128
RET_HEADS = 4
ML_HEADS = 4
HEAD_W = 256
CONV_W = 4
ROPE_BASE = 10000.0
ATT_HEADS = 32
KV_HEADS = 4
HEAD_DIM = 64
WINDOW = 128
ATT_BLOCK = 128
N_BUCKETS = 32
MAX_DIST = 128
LANES = 128
SUBLANES = 8
VMEM_LIMIT = 56 * 1024 * 1024

ROW_TILE = 1024
OUT_ROW_TILE = 512
PROJ_TN = 1024
PROJ_ROW_BLOCK = 256
MIXER_CHUNKS = 2
MIXER_SEQS = 2
SWA_BLOCKS = 4
FFN_TF = 512
FFN_ROW_BLOCK = 512


def _params(vmem=VMEM_LIMIT):
    return pltpu.CompilerParams(vmem_limit_bytes=vmem)


def _rms_rows(x, g):
    ms = jnp.mean(x * x, axis=-1, keepdims=True)
    return x * lax.rsqrt(ms + EPS) * g


def _silu(x):
    return x * jax.nn.sigmoid(x)


def _cast_specs(casts):
    in_specs = [pl.BlockSpec((None, *blk), (lambda *ids, layer=layer, idx=idx: (layer, *idx(*ids))))
                for _, layer, blk, idx in casts]
    out_specs = [pl.BlockSpec(blk, idx) for _, _, blk, idx in casts]
    out_shape = [jax.ShapeDtypeStruct(src.shape[1:], BF16) for src, _, _, _ in casts]
    return in_specs, out_specs, out_shape


def _cast_pieces(src_refs, dst_refs):
    for src_ref, dst_ref in zip(src_refs, dst_refs):
        dst_ref[...] = src_ref[...].astype(dst_ref.dtype)


def _norm_matmul_body(x_ref, g_ref, w_ref, ws_ref, *rest, row_block, n_cast):
    cast_src, (o_ref, os_ref), cast_dst, h_ref = rest[:n_cast], rest[n_cast:n_cast + 2], rest[n_cast + 2:-1], rest[-1]
    j = pl.program_id(1)

    @pl.when(j == 0)
    def _():
        _cast_pieces(cast_src, cast_dst)
        for r in range(x_ref.shape[0] // row_block):
            rows = slice(r * row_block, (r + 1) * row_block)
            h = _rms_rows(x_ref[rows, :], g_ref[...]).astype(h_ref.dtype)
            h_ref[rows, :] = h
            os_ref[rows, :] = jnp.dot(h, ws_ref[...], preferred_element_type=F32)
            o_ref[rows, :] = jnp.dot(h, w_ref[...], preferred_element_type=F32).astype(o_ref.dtype)

    @pl.when(j > 0)
    def _():
        _cast_pieces(cast_src, cast_dst)
        o_ref[...] = jnp.dot(h_ref[...], w_ref[...], preferred_element_type=F32).astype(o_ref.dtype)


def _norm_matmul(x, g, w, w_side, tiles, tm, tn, casts):
    n, d = x.shape
    ns = w_side.shape[1]
    cast_in, cast_out, cast_shape = _cast_specs(casts)
    return pl.pallas_call(
        functools.partial(_norm_matmul_body, row_block=PROJ_ROW_BLOCK, n_cast=len(casts)),
        grid=(n // tm, tiles),
        in_specs=[
            pl.BlockSpec((tm, d), lambda i, j: (i, 0)),
            pl.BlockSpec((1, d), lambda i, j: (0, 0)),
            pl.BlockSpec((d, tn), lambda i, j: (0, j)),
            pl.BlockSpec((d, ns), lambda i, j: (0, 0)),
            *cast_in,
        ],
        out_specs=[pl.BlockSpec((None, tm, tn), lambda i, j: (j, i, 0)),
                   pl.BlockSpec((tm, ns), lambda i, j: (i, 0)),
                   *cast_out],
        out_shape=[jax.ShapeDtypeStruct((tiles, n, tn), BF16), jax.ShapeDtypeStruct((n, ns), F32), *cast_shape],
        scratch_shapes=[pltpu.VMEM((tm, d), BF16)],
        compiler_params=_params(),
        name="norm_matmul",
    )(x, g.reshape(1, d), w, w_side, *[c[0] for c in casts])


def _proj_norm_res_body(a_ref, w_ref, x_ref, g_ref, o_ref):
    y = jnp.dot(a_ref[...], w_ref[...], preferred_element_type=F32)
    o_ref[...] = x_ref[...] + _rms_rows(y, g_ref[...])


def _proj_norm_res(a, w, x, g, tm):
    n, d = x.shape
    return pl.pallas_call(
        _proj_norm_res_body,
        grid=(n // tm,),
        in_specs=[
            pl.BlockSpec((tm, a.shape[1]), lambda i: (i, 0)),
            pl.BlockSpec(w.shape, lambda i: (0, 0)),
            pl.BlockSpec((tm, d), lambda i: (i, 0)),
            pl.BlockSpec((1, d), lambda i: (0, 0)),
        ],
        out_specs=pl.BlockSpec((tm, d), lambda i: (i, 0)),
        out_shape=jax.ShapeDtypeStruct((n, d), F32),
        compiler_params=_params(),
        name="proj_norm_res",
    )(a, w, x, g.reshape(1, d))


def _ffn_body(x_ref, g_in_ref, wg_ref, wu_ref, wd_ref, g_out_ref, o_ref, h_ref, *, row_block):
    f = pl.program_id(1)
    last = pl.num_programs(1) - 1
    blocks = [slice(r * row_block, (r + 1) * row_block) for r in range(x_ref.shape[0] // row_block)]

    def partial_down(rows):
        h = h_ref[rows, :]
        gate = jnp.dot(h, wg_ref[...], preferred_element_type=F32)
        up = jnp.dot(h, wu_ref[...], preferred_element_type=F32)
        act = (_silu(gate) * up).astype(BF16)
        return jnp.dot(act, wd_ref[...], preferred_element_type=F32)

    @pl.when(f == 0)
    def _():
        for rows in blocks:
            h_ref[rows, :] = _rms_rows(x_ref[rows, :], g_in_ref[...]).astype(h_ref.dtype)
            o_ref[rows, :] = partial_down(rows)

    @pl.when((f > 0) & (f < last))
    def _():
        for rows in blocks:
            o_ref[rows, :] += partial_down(rows)

    @pl.when(f == last)
    def _():
        for rows in blocks:
            y = o_ref[rows, :] + partial_down(rows)
            o_ref[rows, :] = x_ref[rows, :] + _rms_rows(y, g_out_ref[...])


def _ffn(x, g_in, w_gu, w_down, g_out, tm, tf, row_block):
    n, d = x.shape
    nf = w_down.shape[0] // tf
    return pl.pallas_call(
        functools.partial(_ffn_body, row_block=row_block),
        grid=(n // tm, nf),
        in_specs=[
            pl.BlockSpec((tm, d), lambda i, f: (i, 0)),
            pl.BlockSpec((1, d), lambda i, f: (0, 0)),
            pl.BlockSpec((d, tf), lambda i, f: (0, f)),
            pl.BlockSpec((d, tf), lambda i, f: (0, f + nf)),
            pl.BlockSpec((tf, d), lambda i, f: (f, 0)),
            pl.BlockSpec((1, d), lambda i, f: (0, 0)),
        ],
        out_specs=pl.BlockSpec((tm, d), lambda i, f: (i, 0)),
        out_shape=jax.ShapeDtypeStruct((n, d), F32),
        scratch_shapes=[pltpu.VMEM((tm, d), BF16)],
        compiler_params=_params(),
        name="ffn",
    )(x, g_in.reshape(1, d), w_gu, w_gu, w_down, g_out.reshape(1, d))


def _log_sigmoid(x):
    return jnp.minimum(x, 0.0) - jnp.log1p(jnp.exp(-jnp.abs(x)))


_NT = (((1,), (1,)), ((), ()))
_TN = (((0,), (0,)), ((), ()))


def _retention_head(h, bb, rows, q_ref, k_ref, v_ref, g_ref, cos, sin, dm_ref, wq_ref, wk_ref, gl_ref, hg_ref,
                    o_ref, state_ref):
    cols = slice(h * HEAD_W, (h + 1) * HEAD_W)
    half = HEAD_W // 2

    def rot(x):
        x = x.astype(F32)
        x1, x2 = x[:, :half], x[:, half:]
        return jnp.concatenate([x1 * cos - x2 * sin, x2 * cos + x1 * sin], axis=-1)

    q = rot(q_ref[bb, rows, cols])
    k = rot(k_ref[bb, rows, cols]) * (HEAD_W ** -0.5)
    v = v_ref[bb, rows, cols]
    qb = q.astype(BF16)
    s = lax.dot_general(qb, k.astype(BF16), _NT, preferred_element_type=F32) * dm_ref[h]
    intra = jnp.dot(s.astype(BF16), v, preferred_element_type=F32)
    si = bb * RET_HEADS + h
    state = state_ref[si]
    inter = jnp.dot(qb, state.astype(BF16), preferred_element_type=F32) * wq_ref[h]
    out = intra + inter
    kw = (k * wk_ref[h]).astype(BF16)
    state_ref[si] = state * gl_ref[h] + lax.dot_general(kw, v, _TN, preferred_element_type=F32)
    y = _rms_rows(out, hg_ref[:, cols])
    o_ref[bb, rows, cols] = (y * _silu(g_ref[bb, rows, cols].astype(F32))).astype(o_ref.dtype)


def _causal_conv(cur, prev, shift_ref, w):
    shifted = jnp.dot(shift_ref[...], jnp.concatenate([prev, cur], axis=0), preferred_element_type=F32)
    y = cur.astype(F32) * w[CONV_W - 1:CONV_W]
    for s in range(1, CONV_W):
        y = y + shifted[(s - 1) * CHUNK:s * CHUNK] * w[CONV_W - 1 - s:CONV_W - s]
    return y


def _mlstm_head(h, bb, rows, q_all, k_all, v_ref, og_ref, gx, b_rows, lower_tri, hg_ref, o_ref, c_ref, n_ref,
                m_ref):
    cols = slice(h * HEAD_W, (h + 1) * HEAD_W)
    q = q_all[:, cols]
    k = k_all[:, cols]
    v = v_ref[bb, rows, cols]
    si = bb * ML_HEADS + h

    i_row = gx[h:h + 1]
    lf_row = gx[ML_HEADS + h:ML_HEADS + h + 1]
    b_row = b_rows[ML_HEADS + h:ML_HEADS + h + 1]
    b_cols = lax.dot_general(lower_tri, jnp.broadcast_to(lf_row, (CHUNK, CHUNK)), _NT,
                             precision=lax.Precision.HIGHEST, preferred_element_type=F32)
    i_cols = jnp.broadcast_to(i_row, (CHUNK, CHUNK)).T
    rowi = lax.broadcasted_iota(jnp.int32, (CHUNK, CHUNK), 0)
    coli = lax.broadcasted_iota(jnp.int32, (CHUNK, CHUNK), 1)
    log_d = jnp.where(rowi >= coli, b_cols - b_row + i_row, -jnp.inf)
    m_prev = m_ref[si]
    inter_log = b_cols + m_prev
    m_t = jnp.maximum(inter_log, jnp.max(log_d, axis=-1, keepdims=True))
    d_mat = jnp.exp(log_d - m_t)
    w_inter = jnp.exp(inter_log - m_t)

    def wide(a):
        return jnp.concatenate([a] * (HEAD_W // LANES), axis=1)

    qb = q.astype(BF16)
    s = lax.dot_general(qb, k.astype(BF16), _NT, preferred_element_type=F32) * d_mat
    c_state = c_ref[si]
    n_state = n_ref[si]
    num = (jnp.dot(s.astype(BF16), v, preferred_element_type=F32)
           + wide(w_inter) * jnp.dot(qb, c_state.astype(BF16), preferred_element_type=F32))
    den = (jnp.sum(s, axis=-1, keepdims=True)
           + w_inter * jnp.sum(q * n_state, axis=-1, keepdims=True))
    hid = num / wide(jnp.maximum(jnp.abs(den), jnp.exp(-m_t)))

    b_last = b_cols[CHUNK - 1:CHUNK]
    log_w = b_last - b_cols + i_cols
    m_new = jnp.maximum(b_last + m_prev, jnp.max(log_w, axis=0, keepdims=True))
    w = jnp.exp(log_w - m_new)
    decay = wide(jnp.exp(b_last + m_prev - m_new))
    kw = k * wide(w)
    c_ref[si] = decay * c_state + lax.dot_general(kw.astype(BF16), v, _TN, preferred_element_type=F32)
    n_ref[si] = decay * n_state + jnp.sum(kw, axis=0, keepdims=True)
    m_ref[si] = m_new

    out_cols = slice(RET_HEADS * HEAD_W + h * HEAD_W, RET_HEADS * HEAD_W + (h + 1) * HEAD_W)
    y = _rms_rows(hid, hg_ref[:, out_cols])
    o_ref[bb, rows, out_cols] = (y * jax.nn.sigmoid(og_ref[bb, rows, cols].astype(F32))).astype(o_ref.dtype)


def _mixer_body(gl_ref, rq_ref, rk_ref, rv_ref, rg_ref, mq_ref, mqp_ref, mk_ref, mkp_ref, mv_ref, mo_ref,
                cos_ref, sin_ref, dm_ref, wq_ref, wk_ref, gr_ref, gb_ref, cw_ref, shift_ref, hg_ref,
                o_ref, r_ref, c_ref, n_ref, m_ref):
    first = pl.program_id(1) == 0

    @pl.when(first)
    def _():
        r_ref[...] = jnp.zeros_like(r_ref)
        c_ref[...] = jnp.zeros_like(c_ref)
        n_ref[...] = jnp.zeros_like(n_ref)
        m_ref[...] = jnp.zeros_like(m_ref)

    ml_w = ML_HEADS * HEAD_W
    rowi = lax.broadcasted_iota(jnp.int32, (CHUNK, CHUNK), 0)
    coli = lax.broadcasted_iota(jnp.int32, (CHUNK, CHUNK), 1)
    upper_tri = jnp.where(rowi <= coli, 1.0, 0.0).astype(F32)
    lower_tri = jnp.where(rowi >= coli, 1.0, 0.0).astype(F32)
    is_input = lax.broadcasted_iota(jnp.int32, (2 * ML_HEADS, CHUNK), 0) < ML_HEADS

    for bb in range(mq_ref.shape[0]):
        for ci in range(mq_ref.shape[1] // CHUNK):
            rows = slice(ci * CHUNK, (ci + 1) * CHUNK)
            cos = cos_ref[rows, :]
            sin = sin_ref[rows, :]
            for h in range(RET_HEADS):
                _retention_head(h, bb, rows, rq_ref, rk_ref, rv_ref, rg_ref, cos, sin, dm_ref, wq_ref, wk_ref,
                                gl_ref, hg_ref, o_ref, r_ref)

            if ci == 0:
                q_prev = jnp.where(first, jnp.zeros_like(mqp_ref[bb]), mqp_ref[bb])
                k_prev = jnp.where(first, jnp.zeros_like(mkp_ref[bb]), mkp_ref[bb])
            else:
                before = slice((ci - 1) * CHUNK, ci * CHUNK)
                q_prev, k_prev = mq_ref[bb, before, :], mk_ref[bb, before, :]
            q_all = _silu(_causal_conv(mq_ref[bb, rows, :], q_prev, shift_ref, cw_ref[:, :ml_w]))
            k_all = _silu(_causal_conv(mk_ref[bb, rows, :], k_prev, shift_ref, cw_ref[:, ml_w:])) * (HEAD_W ** -0.5)
            gates = gr_ref[bb, :, rows] + gb_ref[...]
            gx = jnp.where(is_input, gates, _log_sigmoid(gates))
            b_rows = jnp.dot(gx, upper_tri, precision=lax.Precision.HIGHEST, preferred_element_type=F32)
            for h in range(ML_HEADS):
                _mlstm_head(h, bb, rows, q_all, k_all, mv_ref, mo_ref, gx, b_rows, lower_tri, hg_ref, o_ref,
                            c_ref, n_ref, m_ref)


def _mixer(proj, gates_row, gate_b, conv_w, head_g):
    _, b, t, _ = proj.shape
    seqs = MIXER_SEQS
    step = MIXER_CHUNKS * CHUNK
    nc = t // step
    ret_w = RET_HEADS * HEAD_W
    ml_w = ML_HEADS * HEAD_W
    assert ret_w == ml_w == proj.shape[3]
    log_g = jnp.log1p(-jnp.exp2(-5.0 - jnp.arange(RET_HEADS, dtype=F32)))
    idx = jnp.arange(CHUNK, dtype=F32)
    diff = idx[:, None] - idx[None, :]
    dmask = jnp.where(diff >= 0, jnp.exp(log_g[:, None, None] * jnp.maximum(diff, 0.0)), 0.0)
    w_k = jnp.exp(log_g[:, None] * (CHUNK - 1.0 - idx)[None, :])
    w_q = jnp.exp(log_g[:, None] * (idx + 1.0)[None, :])
    w_k = jnp.broadcast_to(w_k[..., None], (RET_HEADS, CHUNK, HEAD_W))
    w_q = jnp.broadcast_to(w_q[..., None], (RET_HEADS, CHUNK, HEAD_W))
    g_l = jnp.exp(log_g * CHUNK)
    half = HEAD_W // 2
    inv = 1.0 / (ROPE_BASE ** jnp.linspace(0.0, 1.0, half, dtype=F32))
    ang = jnp.arange(t).astype(F32)[:, None] * inv[None, :]
    cos, sin = jnp.cos(ang), jnp.sin(ang)
    r = jnp.arange((CONV_W - 1) * CHUNK)
    src = CHUNK + r % CHUNK - (r // CHUNK + 1)
    shift = (jnp.arange(2 * CHUNK)[None, :] == src[:, None]).astype(BF16)

    def group(gi):
        return pl.BlockSpec((None, seqs, step, ret_w), lambda bi, c: (gi, bi, c, 0))

    def prev_group(gi):
        return pl.BlockSpec((None, seqs, CHUNK, ml_w),
                            lambda bi, c: (gi, bi, jnp.maximum(c * MIXER_CHUNKS - 1, 0), 0))

    def whole(a):
        return pl.BlockSpec(a.shape, lambda bi, c: (0,) * a.ndim)

    gb = gate_b.reshape(2 * ML_HEADS, 1)
    hg = head_g.reshape(1, ret_w + ml_w)
    return pl.pallas_call(
        _mixer_body,
        grid=(b // seqs, nc),
        in_specs=[
            pl.BlockSpec(memory_space=pltpu.SMEM),
            group(0), group(1), group(2), group(3),
            group(4), prev_group(4), group(5), prev_group(5), group(6), group(7),
            pl.BlockSpec((step, half), lambda bi, c: (c, 0)),
            pl.BlockSpec((step, half), lambda bi, c: (c, 0)),
            whole(dmask), whole(w_q), whole(w_k),
            pl.BlockSpec((seqs, 2 * ML_HEADS, step), lambda bi, c: (bi, 0, c)),
            whole(gb), whole(conv_w), whole(shift), whole(hg),
        ],
        out_specs=pl.BlockSpec((seqs, step, ret_w + ml_w), lambda bi, c: (bi, c, 0)),
        out_shape=jax.ShapeDtypeStruct((b, t, ret_w + ml_w), BF16),
        scratch_shapes=[pltpu.VMEM((seqs * RET_HEADS, HEAD_W, HEAD_W), F32),
                        pltpu.VMEM((seqs * ML_HEADS, HEAD_W, HEAD_W), F32),
                        pltpu.VMEM((seqs * ML_HEADS, 1, HEAD_W), F32),
                        pltpu.VMEM((seqs * ML_HEADS, 1, LANES), F32)],
        compiler_params=_params(),
        name="mixer",
    )(g_l, proj, proj, proj, proj, proj, proj, proj, proj, proj, proj, cos, sin, dmask, w_q, w_k,
      gates_row, gb, conv_w, shift, hg)


PAIR_W = 2 * HEAD_DIM
GROUP = ATT_HEADS // KV_HEADS
PAIRS = GROUP // 2


def _swa_in_proj_body(x_ref, g_ref, w_ref, q_ref, kv_ref, *, row_block):
    qw = ATT_HEADS * HEAD_DIM
    per_block = row_block // ATT_BLOCK
    for r in range(x_ref.shape[0] // row_block):
        rows = slice(r * row_block, (r + 1) * row_block)
        h = _rms_rows(x_ref[rows, :], g_ref[...]).astype(BF16)
        res = jnp.dot(h, w_ref[...], preferred_element_type=F32)
        for nbl in range(per_block):
            sub = slice(nbl * ATT_BLOCK, (nbl + 1) * ATT_BLOCK)
            for p in range(q_ref.shape[2]):
                q_ref[0, r * per_block + nbl, p] = res[sub, p * PAIR_W:(p + 1) * PAIR_W].astype(q_ref.dtype)
        kv_ref[rows, :] = res[:, qw:].astype(kv_ref.dtype)


def _swa_in_proj(x, g, w, b, t, tm, row_block):
    n, d = x.shape
    cols = w.shape[1]
    kvw = cols - ATT_HEADS * HEAD_DIM
    tiles_per_seq = t // tm
    return pl.pallas_call(
        functools.partial(_swa_in_proj_body, row_block=row_block),
        grid=(n // tm,),
        in_specs=[
            pl.BlockSpec((tm, d), lambda i: (i, 0)),
            pl.BlockSpec((1, d), lambda i: (0, 0)),
            pl.BlockSpec((d, cols), lambda i: (0, 0)),
        ],
        out_specs=[
            pl.BlockSpec((1, tm // ATT_BLOCK, ATT_HEADS // 2, ATT_BLOCK, PAIR_W),
                         lambda i: (i // tiles_per_seq, i % tiles_per_seq, 0, 0, 0)),
            pl.BlockSpec((tm, kvw), lambda i: (i, 0)),
        ],
        out_shape=[jax.ShapeDtypeStruct((b, t // ATT_BLOCK, ATT_HEADS // 2, ATT_BLOCK, PAIR_W), BF16),
                   jax.ShapeDtypeStruct((n, kvw), BF16)],
        compiler_params=_params(),
        name="swa_in_proj",
    )(x, g.reshape(1, d), w)


def _swa_body(rb_ref, sink_ref, idx_ref, q_ref, kp_ref, kc_ref, vp_ref, vc_ref, o_ref, bias_ref, sinkcol_ref):
    bi = pl.program_id(0)
    nb = pl.program_id(1)
    blk = ATT_BLOCK
    rows = PAIRS * blk

    @pl.when((bi == 0) & (nb == 0))
    def _():
        idx = idx_ref[...]
        col = lax.broadcasted_iota(jnp.int32, idx.shape, 1)

        def head_body(hd, carry):
            def bucket_body(bk, acc):
                return jnp.where(idx == bk, rb_ref[bk * ATT_HEADS + hd], acc)

            tbl = lax.fori_loop(0, N_BUCKETS, bucket_body, jnp.full(idx.shape, -jnp.inf, F32))
            kv = hd // GROUP
            half = hd % 2
            at = pl.ds(pl.multiple_of(((hd % GROUP) // 2) * blk, blk), blk)
            bias_ref[0, kv, half, at, :] = tbl
            bias_ref[1, kv, half, at, :] = jnp.where(col >= blk, tbl, -jnp.inf)
            sinkcol_ref[kv, half, at, :] = jnp.full((blk, LANES), sink_ref[hd], F32)
            return carry

        lax.fori_loop(0, ATT_HEADS, head_body, 0)

    low = lax.broadcasted_iota(jnp.int32, (2 * blk, LANES), 1) < HEAD_DIM
    low_out = lax.broadcasted_iota(jnp.int32, (rows, LANES), 1) < HEAD_DIM
    scale = HEAD_DIM ** -0.5

    def halves(xf, kv):
        xc = xf[:, (kv // 2) * LANES:(kv // 2 + 1) * LANES]
        xr = pltpu.roll(xc, HEAD_DIM, axis=1)
        lo_src, hi_src = (xc, xr) if kv % 2 == 0 else (xr, xc)
        return (jnp.where(low, lo_src, 0.0).astype(BF16), jnp.where(low, 0.0, hi_src).astype(BF16))

    for sb in range(q_ref.shape[0]):
        here = slice(sb * blk, (sb + 1) * blk)
        before = slice((sb - 1) * blk, sb * blk)
        k_prev, v_prev = (kp_ref[0], vp_ref[0]) if sb == 0 else (kc_ref[0, before, :], vc_ref[0, before, :])
        kf = jnp.concatenate([k_prev, kc_ref[0, here, :]], axis=0).astype(F32)
        vf = jnp.concatenate([v_prev, vc_ref[0, here, :]], axis=0).astype(F32)
        first = (nb == 0).astype(jnp.int32) if sb == 0 else 0
        for kv in range(KV_HEADS):
            k_lo, k_hi = halves(kf, kv)
            v_lo, v_hi = halves(vf, kv)
            q4 = q_ref[sb, kv * PAIRS:(kv + 1) * PAIRS].reshape(rows, PAIR_W) * scale

            def probs(k_half, half):
                s = lax.dot_general(q4, k_half, (((1,), (1,)), ((), ())), preferred_element_type=F32)
                s = s + bias_ref[first, kv, half]
                sink = sinkcol_ref[kv, half]
                mx = jnp.maximum(jnp.max(s, axis=-1, keepdims=True), sink)
                p = jnp.exp(s - jnp.concatenate([mx, mx], axis=1))
                den = jnp.sum(p, axis=-1, keepdims=True) + jnp.exp(sink - mx)
                return p.astype(BF16), den

            p_lo, den_lo = probs(k_lo, 0)
            p_hi, den_hi = probs(k_hi, 1)
            o4 = (jnp.dot(p_lo, v_lo, preferred_element_type=F32)
                  + jnp.dot(p_hi, v_hi, preferred_element_type=F32)) / jnp.where(low_out, den_lo, den_hi)
            for pr in range(PAIRS):
                at = (kv * PAIRS + pr) * PAIR_W
                o_ref[0, here, at:at + PAIR_W] = o4[pr * blk:(pr + 1) * blk].astype(o_ref.dtype)


def _t5_bucket(dist):
    n = jnp.maximum(dist, 0)
    max_exact = N_BUCKETS // 2
    nf = jnp.maximum(n, 1).astype(F32)
    large = max_exact + (jnp.log(nf / max_exact) / math.log(MAX_DIST / max_exact)
                         * (N_BUCKETS - max_exact)).astype(jnp.int32)
    large = jnp.minimum(large, N_BUCKETS - 1)
    return jnp.where(n < max_exact, n, large)


def _swa(q, kv, sinks, rel_bias):
    b, nblk, n_pairs, blk, _ = q.shape
    t = nblk * blk
    qw = ATT_HEADS * HEAD_DIM
    kvw = KV_HEADS * HEAD_DIM
    i = jnp.arange(blk)
    j = jnp.arange(2 * blk)
    dist = (blk + i)[:, None] - j[None, :]
    idx = jnp.where((dist >= 0) & (dist < WINDOW), _t5_bucket(dist), -1).astype(jnp.int32)
    sb = SWA_BLOCKS
    return pl.pallas_call(
        _swa_body,
        grid=(b, nblk // sb),
        in_specs=[
            pl.BlockSpec(memory_space=pltpu.SMEM),
            pl.BlockSpec(memory_space=pltpu.SMEM),
            pl.BlockSpec((blk, 2 * blk), lambda bi, nb: (0, 0)),
            pl.BlockSpec((None, sb, n_pairs, blk, PAIR_W), lambda bi, nb: (bi, nb, 0, 0, 0)),
            pl.BlockSpec((1, blk, kvw), lambda bi, nb: (bi, jnp.maximum(nb * sb - 1, 0), 0)),
            pl.BlockSpec((1, sb * blk, kvw), lambda bi, nb: (bi, nb, 0)),
            pl.BlockSpec((1, blk, kvw), lambda bi, nb: (bi, jnp.maximum(nb * sb - 1, 0), 1)),
            pl.BlockSpec((1, sb * blk, kvw), lambda bi, nb: (bi, nb, 1)),
        ],
        out_specs=pl.BlockSpec((1, sb * blk, qw), lambda bi, nb: (bi, nb, 0)),
        out_shape=jax.ShapeDtypeStruct((b, t, qw), BF16),
        scratch_shapes=[pltpu.VMEM((2, KV_HEADS, 2, PAIRS * blk, 2 * blk), F32),
                        pltpu.VMEM((KV_HEADS, 2, PAIRS * blk, LANES), F32)],
        compiler_params=_params(),
        name="swa",
    )(rel_bias.reshape(-1), sinks, idx, q, kv, kv, kv, kv)


def kernel(x, rel_bias, norm_g, ffn_w_gu, ffn_w_down, rm_w_in, ml_conv_w, ml_gate_b, rm_head_g,
           rm_w_out, swa_w_in, swa_sinks, swa_w_out):
    b, t, d = x.shape
    n = b * t
    depth = norm_g.shape[0]
    main_cols = 4 * RET_HEADS * HEAD_W + 4 * ML_HEADS * HEAD_W
    xs = x.reshape(n, d)
    row_tiles = n // ROW_TILE
    proj_tiles = main_cols // PROJ_TN

    def row_bands(src, layer):
        rows = src.shape[1] // (row_tiles * proj_tiles)
        return src, layer, (rows, src.shape[2]), lambda i, j: (i * proj_tiles + j, 0)

    def blocks(src, layer):
        return src, layer, (src.shape[1] // row_tiles, src.shape[2] // proj_tiles), lambda i, j: (i, j)

    ready = {}

    def bf16_weight(name, src, layer):
        if (name, layer) in ready:
            return ready.pop((name, layer))
        return src[layer].astype(BF16)

    for layer in range(depth):
        g = norm_g[layer]
        if layer % 2 == 0:
            e = layer // 2
            w_in = rm_w_in[e]
            gate_cols = w_in.shape[1] - main_cols
            w_gate = jnp.pad(w_in[:, main_cols:], ((0, 0), (0, LANES - gate_cols))).astype(BF16)
            names = [("gu", layer), ("down", layer), ("rm_out", e)]
            casts = [row_bands(ffn_w_gu, layer), blocks(ffn_w_down, layer), row_bands(rm_w_out, e)]
            if layer + 1 < depth:
                o = (layer + 1) // 2
                names += [("gu", layer + 1), ("down", layer + 1), ("swa_in", o), ("swa_out", o)]
                casts += [row_bands(ffn_w_gu, layer + 1), blocks(ffn_w_down, layer + 1),
                          row_bands(swa_w_in, o), row_bands(swa_w_out, o)]
            proj, gates, *cast = _norm_matmul(xs, g[0], w_in.astype(BF16), w_gate, proj_tiles, ROW_TILE, PROJ_TN, casts)
            ready.update(zip(names, cast))
            gates_row = gates[:, :gate_cols].reshape(b, t, gate_cols).transpose(0, 2, 1)
            mixed = _mixer(proj.reshape(main_cols // PROJ_TN, b, t, PROJ_TN), gates_row, ml_gate_b[e], ml_conv_w[e],
                           rm_head_g[e])
            xs = _proj_norm_res(mixed.reshape(n, -1), bf16_weight("rm_out", rm_w_out, e), xs, g[1], OUT_ROW_TILE)
        else:
            o = layer // 2
            q, kv = _swa_in_proj(xs, g[0], bf16_weight("swa_in", swa_w_in, o), b, t, ROW_TILE, PROJ_ROW_BLOCK)
            att = _swa(q, kv.reshape(b, t, -1), swa_sinks[o], rel_bias)
            xs = _proj_norm_res(att.reshape(n, -1), bf16_weight("swa_out", swa_w_out, o), xs, g[1], OUT_ROW_TILE)
        xs = _ffn(xs, g[2], bf16_weight("gu", ffn_w_gu, layer), bf16_weight("down", ffn_w_down, layer), g[3],
                  ROW_TILE, FFN_TF, FFN_ROW_BLOCK)
    return xs.reshape(b, t, d)
```

```python
import functools
import math

import jax
import jax.numpy as jnp
from jax import lax
from jax.experimental import pallas as pl
from jax.experimental.pallas import tpu as pltpu

F32 = jnp.float32
BF16 = jnp.bfloat16

EPS = 1e-6
CHUNK = 128
RET_HEADS = 4
ML_HEADS = 4
HEAD_W = 256
CONV_W = 4
ROPE_BASE = 10000.0
ATT_HEADS = 32
KV_HEADS = 4
HEAD_DIM = 64
WINDOW = 128
ATT_BLOCK = 128
N_BUCKETS = 32
MAX_DIST = 128
LANES = 128
SUBLANES = 8
VMEM_LIMIT = 56 * 1024 * 1024

ROW_TILE = 1024
OUT_ROW_TILE = 512
PROJ_TN = 1024
PROJ_ROW_BLOCK = 256
MIXER_CHUNKS = 2
MIXER_SEQS = 2
SWA_BLOCKS = 4
FFN_TF = 512
FFN_ROW_BLOCK = 512


def _params(vmem=VMEM_LIMIT):
    return pltpu.CompilerParams(vmem_limit_bytes=vmem)


def _rms_rows(x, g):
    ms = jnp.mean(x * x, axis=-1, keepdims=True)
    return x * lax.rsqrt(ms + EPS) * g


def _silu(x):
    return x * jax.nn.sigmoid(x)


def _cast_specs(casts):
    in_specs, out_specs, out_shape = [], [], []
    for src, layer, blk, idx in casts:
        if layer is None:
            spec = pl.BlockSpec((src.shape[0], *blk), lambda *ids, idx=idx: (0, *idx(*ids)))
            in_specs.append(spec)
            out_specs.append(spec)
            out_shape.append(jax.ShapeDtypeStruct(src.shape, BF16))
        else:
            in_specs.append(pl.BlockSpec((None, *blk), lambda *ids, layer=layer, idx=idx: (layer, *idx(*ids))))
            out_specs.append(pl.BlockSpec(blk, idx))
            out_shape.append(jax.ShapeDtypeStruct(src.shape[1:], BF16))
    return in_specs, out_specs, out_shape


def _cast_pieces(src_refs, dst_refs):
    for src_ref, dst_ref in zip(src_refs, dst_refs):
        dst_ref[...] = src_ref[...].astype(dst_ref.dtype)


def _norm_matmul_body(x_ref, g_ref, w_ref, ws_ref, *rest, row_block, n_cast):
    cast_src, (o_ref, os_ref), cast_dst, h_ref = rest[:n_cast], rest[n_cast:n_cast + 2], rest[n_cast + 2:-1], rest[-1]
    j = pl.program_id(1)

    @pl.when(j == 0)
    def _():
        _cast_pieces(cast_src, cast_dst)
        for r in range(x_ref.shape[0] // row_block):
            rows = slice(r * row_block, (r + 1) * row_block)
            h = _rms_rows(x_ref[rows, :], g_ref[...]).astype(h_ref.dtype)
            h_ref[rows, :] = h
            os_ref[rows, :] = jnp.dot(h, ws_ref[...], preferred_element_type=F32)
            o_ref[rows, :] = jnp.dot(h, w_ref[...], preferred_element_type=F32).astype(o_ref.dtype)

    @pl.when(j > 0)
    def _():
        _cast_pieces(cast_src, cast_dst)
        o_ref[...] = jnp.dot(h_ref[...], w_ref[...], preferred_element_type=F32).astype(o_ref.dtype)


def _norm_matmul(x, g, w, w_side, tiles, tm, tn, casts):
    n, d = x.shape
    ns = w_side.shape[1]
    cast_in, cast_out, cast_shape = _cast_specs(casts)
    return pl.pallas_call(
        functools.partial(_norm_matmul_body, row_block=PROJ_ROW_BLOCK, n_cast=len(casts)),
        grid=(n // tm, tiles),
        in_specs=[
            pl.BlockSpec((tm, d), lambda i, j: (i, 0)),
            pl.BlockSpec((1, d), lambda i, j: (0, 0)),
            pl.BlockSpec((d, tn), lambda i, j: (0, j)),
            pl.BlockSpec((d, ns), lambda i, j: (0, 0)),
            *cast_in,
        ],
        out_specs=[pl.BlockSpec((None, tm, tn), lambda i, j: (j, i, 0)),
                   pl.BlockSpec((tm, ns), lambda i, j: (i, 0)),
                   *cast_out],
        out_shape=[jax.ShapeDtypeStruct((tiles, n, tn), BF16), jax.ShapeDtypeStruct((n, ns), F32), *cast_shape],
        scratch_shapes=[pltpu.VMEM((tm, d), BF16)],
        compiler_params=_params(),
        name="norm_matmul",
    )(x, g.reshape(1, d), w, w_side, *[c[0] for c in casts])


def _proj_norm_res_body(a_ref, w_ref, x_ref, g_ref, o_ref):
    y = jnp.dot(a_ref[...], w_ref[...], preferred_element_type=F32)
    o_ref[...] = x_ref[...] + _rms_rows(y, g_ref[...])


def _proj_norm_res(a, w, x, g, tm):
    n, d = x.shape
    return pl.pallas_call(
        _proj_norm_res_body,
        grid=(n // tm,),
        in_specs=[
            pl.BlockSpec((tm, a.shape[1]), lambda i: (i, 0)),
            pl.BlockSpec(w.shape, lambda i: (0, 0)),
            pl.BlockSpec((tm, d), lambda i: (i, 0)),
            pl.BlockSpec((1, d), lambda i: (0, 0)),
        ],
        out_specs=pl.BlockSpec((tm, d), lambda i: (i, 0)),
        out_shape=jax.ShapeDtypeStruct((n, d), F32),
        compiler_params=_params(),
        name="proj_norm_res",
    )(a, w, x, g.reshape(1, d))


def _ffn_body(x_ref, g_in_ref, wg_ref, wu_ref, wd_ref, g_out_ref, o_ref, h_ref, *, row_block):
    f = pl.program_id(1)
    last = pl.num_programs(1) - 1
    blocks = [slice(r * row_block, (r + 1) * row_block) for r in range(x_ref.shape[0] // row_block)]

    def partial_down(rows):
        h = h_ref[rows, :]
        gate = jnp.dot(h, wg_ref[...], preferred_element_type=F32)
        up = jnp.dot(h, wu_ref[...], preferred_element_type=F32)
        act = (_silu(gate) * up).astype(BF16)
        return jnp.dot(act, wd_ref[...], preferred_element_type=F32)

    @pl.when(f == 0)
    def _():
        for rows in blocks:
            h_ref[rows, :] = _rms_rows(x_ref[rows, :], g_in_ref[...]).astype(h_ref.dtype)
            o_ref[rows, :] = partial_down(rows)

    @pl.when((f > 0) & (f < last))
    def _():
        for rows in blocks:
            o_ref[rows, :] += partial_down(rows)

    @pl.when(f == last)
    def _():
        for rows in blocks:
            y = o_ref[rows, :] + partial_down(rows)
            o_ref[rows, :] = x_ref[rows, :] + _rms_rows(y, g_out_ref[...])


def _ffn(x, g_in, w_gu, w_down, g_out, layer, tm, tf, row_block):
    n, d = x.shape
    nf = w_down.shape[1] // tf
    return pl.pallas_call(
        functools.partial(_ffn_body, row_block=row_block),
        grid=(n // tm, nf),
        in_specs=[
            pl.BlockSpec((tm, d), lambda i, f: (i, 0)),
            pl.BlockSpec((1, d), lambda i, f: (0, 0)),
            pl.BlockSpec((None, d, tf), lambda i, f: (layer, 0, f)),
            pl.BlockSpec((None, d, tf), lambda i, f: (layer, 0, f + nf)),
            pl.BlockSpec((None, tf, d), lambda i, f: (layer, f, 0)),
            pl.BlockSpec((1, d), lambda i, f: (0, 0)),
        ],
        out_specs=pl.BlockSpec((tm, d), lambda i, f: (i, 0)),
        out_shape=jax.ShapeDtypeStruct((n, d), F32),
        scratch_shapes=[pltpu.VMEM((tm, d), BF16)],
        compiler_params=_params(),
        name="ffn",
    )(x, g_in.reshape(1, d), w_gu, w_gu, w_down, g_out.reshape(1, d))


def _log_sigmoid(x):
    return jnp.minimum(x, 0.0) - jnp.log1p(jnp.exp(-jnp.abs(x)))


_NT = (((1,), (1,)), ((), ()))
_TN = (((0,), (0,)), ((), ()))


def _retention_head(h, bb, rows, q_ref, k_ref, v_ref, g_ref, cos, sin, dm_ref, wq_ref, wk_ref, gl_ref, hg_ref,
                    o_ref, state_ref):
    cols = slice(h * HEAD_W, (h + 1) * HEAD_W)
    half = HEAD_W // 2

    def rot(x):
        x = x.astype(F32)
        x1, x2 = x[:, :half], x[:, half:]
        return jnp.concatenate([x1 * cos - x2 * sin, x2 * cos + x1 * sin], axis=-1)

    q = rot(q_ref[bb, rows, cols])
    k = rot(k_ref[bb, rows, cols]) * (HEAD_W ** -0.5)
    v = v_ref[bb, rows, cols]
    qb = q.astype(BF16)
    s = lax.dot_general(qb, k.astype(BF16), _NT, preferred_element_type=F32) * dm_ref[h]
    intra = jnp.dot(s.astype(BF16), v, preferred_element_type=F32)
    si = bb * RET_HEADS + h
    state = state_ref[si]
    inter = jnp.dot(qb, state.astype(BF16), preferred_element_type=F32) * wq_ref[h]
    out = intra + inter
    kw = (k * wk_ref[h]).astype(BF16)
    state_ref[si] = state * gl_ref[h] + lax.dot_general(kw, v, _TN, preferred_element_type=F32)
    y = _rms_rows(out, hg_ref[:, cols])
    o_ref[bb, rows, cols] = (y * _silu(g_ref[bb, rows, cols].astype(F32))).astype(o_ref.dtype)


def _causal_conv(cur, prev, shift_ref, w):
    shifted = jnp.dot(shift_ref[...], jnp.concatenate([prev, cur], axis=0), preferred_element_type=F32)
    y = cur.astype(F32) * w[CONV_W - 1:CONV_W]
    for s in range(1, CONV_W):
        y = y + shifted[(s - 1) * CHUNK:s * CHUNK] * w[CONV_W - 1 - s:CONV_W - s]
    return y


def _mlstm_head(h, bb, rows, q_all, k_all, v_ref, og_ref, gx, b_rows, lower_tri, hg_ref, o_ref, c_ref, n_ref,
                m_ref):
    cols = slice(h * HEAD_W, (h + 1) * HEAD_W)
    q = q_all[:, cols]
    k = k_all[:, cols]
    v = v_ref[bb, rows, cols]
    si = bb * ML_HEADS + h

    i_row = gx[h:h + 1]
    lf_row = gx[ML_HEADS + h:ML_HEADS + h + 1]
    b_row = b_rows[ML_HEADS + h:ML_HEADS + h + 1]
    b_cols = lax.dot_general(lower_tri, jnp.broadcast_to(lf_row, (CHUNK, CHUNK)), _NT,
                             precision=lax.Precision.HIGHEST, preferred_element_type=F32)
    i_cols = jnp.broadcast_to(i_row, (CHUNK, CHUNK)).T
    rowi = lax.broadcasted_iota(jnp.int32, (CHUNK, CHUNK), 0)
    coli = lax.broadcasted_iota(jnp.int32, (CHUNK, CHUNK), 1)
    log_d = jnp.where(rowi >= coli, b_cols - b_row + i_row, -jnp.inf)
    m_prev = m_ref[si]
    inter_log = b_cols + m_prev
    m_t = jnp.maximum(inter_log, jnp.max(log_d, axis=-1, keepdims=True))
    d_mat = jnp.exp(log_d - m_t)
    w_inter = jnp.exp(inter_log - m_t)

    def wide(a):
        return jnp.concatenate([a] * (HEAD_W // LANES), axis=1)

    qb = q.astype(BF16)
    s = lax.dot_general(qb, k.astype(BF16), _NT, preferred_element_type=F32) * d_mat
    c_state = c_ref[si]
    n_state = n_ref[si]
    num = (jnp.dot(s.astype(BF16), v, preferred_element_type=F32)
           + wide(w_inter) * jnp.dot(qb, c_state.astype(BF16), preferred_element_type=F32))
    den = (jnp.sum(s, axis=-1, keepdims=True)
           + w_inter * jnp.sum(q * n_state, axis=-1, keepdims=True))
    hid = num / wide(jnp.maximum(jnp.abs(den), jnp.exp(-m_t)))

    b_last = b_cols[CHUNK - 1:CHUNK]
    log_w = b_last - b_cols + i_cols
    m_new = jnp.maximum(b_last + m_prev, jnp.max(log_w, axis=0, keepdims=True))
    w = jnp.exp(log_w - m_new)
    decay = wide(jnp.exp(b_last + m_prev - m_new))
    kw = k * wide(w)
    c_ref[si] = decay * c_state + lax.dot_general(kw.astype(BF16), v, _TN, preferred_element_type=F32)
    n_ref[si] = decay * n_state + jnp.sum(kw, axis=0, keepdims=True)
    m_ref[si] = m_new

    out_cols = slice(RET_HEADS * HEAD_W + h * HEAD_W, RET_HEADS * HEAD_W + (h + 1) * HEAD_W)
    y = _rms_rows(hid, hg_ref[:, out_cols])
    o_ref[bb, rows, out_cols] = (y * jax.nn.sigmoid(og_ref[bb, rows, cols].astype(F32))).astype(o_ref.dtype)


def _mixer_body(gl_ref, rq_ref, rk_ref, rv_ref, rg_ref, mq_ref, mqp_ref, mk_ref, mkp_ref, mv_ref, mo_ref,
                cos_ref, sin_ref, dm_ref, wq_ref, wk_ref, gr_ref, gb_ref, cw_ref, shift_ref, hg_ref,
                o_ref, r_ref, c_ref, n_ref, m_ref):
    first = pl.program_id(1) == 0

    @pl.when(first)
    def _():
        r_ref[...] = jnp.zeros_like(r_ref)
        c_ref[...] = jnp.zeros_like(c_ref)
        n_ref[...] = jnp.zeros_like(n_ref)
        m_ref[...] = jnp.zeros_like(m_ref)

    ml_w = ML_HEADS * HEAD_W
    rowi = lax.broadcasted_iota(jnp.int32, (CHUNK, CHUNK), 0)
    coli = lax.broadcasted_iota(jnp.int32, (CHUNK, CHUNK), 1)
    upper_tri = jnp.where(rowi <= coli, 1.0, 0.0).astype(F32)
    lower_tri = jnp.where(rowi >= coli, 1.0, 0.0).astype(F32)
    is_input = lax.broadcasted_iota(jnp.int32, (2 * ML_HEADS, CHUNK), 0) < ML_HEADS

    for bb in range(mq_ref.shape[0]):
        for ci in range(mq_ref.shape[1] // CHUNK):
            rows = slice(ci * CHUNK, (ci + 1) * CHUNK)
            cos = cos_ref[rows, :]
            sin = sin_ref[rows, :]
            for h in range(RET_HEADS):
                _retention_head(h, bb, rows, rq_ref, rk_ref, rv_ref, rg_ref, cos, sin, dm_ref, wq_ref, wk_ref,
                                gl_ref, hg_ref, o_ref, r_ref)

            if ci == 0:
                q_prev = jnp.where(first, jnp.zeros_like(mqp_ref[bb]), mqp_ref[bb])
                k_prev = jnp.where(first, jnp.zeros_like(mkp_ref[bb]), mkp_ref[bb])
            else:
                before = slice((ci - 1) * CHUNK, ci * CHUNK)
                q_prev, k_prev = mq_ref[bb, before, :], mk_ref[bb, before, :]
            q_all = _silu(_causal_conv(mq_ref[bb, rows, :], q_prev, shift_ref, cw_ref[:, :ml_w]))
            k_all = _silu(_causal_conv(mk_ref[bb, rows, :], k_prev, shift_ref, cw_ref[:, ml_w:])) * (HEAD_W ** -0.5)
            gates = gr_ref[bb, :, rows] + gb_ref[...]
            gx = jnp.where(is_input, gates, _log_sigmoid(gates))
            b_rows = jnp.dot(gx, upper_tri, precision=lax.Precision.HIGHEST, preferred_element_type=F32)
            for h in range(ML_HEADS):
                _mlstm_head(h, bb, rows, q_all, k_all, mv_ref, mo_ref, gx, b_rows, lower_tri, hg_ref, o_ref,
                            c_ref, n_ref, m_ref)


def _mixer(proj, gates_row, gate_b, conv_w, head_g):
    _, b, t, _ = proj.shape
    seqs = MIXER_SEQS
    step = MIXER_CHUNKS * CHUNK
    nc = t // step
    ret_w = RET_HEADS * HEAD_W
    ml_w = ML_HEADS * HEAD_W
    assert ret_w == ml_w == proj.shape[3]
    log_g = jnp.log1p(-jnp.exp2(-5.0 - jnp.arange(RET_HEADS, dtype=F32)))
    idx = jnp.arange(CHUNK, dtype=F32)
    diff = idx[:, None] - idx[None, :]
    dmask = jnp.where(diff >= 0, jnp.exp(log_g[:, None, None] * jnp.maximum(diff, 0.0)), 0.0)
    w_k = jnp.exp(log_g[:, None] * (CHUNK - 1.0 - idx)[None, :])
    w_q = jnp.exp(log_g[:, None] * (idx + 1.0)[None, :])
    w_k = jnp.broadcast_to(w_k[..., None], (RET_HEADS, CHUNK, HEAD_W))
    w_q = jnp.broadcast_to(w_q[..., None], (RET_HEADS, CHUNK, HEAD_W))
    g_l = jnp.exp(log_g * CHUNK)
    half = HEAD_W // 2
    inv = 1.0 / (ROPE_BASE ** jnp.linspace(0.0, 1.0, half, dtype=F32))
    ang = jnp.arange(t).astype(F32)[:, None] * inv[None, :]
    cos, sin = jnp.cos(ang), jnp.sin(ang)
    r = jnp.arange((CONV_W - 1) * CHUNK)
    src = CHUNK + r % CHUNK - (r // CHUNK + 1)
    shift = (jnp.arange(2 * CHUNK)[None, :] == src[:, None]).astype(BF16)

    def group(gi):
        return pl.BlockSpec((None, seqs, step, ret_w), lambda bi, c: (gi, bi, c, 0))

    def prev_group(gi):
        return pl.BlockSpec((None, seqs, CHUNK, ml_w),
                            lambda bi, c: (gi, bi, jnp.maximum(c * MIXER_CHUNKS - 1, 0), 0))

    def whole(a):
        return pl.BlockSpec(a.shape, lambda bi, c: (0,) * a.ndim)

    gb = gate_b.reshape(2 * ML_HEADS, 1)
    hg = head_g.reshape(1, ret_w + ml_w)
    return pl.pallas_call(
        _mixer_body,
        grid=(b // seqs, nc),
        in_specs=[
            pl.BlockSpec(memory_space=pltpu.SMEM),
            group(0), group(1), group(2), group(3),
            group(4), prev_group(4), group(5), prev_group(5), group(6), group(7),
            pl.BlockSpec((step, half), lambda bi, c: (c, 0)),
            pl.BlockSpec((step, half), lambda bi, c: (c, 0)),
            whole(dmask), whole(w_q), whole(w_k),
            pl.BlockSpec((seqs, 2 * ML_HEADS, step), lambda bi, c: (bi, 0, c)),
            whole(gb), whole(conv_w), whole(shift), whole(hg),
        ],
        out_specs=pl.BlockSpec((seqs, step, ret_w + ml_w), lambda bi, c: (bi, c, 0)),
        out_shape=jax.ShapeDtypeStruct((b, t, ret_w + ml_w), BF16),
        scratch_shapes=[pltpu.VMEM((seqs * RET_HEADS, HEAD_W, HEAD_W), F32),
                        pltpu.VMEM((seqs * ML_HEADS, HEAD_W, HEAD_W), F32),
                        pltpu.VMEM((seqs * ML_HEADS, 1, HEAD_W), F32),
                        pltpu.VMEM((seqs * ML_HEADS, 1, LANES), F32)],
        compiler_params=_params(),
        name="mixer",
    )(g_l, proj, proj, proj, proj, proj, proj, proj, proj, proj, proj, cos, sin, dmask, w_q, w_k,
      gates_row, gb, conv_w, shift, hg)


PAIR_W = 2 * HEAD_DIM
GROUP = ATT_HEADS // KV_HEADS
PAIRS = GROUP // 2


def _swa_in_proj_body(x_ref, g_ref, w_ref, q_ref, kv_ref, *, row_block):
    qw = ATT_HEADS * HEAD_DIM
    per_block = row_block // ATT_BLOCK
    for r in range(x_ref.shape[0] // row_block):
        rows = slice(r * row_block, (r + 1) * row_block)
        h = _rms_rows(x_ref[rows, :], g_ref[...]).astype(BF16)
        res = jnp.dot(h, w_ref[...], preferred_element_type=F32)
        for nbl in range(per_block):
            sub = slice(nbl * ATT_BLOCK, (nbl + 1) * ATT_BLOCK)
            for p in range(q_ref.shape[2]):
                q_ref[0, r * per_block + nbl, p] = res[sub, p * PAIR_W:(p + 1) * PAIR_W].astype(q_ref.dtype)
        kv_ref[rows, :] = res[:, qw:].astype(kv_ref.dtype)


def _swa_in_proj(x, g, w, b, t, tm, row_block):
    n, d = x.shape
    cols = w.shape[1]
    kvw = cols - ATT_HEADS * HEAD_DIM
    tiles_per_seq = t // tm
    return pl.pallas_call(
        functools.partial(_swa_in_proj_body, row_block=row_block),
        grid=(n // tm,),
        in_specs=[
            pl.BlockSpec((tm, d), lambda i: (i, 0)),
            pl.BlockSpec((1, d), lambda i: (0, 0)),
            pl.BlockSpec((d, cols), lambda i: (0, 0)),
        ],
        out_specs=[
            pl.BlockSpec((1, tm // ATT_BLOCK, ATT_HEADS // 2, ATT_BLOCK, PAIR_W),
                         lambda i: (i // tiles_per_seq, i % tiles_per_seq, 0, 0, 0)),
            pl.BlockSpec((tm, kvw), lambda i: (i, 0)),
        ],
        out_shape=[jax.ShapeDtypeStruct((b, t // ATT_BLOCK, ATT_HEADS // 2, ATT_BLOCK, PAIR_W), BF16),
                   jax.ShapeDtypeStruct((n, kvw), BF16)],
        compiler_params=_params(),
        name="swa_in_proj",
    )(x, g.reshape(1, d), w)


def _swa_body(rb_ref, sink_ref, idx_ref, q_ref, kp_ref, kc_ref, vp_ref, vc_ref, o_ref, bias_ref, sinkcol_ref):
    bi = pl.program_id(0)
    nb = pl.program_id(1)
    blk = ATT_BLOCK
    rows = PAIRS * blk

    @pl.when((bi == 0) & (nb == 0))
    def _():
        idx = idx_ref[...]
        col = lax.broadcasted_iota(jnp.int32, idx.shape, 1)

        def head_body(hd, carry):
            def bucket_body(bk, acc):
                return jnp.where(idx == bk, rb_ref[bk * ATT_HEADS + hd], acc)

            tbl = lax.fori_loop(0, N_BUCKETS, bucket_body, jnp.full(idx.shape, -jnp.inf, F32))
            kv = hd // GROUP
            half = hd % 2
            at = pl.ds(pl.multiple_of(((hd % GROUP) // 2) * blk, blk), blk)
            bias_ref[0, kv, half, at, :] = tbl
            bias_ref[1, kv, half, at, :] = jnp.where(col >= blk, tbl, -jnp.inf)
            sinkcol_ref[kv, half, at, :] = jnp.full((blk, LANES), sink_ref[hd], F32)
            return carry

        lax.fori_loop(0, ATT_HEADS, head_body, 0)

    low = lax.broadcasted_iota(jnp.int32, (2 * blk, LANES), 1) < HEAD_DIM
    low_out = lax.broadcasted_iota(jnp.int32, (rows, LANES), 1) < HEAD_DIM
    scale = HEAD_DIM ** -0.5

    def halves(xf, kv):
        xc = xf[:, (kv // 2) * LANES:(kv // 2 + 1) * LANES]
        xr = pltpu.roll(xc, HEAD_DIM, axis=1)
        lo_src, hi_src = (xc, xr) if kv % 2 == 0 else (xr, xc)
        return (jnp.where(low, lo_src, 0.0).astype(BF16), jnp.where(low, 0.0, hi_src).astype(BF16))

    for sb in range(q_ref.shape[0]):
        here = slice(sb * blk, (sb + 1) * blk)
        before = slice((sb - 1) * blk, sb * blk)
        k_prev, v_prev = (kp_ref[0], vp_ref[0]) if sb == 0 else (kc_ref[0, before, :], vc_ref[0, before, :])
        kf = jnp.concatenate([k_prev, kc_ref[0, here, :]], axis=0).astype(F32)
        vf = jnp.concatenate([v_prev, vc_ref[0, here, :]], axis=0).astype(F32)
        first = (nb == 0).astype(jnp.int32) if sb == 0 else 0
        for kv in range(KV_HEADS):
            k_lo, k_hi = halves(kf, kv)
            v_lo, v_hi = halves(vf, kv)
            q4 = q_ref[sb, kv * PAIRS:(kv + 1) * PAIRS].reshape(rows, PAIR_W) * scale

            def probs(k_half, half):
                s = lax.dot_general(q4, k_half, (((1,), (1,)), ((), ())), preferred_element_type=F32)
                s = s + bias_ref[first, kv, half]
                sink = sinkcol_ref[kv, half]
                mx = jnp.maximum(jnp.max(s, axis=-1, keepdims=True), sink)
                p = jnp.exp(s - jnp.concatenate([mx, mx], axis=1))
                den = jnp.sum(p, axis=-1, keepdims=True) + jnp.exp(sink - mx)
                return p.astype(BF16), den

            p_lo, den_lo = probs(k_lo, 0)
            p_hi, den_hi = probs(k_hi, 1)
            o4 = (jnp.dot(p_lo, v_lo, preferred_element_type=F32)
                  + jnp.dot(p_hi, v_hi, preferred_element_type=F32)) / jnp.where(low_out, den_lo, den_hi)
            for pr in range(PAIRS):
                at = (kv * PAIRS + pr) * PAIR_W
                o_ref[0, here, at:at + PAIR_W] = o4[pr * blk:(pr + 1) * blk].astype(o_ref.dtype)


def _t5_bucket(dist):
    n = jnp.maximum(dist, 0)
    max_exact = N_BUCKETS // 2
    nf = jnp.maximum(n, 1).astype(F32)
    large = max_exact + (jnp.log(nf / max_exact) / math.log(MAX_DIST / max_exact)
                         * (N_BUCKETS - max_exact)).astype(jnp.int32)
    large = jnp.minimum(large, N_BUCKETS - 1)
    return jnp.where(n < max_exact, n, large)


def _swa(q, kv, sinks, rel_bias):
    b, nblk, n_pairs, blk, _ = q.shape
    t = nblk * blk
    qw = ATT_HEADS * HEAD_DIM
    kvw = KV_HEADS * HEAD_DIM
    i = jnp.arange(blk)
    j = jnp.arange(2 * blk)
    dist = (blk + i)[:, None] - j[None, :]
    idx = jnp.where((dist >= 0) & (dist < WINDOW), _t5_bucket(dist), -1).astype(jnp.int32)
    sb = SWA_BLOCKS
    return pl.pallas_call(
        _swa_body,
        grid=(b, nblk // sb),
        in_specs=[
            pl.BlockSpec(memory_space=pltpu.SMEM),
            pl.BlockSpec(memory_space=pltpu.SMEM),
            pl.BlockSpec((blk, 2 * blk), lambda bi, nb: (0, 0)),
            pl.BlockSpec((None, sb, n_pairs, blk, PAIR_W), lambda bi, nb: (bi, nb, 0, 0, 0)),
            pl.BlockSpec((1, blk, kvw), lambda bi, nb: (bi, jnp.maximum(nb * sb - 1, 0), 0)),
            pl.BlockSpec((1, sb * blk, kvw), lambda bi, nb: (bi, nb, 0)),
            pl.BlockSpec((1, blk, kvw), lambda bi, nb: (bi, jnp.maximum(nb * sb - 1, 0), 1)),
            pl.BlockSpec((1, sb * blk, kvw), lambda bi, nb: (bi, nb, 1)),
        ],
        out_specs=pl.BlockSpec((1, sb * blk, qw), lambda bi, nb: (bi, nb, 0)),
        out_shape=jax.ShapeDtypeStruct((b, t, qw), BF16),
        scratch_shapes=[pltpu.VMEM((2, KV_HEADS, 2, PAIRS * blk, 2 * blk), F32),
                        pltpu.VMEM((KV_HEADS, 2, PAIRS * blk, LANES), F32)],
        compiler_params=_params(),
        name="swa",
    )(rel_bias.reshape(-1), sinks, idx, q, kv, kv, kv, kv)


def kernel(x, rel_bias, norm_g, ffn_w_gu, ffn_w_down, rm_w_in, ml_conv_w, ml_gate_b, rm_head_g,
           rm_w_out, swa_w_in, swa_sinks, swa_w_out):
    b, t, d = x.shape
    n = b * t
    depth = norm_g.shape[0]
    main_cols = 4 * RET_HEADS * HEAD_W + 4 * ML_HEADS * HEAD_W
    xs = x.reshape(n, d)
    row_tiles = n // ROW_TILE
    proj_tiles = main_cols // PROJ_TN

    def row_bands(src, layer):
        rows = src.shape[1] // (row_tiles * proj_tiles)
        return src, layer, (rows, src.shape[2]), lambda i, j: (i * proj_tiles + j, 0)

    def blocks(src, layer):
        return src, layer, (src.shape[1] // row_tiles, src.shape[2] // proj_tiles), lambda i, j: (i, j)

    ready = {}

    def bf16_weight(name, src, layer):
        if (name, layer) in ready:
            return ready.pop((name, layer))
        return src[layer].astype(BF16)

    for layer in range(depth):
        g = norm_g[layer]
        if layer % 2 == 0:
            e = layer // 2
            w_in = rm_w_in[e]
            gate_cols = w_in.shape[1] - main_cols
            w_gate = jnp.pad(w_in[:, main_cols:], ((0, 0), (0, LANES - gate_cols))).astype(BF16)
            names, casts = [("rm_out", e)], [row_bands(rm_w_out, e)]
            if layer == 0:
                names += ["gu", "down"]
                casts += [row_bands(ffn_w_gu, None), blocks(ffn_w_down, None)]
            if layer + 1 < depth:
                o = (layer + 1) // 2
                names += [("swa_in", o), ("swa_out", o)]
                casts += [row_bands(swa_w_in, o), row_bands(swa_w_out, o)]
            proj, gates, *cast = _norm_matmul(xs, g[0], w_in.astype(BF16), w_gate, proj_tiles, ROW_TILE, PROJ_TN, casts)
            ready.update(zip(names, cast))
            gates_row = gates[:, :gate_cols].reshape(b, t, gate_cols).transpose(0, 2, 1)
            mixed = _mixer(proj.reshape(main_cols // PROJ_TN, b, t, PROJ_TN), gates_row, ml_gate_b[e], ml_conv_w[e],
                           rm_head_g[e])
            xs = _proj_norm_res(mixed.reshape(n, -1), bf16_weight("rm_out", rm_w_out, e), xs, g[1], OUT_ROW_TILE)
        else:
            o = layer // 2
            q, kv = _swa_in_proj(xs, g[0], bf16_weight("swa_in", swa_w_in, o), b, t, ROW_TILE, PROJ_ROW_BLOCK)
            att = _swa(q, kv.reshape(b, t, -1), swa_sinks[o], rel_bias)
            xs = _proj_norm_res(att.reshape(n, -1), bf16_weight("swa_out", swa_w_out, o), xs, g[1], OUT_ROW_TILE)
        xs = _ffn(xs, g[2], ready["gu"], ready["down"], g[3], layer, ROW_TILE, FFN_TF, FFN_ROW_BLOCK)
    return xs.reshape(b, t, d)
```

```python
import functools
import math

import jax
import jax.numpy as jnp
from jax import lax
from jax.experimental import pallas as pl
from jax.experimental.pallas import tpu as pltpu

F32 = jnp.float32
BF16 = jnp.bfloat16

EPS = 1e-6
CHUNK = 128
RET_HEADS = 4
ML_HEADS = 4
HEAD_W = 256
CONV_W = 4
ROPE_BASE = 10000.0
ATT_HEADS = 32
KV_HEADS = 4
HEAD_DIM = 64
WINDOW = 128
ATT_BLOCK = 128
N_BUCKETS = 32
MAX_DIST = 128
LANES = 128
SUBLANES = 8
VMEM_LIMIT = 56 * 1024 * 1024

ROW_TILE = 1024
OUT_ROW_TILE = 512
PROJ_TN = 1024
PROJ_ROW_BLOCK = 256
MIXER_CHUNKS = 2
MIXER_SEQS = 2
SWA_BLOCKS = 4
FFN_TF = 512
FFN_ROW_BLOCK = 512


def _params(vmem=VMEM_LIMIT):
    return pltpu.CompilerParams(vmem_limit_bytes=vmem)


def _rms_rows(x, g):
    ms = jnp.mean(x * x, axis=-1, keepdims=True)
    return x * lax.rsqrt(ms + EPS) * g


def _silu(x):
    return x * jax.nn.sigmoid(x)


def _cast_specs(casts):
    in_specs = [pl.BlockSpec((None, *blk), (lambda *ids, layer=layer, idx=idx: (layer, *idx(*ids))))
                for _, layer, blk, idx in casts]
    out_specs = [pl.BlockSpec(blk, idx) for _, _, blk, idx in casts]
    out_shape = [jax.ShapeDtypeStruct(src.shape[1:], BF16) for src, _, _, _ in casts]
    return in_specs, out_specs, out_shape


def _cast_pieces(src_refs, dst_refs):
    for src_ref, dst_ref in zip(src_refs, dst_refs):
        dst_ref[...] = src_ref[...].astype(dst_ref.dtype)


def _norm_matmul_body(x_ref, g_ref, w_ref, ws_ref, *rest, row_block, n_cast):
    cast_src, (o_ref, os_ref), cast_dst, h_ref = rest[:n_cast], rest[n_cast:n_cast + 2], rest[n_cast + 2:-1], rest[-1]
    j = pl.program_id(1)

    @pl.when(j == 0)
    def _():
        _cast_pieces(cast_src, cast_dst)
        for r in range(x_ref.shape[0] // row_block):
            rows = slice(r * row_block, (r + 1) * row_block)
            h = _rms_rows(x_ref[rows, :], g_ref[...]).astype(h_ref.dtype)
            h_ref[rows, :] = h
            os_ref[rows, :] = jnp.dot(h, ws_ref[...], preferred_element_type=F32)
            o_ref[rows, :] = jnp.dot(h, w_ref[...], preferred_element_type=F32).astype(o_ref.dtype)

    @pl.when(j > 0)
    def _():
        _cast_pieces(cast_src, cast_dst)
        o_ref[...] = jnp.dot(h_ref[...], w_ref[...], preferred_element_type=F32).astype(o_ref.dtype)


def _norm_matmul(x, g, w, w_side, tiles, tm, tn, casts):
    n, d = x.shape
    ns = w_side.shape[1]
    cast_in, cast_out, cast_shape = _cast_specs(casts)
    return pl.pallas_call(
        functools.partial(_norm_matmul_body, row_block=PROJ_ROW_BLOCK, n_cast=len(casts)),
        grid=(n // tm, tiles),
        in_specs=[
            pl.BlockSpec((tm, d), lambda i, j: (i, 0)),
            pl.BlockSpec((1, d), lambda i, j: (0, 0)),
            pl.BlockSpec((d, tn), lambda i, j: (0, j)),
            pl.BlockSpec((d, ns), lambda i, j: (0, 0)),
            *cast_in,
        ],
        out_specs=[pl.BlockSpec((None, tm, tn), lambda i, j: (j, i, 0)),
                   pl.BlockSpec((tm, ns), lambda i, j: (i, 0)),
                   *cast_out],
        out_shape=[jax.ShapeDtypeStruct((tiles, n, tn), BF16), jax.ShapeDtypeStruct((n, ns), F32), *cast_shape],
        scratch_shapes=[pltpu.VMEM((tm, d), BF16)],
        compiler_params=_params(),
        name="norm_matmul",
    )(x, g.reshape(1, d), w, w_side, *[c[0] for c in casts])


def _proj_norm_res_body(a_ref, w_ref, x_ref, g_ref, o_ref):
    y = jnp.dot(a_ref[...], w_ref[...], preferred_element_type=F32)
    o_ref[...] = x_ref[...] + _rms_rows(y, g_ref[...])


def _proj_norm_res(a, w, x, g, tm):
    n, d = x.shape
    return pl.pallas_call(
        _proj_norm_res_body,
        grid=(n // tm,),
        in_specs=[
            pl.BlockSpec((tm, a.shape[1]), lambda i: (i, 0)),
            pl.BlockSpec(w.shape, lambda i: (0, 0)),
            pl.BlockSpec((tm, d), lambda i: (i, 0)),
            pl.BlockSpec((1, d), lambda i: (0, 0)),
        ],
        out_specs=pl.BlockSpec((tm, d), lambda i: (i, 0)),
        out_shape=jax.ShapeDtypeStruct((n, d), F32),
        compiler_params=_params(),
        name="proj_norm_res",
    )(a, w, x, g.reshape(1, d))


def _ffn_body(x_ref, g_in_ref, wg_ref, wu_ref, wd_ref, g_out_ref, o_ref, h_ref, *, row_block):
    f = pl.program_id(1)
    last = pl.num_programs(1) - 1
    blocks = [slice(r * row_block, (r + 1) * row_block) for r in range(x_ref.shape[0] // row_block)]

    def partial_down(rows):
        h = h_ref[rows, :]
        gate = jnp.dot(h, wg_ref[...], preferred_element_type=F32)
        up = jnp.dot(h, wu_ref[...], preferred_element_type=F32)
        act = (_silu(gate) * up).astype(BF16)
        return jnp.dot(act, wd_ref[...], preferred_element_type=F32)

    @pl.when(f == 0)
    def _():
        for rows in blocks:
            h_ref[rows, :] = _rms_rows(x_ref[rows, :], g_in_ref[...]).astype(h_ref.dtype)
            o_ref[rows, :] = partial_down(rows)

    @pl.when((f > 0) & (f < last))
    def _():
        for rows in blocks:
            o_ref[rows, :] += partial_down(rows)

    @pl.when(f == last)
    def _():
        for rows in blocks:
            y = o_ref[rows, :] + partial_down(rows)
            o_ref[rows, :] = x_ref[rows, :] + _rms_rows(y, g_out_ref[...])


def _ffn(x, g_in, w_gu, w_down, g_out, tm, tf, row_block):
    n, d = x.shape
    nf = w_down.shape[0] // tf
    return pl.pallas_call(
        functools.partial(_ffn_body, row_block=row_block),
        grid=(n // tm, nf),
        in_specs=[
            pl.BlockSpec((tm, d), lambda i, f: (i, 0)),
            pl.BlockSpec((1, d), lambda i, f: (0, 0)),
            pl.BlockSpec((d, tf), lambda i, f: (0, f)),
            pl.BlockSpec((d, tf), lambda i, f: (0, f + nf)),
            pl.BlockSpec((tf, d), lambda i, f: (f, 0)),
            pl.BlockSpec((1, d), lambda i, f: (0, 0)),
        ],
        out_specs=pl.BlockSpec((tm, d), lambda i, f: (i, 0)),
        out_shape=jax.ShapeDtypeStruct((n, d), F32),
        scratch_shapes=[pltpu.VMEM((tm, d), BF16)],
        compiler_params=_params(),
        name="ffn",
    )(x, g_in.reshape(1, d), w_gu, w_gu, w_down, g_out.reshape(1, d))


def _log_sigmoid(x):
    return jnp.minimum(x, 0.0) - jnp.log1p(jnp.exp(-jnp.abs(x)))


_NT = (((1,), (1,)), ((), ()))
_TN = (((0,), (0,)), ((), ()))


def _retention_head(h, bb, rows, q_ref, k_ref, v_ref, g_ref, cos, sin, dm_ref, wq_ref, wk_ref, gl_ref, hg_ref,
                    o_ref, state_ref):
    cols = slice(h * HEAD_W, (h + 1) * HEAD_W)
    half = HEAD_W // 2

    def rot(x):
        x = x.astype(F32)
        x1, x2 = x[:, :half], x[:, half:]
        return jnp.concatenate([x1 * cos - x2 * sin, x2 * cos + x1 * sin], axis=-1)

    q = rot(q_ref[bb, rows, cols])
    k = rot(k_ref[bb, rows, cols]) * (HEAD_W ** -0.5)
    v = v_ref[bb, rows, cols]
    qb = q.astype(BF16)
    s = lax.dot_general(qb, k.astype(BF16), _NT, preferred_element_type=F32) * dm_ref[h]
    intra = jnp.dot(s.astype(BF16), v, preferred_element_type=F32)
    si = bb * RET_HEADS + h
    state = state_ref[si]
    inter = jnp.dot(qb, state.astype(BF16), preferred_element_type=F32) * wq_ref[h]
    out = intra + inter
    kw = (k * wk_ref[h]).astype(BF16)
    state_ref[si] = state * gl_ref[h] + lax.dot_general(kw, v, _TN, preferred_element_type=F32)
    y = _rms_rows(out, hg_ref[:, cols])
    o_ref[bb, rows, cols] = (y * _silu(g_ref[bb, rows, cols].astype(F32))).astype(o_ref.dtype)


def _causal_conv(cur, prev, shift_ref, w):
    shifted = jnp.dot(shift_ref[...], jnp.concatenate([prev, cur], axis=0), preferred_element_type=F32)
    y = cur.astype(F32) * w[CONV_W - 1:CONV_W]
    for s in range(1, CONV_W):
        y = y + shifted[(s - 1) * CHUNK:s * CHUNK] * w[CONV_W - 1 - s:CONV_W - s]
    return y


def _mlstm_head(h, bb, rows, q_all, k_all, v_ref, og_ref, gx, b_rows, lower_tri, hg_ref, o_ref, c_ref, n_ref,
                m_ref):
    cols = slice(h * HEAD_W, (h + 1) * HEAD_W)
    q = q_all[:, cols]
    k = k_all[:, cols]
    v = v_ref[bb, rows, cols]
    si = bb * ML_HEADS + h

    i_row = gx[h:h + 1]
    lf_row = gx[ML_HEADS + h:ML_HEADS + h + 1]
    b_row = b_rows[ML_HEADS + h:ML_HEADS + h + 1]
    b_cols = lax.dot_general(lower_tri, jnp.broadcast_to(lf_row, (CHUNK, CHUNK)), _NT,
                             precision=lax.Precision.HIGHEST, preferred_element_type=F32)
    i_cols = jnp.broadcast_to(i_row, (CHUNK, CHUNK)).T
    rowi = lax.broadcasted_iota(jnp.int32, (CHUNK, CHUNK), 0)
    coli = lax.broadcasted_iota(jnp.int32, (CHUNK, CHUNK), 1)
    log_d = jnp.where(rowi >= coli, b_cols - b_row + i_row, -jnp.inf)
    m_prev = m_ref[si]
    inter_log = b_cols + m_prev
    m_t = jnp.maximum(inter_log, jnp.max(log_d, axis=-1, keepdims=True))
    d_mat = jnp.exp(log_d - m_t)
    w_inter = jnp.exp(inter_log - m_t)

    def wide(a):
        return jnp.concatenate([a] * (HEAD_W // LANES), axis=1)

    qb = q.astype(BF16)
    s = lax.dot_general(qb, k.astype(BF16), _NT, preferred_element_type=F32) * d_mat
    c_state = c_ref[si]
    n_state = n_ref[si]
    num = (jnp.dot(s.astype(BF16), v, preferred_element_type=F32)
           + wide(w_inter) * jnp.dot(qb, c_state.astype(BF16), preferred_element_type=F32))
    den = (jnp.sum(s, axis=-1, keepdims=True)
           + w_inter * jnp.sum(q * n_state, axis=-1, keepdims=True))
    hid = num / wide(jnp.maximum(jnp.abs(den), jnp.exp(-m_t)))

    b_last = b_cols[CHUNK - 1:CHUNK]
    log_w = b_last - b_cols + i_cols
    m_new = jnp.maximum(b_last + m_prev, jnp.max(log_w, axis=0, keepdims=True))
    w = jnp.exp(log_w - m_new)
    decay = wide(jnp.exp(b_last + m_prev - m_new))
    kw = k * wide(w)
    c_ref[si] = decay * c_state + lax.dot_general(kw.astype(BF16), v, _TN, preferred_element_type=F32)
    n_ref[si] = decay * n_state + jnp.sum(kw, axis=0, keepdims=True)
    m_ref[si] = m_new

    out_cols = slice(RET_HEADS * HEAD_W + h * HEAD_W, RET_HEADS * HEAD_W + (h + 1) * HEAD_W)
    y = _rms_rows(hid, hg_ref[:, out_cols])
    o_ref[bb, rows, out_cols] = (y * jax.nn.sigmoid(og_ref[bb, rows, cols].astype(F32))).astype(o_ref.dtype)


def _mixer_body(gl_ref, rq_ref, rk_ref, rv_ref, rg_ref, mq_ref, mqp_ref, mk_ref, mkp_ref, mv_ref, mo_ref,
                cos_ref, sin_ref, dm_ref, wq_ref, wk_ref, gr_ref, gb_ref, cw_ref, shift_ref, hg_ref,
                o_ref, r_ref, c_ref, n_ref, m_ref):
    first = pl.program_id(1) == 0

    @pl.when(first)
    def _():
        r_ref[...] = jnp.zeros_like(r_ref)
        c_ref[...] = jnp.zeros_like(c_ref)
        n_ref[...] = jnp.zeros_like(n_ref)
        m_ref[...] = jnp.zeros_like(m_ref)

    ml_w = ML_HEADS * HEAD_W
    rowi = lax.broadcasted_iota(jnp.int32, (CHUNK, CHUNK), 0)
    coli = lax.broadcasted_iota(jnp.int32, (CHUNK, CHUNK), 1)
    upper_tri = jnp.where(rowi <= coli, 1.0, 0.0).astype(F32)
    lower_tri = jnp.where(rowi >= coli, 1.0, 0.0).astype(F32)
    is_input = lax.broadcasted_iota(jnp.int32, (2 * ML_HEADS, CHUNK), 0) < ML_HEADS

    for bb in range(mq_ref.shape[0]):
        for ci in range(mq_ref.shape[1] // CHUNK):
            rows = slice(ci * CHUNK, (ci + 1) * CHUNK)
            cos = cos_ref[rows, :]
            sin = sin_ref[rows, :]
            for h in range(RET_HEADS):
                _retention_head(h, bb, rows, rq_ref, rk_ref, rv_ref, rg_ref, cos, sin, dm_ref, wq_ref, wk_ref,
                                gl_ref, hg_ref, o_ref, r_ref)

            if ci == 0:
                q_prev = jnp.where(first, jnp.zeros_like(mqp_ref[bb]), mqp_ref[bb])
                k_prev = jnp.where(first, jnp.zeros_like(mkp_ref[bb]), mkp_ref[bb])
            else:
                before = slice((ci - 1) * CHUNK, ci * CHUNK)
                q_prev, k_prev = mq_ref[bb, before, :], mk_ref[bb, before, :]
            q_all = _silu(_causal_conv(mq_ref[bb, rows, :], q_prev, shift_ref, cw_ref[:, :ml_w]))
            k_all = _silu(_causal_conv(mk_ref[bb, rows, :], k_prev, shift_ref, cw_ref[:, ml_w:])) * (HEAD_W ** -0.5)
            gates = gr_ref[bb, :, rows] + gb_ref[...]
            gx = jnp.where(is_input, gates, _log_sigmoid(gates))
            b_rows = jnp.dot(gx, upper_tri, precision=lax.Precision.HIGHEST, preferred_element_type=F32)
            for h in range(ML_HEADS):
                _mlstm_head(h, bb, rows, q_all, k_all, mv_ref, mo_ref, gx, b_rows, lower_tri, hg_ref, o_ref,
                            c_ref, n_ref, m_ref)


def _mixer(proj, gates_row, gate_b, conv_w, head_g):
    _, b, t, _ = proj.shape
    seqs = MIXER_SEQS
    step = MIXER_CHUNKS * CHUNK
    nc = t // step
    ret_w = RET_HEADS * HEAD_W
    ml_w = ML_HEADS * HEAD_W
    assert ret_w == ml_w == proj.shape[3]
    log_g = jnp.log1p(-jnp.exp2(-5.0 - jnp.arange(RET_HEADS, dtype=F32)))
    idx = jnp.arange(CHUNK, dtype=F32)
    diff = idx[:, None] - idx[None, :]
    dmask = jnp.where(diff >= 0, jnp.exp(log_g[:, None, None] * jnp.maximum(diff, 0.0)), 0.0)
    w_k = jnp.exp(log_g[:, None] * (CHUNK - 1.0 - idx)[None, :])
    w_q = jnp.exp(log_g[:, None] * (idx + 1.0)[None, :])
    w_k = jnp.broadcast_to(w_k[..., None], (RET_HEADS, CHUNK, HEAD_W))
    w_q = jnp.broadcast_to(w_q[..., None], (RET_HEADS, CHUNK, HEAD_W))
    g_l = jnp.exp(log_g * CHUNK)
    half = HEAD_W // 2
    inv = 1.0 / (ROPE_BASE ** jnp.linspace(0.0, 1.0, half, dtype=F32))
    ang = jnp.arange(t).astype(F32)[:, None] * inv[None, :]
    cos, sin = jnp.cos(ang), jnp.sin(ang)
    r = jnp.arange((CONV_W - 1) * CHUNK)
    src = CHUNK + r % CHUNK - (r // CHUNK + 1)
    shift = (jnp.arange(2 * CHUNK)[None, :] == src[:, None]).astype(BF16)

    def group(gi):
        return pl.BlockSpec((None, seqs, step, ret_w), lambda bi, c: (gi, bi, c, 0))

    def prev_group(gi):
        return pl.BlockSpec((None, seqs, CHUNK, ml_w),
                            lambda bi, c: (gi, bi, jnp.maximum(c * MIXER_CHUNKS - 1, 0), 0))

    def whole(a):
        return pl.BlockSpec(a.shape, lambda bi, c: (0,) * a.ndim)

    gb = gate_b.reshape(2 * ML_HEADS, 1)
    hg = head_g.reshape(1, ret_w + ml_w)
    return pl.pallas_call(
        _mixer_body,
        grid=(b // seqs, nc),
        in_specs=[
            pl.BlockSpec(memory_space=pltpu.SMEM),
            group(0), group(1), group(2), group(3),
            group(4), prev_group(4), group(5), prev_group(5), group(6), group(7),
            pl.BlockSpec((step, half), lambda bi, c: (c, 0)),
            pl.BlockSpec((step, half), lambda bi, c: (c, 0)),
            whole(dmask), whole(w_q), whole(w_k),
            pl.BlockSpec((seqs, 2 * ML_HEADS, step), lambda bi, c: (bi, 0, c)),
            whole(gb), whole(conv_w), whole(shift), whole(hg),
        ],
        out_specs=pl.BlockSpec((seqs, step, ret_w + ml_w), lambda bi, c: (bi, c, 0)),
        out_shape=jax.ShapeDtypeStruct((b, t, ret_w + ml_w), BF16),
        scratch_shapes=[pltpu.VMEM((seqs * RET_HEADS, HEAD_W, HEAD_W), F32),
                        pltpu.VMEM((seqs * ML_HEADS, HEAD_W, HEAD_W), F32),
                        pltpu.VMEM((seqs * ML_HEADS, 1, HEAD_W), F32),
                        pltpu.VMEM((seqs * ML_HEADS, 1, LANES), F32)],
        compiler_params=_params(),
        name="mixer",
    )(g_l, proj, proj, proj, proj, proj, proj, proj, proj, proj, proj, cos, sin, dmask, w_q, w_k,
      gates_row, gb, conv_w, shift, hg)


PAIR_W = 2 * HEAD_DIM
GROUP = ATT_HEADS // KV_HEADS
PAIRS = GROUP // 2
LOG2E = math.log2(math.e)
Q_PRESCALE = HEAD_DIM ** -0.5 * LOG2E


def _swa_in_proj_body(x_ref, g_ref, w_ref, q_ref, kv_ref, *, row_block):
    qw = ATT_HEADS * HEAD_DIM
    per_block = row_block // ATT_BLOCK
    for r in range(x_ref.shape[0] // row_block):
        rows = slice(r * row_block, (r + 1) * row_block)
        h = _rms_rows(x_ref[rows, :], g_ref[...]).astype(BF16)
        res = jnp.dot(h, w_ref[...], preferred_element_type=F32)
        for nbl in range(per_block):
            sub = slice(nbl * ATT_BLOCK, (nbl + 1) * ATT_BLOCK)
            for p in range(q_ref.shape[2]):
                q_ref[0, r * per_block + nbl, p] = (res[sub, p * PAIR_W:(p + 1) * PAIR_W]
                                                    * Q_PRESCALE).astype(q_ref.dtype)
        kv_ref[rows, :] = res[:, qw:].astype(kv_ref.dtype)


def _swa_in_proj(x, g, w, b, t, tm, row_block):
    n, d = x.shape
    cols = w.shape[1]
    kvw = cols - ATT_HEADS * HEAD_DIM
    tiles_per_seq = t // tm
    return pl.pallas_call(
        functools.partial(_swa_in_proj_body, row_block=row_block),
        grid=(n // tm,),
        in_specs=[
            pl.BlockSpec((tm, d), lambda i: (i, 0)),
            pl.BlockSpec((1, d), lambda i: (0, 0)),
            pl.BlockSpec((d, cols), lambda i: (0, 0)),
        ],
        out_specs=[
            pl.BlockSpec((1, tm // ATT_BLOCK, ATT_HEADS // 2, ATT_BLOCK, PAIR_W),
                         lambda i: (i // tiles_per_seq, i % tiles_per_seq, 0, 0, 0)),
            pl.BlockSpec((tm, kvw), lambda i: (i, 0)),
        ],
        out_shape=[jax.ShapeDtypeStruct((b, t // ATT_BLOCK, ATT_HEADS // 2, ATT_BLOCK, PAIR_W), BF16),
                   jax.ShapeDtypeStruct((n, kvw), BF16)],
        compiler_params=_params(),
        name="swa_in_proj",
    )(x, g.reshape(1, d), w)


def _swa_body(rb_ref, sink_ref, idx_ref, q_ref, kp_ref, kc_ref, vp_ref, vc_ref, o_ref, bias_ref, sinkcol_ref):
    bi = pl.program_id(0)
    nb = pl.program_id(1)
    blk = ATT_BLOCK
    rows = PAIRS * blk

    @pl.when((bi == 0) & (nb == 0))
    def _():
        idx = idx_ref[...]
        col = lax.broadcasted_iota(jnp.int32, idx.shape, 1)

        def head_body(hd, carry):
            def bucket_body(bk, acc):
                return jnp.where(idx == bk, rb_ref[bk * ATT_HEADS + hd] * LOG2E, acc)

            tbl = lax.fori_loop(0, N_BUCKETS, bucket_body, jnp.full(idx.shape, -jnp.inf, F32))
            kv = hd // GROUP
            half = hd % 2
            at = pl.ds(pl.multiple_of(((hd % GROUP) // 2) * blk, blk), blk)
            bias_ref[0, kv, half, at, :] = tbl
            bias_ref[1, kv, half, at, :] = jnp.where(col >= blk, tbl, -jnp.inf)
            sinkcol_ref[kv, half, at, :] = jnp.full((blk, LANES), sink_ref[hd] * LOG2E, F32)
            return carry

        lax.fori_loop(0, ATT_HEADS, head_body, 0)

    low = lax.broadcasted_iota(jnp.int32, (2 * blk, LANES), 1) < HEAD_DIM
    low_out = lax.broadcasted_iota(jnp.int32, (rows, LANES), 1) < HEAD_DIM

    def halves(xf, kv):
        xc = xf[:, (kv // 2) * LANES:(kv // 2 + 1) * LANES]
        xr = pltpu.roll(xc, HEAD_DIM, axis=1)
        lo_src, hi_src = (xc, xr) if kv % 2 == 0 else (xr, xc)
        return (jnp.where(low, lo_src, 0.0).astype(BF16), jnp.where(low, 0.0, hi_src).astype(BF16))

    for sb in range(q_ref.shape[0]):
        here = slice(sb * blk, (sb + 1) * blk)
        before = slice((sb - 1) * blk, sb * blk)
        k_prev, v_prev = (kp_ref[0], vp_ref[0]) if sb == 0 else (kc_ref[0, before, :], vc_ref[0, before, :])
        kf = jnp.concatenate([k_prev, kc_ref[0, here, :]], axis=0).astype(F32)
        vf = jnp.concatenate([v_prev, vc_ref[0, here, :]], axis=0).astype(F32)
        first = (nb == 0).astype(jnp.int32) if sb == 0 else 0
        for kv in range(KV_HEADS):
            k_lo, k_hi = halves(kf, kv)
            v_lo, v_hi = halves(vf, kv)
            q4 = q_ref[sb, kv * PAIRS:(kv + 1) * PAIRS].reshape(rows, PAIR_W)

            def probs(k_half, half):
                s = lax.dot_general(q4, k_half, (((1,), (1,)), ((), ())), preferred_element_type=F32)
                s = s + bias_ref[first, kv, half]
                sink = sinkcol_ref[kv, half]
                mx = jnp.maximum(jnp.max(s, axis=-1, keepdims=True), sink)
                p = jnp.exp2(s - jnp.concatenate([mx, mx], axis=1))
                den = jnp.sum(p, axis=-1, keepdims=True) + jnp.exp2(sink - mx)
                return p.astype(BF16), den

            p_lo, den_lo = probs(k_lo, 0)
            p_hi, den_hi = probs(k_hi, 1)
            o4 = (jnp.dot(p_lo, v_lo, preferred_element_type=F32)
                  + jnp.dot(p_hi, v_hi, preferred_element_type=F32)) / jnp.where(low_out, den_lo, den_hi)
            for pr in range(PAIRS):
                at = (kv * PAIRS + pr) * PAIR_W
                o_ref[0, here, at:at + PAIR_W] = o4[pr * blk:(pr + 1) * blk].astype(o_ref.dtype)


def _t5_bucket(dist):
    n = jnp.maximum(dist, 0)
    max_exact = N_BUCKETS // 2
    nf = jnp.maximum(n, 1).astype(F32)
    large = max_exact + (jnp.log(nf / max_exact) / math.log(MAX_DIST / max_exact)
                         * (N_BUCKETS - max_exact)).astype(jnp.int32)
    large = jnp.minimum(large, N_BUCKETS - 1)
    return jnp.where(n < max_exact, n, large)


def _swa(q, kv, sinks, rel_bias):
    b, nblk, n_pairs, blk, _ = q.shape
    t = nblk * blk
    qw = ATT_HEADS * HEAD_DIM
    kvw = KV_HEADS * HEAD_DIM
    i = jnp.arange(blk)
    j = jnp.arange(2 * blk)
    dist = (blk + i)[:, None] - j[None, :]
    idx = jnp.where((dist >= 0) & (dist < WINDOW), _t5_bucket(dist), -1).astype(jnp.int32)
    sb = SWA_BLOCKS
    return pl.pallas_call(
        _swa_body,
        grid=(b, nblk // sb),
        in_specs=[
            pl.BlockSpec(memory_space=pltpu.SMEM),
            pl.BlockSpec(memory_space=pltpu.SMEM),
            pl.BlockSpec((blk, 2 * blk), lambda bi, nb: (0, 0)),
            pl.BlockSpec((None, sb, n_pairs, blk, PAIR_W), lambda bi, nb: (bi, nb, 0, 0, 0)),
            pl.BlockSpec((1, blk, kvw), lambda bi, nb: (bi, jnp.maximum(nb * sb - 1, 0), 0)),
            pl.BlockSpec((1, sb * blk, kvw), lambda bi, nb: (bi, nb, 0)),
            pl.BlockSpec((1, blk, kvw), lambda bi, nb: (bi, jnp.maximum(nb * sb - 1, 0), 1)),
            pl.BlockSpec((1, sb * blk, kvw), lambda bi, nb: (bi, nb, 1)),
        ],
        out_specs=pl.BlockSpec((1, sb * blk, qw), lambda bi, nb: (bi, nb, 0)),
        out_shape=jax.ShapeDtypeStruct((b, t, qw), BF16),
        scratch_shapes=[pltpu.VMEM((2, KV_HEADS, 2, PAIRS * blk, 2 * blk), F32),
                        pltpu.VMEM((KV_HEADS, 2, PAIRS * blk, LANES), F32)],
        compiler_params=_params(),
        name="swa",
    )(rel_bias.reshape(-1), sinks, idx, q, kv, kv, kv, kv)


def kernel(x, rel_bias, norm_g, ffn_w_gu, ffn_w_down, rm_w_in, ml_conv_w, ml_gate_b, rm_head_g,
           rm_w_out, swa_w_in, swa_sinks, swa_w_out):
    b, t, d = x.shape
    n = b * t
    depth = norm_g.shape[0]
    main_cols = 4 * RET_HEADS * HEAD_W + 4 * ML_HEADS * HEAD_W
    xs = x.reshape(n, d)
    row_tiles = n // ROW_TILE
    proj_tiles = main_cols // PROJ_TN

    def row_bands(src, layer):
        rows = src.shape[1] // (row_tiles * proj_tiles)
        return src, layer, (rows, src.shape[2]), lambda i, j: (i * proj_tiles + j, 0)

    def blocks(src, layer):
        return src, layer, (src.shape[1] // row_tiles, src.shape[2] // proj_tiles), lambda i, j: (i, j)

    ready = {}

    def bf16_weight(name, src, layer):
        if (name, layer) in ready:
            return ready.pop((name, layer))
        return src[layer].astype(BF16)

    for layer in range(depth):
        g = norm_g[layer]
        if layer % 2 == 0:
            e = layer // 2
            w_in = rm_w_in[e]
            gate_cols = w_in.shape[1] - main_cols
            w_gate = jnp.pad(w_in[:, main_cols:], ((0, 0), (0, LANES - gate_cols))).astype(BF16)
            names = [("gu", layer), ("down", layer), ("rm_out", e)]
            casts = [row_bands(ffn_w_gu, layer), blocks(ffn_w_down, layer), row_bands(rm_w_out, e)]
            if layer + 1 < depth:
                o = (layer + 1) // 2
                names += [("gu", layer + 1), ("down", layer + 1), ("swa_in", o), ("swa_out", o)]
                casts += [row_bands(ffn_w_gu, layer + 1), blocks(ffn_w_down, layer + 1),
                          row_bands(swa_w_in, o), row_bands(swa_w_out, o)]
            proj, gates, *cast = _norm_matmul(xs, g[0], w_in.astype(BF16), w_gate, proj_tiles, ROW_TILE, PROJ_TN, casts)
            ready.update(zip(names, cast))
            gates_row = gates[:, :gate_cols].reshape(b, t, gate_cols).transpose(0, 2, 1)
            mixed = _mixer(proj.reshape(main_cols // PROJ_TN, b, t, PROJ_TN), gates_row, ml_gate_b[e], ml_conv_w[e],
                           rm_head_g[e])
            xs = _proj_norm_res(mixed.reshape(n, -1), bf16_weight("rm_out", rm_w_out, e), xs, g[1], OUT_ROW_TILE)
        else:
            o = layer // 2
            q, kv = _swa_in_proj(xs, g[0], bf16_weight("swa_in", swa_w_in, o), b, t, ROW_TILE, PROJ_ROW_BLOCK)
            att = _swa(q, kv.reshape(b, t, -1), swa_sinks[o], rel_bias)
            xs = _proj_norm_res(att.reshape(n, -1), bf16_weight("swa_out", swa_w_out, o), xs, g[1], OUT_ROW_TILE)
        xs = _ffn(xs, g[2], bf16_weight("gu", ffn_w_gu, layer), bf16_weight("down", ffn_w_down, layer), g[3],
                  ROW_TILE, FFN_TF, FFN_ROW_BLOCK)
    return xs.reshape(b, t, d)
```

```python
import functools
import math

import jax
import jax.numpy as jnp
from jax import lax
from jax.experimental import pallas as pl
from jax.experimental.pallas import tpu as pltpu

F32 = jnp.float32
BF16 = jnp.bfloat16

EPS = 1e-6
CHUNK = 128
RET_HEADS = 4
ML_HEADS = 4
HEAD_W = 256
CONV_W = 4
ROPE_BASE = 10000.0
ATT_HEADS = 32
KV_HEADS = 4
HEAD_DIM = 64
WINDOW = 128
ATT_BLOCK = 128
N_BUCKETS = 32
MAX_DIST = 128
LANES = 128
SUBLANES = 8
VMEM_LIMIT = 56 * 1024 * 1024

ROW_TILE = 1024
OUT_ROW_TILE = 512
PROJ_TN = 1024
PROJ_ROW_BLOCK = 256
MIXER_CHUNKS = 2
MIXER_SEQS = 2
SWA_BLOCKS = 8
FFN_TF = 512
FFN_ROW_BLOCK = 512


def _params(vmem=VMEM_LIMIT):
    return pltpu.CompilerParams(vmem_limit_bytes=vmem)


def _rms_rows(x, g):
    ms = jnp.mean(x * x, axis=-1, keepdims=True)
    return x * lax.rsqrt(ms + EPS) * g


def _silu(x):
    return x * jax.nn.sigmoid(x)


def _cast_specs(casts):
    in_specs = [pl.BlockSpec((None, *blk), (lambda *ids, layer=layer, idx=idx: (layer, *idx(*ids))))
                for _, layer, blk, idx in casts]
    out_specs = [pl.BlockSpec(blk, idx) for _, _, blk, idx in casts]
    out_shape = [jax.ShapeDtypeStruct(src.shape[1:], BF16) for src, _, _, _ in casts]
    return in_specs, out_specs, out_shape


def _cast_pieces(src_refs, dst_refs):
    for src_ref, dst_ref in zip(src_refs, dst_refs):
        dst_ref[...] = src_ref[...].astype(dst_ref.dtype)


def _norm_matmul_body(x_ref, g_ref, w_ref, ws_ref, *rest, row_block, n_cast):
    cast_src, (o_ref, os_ref), cast_dst, h_ref = rest[:n_cast], rest[n_cast:n_cast + 2], rest[n_cast + 2:-1], rest[-1]
    j = pl.program_id(1)

    @pl.when(j == 0)
    def _():
        _cast_pieces(cast_src, cast_dst)
        for r in range(x_ref.shape[0] // row_block):
            rows = slice(r * row_block, (r + 1) * row_block)
            h = _rms_rows(x_ref[rows, :], g_ref[...]).astype(h_ref.dtype)
            h_ref[rows, :] = h
            os_ref[rows, :] = jnp.dot(h, ws_ref[...], preferred_element_type=F32)
            o_ref[rows, :] = jnp.dot(h, w_ref[...], preferred_element_type=F32).astype(o_ref.dtype)

    @pl.when(j > 0)
    def _():
        _cast_pieces(cast_src, cast_dst)
        o_ref[...] = jnp.dot(h_ref[...], w_ref[...], preferred_element_type=F32).astype(o_ref.dtype)


def _norm_matmul(x, g, w, w_side, tiles, tm, tn, casts):
    n, d = x.shape
    ns = w_side.shape[1]
    cast_in, cast_out, cast_shape = _cast_specs(casts)
    return pl.pallas_call(
        functools.partial(_norm_matmul_body, row_block=PROJ_ROW_BLOCK, n_cast=len(casts)),
        grid=(n // tm, tiles),
        in_specs=[
            pl.BlockSpec((tm, d), lambda i, j: (i, 0)),
            pl.BlockSpec((1, d), lambda i, j: (0, 0)),
            pl.BlockSpec((d, tn), lambda i, j: (0, j)),
            pl.BlockSpec((d, ns), lambda i, j: (0, 0)),
            *cast_in,
        ],
        out_specs=[pl.BlockSpec((None, tm, tn), lambda i, j: (j, i, 0)),
                   pl.BlockSpec((tm, ns), lambda i, j: (i, 0)),
                   *cast_out],
        out_shape=[jax.ShapeDtypeStruct((tiles, n, tn), BF16), jax.ShapeDtypeStruct((n, ns), F32), *cast_shape],
        scratch_shapes=[pltpu.VMEM((tm, d), BF16)],
        compiler_params=_params(),
        name="norm_matmul",
    )(x, g.reshape(1, d), w, w_side, *[c[0] for c in casts])


def _proj_norm_res_body(a_ref, w_ref, x_ref, g_ref, o_ref):
    y = jnp.dot(a_ref[...], w_ref[...], preferred_element_type=F32)
    o_ref[...] = x_ref[...] + _rms_rows(y, g_ref[...])


def _proj_norm_res(a, w, x, g, tm):
    n, d = x.shape
    return pl.pallas_call(
        _proj_norm_res_body,
        grid=(n // tm,),
        in_specs=[
            pl.BlockSpec((tm, a.shape[1]), lambda i: (i, 0)),
            pl.BlockSpec(w.shape, lambda i: (0, 0)),
            pl.BlockSpec((tm, d), lambda i: (i, 0)),
            pl.BlockSpec((1, d), lambda i: (0, 0)),
        ],
        out_specs=pl.BlockSpec((tm, d), lambda i: (i, 0)),
        out_shape=jax.ShapeDtypeStruct((n, d), F32),
        compiler_params=_params(),
        name="proj_norm_res",
    )(a, w, x, g.reshape(1, d))


def _ffn_body(x_ref, g_in_ref, wg_ref, wu_ref, wd_ref, g_out_ref, o_ref, h_ref, *, row_block):
    f = pl.program_id(1)
    last = pl.num_programs(1) - 1
    blocks = [slice(r * row_block, (r + 1) * row_block) for r in range(x_ref.shape[0] // row_block)]

    def partial_down(rows):
        h = h_ref[rows, :]
        gate = jnp.dot(h, wg_ref[...], preferred_element_type=F32)
        up = jnp.dot(h, wu_ref[...], preferred_element_type=F32)
        act = (_silu(gate) * up).astype(BF16)
        return jnp.dot(act, wd_ref[...], preferred_element_type=F32)

    @pl.when(f == 0)
    def _():
        for rows in blocks:
            h_ref[rows, :] = _rms_rows(x_ref[rows, :], g_in_ref[...]).astype(h_ref.dtype)
            o_ref[rows, :] = partial_down(rows)

    @pl.when((f > 0) & (f < last))
    def _():
        for rows in blocks:
            o_ref[rows, :] += partial_down(rows)

    @pl.when(f == last)
    def _():
        for rows in blocks:
            y = o_ref[rows, :] + partial_down(rows)
            o_ref[rows, :] = x_ref[rows, :] + _rms_rows(y, g_out_ref[...])


def _ffn(x, g_in, w_gu, w_down, g_out, tm, tf, row_block):
    n, d = x.shape
    nf = w_down.shape[0] // tf
    return pl.pallas_call(
        functools.partial(_ffn_body, row_block=row_block),
        grid=(n // tm, nf),
        in_specs=[
            pl.BlockSpec((tm, d), lambda i, f: (i, 0)),
            pl.BlockSpec((1, d), lambda i, f: (0, 0)),
            pl.BlockSpec((d, tf), lambda i, f: (0, f)),
            pl.BlockSpec((d, tf), lambda i, f: (0, f + nf)),
            pl.BlockSpec((tf, d), lambda i, f: (f, 0)),
            pl.BlockSpec((1, d), lambda i, f: (0, 0)),
        ],
        out_specs=pl.BlockSpec((tm, d), lambda i, f: (i, 0)),
        out_shape=jax.ShapeDtypeStruct((n, d), F32),
        scratch_shapes=[pltpu.VMEM((tm, d), BF16)],
        compiler_params=_params(),
        name="ffn",
    )(x, g_in.reshape(1, d), w_gu, w_gu, w_down, g_out.reshape(1, d))


def _log_sigmoid(x):
    return jnp.minimum(x, 0.0) - jnp.log1p(jnp.exp(-jnp.abs(x)))


_NT = (((1,), (1,)), ((), ()))
_TN = (((0,), (0,)), ((), ()))


def _retention_head(h, bb, rows, q_ref, k_ref, v_ref, g_ref, cos, sin, dm_ref, wq_ref, wk_ref, gl_ref, hg_ref,
                    o_ref, state_ref):
    cols = slice(h * HEAD_W, (h + 1) * HEAD_W)
    half = HEAD_W // 2

    def rot(x):
        x = x.astype(F32)
        x1, x2 = x[:, :half], x[:, half:]
        return jnp.concatenate([x1 * cos - x2 * sin, x2 * cos + x1 * sin], axis=-1)

    q = rot(q_ref[bb, rows, cols])
    k = rot(k_ref[bb, rows, cols]) * (HEAD_W ** -0.5)
    v = v_ref[bb, rows, cols]
    qb = q.astype(BF16)
    s = lax.dot_general(qb, k.astype(BF16), _NT, preferred_element_type=F32) * dm_ref[h]
    intra = jnp.dot(s.astype(BF16), v, preferred_element_type=F32)
    si = bb * RET_HEADS + h
    state = state_ref[si]
    inter = jnp.dot(qb, state.astype(BF16), preferred_element_type=F32) * wq_ref[h]
    out = intra + inter
    kw = (k * wk_ref[h]).astype(BF16)
    state_ref[si] = state * gl_ref[h] + lax.dot_general(kw, v, _TN, preferred_element_type=F32)
    y = _rms_rows(out, hg_ref[:, cols])
    o_ref[bb, rows, cols] = (y * _silu(g_ref[bb, rows, cols].astype(F32))).astype(o_ref.dtype)


def _causal_conv(cur, prev, shift_ref, w):
    shifted = jnp.dot(shift_ref[...], jnp.concatenate([prev, cur], axis=0), preferred_element_type=F32)
    y = cur.astype(F32) * w[CONV_W - 1:CONV_W]
    for s in range(1, CONV_W):
        y = y + shifted[(s - 1) * CHUNK:s * CHUNK] * w[CONV_W - 1 - s:CONV_W - s]
    return y


def _mlstm_head(h, bb, rows, q_all, k_all, v_ref, og_ref, gx, b_rows, lower_tri, hg_ref, o_ref, c_ref, n_ref,
                m_ref):
    cols = slice(h * HEAD_W, (h + 1) * HEAD_W)
    q = q_all[:, cols]
    k = k_all[:, cols]
    v = v_ref[bb, rows, cols]
    si = bb * ML_HEADS + h

    i_row = gx[h:h + 1]
    lf_row = gx[ML_HEADS + h:ML_HEADS + h + 1]
    b_row = b_rows[ML_HEADS + h:ML_HEADS + h + 1]
    b_cols = lax.dot_general(lower_tri, jnp.broadcast_to(lf_row, (CHUNK, CHUNK)), _NT,
                             precision=lax.Precision.HIGHEST, preferred_element_type=F32)
    i_cols = jnp.broadcast_to(i_row, (CHUNK, CHUNK)).T
    rowi = lax.broadcasted_iota(jnp.int32, (CHUNK, CHUNK), 0)
    coli = lax.broadcasted_iota(jnp.int32, (CHUNK, CHUNK), 1)
    log_d = jnp.where(rowi >= coli, b_cols - b_row + i_row, -jnp.inf)
    m_prev = m_ref[si]
    inter_log = b_cols + m_prev
    m_t = jnp.maximum(inter_log, jnp.max(log_d, axis=-1, keepdims=True))
    d_mat = jnp.exp(log_d - m_t)
    w_inter = jnp.exp(inter_log - m_t)

    def wide(a):
        return jnp.concatenate([a] * (HEAD_W // LANES), axis=1)

    qb = q.astype(BF16)
    s = lax.dot_general(qb, k.astype(BF16), _NT, preferred_element_type=F32) * d_mat
    c_state = c_ref[si]
    n_state = n_ref[si]
    num = (jnp.dot(s.astype(BF16), v, preferred_element_type=F32)
           + wide(w_inter) * jnp.dot(qb, c_state.astype(BF16), preferred_element_type=F32))
    den = (jnp.sum(s, axis=-1, keepdims=True)
           + w_inter * jnp.sum(q * n_state, axis=-1, keepdims=True))
    hid = num / wide(jnp.maximum(jnp.abs(den), jnp.exp(-m_t)))

    b_last = b_cols[CHUNK - 1:CHUNK]
    log_w = b_last - b_cols + i_cols
    m_new = jnp.maximum(b_last + m_prev, jnp.max(log_w, axis=0, keepdims=True))
    w = jnp.exp(log_w - m_new)
    decay = wide(jnp.exp(b_last + m_prev - m_new))
    kw = k * wide(w)
    c_ref[si] = decay * c_state + lax.dot_general(kw.astype(BF16), v, _TN, preferred_element_type=F32)
    n_ref[si] = decay * n_state + jnp.sum(kw, axis=0, keepdims=True)
    m_ref[si] = m_new

    out_cols = slice(RET_HEADS * HEAD_W + h * HEAD_W, RET_HEADS * HEAD_W + (h + 1) * HEAD_W)
    y = _rms_rows(hid, hg_ref[:, out_cols])
    o_ref[bb, rows, out_cols] = (y * jax.nn.sigmoid(og_ref[bb, rows, cols].astype(F32))).astype(o_ref.dtype)


def _mixer_body(gl_ref, rq_ref, rk_ref, rv_ref, rg_ref, mq_ref, mqp_ref, mk_ref, mkp_ref, mv_ref, mo_ref,
                cos_ref, sin_ref, dm_ref, wq_ref, wk_ref, gr_ref, gb_ref, cw_ref, shift_ref, hg_ref,
                o_ref, r_ref, c_ref, n_ref, m_ref):
    first = pl.program_id(1) == 0

    @pl.when(first)
    def _():
        r_ref[...] = jnp.zeros_like(r_ref)
        c_ref[...] = jnp.zeros_like(c_ref)
        n_ref[...] = jnp.zeros_like(n_ref)
        m_ref[...] = jnp.zeros_like(m_ref)

    ml_w = ML_HEADS * HEAD_W
    rowi = lax.broadcasted_iota(jnp.int32, (CHUNK, CHUNK), 0)
    coli = lax.broadcasted_iota(jnp.int32, (CHUNK, CHUNK), 1)
    upper_tri = jnp.where(rowi <= coli, 1.0, 0.0).astype(F32)
    lower_tri = jnp.where(rowi >= coli, 1.0, 0.0).astype(F32)
    is_input = lax.broadcasted_iota(jnp.int32, (2 * ML_HEADS, CHUNK), 0) < ML_HEADS

    for bb in range(mq_ref.shape[0]):
        for ci in range(mq_ref.shape[1] // CHUNK):
            rows = slice(ci * CHUNK, (ci + 1) * CHUNK)
            cos = cos_ref[rows, :]
            sin = sin_ref[rows, :]
            for h in range(RET_HEADS):
                _retention_head(h, bb, rows, rq_ref, rk_ref, rv_ref, rg_ref, cos, sin, dm_ref, wq_ref, wk_ref,
                                gl_ref, hg_ref, o_ref, r_ref)

            if ci == 0:
                q_prev = jnp.where(first, jnp.zeros_like(mqp_ref[bb]), mqp_ref[bb])
                k_prev = jnp.where(first, jnp.zeros_like(mkp_ref[bb]), mkp_ref[bb])
            else:
                before = slice((ci - 1) * CHUNK, ci * CHUNK)
                q_prev, k_prev = mq_ref[bb, before, :], mk_ref[bb, before, :]
            q_all = _silu(_causal_conv(mq_ref[bb, rows, :], q_prev, shift_ref, cw_ref[:, :ml_w]))
            k_all = _silu(_causal_conv(mk_ref[bb, rows, :], k_prev, shift_ref, cw_ref[:, ml_w:])) * (HEAD_W ** -0.5)
            gates = gr_ref[bb, :, rows] + gb_ref[...]
            gx = jnp.where(is_input, gates, _log_sigmoid(gates))
            b_rows = jnp.dot(gx, upper_tri, precision=lax.Precision.HIGHEST, preferred_element_type=F32)
            for h in range(ML_HEADS):
                _mlstm_head(h, bb, rows, q_all, k_all, mv_ref, mo_ref, gx, b_rows, lower_tri, hg_ref, o_ref,
                            c_ref, n_ref, m_ref)


def _mixer(proj, gates_row, gate_b, conv_w, head_g):
    _, b, t, _ = proj.shape
    seqs = MIXER_SEQS
    step = MIXER_CHUNKS * CHUNK
    nc = t // step
    ret_w = RET_HEADS * HEAD_W
    ml_w = ML_HEADS * HEAD_W
    assert ret_w == ml_w == proj.shape[3]
    log_g = jnp.log1p(-jnp.exp2(-5.0 - jnp.arange(RET_HEADS, dtype=F32)))
    idx = jnp.arange(CHUNK, dtype=F32)
    diff = idx[:, None] - idx[None, :]
    dmask = jnp.where(diff >= 0, jnp.exp(log_g[:, None, None] * jnp.maximum(diff, 0.0)), 0.0)
    w_k = jnp.exp(log_g[:, None] * (CHUNK - 1.0 - idx)[None, :])
    w_q = jnp.exp(log_g[:, None] * (idx + 1.0)[None, :])
    w_k = jnp.broadcast_to(w_k[..., None], (RET_HEADS, CHUNK, HEAD_W))
    w_q = jnp.broadcast_to(w_q[..., None], (RET_HEADS, CHUNK, HEAD_W))
    g_l = jnp.exp(log_g * CHUNK)
    half = HEAD_W // 2
    inv = 1.0 / (ROPE_BASE ** jnp.linspace(0.0, 1.0, half, dtype=F32))
    ang = jnp.arange(t).astype(F32)[:, None] * inv[None, :]
    cos, sin = jnp.cos(ang), jnp.sin(ang)
    r = jnp.arange((CONV_W - 1) * CHUNK)
    src = CHUNK + r % CHUNK - (r // CHUNK + 1)
    shift = (jnp.arange(2 * CHUNK)[None, :] == src[:, None]).astype(BF16)

    def group(gi):
        return pl.BlockSpec((None, seqs, step, ret_w), lambda bi, c: (gi, bi, c, 0))

    def prev_group(gi):
        return pl.BlockSpec((None, seqs, CHUNK, ml_w),
                            lambda bi, c: (gi, bi, jnp.maximum(c * MIXER_CHUNKS - 1, 0), 0))

    def whole(a):
        return pl.BlockSpec(a.shape, lambda bi, c: (0,) * a.ndim)

    gb = gate_b.reshape(2 * ML_HEADS, 1)
    hg = head_g.reshape(1, ret_w + ml_w)
    return pl.pallas_call(
        _mixer_body,
        grid=(b // seqs, nc),
        in_specs=[
            pl.BlockSpec(memory_space=pltpu.SMEM),
            group(0), group(1), group(2), group(3),
            group(4), prev_group(4), group(5), prev_group(5), group(6), group(7),
            pl.BlockSpec((step, half), lambda bi, c: (c, 0)),
            pl.BlockSpec((step, half), lambda bi, c: (c, 0)),
            whole(dmask), whole(w_q), whole(w_k),
            pl.BlockSpec((seqs, 2 * ML_HEADS, step), lambda bi, c: (bi, 0, c)),
            whole(gb), whole(conv_w), whole(shift), whole(hg),
        ],
        out_specs=pl.BlockSpec((seqs, step, ret_w + ml_w), lambda bi, c: (bi, c, 0)),
        out_shape=jax.ShapeDtypeStruct((b, t, ret_w + ml_w), BF16),
        scratch_shapes=[pltpu.VMEM((seqs * RET_HEADS, HEAD_W, HEAD_W), F32),
                        pltpu.VMEM((seqs * ML_HEADS, HEAD_W, HEAD_W), F32),
                        pltpu.VMEM((seqs * ML_HEADS, 1, HEAD_W), F32),
                        pltpu.VMEM((seqs * ML_HEADS, 1, LANES), F32)],
        compiler_params=_params(),
        name="mixer",
    )(g_l, proj, proj, proj, proj, proj, proj, proj, proj, proj, proj, cos, sin, dmask, w_q, w_k,
      gates_row, gb, conv_w, shift, hg)


PAIR_W = 2 * HEAD_DIM
GROUP = ATT_HEADS // KV_HEADS
PAIRS = GROUP // 2
LOG2E = math.log2(math.e)
Q_PRESCALE = HEAD_DIM ** -0.5 * LOG2E


def _swa_in_proj_body(x_ref, g_ref, w_ref, q_ref, kv_ref, *, row_block):
    qw = ATT_HEADS * HEAD_DIM
    per_block = row_block // ATT_BLOCK
    for r in range(x_ref.shape[0] // row_block):
        rows = slice(r * row_block, (r + 1) * row_block)
        h = _rms_rows(x_ref[rows, :], g_ref[...]).astype(BF16)
        res = jnp.dot(h, w_ref[...], preferred_element_type=F32)
        for nbl in range(per_block):
            sub = slice(nbl * ATT_BLOCK, (nbl + 1) * ATT_BLOCK)
            for p in range(q_ref.shape[2]):
                q_ref[0, r * per_block + nbl, p] = (res[sub, p * PAIR_W:(p + 1) * PAIR_W]
                                                    * Q_PRESCALE).astype(q_ref.dtype)
        kv_ref[rows, :] = res[:, qw:].astype(kv_ref.dtype)


def _swa_in_proj(x, g, w, b, t, tm, row_block):
    n, d = x.shape
    cols = w.shape[1]
    kvw = cols - ATT_HEADS * HEAD_DIM
    tiles_per_seq = t // tm
    return pl.pallas_call(
        functools.partial(_swa_in_proj_body, row_block=row_block),
        grid=(n // tm,),
        in_specs=[
            pl.BlockSpec((tm, d), lambda i: (i, 0)),
            pl.BlockSpec((1, d), lambda i: (0, 0)),
            pl.BlockSpec((d, cols), lambda i: (0, 0)),
        ],
        out_specs=[
            pl.BlockSpec((1, tm // ATT_BLOCK, ATT_HEADS // 2, ATT_BLOCK, PAIR_W),
                         lambda i: (i // tiles_per_seq, i % tiles_per_seq, 0, 0, 0)),
            pl.BlockSpec((tm, kvw), lambda i: (i, 0)),
        ],
        out_shape=[jax.ShapeDtypeStruct((b, t // ATT_BLOCK, ATT_HEADS // 2, ATT_BLOCK, PAIR_W), BF16),
                   jax.ShapeDtypeStruct((n, kvw), BF16)],
        compiler_params=_params(),
        name="swa_in_proj",
    )(x, g.reshape(1, d), w)


def _swa_body(rb_ref, sink_ref, idx_ref, q_ref, kp_ref, kc_ref, vp_ref, vc_ref, o_ref, bias_ref, sinkcol_ref):
    bi = pl.program_id(0)
    nb = pl.program_id(1)
    blk = ATT_BLOCK
    rows = PAIRS * blk

    @pl.when((bi == 0) & (nb == 0))
    def _():
        idx = idx_ref[...]
        col = lax.broadcasted_iota(jnp.int32, idx.shape, 1)

        def head_body(hd, carry):
            def bucket_body(bk, acc):
                return jnp.where(idx == bk, rb_ref[bk * ATT_HEADS + hd] * LOG2E, acc)

            tbl = lax.fori_loop(0, N_BUCKETS, bucket_body, jnp.full(idx.shape, -jnp.inf, F32))
            kv = hd // GROUP
            half = hd % 2
            at = pl.ds(pl.multiple_of(((hd % GROUP) // 2) * blk, blk), blk)
            bias_ref[0, kv, half, at, :] = tbl
            bias_ref[1, kv, half, at, :] = jnp.where(col >= blk, tbl, -jnp.inf)
            sinkcol_ref[kv, half, at, :] = jnp.full((blk, LANES), sink_ref[hd] * LOG2E, F32)
            return carry

        lax.fori_loop(0, ATT_HEADS, head_body, 0)

    low = lax.broadcasted_iota(jnp.int32, (2 * blk, LANES), 1) < HEAD_DIM
    low_out = lax.broadcasted_iota(jnp.int32, (rows, LANES), 1) < HEAD_DIM

    def halves(xf, kv):
        xc = xf[:, (kv // 2) * LANES:(kv // 2 + 1) * LANES]
        xr = pltpu.roll(xc, HEAD_DIM, axis=1)
        lo_src, hi_src = (xc, xr) if kv % 2 == 0 else (xr, xc)
        return (jnp.where(low, lo_src, 0.0).astype(BF16), jnp.where(low, 0.0, hi_src).astype(BF16))

    for sb in range(q_ref.shape[0]):
        here = slice(sb * blk, (sb + 1) * blk)
        before = slice((sb - 1) * blk, sb * blk)
        k_prev, v_prev = (kp_ref[0], vp_ref[0]) if sb == 0 else (kc_ref[0, before, :], vc_ref[0, before, :])
        kf = jnp.concatenate([k_prev, kc_ref[0, here, :]], axis=0).astype(F32)
        vf = jnp.concatenate([v_prev, vc_ref[0, here, :]], axis=0).astype(F32)
        first = (nb == 0).astype(jnp.int32) if sb == 0 else 0
        for kv in range(KV_HEADS):
            k_lo, k_hi = halves(kf, kv)
            v_lo, v_hi = halves(vf, kv)
            q4 = q_ref[sb, kv * PAIRS:(kv + 1) * PAIRS].reshape(rows, PAIR_W)

            def probs(k_half, half):
                s = lax.dot_general(q4, k_half, (((1,), (1,)), ((), ())), preferred_element_type=F32)
                s = s + bias_ref[first, kv, half]
                sink = sinkcol_ref[kv, half]
                mx = jnp.maximum(jnp.max(s, axis=-1, keepdims=True), sink)
                p = jnp.exp2(s - jnp.concatenate([mx, mx], axis=1))
                den = jnp.sum(p, axis=-1, keepdims=True) + jnp.exp2(sink - mx)
                return p.astype(BF16), den

            p_lo, den_lo = probs(k_lo, 0)
            p_hi, den_hi = probs(k_hi, 1)
            o4 = (jnp.dot(p_lo, v_lo, preferred_element_type=F32)
                  + jnp.dot(p_hi, v_hi, preferred_element_type=F32)) / jnp.where(low_out, den_lo, den_hi)
            for pr in range(PAIRS):
                at = (kv * PAIRS + pr) * PAIR_W
                o_ref[0, here, at:at + PAIR_W] = o4[pr * blk:(pr + 1) * blk].astype(o_ref.dtype)


def _t5_bucket(dist):
    n = jnp.maximum(dist, 0)
    max_exact = N_BUCKETS // 2
    nf = jnp.maximum(n, 1).astype(F32)
    large = max_exact + (jnp.log(nf / max_exact) / math.log(MAX_DIST / max_exact)
                         * (N_BUCKETS - max_exact)).astype(jnp.int32)
    large = jnp.minimum(large, N_BUCKETS - 1)
    return jnp.where(n < max_exact, n, large)


def _swa(q, kv, sinks, rel_bias):
    b, nblk, n_pairs, blk, _ = q.shape
    t = nblk * blk
    qw = ATT_HEADS * HEAD_DIM
    kvw = KV_HEADS * HEAD_DIM
    i = jnp.arange(blk)
    j = jnp.arange(2 * blk)
    dist = (blk + i)[:, None] - j[None, :]
    idx = jnp.where((dist >= 0) & (dist < WINDOW), _t5_bucket(dist), -1).astype(jnp.int32)
    sb = SWA_BLOCKS
    return pl.pallas_call(
        _swa_body,
        grid=(b, nblk // sb),
        in_specs=[
            pl.BlockSpec(memory_space=pltpu.SMEM),
            pl.BlockSpec(memory_space=pltpu.SMEM),
            pl.BlockSpec((blk, 2 * blk), lambda bi, nb: (0, 0)),
            pl.BlockSpec((None, sb, n_pairs, blk, PAIR_W), lambda bi, nb: (bi, nb, 0, 0, 0)),
            pl.BlockSpec((1, blk, kvw), lambda bi, nb: (bi, jnp.maximum(nb * sb - 1, 0), 0)),
            pl.BlockSpec((1, sb * blk, kvw), lambda bi, nb: (bi, nb, 0)),
            pl.BlockSpec((1, blk, kvw), lambda bi, nb: (bi, jnp.maximum(nb * sb - 1, 0), 1)),
            pl.BlockSpec((1, sb * blk, kvw), lambda bi, nb: (bi, nb, 1)),
        ],
        out_specs=pl.BlockSpec((1, sb * blk, qw), lambda bi, nb: (bi, nb, 0)),
        out_shape=jax.ShapeDtypeStruct((b, t, qw), BF16),
        scratch_shapes=[pltpu.VMEM((2, KV_HEADS, 2, PAIRS * blk, 2 * blk), F32),
                        pltpu.VMEM((KV_HEADS, 2, PAIRS * blk, LANES), F32)],
        compiler_params=_params(),
        name="swa",
    )(rel_bias.reshape(-1), sinks, idx, q, kv, kv, kv, kv)


def kernel(x, rel_bias, norm_g, ffn_w_gu, ffn_w_down, rm_w_in, ml_conv_w, ml_gate_b, rm_head_g,
           rm_w_out, swa_w_in, swa_sinks, swa_w_out):
    b, t, d = x.shape
    n = b * t
    depth = norm_g.shape[0]
    main_cols = 4 * RET_HEADS * HEAD_W + 4 * ML_HEADS * HEAD_W
    xs = x.reshape(n, d)
    row_tiles = n // ROW_TILE
    proj_tiles = main_cols // PROJ_TN

    def row_bands(src, layer):
        rows = src.shape[1] // (row_tiles * proj_tiles)
        return src, layer, (rows, src.shape[2]), lambda i, j: (i * proj_tiles + j, 0)

    def blocks(src, layer):
        return src, layer, (src.shape[1] // row_tiles, src.shape[2] // proj_tiles), lambda i, j: (i, j)

    ready = {}

    def bf16_weight(name, src, layer):
        if (name, layer) in ready:
            return ready.pop((name, layer))
        return src[layer].astype(BF16)

    for layer in range(depth):
        g = norm_g[layer]
        if layer % 2 == 0:
            e = layer // 2
            w_in = rm_w_in[e]
            gate_cols = w_in.shape[1] - main_cols
            w_gate = jnp.pad(w_in[:, main_cols:], ((0, 0), (0, LANES - gate_cols))).astype(BF16)
            names = [("gu", layer), ("down", layer), ("rm_out", e)]
            casts = [row_bands(ffn_w_gu, layer), blocks(ffn_w_down, layer), row_bands(rm_w_out, e)]
            if layer + 1 < depth:
                o = (layer + 1) // 2
                names += [("gu", layer + 1), ("down", layer + 1), ("swa_in", o), ("swa_out", o)]
                casts += [row_bands(ffn_w_gu, layer + 1), blocks(ffn_w_down, layer + 1),
                          row_bands(swa_w_in, o), row_bands(swa_w_out, o)]
            proj, gates, *cast = _norm_matmul(xs, g[0], w_in.astype(BF16), w_gate, proj_tiles, ROW_TILE, PROJ_TN, casts)
            ready.update(zip(names, cast))
            gates_row = gates[:, :gate_cols].reshape(b, t, gate_cols).transpose(0, 2, 1)
            mixed = _mixer(proj.reshape(main_cols // PROJ_TN, b, t, PROJ_TN), gates_row, ml_gate_b[e], ml_conv_w[e],
                           rm_head_g[e])
            xs = _proj_norm_res(mixed.reshape(n, -1), bf16_weight("rm_out", rm_w_out, e), xs, g[1], OUT_ROW_TILE)
        else:
            o = layer // 2
            q, kv = _swa_in_proj(xs, g[0], bf16_weight("swa_in", swa_w_in, o), b, t, ROW_TILE, PROJ_ROW_BLOCK)
            att = _swa(q, kv.reshape(b, t, -1), swa_sinks[o], rel_bias)
            xs = _proj_norm_res(att.reshape(n, -1), bf16_weight("swa_out", swa_w_out, o), xs, g[1], OUT_ROW_TILE)
        xs = _ffn(xs, g[2], bf16_weight("gu", ffn_w_gu, layer), bf16_weight("down", ffn_w_down, layer), g[3],
                  ROW_TILE, FFN_TF, FFN_ROW_BLOCK)
    return xs.reshape(b, t, d)
```

```python
import functools
import math

import jax
import jax.numpy as jnp
from jax import lax
from jax.experimental import pallas as pl
from jax.experimental.pallas import tpu as pltpu

F32 = jnp.float32
BF16 = jnp.bfloat16

EPS = 1e-6
CHUNK = 128
RET_HEADS = 4
ML_HEADS = 4
HEAD_W = 256
CONV_W = 4
ROPE_BASE = 10000.0
ATT_HEADS = 32
KV_HEADS = 4
HEAD_DIM = 64
WINDOW = 128
ATT_BLOCK = 128
N_BUCKETS = 32
MAX_DIST = 128
LANES = 128
SUBLANES = 8
VMEM_LIMIT = 56 * 1024 * 1024

ROW_TILE = 1024
OUT_ROW_TILE = 512
PROJ_TN = 1024
PROJ_ROW_BLOCK = 256
MIXER_CHUNKS = 2
MIXER_SEQS = 2
SWA_BLOCKS = 4
FFN_TF = 512
FFN_ROW_BLOCK = 512


def _params(vmem=VMEM_LIMIT):
    return pltpu.CompilerParams(vmem_limit_bytes=vmem)


def _rms_rows(x, g):
    ms = jnp.mean(x * x, axis=-1, keepdims=True)
    return x * lax.rsqrt(ms + EPS) * g


def _silu(x):
    return x * jax.nn.sigmoid(x)


def _cast_specs(casts):
    in_specs = [pl.BlockSpec((None, *blk), (lambda *ids, layer=layer, idx=idx: (layer, *idx(*ids))))
                for _, layer, blk, idx in casts]
    out_specs = [pl.BlockSpec(blk, idx) for _, _, blk, idx in casts]
    out_shape = [jax.ShapeDtypeStruct(src.shape[1:], BF16) for src, _, _, _ in casts]
    return in_specs, out_specs, out_shape


def _cast_pieces(src_refs, dst_refs):
    for src_ref, dst_ref in zip(src_refs, dst_refs):
        dst_ref[...] = src_ref[...].astype(dst_ref.dtype)


def _norm_matmul_body(x_ref, g_ref, w_ref, ws_ref, *rest, row_block, n_cast):
    cast_src, (o_ref, os_ref), cast_dst, h_ref = rest[:n_cast], rest[n_cast:n_cast + 2], rest[n_cast + 2:-1], rest[-1]
    j = pl.program_id(1)

    @pl.when(j == 0)
    def _():
        _cast_pieces(cast_src, cast_dst)
        for r in range(x_ref.shape[0] // row_block):
            rows = slice(r * row_block, (r + 1) * row_block)
            h = _rms_rows(x_ref[rows, :], g_ref[...]).astype(h_ref.dtype)
            h_ref[rows, :] = h
            os_ref[:, rows] = lax.dot_general(ws_ref[...], h, _NT, preferred_element_type=F32)
            o_ref[rows, :] = jnp.dot(h, w_ref[...], preferred_element_type=F32).astype(o_ref.dtype)

    @pl.when(j > 0)
    def _():
        _cast_pieces(cast_src, cast_dst)
        o_ref[...] = jnp.dot(h_ref[...], w_ref[...], preferred_element_type=F32).astype(o_ref.dtype)


def _norm_matmul(x, g, w, w_side, tiles, tm, tn, casts):
    n, d = x.shape
    ns = w_side.shape[0]
    cast_in, cast_out, cast_shape = _cast_specs(casts)
    return pl.pallas_call(
        functools.partial(_norm_matmul_body, row_block=PROJ_ROW_BLOCK, n_cast=len(casts)),
        grid=(n // tm, tiles),
        in_specs=[
            pl.BlockSpec((tm, d), lambda i, j: (i, 0)),
            pl.BlockSpec((1, d), lambda i, j: (0, 0)),
            pl.BlockSpec((d, tn), lambda i, j: (0, j)),
            pl.BlockSpec((ns, d), lambda i, j: (0, 0)),
            *cast_in,
        ],
        out_specs=[pl.BlockSpec((None, tm, tn), lambda i, j: (j, i, 0)),
                   pl.BlockSpec((ns, tm), lambda i, j: (0, i)),
                   *cast_out],
        out_shape=[jax.ShapeDtypeStruct((tiles, n, tn), BF16), jax.ShapeDtypeStruct((ns, n), F32), *cast_shape],
        scratch_shapes=[pltpu.VMEM((tm, d), BF16)],
        compiler_params=_params(),
        name="norm_matmul",
    )(x, g.reshape(1, d), w, w_side, *[c[0] for c in casts])


def _proj_norm_res_body(a_ref, w_ref, x_ref, g_ref, o_ref):
    y = jnp.dot(a_ref[...], w_ref[...], preferred_element_type=F32)
    o_ref[...] = x_ref[...] + _rms_rows(y, g_ref[...])


def _proj_norm_res(a, w, x, g, tm):
    n, d = x.shape
    return pl.pallas_call(
        _proj_norm_res_body,
        grid=(n // tm,),
        in_specs=[
            pl.BlockSpec((tm, a.shape[1]), lambda i: (i, 0)),
            pl.BlockSpec(w.shape, lambda i: (0, 0)),
            pl.BlockSpec((tm, d), lambda i: (i, 0)),
            pl.BlockSpec((1, d), lambda i: (0, 0)),
        ],
        out_specs=pl.BlockSpec((tm, d), lambda i: (i, 0)),
        out_shape=jax.ShapeDtypeStruct((n, d), F32),
        compiler_params=_params(),
        name="proj_norm_res",
    )(a, w, x, g.reshape(1, d))


def _ffn_body(x_ref, g_in_ref, wg_ref, wu_ref, wd_ref, g_out_ref, o_ref, h_ref, *, row_block):
    f = pl.program_id(1)
    last = pl.num_programs(1) - 1
    blocks = [slice(r * row_block, (r + 1) * row_block) for r in range(x_ref.shape[0] // row_block)]

    def partial_down(rows):
        h = h_ref[rows, :]
        gate = jnp.dot(h, wg_ref[...], preferred_element_type=F32)
        up = jnp.dot(h, wu_ref[...], preferred_element_type=F32)
        act = (_silu(gate) * up).astype(BF16)
        return jnp.dot(act, wd_ref[...], preferred_element_type=F32)

    @pl.when(f == 0)
    def _():
        for rows in blocks:
            h_ref[rows, :] = _rms_rows(x_ref[rows, :], g_in_ref[...]).astype(h_ref.dtype)
            o_ref[rows, :] = partial_down(rows)

    @pl.when((f > 0) & (f < last))
    def _():
        for rows in blocks:
            o_ref[rows, :] += partial_down(rows)

    @pl.when(f == last)
    def _():
        for rows in blocks:
            y = o_ref[rows, :] + partial_down(rows)
            o_ref[rows, :] = x_ref[rows, :] + _rms_rows(y, g_out_ref[...])


def _ffn(x, g_in, w_gu, w_down, g_out, tm, tf, row_block):
    n, d = x.shape
    nf = w_down.shape[0] // tf
    return pl.pallas_call(
        functools.partial(_ffn_body, row_block=row_block),
        grid=(n // tm, nf),
        in_specs=[
            pl.BlockSpec((tm, d), lambda i, f: (i, 0)),
            pl.BlockSpec((1, d), lambda i, f: (0, 0)),
            pl.BlockSpec((d, tf), lambda i, f: (0, f)),
            pl.BlockSpec((d, tf), lambda i, f: (0, f + nf)),
            pl.BlockSpec((tf, d), lambda i, f: (f, 0)),
            pl.BlockSpec((1, d), lambda i, f: (0, 0)),
        ],
        out_specs=pl.BlockSpec((tm, d), lambda i, f: (i, 0)),
        out_shape=jax.ShapeDtypeStruct((n, d), F32),
        scratch_shapes=[pltpu.VMEM((tm, d), BF16)],
        compiler_params=_params(),
        name="ffn",
    )(x, g_in.reshape(1, d), w_gu, w_gu, w_down, g_out.reshape(1, d))


def _log_sigmoid(x):
    return jnp.minimum(x, 0.0) - jnp.log1p(jnp.exp(-jnp.abs(x)))


_NT = (((1,), (1,)), ((), ()))
_TN = (((0,), (0,)), ((), ()))


def _retention_head(h, bb, rows, q_ref, k_ref, v_ref, g_ref, cos, sin, dm_ref, wq_ref, wk_ref, gl_ref, hg_ref,
                    o_ref, state_ref):
    cols = slice(h * HEAD_W, (h + 1) * HEAD_W)
    half = HEAD_W // 2

    def rot(x):
        x = x.astype(F32)
        x1, x2 = x[:, :half], x[:, half:]
        return jnp.concatenate([x1 * cos - x2 * sin, x2 * cos + x1 * sin], axis=-1)

    q = rot(q_ref[bb, rows, cols])
    k = rot(k_ref[bb, rows, cols]) * (HEAD_W ** -0.5)
    v = v_ref[bb, rows, cols]
    qb = q.astype(BF16)
    s = lax.dot_general(qb, k.astype(BF16), _NT, preferred_element_type=F32) * dm_ref[h]
    intra = jnp.dot(s.astype(BF16), v, preferred_element_type=F32)
    si = bb * RET_HEADS + h
    state = state_ref[si]
    inter = jnp.dot(qb, state.astype(BF16), preferred_element_type=F32) * wq_ref[h]
    out = intra + inter
    kw = (k * wk_ref[h]).astype(BF16)
    state_ref[si] = state * gl_ref[h] + lax.dot_general(kw, v, _TN, preferred_element_type=F32)
    y = _rms_rows(out, hg_ref[:, cols])
    o_ref[bb, rows, cols] = (y * _silu(g_ref[bb, rows, cols].astype(F32))).astype(o_ref.dtype)


def _causal_conv(cur, prev, shift_ref, w):
    shifted = jnp.dot(shift_ref[...], jnp.concatenate([prev, cur], axis=0), preferred_element_type=F32)
    y = cur.astype(F32) * w[CONV_W - 1:CONV_W]
    for s in range(1, CONV_W):
        y = y + shifted[(s - 1) * CHUNK:s * CHUNK] * w[CONV_W - 1 - s:CONV_W - s]
    return y


def _mlstm_head(h, bb, rows, q_all, k_all, v_ref, og_ref, gx, b_rows, lower_tri, hg_ref, o_ref, c_ref, n_ref,
                m_ref):
    cols = slice(h * HEAD_W, (h + 1) * HEAD_W)
    q = q_all[:, cols]
    k = k_all[:, cols]
    v = v_ref[bb, rows, cols]
    si = bb * ML_HEADS + h

    i_row = gx[h:h + 1]
    lf_row = gx[ML_HEADS + h:ML_HEADS + h + 1]
    b_row = b_rows[ML_HEADS + h:ML_HEADS + h + 1]
    b_cols = lax.dot_general(lower_tri, jnp.broadcast_to(lf_row, (CHUNK, CHUNK)), _NT,
                             precision=lax.Precision.HIGHEST, preferred_element_type=F32)
    i_cols = jnp.broadcast_to(i_row, (CHUNK, CHUNK)).T
    rowi = lax.broadcasted_iota(jnp.int32, (CHUNK, CHUNK), 0)
    coli = lax.broadcasted_iota(jnp.int32, (CHUNK, CHUNK), 1)
    log_d = jnp.where(rowi >= coli, b_cols - b_row + i_row, -jnp.inf)
    m_prev = m_ref[si]
    inter_log = b_cols + m_prev
    m_t = jnp.maximum(inter_log, jnp.max(log_d, axis=-1, keepdims=True))
    d_mat = jnp.exp(log_d - m_t)
    w_inter = jnp.exp(inter_log - m_t)

    def wide(a):
        return jnp.concatenate([a] * (HEAD_W // LANES), axis=1)

    qb = q.astype(BF16)
    s = lax.dot_general(qb, k.astype(BF16), _NT, preferred_element_type=F32) * d_mat
    c_state = c_ref[si]
    n_state = n_ref[si]
    num = (jnp.dot(s.astype(BF16), v, preferred_element_type=F32)
           + wide(w_inter) * jnp.dot(qb, c_state.astype(BF16), preferred_element_type=F32))
    den = (jnp.sum(s, axis=-1, keepdims=True)
           + w_inter * jnp.sum(q * n_state, axis=-1, keepdims=True))
    hid = num / wide(jnp.maximum(jnp.abs(den), jnp.exp(-m_t)))

    b_last = b_cols[CHUNK - 1:CHUNK]
    log_w = b_last - b_cols + i_cols
    m_new = jnp.maximum(b_last + m_prev, jnp.max(log_w, axis=0, keepdims=True))
    w = jnp.exp(log_w - m_new)
    decay = wide(jnp.exp(b_last + m_prev - m_new))
    kw = k * wide(w)
    c_ref[si] = decay * c_state + lax.dot_general(kw.astype(BF16), v, _TN, preferred_element_type=F32)
    n_ref[si] = decay * n_state + jnp.sum(kw, axis=0, keepdims=True)
    m_ref[si] = m_new

    out_cols = slice(RET_HEADS * HEAD_W + h * HEAD_W, RET_HEADS * HEAD_W + (h + 1) * HEAD_W)
    y = _rms_rows(hid, hg_ref[:, out_cols])
    o_ref[bb, rows, out_cols] = (y * jax.nn.sigmoid(og_ref[bb, rows, cols].astype(F32))).astype(o_ref.dtype)


def _mixer_body(gl_ref, rq_ref, rk_ref, rv_ref, rg_ref, mq_ref, mqp_ref, mk_ref, mkp_ref, mv_ref, mo_ref,
                cos_ref, sin_ref, dm_ref, wq_ref, wk_ref, *rest):
    seqs = mq_ref.shape[0]
    gr_refs = rest[:seqs]
    gb_ref, cw_ref, shift_ref, hg_ref, o_ref, r_ref, c_ref, n_ref, m_ref = rest[seqs:]
    first = pl.program_id(1) == 0

    @pl.when(first)
    def _():
        r_ref[...] = jnp.zeros_like(r_ref)
        c_ref[...] = jnp.zeros_like(c_ref)
        n_ref[...] = jnp.zeros_like(n_ref)
        m_ref[...] = jnp.zeros_like(m_ref)

    ml_w = ML_HEADS * HEAD_W
    rowi = lax.broadcasted_iota(jnp.int32, (CHUNK, CHUNK), 0)
    coli = lax.broadcasted_iota(jnp.int32, (CHUNK, CHUNK), 1)
    upper_tri = jnp.where(rowi <= coli, 1.0, 0.0).astype(F32)
    lower_tri = jnp.where(rowi >= coli, 1.0, 0.0).astype(F32)
    is_input = lax.broadcasted_iota(jnp.int32, (2 * ML_HEADS, CHUNK), 0) < ML_HEADS

    for bb in range(seqs):
        for ci in range(mq_ref.shape[1] // CHUNK):
            rows = slice(ci * CHUNK, (ci + 1) * CHUNK)
            cos = cos_ref[rows, :]
            sin = sin_ref[rows, :]
            for h in range(RET_HEADS):
                _retention_head(h, bb, rows, rq_ref, rk_ref, rv_ref, rg_ref, cos, sin, dm_ref, wq_ref, wk_ref,
                                gl_ref, hg_ref, o_ref, r_ref)

            if ci == 0:
                q_prev = jnp.where(first, jnp.zeros_like(mqp_ref[bb]), mqp_ref[bb])
                k_prev = jnp.where(first, jnp.zeros_like(mkp_ref[bb]), mkp_ref[bb])
            else:
                before = slice((ci - 1) * CHUNK, ci * CHUNK)
                q_prev, k_prev = mq_ref[bb, before, :], mk_ref[bb, before, :]
            q_all = _silu(_causal_conv(mq_ref[bb, rows, :], q_prev, shift_ref, cw_ref[:, :ml_w]))
            k_all = _silu(_causal_conv(mk_ref[bb, rows, :], k_prev, shift_ref, cw_ref[:, ml_w:])) * (HEAD_W ** -0.5)
            gates = gr_refs[bb][:2 * ML_HEADS, rows] + gb_ref[...]
            gx = jnp.where(is_input, gates, _log_sigmoid(gates))
            b_rows = jnp.dot(gx, upper_tri, precision=lax.Precision.HIGHEST, preferred_element_type=F32)
            for h in range(ML_HEADS):
                _mlstm_head(h, bb, rows, q_all, k_all, mv_ref, mo_ref, gx, b_rows, lower_tri, hg_ref, o_ref,
                            c_ref, n_ref, m_ref)


def _mixer(proj, gates_t, gate_b, conv_w, head_g):
    _, b, t, _ = proj.shape
    seqs = MIXER_SEQS
    step = MIXER_CHUNKS * CHUNK
    nc = t // step
    ret_w = RET_HEADS * HEAD_W
    ml_w = ML_HEADS * HEAD_W
    assert ret_w == ml_w == proj.shape[3]
    log_g = jnp.log1p(-jnp.exp2(-5.0 - jnp.arange(RET_HEADS, dtype=F32)))
    idx = jnp.arange(CHUNK, dtype=F32)
    diff = idx[:, None] - idx[None, :]
    dmask = jnp.where(diff >= 0, jnp.exp(log_g[:, None, None] * jnp.maximum(diff, 0.0)), 0.0)
    w_k = jnp.exp(log_g[:, None] * (CHUNK - 1.0 - idx)[None, :])
    w_q = jnp.exp(log_g[:, None] * (idx + 1.0)[None, :])
    w_k = jnp.broadcast_to(w_k[..., None], (RET_HEADS, CHUNK, HEAD_W))
    w_q = jnp.broadcast_to(w_q[..., None], (RET_HEADS, CHUNK, HEAD_W))
    g_l = jnp.exp(log_g * CHUNK)
    half = HEAD_W // 2
    inv = 1.0 / (ROPE_BASE ** jnp.linspace(0.0, 1.0, half, dtype=F32))
    ang = jnp.arange(t).astype(F32)[:, None] * inv[None, :]
    cos, sin = jnp.cos(ang), jnp.sin(ang)
    r = jnp.arange((CONV_W - 1) * CHUNK)
    src = CHUNK + r % CHUNK - (r // CHUNK + 1)
    shift = (jnp.arange(2 * CHUNK)[None, :] == src[:, None]).astype(BF16)

    def group(gi):
        return pl.BlockSpec((None, seqs, step, ret_w), lambda bi, c: (gi, bi, c, 0))

    def prev_group(gi):
        return pl.BlockSpec((None, seqs, CHUNK, ml_w),
                            lambda bi, c: (gi, bi, jnp.maximum(c * MIXER_CHUNKS - 1, 0), 0))

    def whole(a):
        return pl.BlockSpec(a.shape, lambda bi, c: (0,) * a.ndim)

    gb = gate_b.reshape(2 * ML_HEADS, 1)
    hg = head_g.reshape(1, ret_w + ml_w)
    return pl.pallas_call(
        _mixer_body,
        grid=(b // seqs, nc),
        in_specs=[
            pl.BlockSpec(memory_space=pltpu.SMEM),
            group(0), group(1), group(2), group(3),
            group(4), prev_group(4), group(5), prev_group(5), group(6), group(7),
            pl.BlockSpec((step, half), lambda bi, c: (c, 0)),
            pl.BlockSpec((step, half), lambda bi, c: (c, 0)),
            whole(dmask), whole(w_q), whole(w_k),
            *[pl.BlockSpec((gates_t.shape[0], step), lambda bi, c, bb=bb: (0, (bi * seqs + bb) * nc + c))
              for bb in range(seqs)],
            whole(gb), whole(conv_w), whole(shift), whole(hg),
        ],
        out_specs=pl.BlockSpec((seqs, step, ret_w + ml_w), lambda bi, c: (bi, c, 0)),
        out_shape=jax.ShapeDtypeStruct((b, t, ret_w + ml_w), BF16),
        scratch_shapes=[pltpu.VMEM((seqs * RET_HEADS, HEAD_W, HEAD_W), F32),
                        pltpu.VMEM((seqs * ML_HEADS, HEAD_W, HEAD_W), F32),
                        pltpu.VMEM((seqs * ML_HEADS, 1, HEAD_W), F32),
                        pltpu.VMEM((seqs * ML_HEADS, 1, LANES), F32)],
        compiler_params=_params(),
        name="mixer",
    )(g_l, proj, proj, proj, proj, proj, proj, proj, proj, proj, proj, cos, sin, dmask, w_q, w_k,
      *([gates_t] * seqs), gb, conv_w, shift, hg)


PAIR_W = 2 * HEAD_DIM
GROUP = ATT_HEADS // KV_HEADS
PAIRS = GROUP // 2
LOG2E = math.log2(math.e)
Q_PRESCALE = HEAD_DIM ** -0.5 * LOG2E


def _swa_in_proj_body(x_ref, g_ref, w_ref, q_ref, kv_ref, *, row_block):
    qw = ATT_HEADS * HEAD_DIM
    per_block = row_block // ATT_BLOCK
    for r in range(x_ref.shape[0] // row_block):
        rows = slice(r * row_block, (r + 1) * row_block)
        h = _rms_rows(x_ref[rows, :], g_ref[...]).astype(BF16)
        res = jnp.dot(h, w_ref[...], preferred_element_type=F32)
        for nbl in range(per_block):
            sub = slice(nbl * ATT_BLOCK, (nbl + 1) * ATT_BLOCK)
            for p in range(q_ref.shape[2]):
                q_ref[0, r * per_block + nbl, p] = (res[sub, p * PAIR_W:(p + 1) * PAIR_W]
                                                    * Q_PRESCALE).astype(q_ref.dtype)
        kv_ref[rows, :] = res[:, qw:].astype(kv_ref.dtype)


def _swa_in_proj(x, g, w, b, t, tm, row_block):
    n, d = x.shape
    cols = w.shape[1]
    kvw = cols - ATT_HEADS * HEAD_DIM
    tiles_per_seq = t // tm
    return pl.pallas_call(
        functools.partial(_swa_in_proj_body, row_block=row_block),
        grid=(n // tm,),
        in_specs=[
            pl.BlockSpec((tm, d), lambda i: (i, 0)),
            pl.BlockSpec((1, d), lambda i: (0, 0)),
            pl.BlockSpec((d, cols), lambda i: (0, 0)),
        ],
        out_specs=[
            pl.BlockSpec((1, tm // ATT_BLOCK, ATT_HEADS // 2, ATT_BLOCK, PAIR_W),
                         lambda i: (i // tiles_per_seq, i % tiles_per_seq, 0, 0, 0)),
            pl.BlockSpec((tm, kvw), lambda i: (i, 0)),
        ],
        out_shape=[jax.ShapeDtypeStruct((b, t // ATT_BLOCK, ATT_HEADS // 2, ATT_BLOCK, PAIR_W), BF16),
                   jax.ShapeDtypeStruct((n, kvw), BF16)],
        compiler_params=_params(),
        name="swa_in_proj",
    )(x, g.reshape(1, d), w)


def _swa_body(rb_ref, sink_ref, idx_ref, q_ref, kp_ref, kc_ref, vp_ref, vc_ref, o_ref, bias_ref, sinkcol_ref):
    bi = pl.program_id(0)
    nb = pl.program_id(1)
    blk = ATT_BLOCK
    rows = PAIRS * blk

    @pl.when((bi == 0) & (nb == 0))
    def _():
        idx = idx_ref[...]
        col = lax.broadcasted_iota(jnp.int32, idx.shape, 1)

        def head_body(hd, carry):
            def bucket_body(bk, acc):
                return jnp.where(idx == bk, rb_ref[bk * ATT_HEADS + hd] * LOG2E, acc)

            tbl = lax.fori_loop(0, N_BUCKETS, bucket_body, jnp.full(idx.shape, -jnp.inf, F32))
            kv = hd // GROUP
            half = hd % 2
            at = pl.ds(pl.multiple_of(((hd % GROUP) // 2) * blk, blk), blk)
            bias_ref[0, kv, half, at, :] = tbl
            bias_ref[1, kv, half, at, :] = jnp.where(col >= blk, tbl, -jnp.inf)
            sinkcol_ref[kv, half, at, :] = jnp.full((blk, LANES), sink_ref[hd] * LOG2E, F32)
            return carry

        lax.fori_loop(0, ATT_HEADS, head_body, 0)

    low = lax.broadcasted_iota(jnp.int32, (2 * blk, LANES), 1) < HEAD_DIM
    low_out = lax.broadcasted_iota(jnp.int32, (rows, LANES), 1) < HEAD_DIM

    def halves(xf, kv):
        xc = xf[:, (kv // 2) * LANES:(kv // 2 + 1) * LANES]
        xr = pltpu.roll(xc, HEAD_DIM, axis=1)
        lo_src, hi_src = (xc, xr) if kv % 2 == 0 else (xr, xc)
        return (jnp.where(low, lo_src, 0.0).astype(BF16), jnp.where(low, 0.0, hi_src).astype(BF16))

    for sb in range(q_ref.shape[0]):
        here = slice(sb * blk, (sb + 1) * blk)
        before = slice((sb - 1) * blk, sb * blk)
        k_prev, v_prev = (kp_ref[0], vp_ref[0]) if sb == 0 else (kc_ref[0, before, :], vc_ref[0, before, :])
        kf = jnp.concatenate([k_prev, kc_ref[0, here, :]], axis=0).astype(F32)
        vf = jnp.concatenate([v_prev, vc_ref[0, here, :]], axis=0).astype(F32)
        first = (nb == 0).astype(jnp.int32) if sb == 0 else 0
        for kv in range(KV_HEADS):
            k_lo, k_hi = halves(kf, kv)
            v_lo, v_hi = halves(vf, kv)
            q4 = q_ref[sb, kv * PAIRS:(kv + 1) * PAIRS].reshape(rows, PAIR_W)

            def probs(k_half, half):
                s = lax.dot_general(q4, k_half, (((1,), (1,)), ((), ())), preferred_element_type=F32)
                s = s + bias_ref[first, kv, half]
                sink = sinkcol_ref[kv, half]
                mx = jnp.maximum(jnp.max(s, axis=-1, keepdims=True), sink)
                p = jnp.exp2(s - jnp.concatenate([mx, mx], axis=1))
                den = jnp.sum(p, axis=-1, keepdims=True) + jnp.exp2(sink - mx)
                return p.astype(BF16), den

            p_lo, den_lo = probs(k_lo, 0)
            p_hi, den_hi = probs(k_hi, 1)
            o4 = (jnp.dot(p_lo, v_lo, preferred_element_type=F32)
                  + jnp.dot(p_hi, v_hi, preferred_element_type=F32)) / jnp.where(low_out, den_lo, den_hi)
            for pr in range(PAIRS):
                at = (kv * PAIRS + pr) * PAIR_W
                o_ref[0, here, at:at + PAIR_W] = o4[pr * blk:(pr + 1) * blk].astype(o_ref.dtype)


def _t5_bucket(dist):
    n = jnp.maximum(dist, 0)
    max_exact = N_BUCKETS // 2
    nf = jnp.maximum(n, 1).astype(F32)
    large = max_exact + (jnp.log(nf / max_exact) / math.log(MAX_DIST / max_exact)
                         * (N_BUCKETS - max_exact)).astype(jnp.int32)
    large = jnp.minimum(large, N_BUCKETS - 1)
    return jnp.where(n < max_exact, n, large)


def _swa(q, kv, sinks, rel_bias):
    b, nblk, n_pairs, blk, _ = q.shape
    t = nblk * blk
    qw = ATT_HEADS * HEAD_DIM
    kvw = KV_HEADS * HEAD_DIM
    i = jnp.arange(blk)
    j = jnp.arange(2 * blk)
    dist = (blk + i)[:, None] - j[None, :]
    idx = jnp.where((dist >= 0) & (dist < WINDOW), _t5_bucket(dist), -1).astype(jnp.int32)
    sb = SWA_BLOCKS
    return pl.pallas_call(
        _swa_body,
        grid=(b, nblk // sb),
        in_specs=[
            pl.BlockSpec(memory_space=pltpu.SMEM),
            pl.BlockSpec(memory_space=pltpu.SMEM),
            pl.BlockSpec((blk, 2 * blk), lambda bi, nb: (0, 0)),
            pl.BlockSpec((None, sb, n_pairs, blk, PAIR_W), lambda bi, nb: (bi, nb, 0, 0, 0)),
            pl.BlockSpec((1, blk, kvw), lambda bi, nb: (bi, jnp.maximum(nb * sb - 1, 0), 0)),
            pl.BlockSpec((1, sb * blk, kvw), lambda bi, nb: (bi, nb, 0)),
            pl.BlockSpec((1, blk, kvw), lambda bi, nb: (bi, jnp.maximum(nb * sb - 1, 0), 1)),
            pl.BlockSpec((1, sb * blk, kvw), lambda bi, nb: (bi, nb, 1)),
        ],
        out_specs=pl.BlockSpec((1, sb * blk, qw), lambda bi, nb: (bi, nb, 0)),
        out_shape=jax.ShapeDtypeStruct((b, t, qw), BF16),
        scratch_shapes=[pltpu.VMEM((2, KV_HEADS, 2, PAIRS * blk, 2 * blk), F32),
                        pltpu.VMEM((KV_HEADS, 2, PAIRS * blk, LANES), F32)],
        compiler_params=_params(),
        name="swa",
    )(rel_bias.reshape(-1), sinks, idx, q, kv, kv, kv, kv)


def kernel(x, rel_bias, norm_g, ffn_w_gu, ffn_w_down, rm_w_in, ml_conv_w, ml_gate_b, rm_head_g,
           rm_w_out, swa_w_in, swa_sinks, swa_w_out):
    b, t, d = x.shape
    n = b * t
    depth = norm_g.shape[0]
    main_cols = 4 * RET_HEADS * HEAD_W + 4 * ML_HEADS * HEAD_W
    xs = x.reshape(n, d)
    row_tiles = n // ROW_TILE
    proj_tiles = main_cols // PROJ_TN

    def row_bands(src, layer):
        rows = src.shape[1] // (row_tiles * proj_tiles)
        return src, layer, (rows, src.shape[2]), lambda i, j: (i * proj_tiles + j, 0)

    def blocks(src, layer):
        return src, layer, (src.shape[1] // row_tiles, src.shape[2] // proj_tiles), lambda i, j: (i, j)

    ready = {}

    def bf16_weight(name, src, layer):
        if (name, layer) in ready:
            return ready.pop((name, layer))
        return src[layer].astype(BF16)

    for layer in range(depth):
        g = norm_g[layer]
        if layer % 2 == 0:
            e = layer // 2
            w_in = rm_w_in[e]
            gate_cols = w_in.shape[1] - main_cols
            w_gate = jnp.pad(w_in[:, main_cols:].T, ((0, 2 * SUBLANES - gate_cols), (0, 0))).astype(BF16)
            names = [("gu", layer), ("down", layer), ("rm_out", e)]
            casts = [row_bands(ffn_w_gu, layer), blocks(ffn_w_down, layer), row_bands(rm_w_out, e)]
            if layer + 1 < depth:
                o = (layer + 1) // 2
                names += [("gu", layer + 1), ("down", layer + 1), ("swa_in", o), ("swa_out", o)]
                casts += [row_bands(ffn_w_gu, layer + 1), blocks(ffn_w_down, layer + 1),
                          row_bands(swa_w_in, o), row_bands(swa_w_out, o)]
            proj, gates, *cast = _norm_matmul(xs, g[0], w_in.astype(BF16), w_gate, proj_tiles, ROW_TILE, PROJ_TN, casts)
            ready.update(zip(names, cast))
            mixed = _mixer(proj.reshape(main_cols // PROJ_TN, b, t, PROJ_TN), gates, ml_gate_b[e], ml_conv_w[e],
                           rm_head_g[e])
            xs = _proj_norm_res(mixed.reshape(n, -1), bf16_weight("rm_out", rm_w_out, e), xs, g[1], OUT_ROW_TILE)
        else:
            o = layer // 2
            q, kv = _swa_in_proj(xs, g[0], bf16_weight("swa_in", swa_w_in, o), b, t, ROW_TILE, PROJ_ROW_BLOCK)
            att = _swa(q, kv.reshape(b, t, -1), swa_sinks[o], rel_bias)
            xs = _proj_norm_res(att.reshape(n, -1), bf16_weight("swa_out", swa_w_out, o), xs, g[1], OUT_ROW_TILE)
        xs = _ffn(xs, g[2], bf16_weight("gu", ffn_w_gu, layer), bf16_weight("down", ffn_w_down, layer), g[3],
                  ROW_TILE, FFN_TF, FFN_ROW_BLOCK)
    return xs.reshape(b, t, d)
```

```python
import functools
import math

import jax
import jax.numpy as jnp
from jax import lax
from jax.experimental import pallas as pl
from jax.experimental.pallas import tpu as pltpu

F32 = jnp.float32
BF16 = jnp.bfloat16

EPS = 1e-6
CHUNK = 128
RET_HEADS = 4
ML_HEADS = 4
HEAD_W = 256
CONV_W = 4
ROPE_BASE = 10000.0
ATT_HEADS = 32
KV_HEADS = 4
HEAD_DIM = 64
WINDOW = 128
ATT_BLOCK = 128
N_BUCKETS = 32
MAX_DIST = 128
LANES = 128
SUBLANES = 8
VMEM_LIMIT = 56 * 1024 * 1024

ROW_TILE = 1024
OUT_ROW_TILE = 512
PROJ_TN = 1024
PROJ_ROW_BLOCK = 256
MIXER_CHUNKS = 2
MIXER_SEQS = 2
SWA_BLOCKS = 4
FFN_TF = 512
FFN_ROW_BLOCK = 512


def _params(vmem=VMEM_LIMIT):
    return pltpu.CompilerParams(vmem_limit_bytes=vmem)


def _rms_rows(x, g):
    ms = jnp.mean(x * x, axis=-1, keepdims=True)
    return x * lax.rsqrt(ms + EPS) * g


def _silu(x):
    return x * jax.nn.sigmoid(x)


def _cast_specs(casts):
    in_specs = [pl.BlockSpec((None, *blk), (lambda *ids, layer=layer, idx=idx: (layer, *idx(*ids))))
                for _, layer, blk, idx in casts]
    out_specs = [pl.BlockSpec(blk, idx) for _, _, blk, idx in casts]
    out_shape = [jax.ShapeDtypeStruct(src.shape[1:], BF16) for src, _, _, _ in casts]
    return in_specs, out_specs, out_shape


def _cast_pieces(src_refs, dst_refs):
    for src_ref, dst_ref in zip(src_refs, dst_refs):
        dst_ref[...] = src_ref[...].astype(dst_ref.dtype)


def _norm_matmul_body(x_ref, g_ref, w_ref, ws_ref, *rest, row_block, n_cast):
    cast_src, (o_ref, os_ref), cast_dst, h_ref = rest[:n_cast], rest[n_cast:n_cast + 2], rest[n_cast + 2:-1], rest[-1]
    j = pl.program_id(1)

    @pl.when(j == 0)
    def _():
        _cast_pieces(cast_src, cast_dst)
        for r in range(x_ref.shape[0] // row_block):
            rows = slice(r * row_block, (r + 1) * row_block)
            h = _rms_rows(x_ref[rows, :], g_ref[...]).astype(h_ref.dtype)
            h_ref[rows, :] = h
            os_ref[:, rows] = lax.dot_general(ws_ref[...], h, _NT, preferred_element_type=F32)
            o_ref[rows, :] = jnp.dot(h, w_ref[...], preferred_element_type=F32).astype(o_ref.dtype)

    @pl.when(j > 0)
    def _():
        _cast_pieces(cast_src, cast_dst)
        o_ref[...] = jnp.dot(h_ref[...], w_ref[...], preferred_element_type=F32).astype(o_ref.dtype)


def _norm_matmul(x, g, w, w_side, tiles, tm, tn, casts):
    n, d = x.shape
    ns = w_side.shape[0]
    cast_in, cast_out, cast_shape = _cast_specs(casts)
    return pl.pallas_call(
        functools.partial(_norm_matmul_body, row_block=PROJ_ROW_BLOCK, n_cast=len(casts)),
        grid=(n // tm, tiles),
        in_specs=[
            pl.BlockSpec((tm, d), lambda i, j: (i, 0)),
            pl.BlockSpec((1, d), lambda i, j: (0, 0)),
            pl.BlockSpec((d, tn), lambda i, j: (0, j)),
            pl.BlockSpec((ns, d), lambda i, j: (0, 0)),
            *cast_in,
        ],
        out_specs=[pl.BlockSpec((None, tm, tn), lambda i, j: (j, i, 0)),
                   pl.BlockSpec((ns, tm), lambda i, j: (0, i)),
                   *cast_out],
        out_shape=[jax.ShapeDtypeStruct((tiles, n, tn), BF16), jax.ShapeDtypeStruct((ns, n), F32), *cast_shape],
        scratch_shapes=[pltpu.VMEM((tm, d), BF16)],
        compiler_params=_params(),
        name="norm_matmul",
    )(x, g.reshape(1, d), w, w_side, *[c[0] for c in casts])


def _proj_norm_res_body(a_ref, w_ref, x_ref, g_ref, o_ref):
    y = jnp.dot(a_ref[...], w_ref[...], preferred_element_type=F32)
    o_ref[...] = x_ref[...] + _rms_rows(y, g_ref[...])


def _proj_norm_res(a, w, x, g, tm):
    n, d = x.shape
    return pl.pallas_call(
        _proj_norm_res_body,
        grid=(n // tm,),
        in_specs=[
            pl.BlockSpec((tm, a.shape[1]), lambda i: (i, 0)),
            pl.BlockSpec(w.shape, lambda i: (0, 0)),
            pl.BlockSpec((tm, d), lambda i: (i, 0)),
            pl.BlockSpec((1, d), lambda i: (0, 0)),
        ],
        out_specs=pl.BlockSpec((tm, d), lambda i: (i, 0)),
        out_shape=jax.ShapeDtypeStruct((n, d), F32),
        compiler_params=_params(),
        name="proj_norm_res",
    )(a, w, x, g.reshape(1, d))


def _ffn_body(x_ref, g_in_ref, wg_ref, wu_ref, wd_ref, g_out_ref, o_ref, h_ref, *, row_block):
    f = pl.program_id(1)
    last = pl.num_programs(1) - 1
    blocks = [slice(r * row_block, (r + 1) * row_block) for r in range(x_ref.shape[0] // row_block)]

    def partial_down(rows):
        h = h_ref[rows, :]
        gate = jnp.dot(h, wg_ref[...], preferred_element_type=F32)
        up = jnp.dot(h, wu_ref[...], preferred_element_type=F32)
        act = (_silu(gate) * up).astype(BF16)
        return jnp.dot(act, wd_ref[...], preferred_element_type=F32)

    @pl.when(f == 0)
    def _():
        for rows in blocks:
            h_ref[rows, :] = _rms_rows(x_ref[rows, :], g_in_ref[...]).astype(h_ref.dtype)
            o_ref[rows, :] = partial_down(rows)

    @pl.when((f > 0) & (f < last))
    def _():
        for rows in blocks:
            o_ref[rows, :] += partial_down(rows)

    @pl.when(f == last)
    def _():
        for rows in blocks:
            y = o_ref[rows, :] + partial_down(rows)
            o_ref[rows, :] = x_ref[rows, :] + _rms_rows(y, g_out_ref[...])


def _ffn(x, g_in, w_gu, w_down, g_out, tm, tf, row_block):
    n, d = x.shape
    nf = w_down.shape[0] // tf
    return pl.pallas_call(
        functools.partial(_ffn_body, row_block=row_block),
        grid=(n // tm, nf),
        in_specs=[
            pl.BlockSpec((tm, d), lambda i, f: (i, 0)),
            pl.BlockSpec((1, d), lambda i, f: (0, 0)),
            pl.BlockSpec((d, tf), lambda i, f: (0, f)),
            pl.BlockSpec((d, tf), lambda i, f: (0, f + nf)),
            pl.BlockSpec((tf, d), lambda i, f: (f, 0)),
            pl.BlockSpec((1, d), lambda i, f: (0, 0)),
        ],
        out_specs=pl.BlockSpec((tm, d), lambda i, f: (i, 0)),
        out_shape=jax.ShapeDtypeStruct((n, d), F32),
        scratch_shapes=[pltpu.VMEM((tm, d), BF16)],
        compiler_params=_params(),
        name="ffn",
    )(x, g_in.reshape(1, d), w_gu, w_gu, w_down, g_out.reshape(1, d))


def _log_sigmoid(x):
    return jnp.minimum(x, 0.0) - jnp.log1p(jnp.exp(-jnp.abs(x)))


_NT = (((1,), (1,)), ((), ()))
_TN = (((0,), (0,)), ((), ()))


def _retention_head(h, bb, rows, q_ref, k_ref, v_ref, g_ref, cos, sin, dm_ref, wq_ref, wk_ref, gl_ref, hg_ref,
                    o_ref, state_ref):
    cols = slice(h * HEAD_W, (h + 1) * HEAD_W)
    half = HEAD_W // 2

    def rot(x):
        x = x.astype(F32)
        x1, x2 = x[:, :half], x[:, half:]
        return jnp.concatenate([x1 * cos - x2 * sin, x2 * cos + x1 * sin], axis=-1)

    q = rot(q_ref[bb, rows, cols])
    k = rot(k_ref[bb, rows, cols]) * (HEAD_W ** -0.5)
    v = v_ref[bb, rows, cols]
    qb = q.astype(BF16)
    s = lax.dot_general(qb, k.astype(BF16), _NT, preferred_element_type=F32) * dm_ref[h]
    intra = jnp.dot(s.astype(BF16), v, preferred_element_type=F32)
    si = bb * RET_HEADS + h
    state = state_ref[si]
    inter = jnp.dot(qb, state.astype(BF16), preferred_element_type=F32) * wq_ref[h]
    out = intra + inter
    kw = (k * wk_ref[h]).astype(BF16)
    state_ref[si] = state * gl_ref[h] + lax.dot_general(kw, v, _TN, preferred_element_type=F32)
    y = _rms_rows(out, hg_ref[:, cols])
    o_ref[bb, rows, cols] = (y * _silu(g_ref[bb, rows, cols].astype(F32))).astype(o_ref.dtype)


def _causal_conv(cur, prev, shift_ref, w):
    shifted = jnp.dot(shift_ref[...], jnp.concatenate([prev, cur], axis=0), preferred_element_type=F32)
    y = cur.astype(F32) * w[CONV_W - 1:CONV_W]
    for s in range(1, CONV_W):
        y = y + shifted[(s - 1) * CHUNK:s * CHUNK] * w[CONV_W - 1 - s:CONV_W - s]
    return y


def _mlstm_head(h, bb, rows, q_all, k_all, v_ref, og_ref, gx, b_rows, lower_tri, hg_ref, o_ref, c_ref, n_ref,
                m_ref):
    cols = slice(h * HEAD_W, (h + 1) * HEAD_W)
    q = q_all[:, cols]
    k = k_all[:, cols]
    v = v_ref[bb, rows, cols]
    si = bb * ML_HEADS + h

    i_row = gx[h:h + 1]
    lf_row = gx[ML_HEADS + h:ML_HEADS + h + 1]
    b_row = b_rows[ML_HEADS + h:ML_HEADS + h + 1]
    b_cols = lax.dot_general(lower_tri, jnp.broadcast_to(lf_row, (CHUNK, CHUNK)), _NT,
                             precision=lax.Precision.HIGHEST, preferred_element_type=F32)
    i_cols = jnp.broadcast_to(i_row, (CHUNK, CHUNK)).T
    rowi = lax.broadcasted_iota(jnp.int32, (CHUNK, CHUNK), 0)
    coli = lax.broadcasted_iota(jnp.int32, (CHUNK, CHUNK), 1)
    log_d = jnp.where(rowi >= coli, b_cols - b_row + i_row, -jnp.inf)
    m_prev = m_ref[si]
    inter_log = b_cols + m_prev
    m_t = jnp.maximum(inter_log, jnp.max(log_d, axis=-1, keepdims=True))
    d_mat = jnp.exp(log_d - m_t)
    w_inter = jnp.exp(inter_log - m_t)

    def wide(a):
        return jnp.concatenate([a] * (HEAD_W // LANES), axis=1)

    qb = q.astype(BF16)
    s = lax.dot_general(qb, k.astype(BF16), _NT, preferred_element_type=F32) * d_mat
    c_state = c_ref[si]
    n_state = n_ref[si]
    num = (jnp.dot(s.astype(BF16), v, preferred_element_type=F32)
           + wide(w_inter) * jnp.dot(qb, c_state.astype(BF16), preferred_element_type=F32))
    den = (jnp.sum(s, axis=-1, keepdims=True)
           + w_inter * jnp.sum(q * n_state, axis=-1, keepdims=True))
    hid = num / wide(jnp.maximum(jnp.abs(den), jnp.exp(-m_t)))

    b_last = b_cols[CHUNK - 1:CHUNK]
    log_w = b_last - b_cols + i_cols
    m_new = jnp.maximum(b_last + m_prev, jnp.max(log_w, axis=0, keepdims=True))
    w = jnp.exp(log_w - m_new)
    decay = wide(jnp.exp(b_last + m_prev - m_new))
    kw = k * wide(w)
    c_ref[si] = decay * c_state + lax.dot_general(kw.astype(BF16), v, _TN, preferred_element_type=F32)
    n_ref[si] = decay * n_state + jnp.sum(kw, axis=0, keepdims=True)
    m_ref[si] = m_new

    out_cols = slice(RET_HEADS * HEAD_W + h * HEAD_W, RET_HEADS * HEAD_W + (h + 1) * HEAD_W)
    y = _rms_rows(hid, hg_ref[:, out_cols])
    o_ref[bb, rows, out_cols] = (y * jax.nn.sigmoid(og_ref[bb, rows, cols].astype(F32))).astype(o_ref.dtype)


def _mixer_body(gl_ref, rq_ref, rk_ref, rv_ref, rg_ref, mq_ref, mqp_ref, mk_ref, mkp_ref, mv_ref, mo_ref,
                cos_ref, sin_ref, dm_ref, wq_ref, wk_ref, *rest):
    seqs = mq_ref.shape[0]
    gr_refs = rest[:seqs]
    gb_ref, cw_ref, shift_ref, hg_ref, o_ref, r_ref, c_ref, n_ref, m_ref = rest[seqs:]
    first = pl.program_id(1) == 0

    @pl.when(first)
    def _():
        r_ref[...] = jnp.zeros_like(r_ref)
        c_ref[...] = jnp.zeros_like(c_ref)
        n_ref[...] = jnp.zeros_like(n_ref)
        m_ref[...] = jnp.zeros_like(m_ref)

    ml_w = ML_HEADS * HEAD_W
    rowi = lax.broadcasted_iota(jnp.int32, (CHUNK, CHUNK), 0)
    coli = lax.broadcasted_iota(jnp.int32, (CHUNK, CHUNK), 1)
    upper_tri = jnp.where(rowi <= coli, 1.0, 0.0).astype(F32)
    lower_tri = jnp.where(rowi >= coli, 1.0, 0.0).astype(F32)
    is_input = lax.broadcasted_iota(jnp.int32, (2 * ML_HEADS, CHUNK), 0) < ML_HEADS

    for bb in range(seqs):
        for ci in range(mq_ref.shape[1] // CHUNK):
            rows = slice(ci * CHUNK, (ci + 1) * CHUNK)
            cos = cos_ref[rows, :]
            sin = sin_ref[rows, :]
            for h in range(RET_HEADS):
                _retention_head(h, bb, rows, rq_ref, rk_ref, rv_ref, rg_ref, cos, sin, dm_ref, wq_ref, wk_ref,
                                gl_ref, hg_ref, o_ref, r_ref)

            if ci == 0:
                q_prev = jnp.where(first, jnp.zeros_like(mqp_ref[bb]), mqp_ref[bb])
                k_prev = jnp.where(first, jnp.zeros_like(mkp_ref[bb]), mkp_ref[bb])
            else:
                before = slice((ci - 1) * CHUNK, ci * CHUNK)
                q_prev, k_prev = mq_ref[bb, before, :], mk_ref[bb, before, :]
            q_all = _silu(_causal_conv(mq_ref[bb, rows, :], q_prev, shift_ref, cw_ref[:, :ml_w]))
            k_all = _silu(_causal_conv(mk_ref[bb, rows, :], k_prev, shift_ref, cw_ref[:, ml_w:])) * (HEAD_W ** -0.5)
            gates = gr_refs[bb][:2 * ML_HEADS, rows] + gb_ref[...]
            gx = jnp.where(is_input, gates, _log_sigmoid(gates))
            b_rows = jnp.dot(gx, upper_tri, precision=lax.Precision.HIGHEST, preferred_element_type=F32)
            for h in range(ML_HEADS):
                _mlstm_head(h, bb, rows, q_all, k_all, mv_ref, mo_ref, gx, b_rows, lower_tri, hg_ref, o_ref,
                            c_ref, n_ref, m_ref)


def _mixer(proj, gates_t, gate_b, conv_w, head_g):
    _, b, t, _ = proj.shape
    seqs = MIXER_SEQS
    step = MIXER_CHUNKS * CHUNK
    nc = t // step
    ret_w = RET_HEADS * HEAD_W
    ml_w = ML_HEADS * HEAD_W
    assert ret_w == ml_w == proj.shape[3]
    log_g = jnp.log1p(-jnp.exp2(-5.0 - jnp.arange(RET_HEADS, dtype=F32)))
    idx = jnp.arange(CHUNK, dtype=F32)
    diff = idx[:, None] - idx[None, :]
    dmask = jnp.where(diff >= 0, jnp.exp(log_g[:, None, None] * jnp.maximum(diff, 0.0)), 0.0)
    w_k = jnp.exp(log_g[:, None] * (CHUNK - 1.0 - idx)[None, :])
    w_q = jnp.exp(log_g[:, None] * (idx + 1.0)[None, :])
    w_k = jnp.broadcast_to(w_k[..., None], (RET_HEADS, CHUNK, HEAD_W))
    w_q = jnp.broadcast_to(w_q[..., None], (RET_HEADS, CHUNK, HEAD_W))
    g_l = jnp.exp(log_g * CHUNK)
    half = HEAD_W // 2
    inv = 1.0 / (ROPE_BASE ** jnp.linspace(0.0, 1.0, half, dtype=F32))
    ang = jnp.arange(t).astype(F32)[:, None] * inv[None, :]
    cos, sin = jnp.cos(ang), jnp.sin(ang)
    r = jnp.arange((CONV_W - 1) * CHUNK)
    src = CHUNK + r % CHUNK - (r // CHUNK + 1)
    shift = (jnp.arange(2 * CHUNK)[None, :] == src[:, None]).astype(BF16)

    def group(gi):
        return pl.BlockSpec((None, seqs, step, ret_w), lambda bi, c: (gi, bi, c, 0))

    def prev_group(gi):
        return pl.BlockSpec((None, seqs, CHUNK, ml_w),
                            lambda bi, c: (gi, bi, jnp.maximum(c * MIXER_CHUNKS - 1, 0), 0))

    def whole(a):
        return pl.BlockSpec(a.shape, lambda bi, c: (0,) * a.ndim)

    gb = gate_b.reshape(2 * ML_HEADS, 1)
    hg = head_g.reshape(1, ret_w + ml_w)
    return pl.pallas_call(
        _mixer_body,
        grid=(b // seqs, nc),
        in_specs=[
            pl.BlockSpec(memory_space=pltpu.SMEM),
            group(0), group(1), group(2), group(3),
            group(4), prev_group(4), group(5), prev_group(5), group(6), group(7),
            pl.BlockSpec((step, half), lambda bi, c: (c, 0)),
            pl.BlockSpec((step, half), lambda bi, c: (c, 0)),
            whole(dmask), whole(w_q), whole(w_k),
            *[pl.BlockSpec((gates_t.shape[0], step), lambda bi, c, bb=bb: (0, (bi * seqs + bb) * nc + c))
              for bb in range(seqs)],
            whole(gb), whole(conv_w), whole(shift), whole(hg),
        ],
        out_specs=pl.BlockSpec((seqs, step, ret_w + ml_w), lambda bi, c: (bi, c, 0)),
        out_shape=jax.ShapeDtypeStruct((b, t, ret_w + ml_w), BF16),
        scratch_shapes=[pltpu.VMEM((seqs * RET_HEADS, HEAD_W, HEAD_W), F32),
                        pltpu.VMEM((seqs * ML_HEADS, HEAD_W, HEAD_W), F32),
                        pltpu.VMEM((seqs * ML_HEADS, 1, HEAD_W), F32),
                        pltpu.VMEM((seqs * ML_HEADS, 1, LANES), F32)],
        compiler_params=_params(),
        name="mixer",
    )(g_l, proj, proj, proj, proj, proj, proj, proj, proj, proj, proj, cos, sin, dmask, w_q, w_k,
      *([gates_t] * seqs), gb, conv_w, shift, hg)


PAIR_W = 2 * HEAD_DIM
GROUP = ATT_HEADS // KV_HEADS
PAIRS = GROUP // 2
LOG2E = math.log2(math.e)
Q_PRESCALE = HEAD_DIM ** -0.5 * LOG2E


def _swa_in_proj_body(x_ref, g_ref, w_ref, q_ref, kv_ref, *, row_block):
    qw = ATT_HEADS * HEAD_DIM
    per_block = row_block // ATT_BLOCK
    for r in range(x_ref.shape[0] // row_block):
        rows = slice(r * row_block, (r + 1) * row_block)
        h = _rms_rows(x_ref[rows, :], g_ref[...]).astype(BF16)
        res = jnp.dot(h, w_ref[...], preferred_element_type=F32)
        for nbl in range(per_block):
            sub = slice(nbl * ATT_BLOCK, (nbl + 1) * ATT_BLOCK)
            for p in range(q_ref.shape[2]):
                q_ref[0, r * per_block + nbl, p] = (res[sub, p * PAIR_W:(p + 1) * PAIR_W]
                                                    * Q_PRESCALE).astype(q_ref.dtype)
        kv_ref[rows, :] = res[:, qw:].astype(kv_ref.dtype)


def _swa_in_proj(x, g, w, b, t, tm, row_block):
    n, d = x.shape
    cols = w.shape[1]
    kvw = cols - ATT_HEADS * HEAD_DIM
    tiles_per_seq = t // tm
    return pl.pallas_call(
        functools.partial(_swa_in_proj_body, row_block=row_block),
        grid=(n // tm,),
        in_specs=[
            pl.BlockSpec((tm, d), lambda i: (i, 0)),
            pl.BlockSpec((1, d), lambda i: (0, 0)),
            pl.BlockSpec((d, cols), lambda i: (0, 0)),
        ],
        out_specs=[
            pl.BlockSpec((1, tm // ATT_BLOCK, ATT_HEADS // 2, ATT_BLOCK, PAIR_W),
                         lambda i: (i // tiles_per_seq, i % tiles_per_seq, 0, 0, 0)),
            pl.BlockSpec((tm, kvw), lambda i: (i, 0)),
        ],
        out_shape=[jax.ShapeDtypeStruct((b, t // ATT_BLOCK, ATT_HEADS // 2, ATT_BLOCK, PAIR_W), BF16),
                   jax.ShapeDtypeStruct((n, kvw), BF16)],
        compiler_params=_params(),
        name="swa_in_proj",
    )(x, g.reshape(1, d), w)


def _swa_body(rb_ref, sink_ref, idx_ref, q_ref, kp_ref, kc_ref, vp_ref, vc_ref, o_ref, bias_ref, sinkcol_ref):
    bi = pl.program_id(0)
    nb = pl.program_id(1)
    blk = ATT_BLOCK
    rows = PAIRS * blk

    @pl.when((bi == 0) & (nb == 0))
    def _():
        idx = idx_ref[...]
        col = lax.broadcasted_iota(jnp.int32, idx.shape, 1)

        def head_body(hd, carry):
            def bucket_body(bk, acc):
                return jnp.where(idx == bk, rb_ref[bk * ATT_HEADS + hd] * LOG2E, acc)

            tbl = lax.fori_loop(0, N_BUCKETS, bucket_body, jnp.full(idx.shape, -jnp.inf, F32))
            kv = hd // GROUP
            half = hd % 2
            at = pl.ds(pl.multiple_of(((hd % GROUP) // 2) * blk, blk), blk)
            bias_ref[0, kv, half, at, :] = tbl
            bias_ref[1, kv, half, at, :] = jnp.where(col >= blk, tbl, -jnp.inf)
            sinkcol_ref[kv, half, at, :] = jnp.full((blk, LANES), sink_ref[hd] * LOG2E, F32)
            return carry

        lax.fori_loop(0, ATT_HEADS, head_body, 0)

    low = lax.broadcasted_iota(jnp.int32, (2 * blk, LANES), 1) < HEAD_DIM
    low_out = lax.broadcasted_iota(jnp.int32, (rows, LANES), 1) < HEAD_DIM

    def halves(xf, kv):
        xc = xf[:, (kv // 2) * LANES:(kv // 2 + 1) * LANES]
        xr = pltpu.roll(xc, HEAD_DIM, axis=1)
        lo_src, hi_src = (xc, xr) if kv % 2 == 0 else (xr, xc)
        return (jnp.where(low, lo_src, 0.0).astype(BF16), jnp.where(low, 0.0, hi_src).astype(BF16))

    for sb in range(q_ref.shape[0]):
        here = slice(sb * blk, (sb + 1) * blk)
        before = slice((sb - 1) * blk, sb * blk)
        k_prev, v_prev = (kp_ref[0], vp_ref[0]) if sb == 0 else (kc_ref[0, before, :], vc_ref[0, before, :])
        kf = jnp.concatenate([k_prev, kc_ref[0, here, :]], axis=0).astype(F32)
        vf = jnp.concatenate([v_prev, vc_ref[0, here, :]], axis=0).astype(F32)
        first = (nb == 0).astype(jnp.int32) if sb == 0 else 0
        for kv in range(KV_HEADS):
            k_lo, k_hi = halves(kf, kv)
            v_lo, v_hi = halves(vf, kv)
            q4 = q_ref[sb, kv * PAIRS:(kv + 1) * PAIRS].reshape(rows, PAIR_W)

            def probs(k_half, half):
                s = lax.dot_general(q4, k_half, (((1,), (1,)), ((), ())), preferred_element_type=F32)
                s = s + bias_ref[first, kv, half]
                sink = sinkcol_ref[kv, half]
                mx = jnp.maximum(jnp.max(s, axis=-1, keepdims=True), sink)
                p = jnp.exp2(s - jnp.concatenate([mx, mx], axis=1))
                den = jnp.sum(p, axis=-1, keepdims=True) + jnp.exp2(sink - mx)
                return p.astype(BF16), den

            p_lo, den_lo = probs(k_lo, 0)
            p_hi, den_hi = probs(k_hi, 1)
            o4 = (jnp.dot(p_lo, v_lo, preferred_element_type=F32)
                  + jnp.dot(p_hi, v_hi, preferred_element_type=F32)) / jnp.where(low_out, den_lo, den_hi)
            for pr in range(PAIRS):
                at = (kv * PAIRS + pr) * PAIR_W
                o_ref[0, here, at:at + PAIR_W] = o4[pr * blk:(pr + 1) * blk].astype(o_ref.dtype)


def _t5_bucket(dist):
    n = jnp.maximum(dist, 0)
    max_exact = N_BUCKETS // 2
    nf = jnp.maximum(n, 1).astype(F32)
    large = max_exact + (jnp.log(nf / max_exact) / math.log(MAX_DIST / max_exact)
                         * (N_BUCKETS - max_exact)).astype(jnp.int32)
    large = jnp.minimum(large, N_BUCKETS - 1)
    return jnp.where(n < max_exact, n, large)


def _swa(q, kv, sinks, rel_bias):
    b, nblk, n_pairs, blk, _ = q.shape
    t = nblk * blk
    qw = ATT_HEADS * HEAD_DIM
    kvw = KV_HEADS * HEAD_DIM
    i = jnp.arange(blk)
    j = jnp.arange(2 * blk)
    dist = (blk + i)[:, None] - j[None, :]
    idx = jnp.where((dist >= 0) & (dist < WINDOW), _t5_bucket(dist), -1).astype(jnp.int32)
    sb = SWA_BLOCKS
    return pl.pallas_call(
        _swa_body,
        grid=(b, nblk // sb),
        in_specs=[
            pl.BlockSpec(memory_space=pltpu.SMEM),
            pl.BlockSpec(memory_space=pltpu.SMEM),
            pl.BlockSpec((blk, 2 * blk), lambda bi, nb: (0, 0)),
            pl.BlockSpec((None, sb, n_pairs, blk, PAIR_W), lambda bi, nb: (bi, nb, 0, 0, 0)),
            pl.BlockSpec((1, blk, kvw), lambda bi, nb: (bi, jnp.maximum(nb * sb - 1, 0), 0)),
            pl.BlockSpec((1, sb * blk, kvw), lambda bi, nb: (bi, nb, 0)),
            pl.BlockSpec((1, blk, kvw), lambda bi, nb: (bi, jnp.maximum(nb * sb - 1, 0), 1)),
            pl.BlockSpec((1, sb * blk, kvw), lambda bi, nb: (bi, nb, 1)),
        ],
        out_specs=pl.BlockSpec((1, sb * blk, qw), lambda bi, nb: (bi, nb, 0)),
        out_shape=jax.ShapeDtypeStruct((b, t, qw), BF16),
        scratch_shapes=[pltpu.VMEM((2, KV_HEADS, 2, PAIRS * blk, 2 * blk), F32),
                        pltpu.VMEM((KV_HEADS, 2, PAIRS * blk, LANES), F32)],
        compiler_params=_params(),
        name="swa",
    )(rel_bias.reshape(-1), sinks, idx, q, kv, kv, kv, kv)


def kernel(x, rel_bias, norm_g, ffn_w_gu, ffn_w_down, rm_w_in, ml_conv_w, ml_gate_b, rm_head_g,
           rm_w_out, swa_w_in, swa_sinks, swa_w_out):
    b, t, d = x.shape
    n = b * t
    depth = norm_g.shape[0]
    main_cols = 4 * RET_HEADS * HEAD_W + 4 * ML_HEADS * HEAD_W
    xs = x.reshape(n, d)
    row_tiles = n // ROW_TILE
    proj_tiles = main_cols // PROJ_TN

    def row_bands(src, layer):
        rows = src.shape[1] // (row_tiles * proj_tiles)
        return src, layer, (rows, src.shape[2]), lambda i, j: (i * proj_tiles + j, 0)

    def blocks(src, layer):
        return src, layer, (src.shape[1] // row_tiles, src.shape[2] // proj_tiles), lambda i, j: (i, j)

    ready = {}

    def bf16_weight(name, src, layer):
        if (name, layer) in ready:
            return ready.pop((name, layer))
        return src[layer].astype(BF16)

    for layer in range(depth):
        g = norm_g[layer]
        if layer % 2 == 0:
            e = layer // 2
            w_in = rm_w_in[e]
            gate_cols = w_in.shape[1] - main_cols
            w_gate = jnp.pad(w_in[:, main_cols:], ((0, 0), (0, 2 * SUBLANES - gate_cols))).astype(BF16).T
            names = [("gu", layer), ("down", layer), ("rm_out", e)]
            casts = [row_bands(ffn_w_gu, layer), blocks(ffn_w_down, layer), row_bands(rm_w_out, e)]
            if layer + 1 < depth:
                o = (layer + 1) // 2
                names += [("gu", layer + 1), ("down", layer + 1), ("swa_in", o), ("swa_out", o)]
                casts += [row_bands(ffn_w_gu, layer + 1), blocks(ffn_w_down, layer + 1),
                          row_bands(swa_w_in, o), row_bands(swa_w_out, o)]
            proj, gates, *cast = _norm_matmul(xs, g[0], w_in.astype(BF16), w_gate, proj_tiles, ROW_TILE, PROJ_TN, casts)
            ready.update(zip(names, cast))
            mixed = _mixer(proj.reshape(main_cols // PROJ_TN, b, t, PROJ_TN), gates, ml_gate_b[e], ml_conv_w[e],
                           rm_head_g[e])
            xs = _proj_norm_res(mixed.reshape(n, -1), bf16_weight("rm_out", rm_w_out, e), xs, g[1], OUT_ROW_TILE)
        else:
            o = layer // 2
            q, kv = _swa_in_proj(xs, g[0], bf16_weight("swa_in", swa_w_in, o), b, t, ROW_TILE, PROJ_ROW_BLOCK)
            att = _swa(q, kv.reshape(b, t, -1), swa_sinks[o], rel_bias)
            xs = _proj_norm_res(att.reshape(n, -1), bf16_weight("swa_out", swa_w_out, o), xs, g[1], OUT_ROW_TILE)
        xs = _ffn(xs, g[2], bf16_weight("gu", ffn_w_gu, layer), bf16_weight("down", ffn_w_down, layer), g[3],
                  ROW_TILE, FFN_TF, FFN_ROW_BLOCK)
    return xs.reshape(b, t, d)
```

```python
import functools
import math

import jax
import jax.numpy as jnp
from jax import lax
from jax.experimental import pallas as pl
from jax.experimental.pallas import tpu as pltpu

F32 = jnp.float32
BF16 = jnp.bfloat16

EPS = 1e-6
CHUNK = 128
RET_HEADS = 4
ML_HEADS = 4
HEAD_W = 256
CONV_W = 4
ROPE_BASE = 10000.0
ATT_HEADS = 32
KV_HEADS = 4
HEAD_DIM = 64
WINDOW = 128
ATT_BLOCK = 128
N_BUCKETS = 32
MAX_DIST = 128
LANES = 128
SUBLANES = 8
VMEM_LIMIT = 56 * 1024 * 1024

ROW_TILE = 1024
OUT_ROW_TILE = 512
PROJ_TN = 1024
PROJ_ROW_BLOCK = 256
MIXER_CHUNKS = 2
MIXER_SEQS = 2
SWA_BLOCKS = 4
FFN_TF = 512
FFN_ROW_BLOCK = 512


def _params(vmem=VMEM_LIMIT):
    return pltpu.CompilerParams(vmem_limit_bytes=vmem)


def _rms_rows(x, g):
    ms = jnp.mean(x * x, axis=-1, keepdims=True)
    return x * lax.rsqrt(ms + EPS) * g


def _silu(x):
    return x * jax.nn.sigmoid(x)


def _cast_specs(casts):
    in_specs = [pl.BlockSpec((None, *blk), (lambda *ids, layer=layer, idx=idx: (layer, *idx(*ids))))
                for _, layer, blk, idx in casts]
    out_specs = [pl.BlockSpec(blk, idx) for _, _, blk, idx in casts]
    out_shape = [jax.ShapeDtypeStruct(src.shape[1:], BF16) for src, _, _, _ in casts]
    return in_specs, out_specs, out_shape


def _cast_pieces(src_refs, dst_refs):
    for src_ref, dst_ref in zip(src_refs, dst_refs):
        dst_ref[...] = src_ref[...].astype(dst_ref.dtype)


def _norm_matmul_body(x_ref, g_ref, w_ref, ws_ref, *rest, row_block, n_cast):
    cast_src, (o_ref, os_ref), cast_dst, h_ref = rest[:n_cast], rest[n_cast:n_cast + 2], rest[n_cast + 2:-1], rest[-1]
    j = pl.program_id(1)

    @pl.when(j == 0)
    def _():
        _cast_pieces(cast_src, cast_dst)
        for r in range(x_ref.shape[0] // row_block):
            rows = slice(r * row_block, (r + 1) * row_block)
            h = _rms_rows(x_ref[rows, :], g_ref[...]).astype(h_ref.dtype)
            h_ref[rows, :] = h
            os_ref[rows, :] = jnp.dot(h, ws_ref[...], preferred_element_type=F32)
            o_ref[rows, :] = jnp.dot(h, w_ref[...], preferred_element_type=F32).astype(o_ref.dtype)

    @pl.when(j > 0)
    def _():
        _cast_pieces(cast_src, cast_dst)
        o_ref[...] = jnp.dot(h_ref[...], w_ref[...], preferred_element_type=F32).astype(o_ref.dtype)


def _norm_matmul(x, g, w, w_side, tiles, tm, tn, casts):
    n, d = x.shape
    ns = w_side.shape[1]
    cast_in, cast_out, cast_shape = _cast_specs(casts)
    return pl.pallas_call(
        functools.partial(_norm_matmul_body, row_block=PROJ_ROW_BLOCK, n_cast=len(casts)),
        grid=(n // tm, tiles),
        in_specs=[
            pl.BlockSpec((tm, d), lambda i, j: (i, 0)),
            pl.BlockSpec((1, d), lambda i, j: (0, 0)),
            pl.BlockSpec((d, tn), lambda i, j: (0, j)),
            pl.BlockSpec((d, ns), lambda i, j: (0, 0)),
            *cast_in,
        ],
        out_specs=[pl.BlockSpec((None, tm, tn), lambda i, j: (j, i, 0)),
                   pl.BlockSpec((tm, ns), lambda i, j: (i, 0)),
                   *cast_out],
        out_shape=[jax.ShapeDtypeStruct((tiles, n, tn), BF16), jax.ShapeDtypeStruct((n, ns), F32), *cast_shape],
        scratch_shapes=[pltpu.VMEM((tm, d), BF16)],
        compiler_params=pltpu.CompilerParams(vmem_limit_bytes=VMEM_LIMIT,
                                             allow_input_fusion=[i == 2 for i in range(4 + len(casts))]),
        name="norm_matmul",
    )(x, g.reshape(1, d), w, w_side, *[c[0] for c in casts])


def _proj_norm_res_body(a_ref, w_ref, x_ref, g_ref, o_ref):
    y = jnp.dot(a_ref[...], w_ref[...], preferred_element_type=F32)
    o_ref[...] = x_ref[...] + _rms_rows(y, g_ref[...])


def _proj_norm_res(a, w, x, g, tm):
    n, d = x.shape
    return pl.pallas_call(
        _proj_norm_res_body,
        grid=(n // tm,),
        in_specs=[
            pl.BlockSpec((tm, a.shape[1]), lambda i: (i, 0)),
            pl.BlockSpec(w.shape, lambda i: (0, 0)),
            pl.BlockSpec((tm, d), lambda i: (i, 0)),
            pl.BlockSpec((1, d), lambda i: (0, 0)),
        ],
        out_specs=pl.BlockSpec((tm, d), lambda i: (i, 0)),
        out_shape=jax.ShapeDtypeStruct((n, d), F32),
        compiler_params=_params(),
        name="proj_norm_res",
    )(a, w, x, g.reshape(1, d))


def _ffn_body(x_ref, g_in_ref, wg_ref, wu_ref, wd_ref, g_out_ref, o_ref, h_ref, *, row_block):
    f = pl.program_id(1)
    last = pl.num_programs(1) - 1
    blocks = [slice(r * row_block, (r + 1) * row_block) for r in range(x_ref.shape[0] // row_block)]

    def partial_down(rows):
        h = h_ref[rows, :]
        gate = jnp.dot(h, wg_ref[...], preferred_element_type=F32)
        up = jnp.dot(h, wu_ref[...], preferred_element_type=F32)
        act = (_silu(gate) * up).astype(BF16)
        return jnp.dot(act, wd_ref[...], preferred_element_type=F32)

    @pl.when(f == 0)
    def _():
        for rows in blocks:
            h_ref[rows, :] = _rms_rows(x_ref[rows, :], g_in_ref[...]).astype(h_ref.dtype)
            o_ref[rows, :] = partial_down(rows)

    @pl.when((f > 0) & (f < last))
    def _():
        for rows in blocks:
            o_ref[rows, :] += partial_down(rows)

    @pl.when(f == last)
    def _():
        for rows in blocks:
            y = o_ref[rows, :] + partial_down(rows)
            o_ref[rows, :] = x_ref[rows, :] + _rms_rows(y, g_out_ref[...])


def _ffn(x, g_in, w_gu, w_down, g_out, tm, tf, row_block):
    n, d = x.shape
    nf = w_down.shape[0] // tf
    return pl.pallas_call(
        functools.partial(_ffn_body, row_block=row_block),
        grid=(n // tm, nf),
        in_specs=[
            pl.BlockSpec((tm, d), lambda i, f: (i, 0)),
            pl.BlockSpec((1, d), lambda i, f: (0, 0)),
            pl.BlockSpec((d, tf), lambda i, f: (0, f)),
            pl.BlockSpec((d, tf), lambda i, f: (0, f + nf)),
            pl.BlockSpec((tf, d), lambda i, f: (f, 0)),
            pl.BlockSpec((1, d), lambda i, f: (0, 0)),
        ],
        out_specs=pl.BlockSpec((tm, d), lambda i, f: (i, 0)),
        out_shape=jax.ShapeDtypeStruct((n, d), F32),
        scratch_shapes=[pltpu.VMEM((tm, d), BF16)],
        compiler_params=_params(),
        name="ffn",
    )(x, g_in.reshape(1, d), w_gu, w_gu, w_down, g_out.reshape(1, d))


def _log_sigmoid(x):
    return jnp.minimum(x, 0.0) - jnp.log1p(jnp.exp(-jnp.abs(x)))


_NT = (((1,), (1,)), ((), ()))
_TN = (((0,), (0,)), ((), ()))


def _retention_head(h, bb, rows, q_ref, k_ref, v_ref, g_ref, cos, sin, dm_ref, wq_ref, wk_ref, gl_ref, hg_ref,
                    o_ref, state_ref):
    cols = slice(h * HEAD_W, (h + 1) * HEAD_W)
    half = HEAD_W // 2

    def rot(x):
        x = x.astype(F32)
        x1, x2 = x[:, :half], x[:, half:]
        return jnp.concatenate([x1 * cos - x2 * sin, x2 * cos + x1 * sin], axis=-1)

    q = rot(q_ref[bb, rows, cols])
    k = rot(k_ref[bb, rows, cols]) * (HEAD_W ** -0.5)
    v = v_ref[bb, rows, cols]
    qb = q.astype(BF16)
    s = lax.dot_general(qb, k.astype(BF16), _NT, preferred_element_type=F32) * dm_ref[h]
    intra = jnp.dot(s.astype(BF16), v, preferred_element_type=F32)
    si = bb * RET_HEADS + h
    state = state_ref[si]
    inter = jnp.dot(qb, state.astype(BF16), preferred_element_type=F32) * wq_ref[h]
    out = intra + inter
    kw = (k * wk_ref[h]).astype(BF16)
    state_ref[si] = state * gl_ref[h] + lax.dot_general(kw, v, _TN, preferred_element_type=F32)
    y = _rms_rows(out, hg_ref[:, cols])
    o_ref[bb, rows, cols] = (y * _silu(g_ref[bb, rows, cols].astype(F32))).astype(o_ref.dtype)


def _causal_conv(cur, prev, shift_ref, w):
    shifted = jnp.dot(shift_ref[...], jnp.concatenate([prev, cur], axis=0), preferred_element_type=F32)
    y = cur.astype(F32) * w[CONV_W - 1:CONV_W]
    for s in range(1, CONV_W):
        y = y + shifted[(s - 1) * CHUNK:s * CHUNK] * w[CONV_W - 1 - s:CONV_W - s]
    return y


def _mlstm_head(h, bb, rows, q_all, k_all, v_ref, og_ref, gx, b_rows, lower_tri, hg_ref, o_ref, c_ref, n_ref,
                m_ref):
    cols = slice(h * HEAD_W, (h + 1) * HEAD_W)
    q = q_all[:, cols]
    k = k_all[:, cols]
    v = v_ref[bb, rows, cols]
    si = bb * ML_HEADS + h

    i_row = gx[h:h + 1]
    lf_row = gx[ML_HEADS + h:ML_HEADS + h + 1]
    b_row = b_rows[ML_HEADS + h:ML_HEADS + h + 1]
    b_cols = lax.dot_general(lower_tri, jnp.broadcast_to(lf_row, (CHUNK, CHUNK)), _NT,
                             precision=lax.Precision.HIGHEST, preferred_element_type=F32)
    i_cols = jnp.broadcast_to(i_row, (CHUNK, CHUNK)).T
    rowi = lax.broadcasted_iota(jnp.int32, (CHUNK, CHUNK), 0)
    coli = lax.broadcasted_iota(jnp.int32, (CHUNK, CHUNK), 1)
    log_d = jnp.where(rowi >= coli, b_cols - b_row + i_row, -jnp.inf)
    m_prev = m_ref[si]
    inter_log = b_cols + m_prev
    m_t = jnp.maximum(inter_log, jnp.max(log_d, axis=-1, keepdims=True))
    d_mat = jnp.exp(log_d - m_t)
    w_inter = jnp.exp(inter_log - m_t)

    def wide(a):
        return jnp.concatenate([a] * (HEAD_W // LANES), axis=1)

    qb = q.astype(BF16)
    s = lax.dot_general(qb, k.astype(BF16), _NT, preferred_element_type=F32) * d_mat
    c_state = c_ref[si]
    n_state = n_ref[si]
    num = (jnp.dot(s.astype(BF16), v, preferred_element_type=F32)
           + wide(w_inter) * jnp.dot(qb, c_state.astype(BF16), preferred_element_type=F32))
    den = (jnp.sum(s, axis=-1, keepdims=True)
           + w_inter * jnp.sum(q * n_state, axis=-1, keepdims=True))
    hid = num / wide(jnp.maximum(jnp.abs(den), jnp.exp(-m_t)))

    b_last = b_cols[CHUNK - 1:CHUNK]
    log_w = b_last - b_cols + i_cols
    m_new = jnp.maximum(b_last + m_prev, jnp.max(log_w, axis=0, keepdims=True))
    w = jnp.exp(log_w - m_new)
    decay = wide(jnp.exp(b_last + m_prev - m_new))
    kw = k * wide(w)
    c_ref[si] = decay * c_state + lax.dot_general(kw.astype(BF16), v, _TN, preferred_element_type=F32)
    n_ref[si] = decay * n_state + jnp.sum(kw, axis=0, keepdims=True)
    m_ref[si] = m_new

    out_cols = slice(RET_HEADS * HEAD_W + h * HEAD_W, RET_HEADS * HEAD_W + (h + 1) * HEAD_W)
    y = _rms_rows(hid, hg_ref[:, out_cols])
    o_ref[bb, rows, out_cols] = (y * jax.nn.sigmoid(og_ref[bb, rows, cols].astype(F32))).astype(o_ref.dtype)


def _mixer_body(gl_ref, rq_ref, rk_ref, rv_ref, rg_ref, mq_ref, mqp_ref, mk_ref, mkp_ref, mv_ref, mo_ref,
                cos_ref, sin_ref, dm_ref, wq_ref, wk_ref, gr_ref, gb_ref, cw_ref, shift_ref, hg_ref,
                o_ref, r_ref, c_ref, n_ref, m_ref):
    first = pl.program_id(1) == 0

    @pl.when(first)
    def _():
        r_ref[...] = jnp.zeros_like(r_ref)
        c_ref[...] = jnp.zeros_like(c_ref)
        n_ref[...] = jnp.zeros_like(n_ref)
        m_ref[...] = jnp.zeros_like(m_ref)

    ml_w = ML_HEADS * HEAD_W
    rowi = lax.broadcasted_iota(jnp.int32, (CHUNK, CHUNK), 0)
    coli = lax.broadcasted_iota(jnp.int32, (CHUNK, CHUNK), 1)
    upper_tri = jnp.where(rowi <= coli, 1.0, 0.0).astype(F32)
    lower_tri = jnp.where(rowi >= coli, 1.0, 0.0).astype(F32)
    is_input = lax.broadcasted_iota(jnp.int32, (2 * ML_HEADS, CHUNK), 0) < ML_HEADS

    for bb in range(mq_ref.shape[0]):
        for ci in range(mq_ref.shape[1] // CHUNK):
            rows = slice(ci * CHUNK, (ci + 1) * CHUNK)
            cos = cos_ref[rows, :]
            sin = sin_ref[rows, :]
            for h in range(RET_HEADS):
                _retention_head(h, bb, rows, rq_ref, rk_ref, rv_ref, rg_ref, cos, sin, dm_ref, wq_ref, wk_ref,
                                gl_ref, hg_ref, o_ref, r_ref)

            if ci == 0:
                q_prev = jnp.where(first, jnp.zeros_like(mqp_ref[bb]), mqp_ref[bb])
                k_prev = jnp.where(first, jnp.zeros_like(mkp_ref[bb]), mkp_ref[bb])
            else:
                before = slice((ci - 1) * CHUNK, ci * CHUNK)
                q_prev, k_prev = mq_ref[bb, before, :], mk_ref[bb, before, :]
            q_all = _silu(_causal_conv(mq_ref[bb, rows, :], q_prev, shift_ref, cw_ref[:, :ml_w]))
            k_all = _silu(_causal_conv(mk_ref[bb, rows, :], k_prev, shift_ref, cw_ref[:, ml_w:])) * (HEAD_W ** -0.5)
            gates = gr_ref[bb, :, rows] + gb_ref[...]
            gx = jnp.where(is_input, gates, _log_sigmoid(gates))
            b_rows = jnp.dot(gx, upper_tri, precision=lax.Precision.HIGHEST, preferred_element_type=F32)
            for h in range(ML_HEADS):
                _mlstm_head(h, bb, rows, q_all, k_all, mv_ref, mo_ref, gx, b_rows, lower_tri, hg_ref, o_ref,
                            c_ref, n_ref, m_ref)


def _mixer(proj, gates_row, gate_b, conv_w, head_g):
    _, b, t, _ = proj.shape
    seqs = MIXER_SEQS
    step = MIXER_CHUNKS * CHUNK
    nc = t // step
    ret_w = RET_HEADS * HEAD_W
    ml_w = ML_HEADS * HEAD_W
    assert ret_w == ml_w == proj.shape[3]
    log_g = jnp.log1p(-jnp.exp2(-5.0 - jnp.arange(RET_HEADS, dtype=F32)))
    idx = jnp.arange(CHUNK, dtype=F32)
    diff = idx[:, None] - idx[None, :]
    dmask = jnp.where(diff >= 0, jnp.exp(log_g[:, None, None] * jnp.maximum(diff, 0.0)), 0.0)
    w_k = jnp.exp(log_g[:, None] * (CHUNK - 1.0 - idx)[None, :])
    w_q = jnp.exp(log_g[:, None] * (idx + 1.0)[None, :])
    w_k = jnp.broadcast_to(w_k[..., None], (RET_HEADS, CHUNK, HEAD_W))
    w_q = jnp.broadcast_to(w_q[..., None], (RET_HEADS, CHUNK, HEAD_W))
    g_l = jnp.exp(log_g * CHUNK)
    half = HEAD_W // 2
    inv = 1.0 / (ROPE_BASE ** jnp.linspace(0.0, 1.0, half, dtype=F32))
    ang = jnp.arange(t).astype(F32)[:, None] * inv[None, :]
    cos, sin = jnp.cos(ang), jnp.sin(ang)
    r = jnp.arange((CONV_W - 1) * CHUNK)
    src = CHUNK + r % CHUNK - (r // CHUNK + 1)
    shift = (jnp.arange(2 * CHUNK)[None, :] == src[:, None]).astype(BF16)

    def group(gi):
        return pl.BlockSpec((None, seqs, step, ret_w), lambda bi, c: (gi, bi, c, 0))

    def prev_group(gi):
        return pl.BlockSpec((None, seqs, CHUNK, ml_w),
                            lambda bi, c: (gi, bi, jnp.maximum(c * MIXER_CHUNKS - 1, 0), 0))

    def whole(a):
        return pl.BlockSpec(a.shape, lambda bi, c: (0,) * a.ndim)

    gb = gate_b.reshape(2 * ML_HEADS, 1)
    hg = head_g.reshape(1, ret_w + ml_w)
    return pl.pallas_call(
        _mixer_body,
        grid=(b // seqs, nc),
        in_specs=[
            pl.BlockSpec(memory_space=pltpu.SMEM),
            group(0), group(1), group(2), group(3),
            group(4), prev_group(4), group(5), prev_group(5), group(6), group(7),
            pl.BlockSpec((step, half), lambda bi, c: (c, 0)),
            pl.BlockSpec((step, half), lambda bi, c: (c, 0)),
            whole(dmask), whole(w_q), whole(w_k),
            pl.BlockSpec((seqs, 2 * ML_HEADS, step), lambda bi, c: (bi, 0, c)),
            whole(gb), whole(conv_w), whole(shift), whole(hg),
        ],
        out_specs=pl.BlockSpec((seqs, step, ret_w + ml_w), lambda bi, c: (bi, c, 0)),
        out_shape=jax.ShapeDtypeStruct((b, t, ret_w + ml_w), BF16),
        scratch_shapes=[pltpu.VMEM((seqs * RET_HEADS, HEAD_W, HEAD_W), F32),
                        pltpu.VMEM((seqs * ML_HEADS, HEAD_W, HEAD_W), F32),
                        pltpu.VMEM((seqs * ML_HEADS, 1, HEAD_W), F32),
                        pltpu.VMEM((seqs * ML_HEADS, 1, LANES), F32)],
        compiler_params=_params(),
        name="mixer",
    )(g_l, proj, proj, proj, proj, proj, proj, proj, proj, proj, proj, cos, sin, dmask, w_q, w_k,
      gates_row, gb, conv_w, shift, hg)


PAIR_W = 2 * HEAD_DIM
GROUP = ATT_HEADS // KV_HEADS
PAIRS = GROUP // 2
LOG2E = math.log2(math.e)
Q_PRESCALE = HEAD_DIM ** -0.5 * LOG2E


def _swa_in_proj_body(x_ref, g_ref, w_ref, q_ref, kv_ref, *, row_block):
    qw = ATT_HEADS * HEAD_DIM
    per_block = row_block // ATT_BLOCK
    for r in range(x_ref.shape[0] // row_block):
        rows = slice(r * row_block, (r + 1) * row_block)
        h = _rms_rows(x_ref[rows, :], g_ref[...]).astype(BF16)
        res = jnp.dot(h, w_ref[...], preferred_element_type=F32)
        for nbl in range(per_block):
            sub = slice(nbl * ATT_BLOCK, (nbl + 1) * ATT_BLOCK)
            for p in range(q_ref.shape[2]):
                q_ref[0, r * per_block + nbl, p] = (res[sub, p * PAIR_W:(p + 1) * PAIR_W]
                                                    * Q_PRESCALE).astype(q_ref.dtype)
        kv_ref[rows, :] = res[:, qw:].astype(kv_ref.dtype)


def _swa_in_proj(x, g, w, b, t, tm, row_block):
    n, d = x.shape
    cols = w.shape[1]
    kvw = cols - ATT_HEADS * HEAD_DIM
    tiles_per_seq = t // tm
    return pl.pallas_call(
        functools.partial(_swa_in_proj_body, row_block=row_block),
        grid=(n // tm,),
        in_specs=[
            pl.BlockSpec((tm, d), lambda i: (i, 0)),
            pl.BlockSpec((1, d), lambda i: (0, 0)),
            pl.BlockSpec((d, cols), lambda i: (0, 0)),
        ],
        out_specs=[
            pl.BlockSpec((1, tm // ATT_BLOCK, ATT_HEADS // 2, ATT_BLOCK, PAIR_W),
                         lambda i: (i // tiles_per_seq, i % tiles_per_seq, 0, 0, 0)),
            pl.BlockSpec((tm, kvw), lambda i: (i, 0)),
        ],
        out_shape=[jax.ShapeDtypeStruct((b, t // ATT_BLOCK, ATT_HEADS // 2, ATT_BLOCK, PAIR_W), BF16),
                   jax.ShapeDtypeStruct((n, kvw), BF16)],
        compiler_params=_params(),
        name="swa_in_proj",
    )(x, g.reshape(1, d), w)


def _swa_body(rb_ref, sink_ref, idx_ref, q_ref, kp_ref, kc_ref, vp_ref, vc_ref, o_ref, bias_ref, sinkcol_ref):
    bi = pl.program_id(0)
    nb = pl.program_id(1)
    blk = ATT_BLOCK
    rows = PAIRS * blk

    @pl.when((bi == 0) & (nb == 0))
    def _():
        idx = idx_ref[...]
        col = lax.broadcasted_iota(jnp.int32, idx.shape, 1)

        def head_body(hd, carry):
            def bucket_body(bk, acc):
                return jnp.where(idx == bk, rb_ref[bk * ATT_HEADS + hd] * LOG2E, acc)

            tbl = lax.fori_loop(0, N_BUCKETS, bucket_body, jnp.full(idx.shape, -jnp.inf, F32))
            kv = hd // GROUP
            half = hd % 2
            at = pl.ds(pl.multiple_of(((hd % GROUP) // 2) * blk, blk), blk)
            bias_ref[0, kv, half, at, :] = tbl
            bias_ref[1, kv, half, at, :] = jnp.where(col >= blk, tbl, -jnp.inf)
            sinkcol_ref[kv, half, at, :] = jnp.full((blk, LANES), sink_ref[hd] * LOG2E, F32)
            return carry

        lax.fori_loop(0, ATT_HEADS, head_body, 0)

    low = lax.broadcasted_iota(jnp.int32, (2 * blk, LANES), 1) < HEAD_DIM
    low_out = lax.broadcasted_iota(jnp.int32, (rows, LANES), 1) < HEAD_DIM

    def halves(xf, kv):
        xc = xf[:, (kv // 2) * LANES:(kv // 2 + 1) * LANES]
        xr = pltpu.roll(xc, HEAD_DIM, axis=1)
        lo_src, hi_src = (xc, xr) if kv % 2 == 0 else (xr, xc)
        return (jnp.where(low, lo_src, 0.0).astype(BF16), jnp.where(low, 0.0, hi_src).astype(BF16))

    for sb in range(q_ref.shape[0]):
        here = slice(sb * blk, (sb + 1) * blk)
        before = slice((sb - 1) * blk, sb * blk)
        k_prev, v_prev = (kp_ref[0], vp_ref[0]) if sb == 0 else (kc_ref[0, before, :], vc_ref[0, before, :])
        kf = jnp.concatenate([k_prev, kc_ref[0, here, :]], axis=0).astype(F32)
        vf = jnp.concatenate([v_prev, vc_ref[0, here, :]], axis=0).astype(F32)
        first = (nb == 0).astype(jnp.int32) if sb == 0 else 0
        for kv in range(KV_HEADS):
            k_lo, k_hi = halves(kf, kv)
            v_lo, v_hi = halves(vf, kv)
            q4 = q_ref[sb, kv * PAIRS:(kv + 1) * PAIRS].reshape(rows, PAIR_W)

            def probs(k_half, half):
                s = lax.dot_general(q4, k_half, (((1,), (1,)), ((), ())), preferred_element_type=F32)
                s = s + bias_ref[first, kv, half]
                sink = sinkcol_ref[kv, half]
                mx = jnp.maximum(jnp.max(s, axis=-1, keepdims=True), sink)
                p = jnp.exp2(s - jnp.concatenate([mx, mx], axis=1))
                den = jnp.sum(p, axis=-1, keepdims=True) + jnp.exp2(sink - mx)
                return p.astype(BF16), den

            p_lo, den_lo = probs(k_lo, 0)
            p_hi, den_hi = probs(k_hi, 1)
            o4 = (jnp.dot(p_lo, v_lo, preferred_element_type=F32)
                  + jnp.dot(p_hi, v_hi, preferred_element_type=F32)) / jnp.where(low_out, den_lo, den_hi)
            for pr in range(PAIRS):
                at = (kv * PAIRS + pr) * PAIR_W
                o_ref[0, here, at:at + PAIR_W] = o4[pr * blk:(pr + 1) * blk].astype(o_ref.dtype)


def _t5_bucket(dist):
    n = jnp.maximum(dist, 0)
    max_exact = N_BUCKETS // 2
    nf = jnp.maximum(n, 1).astype(F32)
    large = max_exact + (jnp.log(nf / max_exact) / math.log(MAX_DIST / max_exact)
                         * (N_BUCKETS - max_exact)).astype(jnp.int32)
    large = jnp.minimum(large, N_BUCKETS - 1)
    return jnp.where(n < max_exact, n, large)


def _swa(q, kv, sinks, rel_bias):
    b, nblk, n_pairs, blk, _ = q.shape
    t = nblk * blk
    qw = ATT_HEADS * HEAD_DIM
    kvw = KV_HEADS * HEAD_DIM
    i = jnp.arange(blk)
    j = jnp.arange(2 * blk)
    dist = (blk + i)[:, None] - j[None, :]
    idx = jnp.where((dist >= 0) & (dist < WINDOW), _t5_bucket(dist), -1).astype(jnp.int32)
    sb = SWA_BLOCKS
    return pl.pallas_call(
        _swa_body,
        grid=(b, nblk // sb),
        in_specs=[
            pl.BlockSpec(memory_space=pltpu.SMEM),
            pl.BlockSpec(memory_space=pltpu.SMEM),
            pl.BlockSpec((blk, 2 * blk), lambda bi, nb: (0, 0)),
            pl.BlockSpec((None, sb, n_pairs, blk, PAIR_W), lambda bi, nb: (bi, nb, 0, 0, 0)),
            pl.BlockSpec((1, blk, kvw), lambda bi, nb: (bi, jnp.maximum(nb * sb - 1, 0), 0)),
            pl.BlockSpec((1, sb * blk, kvw), lambda bi, nb: (bi, nb, 0)),
            pl.BlockSpec((1, blk, kvw), lambda bi, nb: (bi, jnp.maximum(nb * sb - 1, 0), 1)),
            pl.BlockSpec((1, sb * blk, kvw), lambda bi, nb: (bi, nb, 1)),
        ],
        out_specs=pl.BlockSpec((1, sb * blk, qw), lambda bi, nb: (bi, nb, 0)),
        out_shape=jax.ShapeDtypeStruct((b, t, qw), BF16),
        scratch_shapes=[pltpu.VMEM((2, KV_HEADS, 2, PAIRS * blk, 2 * blk), F32),
                        pltpu.VMEM((KV_HEADS, 2, PAIRS * blk, LANES), F32)],
        compiler_params=_params(),
        name="swa",
    )(rel_bias.reshape(-1), sinks, idx, q, kv, kv, kv, kv)


def kernel(x, rel_bias, norm_g, ffn_w_gu, ffn_w_down, rm_w_in, ml_conv_w, ml_gate_b, rm_head_g,
           rm_w_out, swa_w_in, swa_sinks, swa_w_out):
    b, t, d = x.shape
    n = b * t
    depth = norm_g.shape[0]
    main_cols = 4 * RET_HEADS * HEAD_W + 4 * ML_HEADS * HEAD_W
    xs = x.reshape(n, d)
    row_tiles = n // ROW_TILE
    proj_tiles = main_cols // PROJ_TN

    def row_bands(src, layer):
        rows = src.shape[1] // (row_tiles * proj_tiles)
        return src, layer, (rows, src.shape[2]), lambda i, j: (i * proj_tiles + j, 0)

    def blocks(src, layer):
        return src, layer, (src.shape[1] // row_tiles, src.shape[2] // proj_tiles), lambda i, j: (i, j)

    ready = {}

    def bf16_weight(name, src, layer):
        if (name, layer) in ready:
            return ready.pop((name, layer))
        return src[layer].astype(BF16)

    for layer in range(depth):
        g = norm_g[layer]
        if layer % 2 == 0:
            e = layer // 2
            w_in = rm_w_in[e]
            gate_cols = w_in.shape[1] - main_cols
            w_gate = jnp.pad(w_in[:, main_cols:], ((0, 0), (0, LANES - gate_cols))).astype(BF16)
            names = [("gu", layer), ("down", layer), ("rm_out", e)]
            casts = [row_bands(ffn_w_gu, layer), blocks(ffn_w_down, layer), row_bands(rm_w_out, e)]
            if layer + 1 < depth:
                o = (layer + 1) // 2
                names += [("gu", layer + 1), ("down", layer + 1), ("swa_in", o), ("swa_out", o)]
                casts += [row_bands(ffn_w_gu, layer + 1), blocks(ffn_w_down, layer + 1),
                          row_bands(swa_w_in, o), row_bands(swa_w_out, o)]
            proj, gates, *cast = _norm_matmul(xs, g[0], w_in.astype(BF16), w_gate, proj_tiles, ROW_TILE, PROJ_TN, casts)
            ready.update(zip(names, cast))
            gates_row = gates[:, :gate_cols].reshape(b, t, gate_cols).transpose(0, 2, 1)
            mixed = _mixer(proj.reshape(main_cols // PROJ_TN, b, t, PROJ_TN), gates_row, ml_gate_b[e], ml_conv_w[e],
                           rm_head_g[e])
            xs = _proj_norm_res(mixed.reshape(n, -1), bf16_weight("rm_out", rm_w_out, e), xs, g[1], OUT_ROW_TILE)
        else:
            o = layer // 2
            q, kv = _swa_in_proj(xs, g[0], bf16_weight("swa_in", swa_w_in, o), b, t, ROW_TILE, PROJ_ROW_BLOCK)
            att = _swa(q, kv.reshape(b, t, -1), swa_sinks[o], rel_bias)
            xs = _proj_norm_res(att.reshape(n, -1), bf16_weight("swa_out", swa_w_out, o), xs, g[1], OUT_ROW_TILE)
        xs = _ffn(xs, g[2], bf16_weight("gu", ffn_w_gu, layer), bf16_weight("down", ffn_w_down, layer), g[3],
                  ROW_TILE, FFN_TF, FFN_ROW_BLOCK)
    return xs.reshape(b, t, d)
```
